```python
import jax, jax.numpy as jnp
from jax import lax
import numpy as np

D_MODEL = 2048
BATCH = 8
SEQ = 8192
DEPTH = 1

N_HEADS_MLA = 8
Q_LORA = 512
KV_LORA = 256
MLA_NOPE = 128
MLA_ROPE = 64
MLA_V = 128
MLA_QK = MLA_NOPE + MLA_ROPE
DIL_GROUPS = ((128, 1), (512, 4), (2048, 16))
N_DIL_GROUPS = len(DIL_GROUPS)
N_HEADS_DIL = 8
DIL_HEAD = 128
DIL_ROT = DIL_HEAD // 4
ROPE_THETA = 500000.0
D_FF = 5632
PLE_DIM = 256
EPS = 1e-6
Q_BLOCK = 128
NEG = -1e30

OFF_CQ = 0
OFF_CKV = OFF_CQ + Q_LORA
OFF_KR = OFF_CKV + KV_LORA
OFF_DIL = OFF_KR + MLA_ROPE
DIL_QKV = N_DIL_GROUPS * 3 * N_HEADS_DIL * DIL_HEAD
OFF_GATE = OFF_DIL + DIL_QKV
N_BRANCH = 2
D_IN = OFF_GATE + N_BRANCH * D_MODEL

kernel_name = "hybrid_mla_dilated_gated_macaron"


def rmsnorm(x, g):
    xf = x.astype(jnp.float32)
    y = xf * lax.rsqrt(jnp.mean(xf * xf, axis=-1, keepdims=True) + EPS)
    return (y * g.astype(jnp.float32)).astype(x.dtype)


def rope(x, pos):
    rd = x.shape[-1]
    half = rd // 2
    inv = ROPE_THETA ** (-jnp.arange(half, dtype=jnp.float32) * 2.0 / rd)
    ang = pos.astype(jnp.float32)[..., None] * inv
    cos = jnp.cos(ang)[:, :, None, :]
    sin = jnp.sin(ang)[:, :, None, :]
    xf = x.astype(jnp.float32)
    x1, x2 = xf[..., :half], xf[..., half:]
    return jnp.concatenate([x1 * cos - x2 * sin, x2 * cos + x1 * sin], axis=-1).astype(x.dtype)


def swiglu(h, wg, wu, wd):
    return (jax.nn.silu(h @ wg) * (h @ wu)) @ wd


def causal_block_attention(q, k, v):
    B, S, H, Dq = q.shape
    nq = S // Q_BLOCK
    qb = q.reshape(B, nq, Q_BLOCK, H, Dq).transpose(1, 0, 2, 3, 4)
    starts = jnp.arange(nq, dtype=jnp.int32) * Q_BLOCK
    kpos = jnp.arange(S, dtype=jnp.int32)

    def one_block(args):
        qi, s0 = args
        sc = jnp.einsum('bqhd,bkhd->bhqk', qi, k).astype(jnp.float32)
        qpos = s0 + jnp.arange(Q_BLOCK, dtype=jnp.int32)
        mask = kpos[None, :] <= qpos[:, None]
        pr = jax.nn.softmax(jnp.where(mask[None, None], sc, NEG), axis=-1)
        return jnp.einsum('bhqk,bkhd->bqhd', pr.astype(v.dtype), v)

    out = lax.map(one_block, (qb, starts))
    return out.transpose(1, 0, 2, 3, 4).reshape(B, S, H, v.shape[-1])


def dilated_window_attention(q, k, v, dil, n):
    B, S, H, Dh = q.shape
    L = S // dil
    Lp = -(-L // n) * n
    nb = Lp // n

    def phase_major(t):
        t = t.reshape(B, L, dil, H, Dh).transpose(0, 2, 1, 3, 4)
        return jnp.pad(t, ((0, 0), (0, 0), (0, Lp - L), (0, 0), (0, 0)))

    def windows(t):
        t = jnp.pad(phase_major(t), ((0, 0), (0, 0), (n, 0), (0, 0), (0, 0)))
        t = t.reshape(B, dil, nb + 1, n, H, Dh)
        return jnp.concatenate([t[:, :, :-1], t[:, :, 1:]], axis=3)

    qb = phase_major(q).reshape(B, dil, nb, n, H, Dh)
    kw = windows(k)
    vw = windows(v)
    sc = jnp.einsum('brnqhc,brnkhc->brnhqk', qb, kw).astype(jnp.float32)
    i = jnp.arange(n)[:, None]
    j = jnp.arange(2 * n)[None, :]
    dist = i + n - j
    band = (dist >= 0) & (dist <= n)
    keyok = ((jnp.arange(nb)[:, None] - 1) * n + jnp.arange(2 * n)[None, :]) >= 0
    valid = (band[None] & keyok[:, None, :])[None, None, :, None]
    m = jnp.max(jnp.where(valid, sc, NEG), axis=-1)
    e = jnp.where(valid, jnp.exp(sc - m[..., None]), 0.0)
    l = jnp.sum(e, axis=-1)
    o = jnp.einsum('brnhqk,brnkhc->brnqhc', (e / l[..., None]).astype(v.dtype), vw)
    o = o.reshape(B, dil, Lp, H, Dh)[:, :, :L].transpose(0, 2, 1, 3, 4).reshape(B, S, H, Dh)

    def back(s):
        s = s.transpose(0, 1, 2, 4, 3).reshape(B, dil, Lp, H)[:, :, :L]
        return s.transpose(0, 2, 1, 3).reshape(B, S, H)

    return o, back(m), back(l)


def _fwd_setup_inputs(seed: int = 0) -> dict:
    key = jax.random.key(seed)
    ks = jax.random.split(key, 32)
    f32 = jnp.float32

    def nrm(k, shape, fan):
        return jax.random.normal(k, shape, f32) * fan ** -0.5

    def gain(k, shape):
        return 1.0 + 0.1 * jax.random.normal(k, shape, f32)

    Dp = DEPTH
    return {
        "x": jax.random.normal(ks[0], (BATCH, SEQ, D_MODEL), f32),
        "p": jax.random.normal(ks[1], (DEPTH, BATCH, SEQ, PLE_DIM), f32),
        "positions": jnp.arange(SEQ, dtype=jnp.int32)[None, :]
        + jax.random.randint(ks[2], (BATCH, 1), 0, 4096, dtype=jnp.int32),
        "g_ffn1": gain(ks[3], (Dp, D_MODEL)),
        "w1_gate": nrm(ks[4], (Dp, D_MODEL, D_FF), D_MODEL),
        "w1_up": nrm(ks[5], (Dp, D_MODEL, D_FF), D_MODEL),
        "w1_down": nrm(ks[6], (Dp, D_FF, D_MODEL), D_FF),
        "g_mix": gain(ks[7], (Dp, D_MODEL)),
        "w_in": nrm(ks[8], (Dp, D_MODEL, D_IN), D_MODEL),
        "g_cq": gain(ks[9], (Dp, Q_LORA)),
        "w_uq": nrm(ks[10], (Dp, Q_LORA, N_HEADS_MLA * MLA_QK), Q_LORA),
        "g_ckv": gain(ks[11], (Dp, KV_LORA)),
        "w_ukv": nrm(ks[12], (Dp, KV_LORA, N_HEADS_MLA * (MLA_NOPE + MLA_V)), KV_LORA),
        "g_q_mla": gain(ks[13], (Dp, MLA_QK)),
        "g_k_mla": gain(ks[14], (Dp, MLA_QK)),
        "g_q_dil": gain(ks[15], (Dp, N_DIL_GROUPS, DIL_HEAD)),
        "g_k_dil": gain(ks[16], (Dp, N_DIL_GROUPS, DIL_HEAD)),
        "w_br_mla": nrm(ks[17], (Dp, N_HEADS_MLA * MLA_V, D_MODEL), N_HEADS_MLA * MLA_V),
        "w_br_dil": nrm(ks[18], (Dp, N_HEADS_DIL * DIL_HEAD, D_MODEL), N_HEADS_DIL * DIL_HEAD),
        "w_o": nrm(ks[19], (Dp, D_MODEL, D_MODEL), D_MODEL),
        "g_ffn2": gain(ks[20], (Dp, D_MODEL)),
        "w2_gate": nrm(ks[21], (Dp, D_MODEL, D_FF), D_MODEL),
        "w2_up": nrm(ks[22], (Dp, D_MODEL, D_FF), D_MODEL),
        "w2_down": nrm(ks[23], (Dp, D_FF, D_MODEL), D_FF),
        "g_ple": gain(ks[24], (Dp, D_MODEL)),
        "w_ple_gate": nrm(ks[25], (Dp, D_MODEL, D_MODEL), D_MODEL),
        "w_ple_proj": nrm(ks[26], (Dp, PLE_DIM, D_MODEL), PLE_DIM),
    }


def _fwd_reference(x, p, positions, g_ffn1, w1_gate, w1_up, w1_down, g_mix, w_in, g_cq, w_uq,
              g_ckv, w_ukv, g_q_mla, g_k_mla, g_q_dil, g_k_dil, w_br_mla, w_br_dil, w_o,
              g_ffn2, w2_gate, w2_up, w2_down, g_ple, w_ple_gate, w_ple_proj):
    B, S, _ = x.shape
    scale_mla = MLA_QK ** -0.5
    scale_dil = DIL_HEAD ** -0.5
    for i in range(DEPTH):
        x = x + 0.5 * swiglu(rmsnorm(x, g_ffn1[i]), w1_gate[i], w1_up[i], w1_down[i])

        h = rmsnorm(x, g_mix[i])
        proj = h @ w_in[i]

        cq = rmsnorm(proj[..., OFF_CQ:OFF_CKV], g_cq[i])
        q = (cq @ w_uq[i]).reshape(B, S, N_HEADS_MLA, MLA_QK)
        ckv = rmsnorm(proj[..., OFF_CKV:OFF_KR], g_ckv[i])
        kv = (ckv @ w_ukv[i]).reshape(B, S, N_HEADS_MLA, MLA_NOPE + MLA_V)
        k_nope, v = kv[..., :MLA_NOPE], kv[..., MLA_NOPE:]
        k_rope = jnp.broadcast_to(proj[:, :, None, OFF_KR:OFF_DIL], (B, S, N_HEADS_MLA, MLA_ROPE))
        k = jnp.concatenate([k_nope, k_rope], axis=-1)
        q = rmsnorm(q, g_q_mla[i])
        k = rmsnorm(k, g_k_mla[i])
        q = jnp.concatenate([q[..., :MLA_NOPE], rope(q[..., MLA_NOPE:], positions)], axis=-1) * scale_mla
        k = jnp.concatenate([k[..., :MLA_NOPE], rope(k[..., MLA_NOPE:], positions)], axis=-1)
        o_mla = causal_block_attention(q, k, v).reshape(B, S, N_HEADS_MLA * MLA_V)

        dqkv = proj[..., OFF_DIL:OFF_GATE].reshape(B, S, N_DIL_GROUPS, 3, N_HEADS_DIL, DIL_HEAD)
        outs, ms, ls = [], [], []
        for g, (win, dil) in enumerate(DIL_GROUPS):
            qg = rmsnorm(dqkv[:, :, g, 0], g_q_dil[i, g])
            kg = rmsnorm(dqkv[:, :, g, 1], g_k_dil[i, g])
            vg = dqkv[:, :, g, 2]
            qg = jnp.concatenate([rope(qg[..., :DIL_ROT], positions), qg[..., DIL_ROT:]], axis=-1) * scale_dil
            kg = jnp.concatenate([rope(kg[..., :DIL_ROT], positions), kg[..., DIL_ROT:]], axis=-1)
            o, m, l = dilated_window_attention(qg, kg, vg, dil, win // dil)
            outs.append(o)
            ms.append(m)
            ls.append(l)
        m_all = jnp.stack(ms, axis=0)
        l_all = jnp.stack(ls, axis=0)
        wgt = l_all * jnp.exp(m_all - jnp.max(m_all, axis=0, keepdims=True))
        wgt = wgt / jnp.sum(wgt, axis=0, keepdims=True)
        o_dil = jnp.einsum('gbsh,gbshc->bshc', wgt.astype(x.dtype), jnp.stack(outs, axis=0))
        o_dil = o_dil.reshape(B, S, N_HEADS_DIL * DIL_HEAD)

        gates = jax.nn.sigmoid(proj[..., OFF_GATE:].reshape(B, S, N_BRANCH, D_MODEL))
        merged = gates[:, :, 0] * (o_mla @ w_br_mla[i]) + gates[:, :, 1] * (o_dil @ w_br_dil[i])
        x = x + merged @ w_o[i]

        x = x + 0.5 * swiglu(rmsnorm(x, g_ffn2[i]), w2_gate[i], w2_up[i], w2_down[i])

        ple_gate = jax.nn.sigmoid(rmsnorm(x, g_ple[i]) @ w_ple_gate[i])
        x = x + ple_gate * (p[i] @ w_ple_proj[i])
    return x


import jax as _jax
import jax.numpy as _jnp

TWIN_FORMAT = 'train_step'
FWD_PARAMS = ['x', 'p', 'positions', 'g_ffn1', 'w1_gate', 'w1_up', 'w1_down', 'g_mix', 'w_in', 'g_cq', 'w_uq', 'g_ckv', 'w_ukv', 'g_q_mla', 'g_k_mla', 'g_q_dil', 'g_k_dil', 'w_br_mla', 'w_br_dil', 'w_o', 'g_ffn2', 'w2_gate', 'w2_up', 'w2_down', 'g_ple', 'w_ple_gate', 'w_ple_proj']
TWIN_WEIGHTS = ['g_ffn1', 'w1_gate', 'w1_up', 'w1_down', 'g_mix', 'w_in', 'g_cq', 'w_uq', 'g_ckv', 'w_ukv', 'g_q_mla', 'g_k_mla', 'g_q_dil', 'g_k_dil', 'w_br_mla', 'w_br_dil', 'w_o', 'g_ffn2', 'w2_gate', 'w2_up', 'w2_down', 'g_ple', 'w_ple_gate', 'w_ple_proj']
TWIN_DIFF_INPUT = 'x'
TWIN_INPUTS = ['x', 'p', 'positions', 'g_ffn1', 'w1_gate', 'w1_up', 'w1_down', 'g_mix', 'w_in', 'g_cq', 'w_uq', 'g_ckv', 'w_ukv', 'g_q_mla', 'g_k_mla', 'g_q_dil', 'g_k_dil', 'w_br_mla', 'w_br_dil', 'w_o', 'g_ffn2', 'w2_gate', 'w2_up', 'w2_down', 'g_ple', 'w_ple_gate', 'w_ple_proj', 'loss_target', 'm_g_ffn1', 'm_w1_gate', 'm_w1_up', 'm_w1_down', 'm_g_mix', 'm_w_in', 'm_g_cq', 'm_w_uq', 'm_g_ckv', 'm_w_ukv', 'm_g_q_mla', 'm_g_k_mla', 'm_g_q_dil', 'm_g_k_dil', 'm_w_br_mla', 'm_w_br_dil', 'm_w_o', 'm_g_ffn2', 'm_w2_gate', 'm_w2_up', 'm_w2_down', 'm_g_ple', 'm_w_ple_gate', 'm_w_ple_proj', 'v_g_ffn1', 'v_w1_gate', 'v_w1_up', 'v_w1_down', 'v_g_mix', 'v_w_in', 'v_g_cq', 'v_w_uq', 'v_g_ckv', 'v_w_ukv', 'v_g_q_mla', 'v_g_k_mla', 'v_g_q_dil', 'v_g_k_dil', 'v_w_br_mla', 'v_w_br_dil', 'v_w_o', 'v_g_ffn2', 'v_w2_gate', 'v_w2_up', 'v_w2_down', 'v_g_ple', 'v_w_ple_gate', 'v_w_ple_proj']
TWIN_OUTPUTS = ['loss', 'grad_x', 'grad_g_ffn1', 'grad_w1_gate', 'grad_w1_up', 'grad_w1_down', 'grad_g_mix', 'grad_w_in', 'grad_g_cq', 'grad_w_uq', 'grad_g_ckv', 'grad_w_ukv', 'grad_g_q_mla', 'grad_g_k_mla', 'grad_g_q_dil', 'grad_g_k_dil', 'grad_w_br_mla', 'grad_w_br_dil', 'grad_w_o', 'grad_g_ffn2', 'grad_w2_gate', 'grad_w2_up', 'grad_w2_down', 'grad_g_ple', 'grad_w_ple_gate', 'grad_w_ple_proj', 'delta_g_ffn1', 'delta_w1_gate', 'delta_w1_up', 'delta_w1_down', 'delta_g_mix', 'delta_w_in', 'delta_g_cq', 'delta_w_uq', 'delta_g_ckv', 'delta_w_ukv', 'delta_g_q_mla', 'delta_g_k_mla', 'delta_g_q_dil', 'delta_g_k_dil', 'delta_w_br_mla', 'delta_w_br_dil', 'delta_w_o', 'delta_g_ffn2', 'delta_w2_gate', 'delta_w2_up', 'delta_w2_down', 'delta_g_ple', 'delta_w_ple_gate', 'delta_w_ple_proj', 'new_m_g_ffn1', 'new_m_w1_gate', 'new_m_w1_up', 'new_m_w1_down', 'new_m_g_mix', 'new_m_w_in', 'new_m_g_cq', 'new_m_w_uq', 'new_m_g_ckv', 'new_m_w_ukv', 'new_m_g_q_mla', 'new_m_g_k_mla', 'new_m_g_q_dil', 'new_m_g_k_dil', 'new_m_w_br_mla', 'new_m_w_br_dil', 'new_m_w_o', 'new_m_g_ffn2', 'new_m_w2_gate', 'new_m_w2_up', 'new_m_w2_down', 'new_m_g_ple', 'new_m_w_ple_gate', 'new_m_w_ple_proj', 'new_v_g_ffn1', 'new_v_w1_gate', 'new_v_w1_up', 'new_v_w1_down', 'new_v_g_mix', 'new_v_w_in', 'new_v_g_cq', 'new_v_w_uq', 'new_v_g_ckv', 'new_v_w_ukv', 'new_v_g_q_mla', 'new_v_g_k_mla', 'new_v_g_q_dil', 'new_v_g_k_dil', 'new_v_w_br_mla', 'new_v_w_br_dil', 'new_v_w_o', 'new_v_g_ffn2', 'new_v_w2_gate', 'new_v_w2_up', 'new_v_w2_down', 'new_v_g_ple', 'new_v_w_ple_gate', 'new_v_w_ple_proj']
TWIN_LEAF_KINDS = {'loss': 'loss', 'grad_x': 'grad_x', 'grad_g_ffn1': 'grad_w', 'grad_w1_gate': 'grad_w', 'grad_w1_up': 'grad_w', 'grad_w1_down': 'grad_w', 'grad_g_mix': 'grad_w', 'grad_w_in': 'grad_w', 'grad_g_cq': 'grad_w', 'grad_w_uq': 'grad_w', 'grad_g_ckv': 'grad_w', 'grad_w_ukv': 'grad_w', 'grad_g_q_mla': 'grad_w', 'grad_g_k_mla': 'grad_w', 'grad_g_q_dil': 'grad_w', 'grad_g_k_dil': 'grad_w', 'grad_w_br_mla': 'grad_w', 'grad_w_br_dil': 'grad_w', 'grad_w_o': 'grad_w', 'grad_g_ffn2': 'grad_w', 'grad_w2_gate': 'grad_w', 'grad_w2_up': 'grad_w', 'grad_w2_down': 'grad_w', 'grad_g_ple': 'grad_w', 'grad_w_ple_gate': 'grad_w', 'grad_w_ple_proj': 'grad_w', 'delta_g_ffn1': 'delta_w', 'delta_w1_gate': 'delta_w', 'delta_w1_up': 'delta_w', 'delta_w1_down': 'delta_w', 'delta_g_mix': 'delta_w', 'delta_w_in': 'delta_w', 'delta_g_cq': 'delta_w', 'delta_w_uq': 'delta_w', 'delta_g_ckv': 'delta_w', 'delta_w_ukv': 'delta_w', 'delta_g_q_mla': 'delta_w', 'delta_g_k_mla': 'delta_w', 'delta_g_q_dil': 'delta_w', 'delta_g_k_dil': 'delta_w', 'delta_w_br_mla': 'delta_w', 'delta_w_br_dil': 'delta_w', 'delta_w_o': 'delta_w', 'delta_g_ffn2': 'delta_w', 'delta_w2_gate': 'delta_w', 'delta_w2_up': 'delta_w', 'delta_w2_down': 'delta_w', 'delta_g_ple': 'delta_w', 'delta_w_ple_gate': 'delta_w', 'delta_w_ple_proj': 'delta_w', 'new_m_g_ffn1': 'new_m', 'new_m_w1_gate': 'new_m', 'new_m_w1_up': 'new_m', 'new_m_w1_down': 'new_m', 'new_m_g_mix': 'new_m', 'new_m_w_in': 'new_m', 'new_m_g_cq': 'new_m', 'new_m_w_uq': 'new_m', 'new_m_g_ckv': 'new_m', 'new_m_w_ukv': 'new_m', 'new_m_g_q_mla': 'new_m', 'new_m_g_k_mla': 'new_m', 'new_m_g_q_dil': 'new_m', 'new_m_g_k_dil': 'new_m', 'new_m_w_br_mla': 'new_m', 'new_m_w_br_dil': 'new_m', 'new_m_w_o': 'new_m', 'new_m_g_ffn2': 'new_m', 'new_m_w2_gate': 'new_m', 'new_m_w2_up': 'new_m', 'new_m_w2_down': 'new_m', 'new_m_g_ple': 'new_m', 'new_m_w_ple_gate': 'new_m', 'new_m_w_ple_proj': 'new_m', 'new_v_g_ffn1': 'new_v', 'new_v_w1_gate': 'new_v', 'new_v_w1_up': 'new_v', 'new_v_w1_down': 'new_v', 'new_v_g_mix': 'new_v', 'new_v_w_in': 'new_v', 'new_v_g_cq': 'new_v', 'new_v_w_uq': 'new_v', 'new_v_g_ckv': 'new_v', 'new_v_w_ukv': 'new_v', 'new_v_g_q_mla': 'new_v', 'new_v_g_k_mla': 'new_v', 'new_v_g_q_dil': 'new_v', 'new_v_g_k_dil': 'new_v', 'new_v_w_br_mla': 'new_v', 'new_v_w_br_dil': 'new_v', 'new_v_w_o': 'new_v', 'new_v_g_ffn2': 'new_v', 'new_v_w2_gate': 'new_v', 'new_v_w2_up': 'new_v', 'new_v_w2_down': 'new_v', 'new_v_g_ple': 'new_v', 'new_v_w_ple_gate': 'new_v', 'new_v_w_ple_proj': 'new_v'}


def _forward(args):
    return _fwd_reference(*[args[k] for k in FWD_PARAMS])


def _output_shape():
    def fwd():
        inp = _fwd_setup_inputs(0)
        return _fwd_reference(*[inp[k] for k in FWD_PARAMS])
    out = _jax.eval_shape(fwd)
    return out.shape, out.dtype

N_MICROBATCH = 1
ADAM_LR = 0.001
ADAM_B1 = 0.9
ADAM_B2 = 0.999
ADAM_EPS = 1e-08
ADAM_WD = 0.01
ADAM_STEP = 10
PER_EXAMPLE_BATCH_AXIS = {'x': 0, 'p': 1, 'positions': 0, 'loss_target': 0}
SHARED_INPUTS = []
_WEIGHT_DTYPES = {'g_ffn1': _jnp.float32, 'w1_gate': _jnp.float32, 'w1_up': _jnp.float32, 'w1_down': _jnp.float32, 'g_mix': _jnp.float32, 'w_in': _jnp.float32, 'g_cq': _jnp.float32, 'w_uq': _jnp.float32, 'g_ckv': _jnp.float32, 'w_ukv': _jnp.float32, 'g_q_mla': _jnp.float32, 'g_k_mla': _jnp.float32, 'g_q_dil': _jnp.float32, 'g_k_dil': _jnp.float32, 'w_br_mla': _jnp.float32, 'w_br_dil': _jnp.float32, 'w_o': _jnp.float32, 'g_ffn2': _jnp.float32, 'w2_gate': _jnp.float32, 'w2_up': _jnp.float32, 'w2_down': _jnp.float32, 'g_ple': _jnp.float32, 'w_ple_gate': _jnp.float32, 'w_ple_proj': _jnp.float32}
MOMENT_SCALE = {'g_ffn1': 6.236347e+00, 'w1_gate': 4.487007e-02, 'w1_up': 4.904969e-02, 'w1_down': 7.924017e-02, 'g_mix': 1.012059e-01, 'w_in': 2.615946e-02, 'g_cq': 5.329777e-02, 'w_uq': 3.088621e-02, 'g_ckv': 6.915485e-01, 'w_ukv': 4.158327e-02, 'g_q_mla': 5.346654e-01, 'g_k_mla': 5.367112e-01, 'g_q_dil': 2.767777e-01, 'g_k_dil': 2.763110e-01, 'w_br_mla': 3.173386e-02, 'w_br_dil': 2.532163e-02, 'w_o': 3.886333e-02, 'g_ffn2': 6.354335e+00, 'w2_gate': 4.335824e-02, 'w2_up': 4.899573e-02, 'w2_down': 7.893243e-02, 'g_ple': 9.903115e-01, 'w_ple_gate': 5.085941e-02, 'w_ple_proj': 4.571378e-01}


def _to_microbatches(a, axis):
    t = _jnp.moveaxis(a, axis, 0)
    t = t.reshape((N_MICROBATCH, t.shape[0] // N_MICROBATCH) + t.shape[1:])
    return _jnp.moveaxis(t, 1, axis + 1)


def setup_inputs(seed: int = 0) -> dict:
    inp = _fwd_setup_inputs(seed)
    key = _jax.random.fold_in(_jax.random.key(seed), 7919)
    shape, _ = _output_shape()
    out = dict(inp)
    out["loss_target"] = _jax.random.normal(_jax.random.fold_in(key, 0), shape, _jnp.float32)
    for i, name in enumerate(TWIN_WEIGHTS):
        w = inp[name].astype(_jnp.float32)
        if MOMENT_SCALE is None:
            s = _jnp.sqrt(_jnp.mean(_jnp.square(w)) + 1e-30)
        else:
            s = MOMENT_SCALE[name]
        km, kv = _jax.random.split(_jax.random.fold_in(key, i + 1))
        out[name] = w
        out["m_" + name] = s * _jax.random.normal(km, w.shape, _jnp.float32)
        out["v_" + name] = (s * s) * _jax.random.uniform(kv, w.shape, _jnp.float32, 0.5, 1.5)
    if N_MICROBATCH > 1:
        for name, axis in PER_EXAMPLE_BATCH_AXIS.items():
            out[name] = _to_microbatches(out[name], axis)
    return {'x': out['x'], 'p': out['p'], 'positions': out['positions'], 'g_ffn1': out['g_ffn1'], 'w1_gate': out['w1_gate'], 'w1_up': out['w1_up'], 'w1_down': out['w1_down'], 'g_mix': out['g_mix'], 'w_in': out['w_in'], 'g_cq': out['g_cq'], 'w_uq': out['w_uq'], 'g_ckv': out['g_ckv'], 'w_ukv': out['w_ukv'], 'g_q_mla': out['g_q_mla'], 'g_k_mla': out['g_k_mla'], 'g_q_dil': out['g_q_dil'], 'g_k_dil': out['g_k_dil'], 'w_br_mla': out['w_br_mla'], 'w_br_dil': out['w_br_dil'], 'w_o': out['w_o'], 'g_ffn2': out['g_ffn2'], 'w2_gate': out['w2_gate'], 'w2_up': out['w2_up'], 'w2_down': out['w2_down'], 'g_ple': out['g_ple'], 'w_ple_gate': out['w_ple_gate'], 'w_ple_proj': out['w_ple_proj'], 'loss_target': out['loss_target'], 'm_g_ffn1': out['m_g_ffn1'], 'm_w1_gate': out['m_w1_gate'], 'm_w1_up': out['m_w1_up'], 'm_w1_down': out['m_w1_down'], 'm_g_mix': out['m_g_mix'], 'm_w_in': out['m_w_in'], 'm_g_cq': out['m_g_cq'], 'm_w_uq': out['m_w_uq'], 'm_g_ckv': out['m_g_ckv'], 'm_w_ukv': out['m_w_ukv'], 'm_g_q_mla': out['m_g_q_mla'], 'm_g_k_mla': out['m_g_k_mla'], 'm_g_q_dil': out['m_g_q_dil'], 'm_g_k_dil': out['m_g_k_dil'], 'm_w_br_mla': out['m_w_br_mla'], 'm_w_br_dil': out['m_w_br_dil'], 'm_w_o': out['m_w_o'], 'm_g_ffn2': out['m_g_ffn2'], 'm_w2_gate': out['m_w2_gate'], 'm_w2_up': out['m_w2_up'], 'm_w2_down': out['m_w2_down'], 'm_g_ple': out['m_g_ple'], 'm_w_ple_gate': out['m_w_ple_gate'], 'm_w_ple_proj': out['m_w_ple_proj'], 'v_g_ffn1': out['v_g_ffn1'], 'v_w1_gate': out['v_w1_gate'], 'v_w1_up': out['v_w1_up'], 'v_w1_down': out['v_w1_down'], 'v_g_mix': out['v_g_mix'], 'v_w_in': out['v_w_in'], 'v_g_cq': out['v_g_cq'], 'v_w_uq': out['v_w_uq'], 'v_g_ckv': out['v_g_ckv'], 'v_w_ukv': out['v_w_ukv'], 'v_g_q_mla': out['v_g_q_mla'], 'v_g_k_mla': out['v_g_k_mla'], 'v_g_q_dil': out['v_g_q_dil'], 'v_g_k_dil': out['v_g_k_dil'], 'v_w_br_mla': out['v_w_br_mla'], 'v_w_br_dil': out['v_w_br_dil'], 'v_w_o': out['v_w_o'], 'v_g_ffn2': out['v_g_ffn2'], 'v_w2_gate': out['v_w2_gate'], 'v_w2_up': out['v_w2_up'], 'v_w2_down': out['v_w2_down'], 'v_g_ple': out['v_g_ple'], 'v_w_ple_gate': out['v_w_ple_gate'], 'v_w_ple_proj': out['v_w_ple_proj']}


def _loss(weights, diff, rest, loss_target):
    with _jax.named_scope("forward"):
        args = {**rest, TWIN_DIFF_INPUT: diff, **{k: w.astype(_WEIGHT_DTYPES[k]) for k, w in weights.items()}}
        y = _forward(args)
    with _jax.named_scope("loss_head"):
        err = _jnp.square(y.astype(_jnp.float32) - loss_target)
        return 0.5 * _jnp.sum(_jnp.mean(err, axis=-1)) if err.ndim else 0.5 * err


def _adamw(w, g, m, v):
    m = ADAM_B1 * m + (1.0 - ADAM_B1) * g
    v = ADAM_B2 * v + (1.0 - ADAM_B2) * _jnp.square(g)
    m_hat = m / (1.0 - ADAM_B1 ** ADAM_STEP)
    v_hat = v / (1.0 - ADAM_B2 ** ADAM_STEP)
    delta = -ADAM_LR * (m_hat / (_jnp.sqrt(v_hat) + ADAM_EPS) + ADAM_WD * w)
    return delta, m, v


def reference(x, p, positions, g_ffn1, w1_gate, w1_up, w1_down, g_mix, w_in, g_cq, w_uq, g_ckv, w_ukv, g_q_mla, g_k_mla, g_q_dil, g_k_dil, w_br_mla, w_br_dil, w_o, g_ffn2, w2_gate, w2_up, w2_down, g_ple, w_ple_gate, w_ple_proj, loss_target, m_g_ffn1, m_w1_gate, m_w1_up, m_w1_down, m_g_mix, m_w_in, m_g_cq, m_w_uq, m_g_ckv, m_w_ukv, m_g_q_mla, m_g_k_mla, m_g_q_dil, m_g_k_dil, m_w_br_mla, m_w_br_dil, m_w_o, m_g_ffn2, m_w2_gate, m_w2_up, m_w2_down, m_g_ple, m_w_ple_gate, m_w_ple_proj, v_g_ffn1, v_w1_gate, v_w1_up, v_w1_down, v_g_mix, v_w_in, v_g_cq, v_w_uq, v_g_ckv, v_w_ukv, v_g_q_mla, v_g_k_mla, v_g_q_dil, v_g_k_dil, v_w_br_mla, v_w_br_dil, v_w_o, v_g_ffn2, v_w2_gate, v_w2_up, v_w2_down, v_g_ple, v_w_ple_gate, v_w_ple_proj):
    given = dict(x=x, p=p, positions=positions, g_ffn1=g_ffn1, w1_gate=w1_gate, w1_up=w1_up, w1_down=w1_down, g_mix=g_mix, w_in=w_in, g_cq=g_cq, w_uq=w_uq, g_ckv=g_ckv, w_ukv=w_ukv, g_q_mla=g_q_mla, g_k_mla=g_k_mla, g_q_dil=g_q_dil, g_k_dil=g_k_dil, w_br_mla=w_br_mla, w_br_dil=w_br_dil, w_o=w_o, g_ffn2=g_ffn2, w2_gate=w2_gate, w2_up=w2_up, w2_down=w2_down, g_ple=g_ple, w_ple_gate=w_ple_gate, w_ple_proj=w_ple_proj, loss_target=loss_target, m_g_ffn1=m_g_ffn1, m_w1_gate=m_w1_gate, m_w1_up=m_w1_up, m_w1_down=m_w1_down, m_g_mix=m_g_mix, m_w_in=m_w_in, m_g_cq=m_g_cq, m_w_uq=m_w_uq, m_g_ckv=m_g_ckv, m_w_ukv=m_w_ukv, m_g_q_mla=m_g_q_mla, m_g_k_mla=m_g_k_mla, m_g_q_dil=m_g_q_dil, m_g_k_dil=m_g_k_dil, m_w_br_mla=m_w_br_mla, m_w_br_dil=m_w_br_dil, m_w_o=m_w_o, m_g_ffn2=m_g_ffn2, m_w2_gate=m_w2_gate, m_w2_up=m_w2_up, m_w2_down=m_w2_down, m_g_ple=m_g_ple, m_w_ple_gate=m_w_ple_gate, m_w_ple_proj=m_w_ple_proj, v_g_ffn1=v_g_ffn1, v_w1_gate=v_w1_gate, v_w1_up=v_w1_up, v_w1_down=v_w1_down, v_g_mix=v_g_mix, v_w_in=v_w_in, v_g_cq=v_g_cq, v_w_uq=v_w_uq, v_g_ckv=v_g_ckv, v_w_ukv=v_w_ukv, v_g_q_mla=v_g_q_mla, v_g_k_mla=v_g_k_mla, v_g_q_dil=v_g_q_dil, v_g_k_dil=v_g_k_dil, v_w_br_mla=v_w_br_mla, v_w_br_dil=v_w_br_dil, v_w_o=v_w_o, v_g_ffn2=v_g_ffn2, v_w2_gate=v_w2_gate, v_w2_up=v_w2_up, v_w2_down=v_w2_down, v_g_ple=v_g_ple, v_w_ple_gate=v_w_ple_gate, v_w_ple_proj=v_w_ple_proj)
    weights = {n: given[n] for n in TWIN_WEIGHTS}
    shared = {n: given[n] for n in SHARED_INPUTS}
    per_example = {n: given[n] for n in ['x', 'p', 'positions']}
    grad_fn = _jax.value_and_grad(_loss, argnums=(0, 1))

    def one_microbatch(ex, loss_target):
        ex = dict(ex)
        diff = ex.pop(TWIN_DIFF_INPUT)
        return grad_fn(weights, diff, {**shared, **ex}, loss_target)

    if N_MICROBATCH == 1:
        loss, (grad_w, grad_x) = one_microbatch(per_example, given["loss_target"])
    else:
        def body(carry, xs):
            loss_sum, grad_sum = carry
            l_k, (gw_k, gx_k) = one_microbatch(xs[0], xs[1])
            with _jax.named_scope("update"):
                return (loss_sum + l_k, _jax.tree.map(_jnp.add, grad_sum, gw_k)), gx_k

        init = (_jnp.zeros((), _jnp.float32), _jax.tree.map(_jnp.zeros_like, weights))
        (loss, grad_w), grad_x = _jax.lax.scan(body, init, (per_example, given["loss_target"]))
    with _jax.named_scope("update"):
        delta_w, new_m, new_v = {}, {}, {}
        for n in TWIN_WEIGHTS:
            delta_w[n], new_m[n], new_v[n] = _adamw(weights[n], grad_w[n], given["m_" + n], given["v_" + n])
    return (loss, grad_x, *[grad_w[n] for n in TWIN_WEIGHTS], *[delta_w[n] for n in TWIN_WEIGHTS],
            *[new_m[n] for n in TWIN_WEIGHTS], *[new_v[n] for n in TWIN_WEIGHTS])
```

```python
import numpy as np
import jax
import jax.numpy as jnp
from jax import lax
from jax.experimental import pallas as pl
from jax.experimental.pallas import tpu as pltpu

F32 = jnp.float32
BF16 = jnp.bfloat16

EPS = 1e-6
ROPE_THETA = 500000.0
MLA_NOPE = 128
MLA_ROPE = 64
MLA_QK = MLA_NOPE + MLA_ROPE
MLA_PAD = 256
HEAD = 128
DIL_ROT = 32
DIL_GROUPS = ((128, 1), (512, 4), (2048, 16))
NEG = -1e30
NO_WINDOW = 1 << 30
N_DEV = 8
ADAM_LR, ADAM_B1, ADAM_B2, ADAM_EPS, ADAM_WD, ADAM_STEP = 0.001, 0.9, 0.999, 1e-08, 0.01, 10
VMEM_LIMIT_V7X = 56 * 1024 * 1024
MESH = pl.DeviceIdType.MESH
HBM_SPEC = pl.BlockSpec(memory_space=pltpu.HBM)


def _cp(n_axes):
    return pltpu.CompilerParams(dimension_semantics=("arbitrary",) * n_axes,
                                vmem_limit_bytes=VMEM_LIMIT_V7X)


def _tile(n, t):
    return t if (n >= t and n % t == 0) else n


def _sigmoid(x):
    return 1.0 / (1.0 + jnp.exp(-x))


_DIMS = {"nn": (((1,), (0,)), ((), ())), "nt": (((1,), (1,)), ((), ())), "tn": (((0,), (0,)), ((), ()))}


def _dot(a, b, mode):
    return lax.dot_general(a.astype(BF16), b.astype(BF16), _DIMS[mode], preferred_element_type=F32)


def _mm(name, grid, ins, in_specs, pairs, n_acc, acc_shape, epilogue, out_shapes, out_specs, k_axis):
    n_in, n_out = len(ins), len(out_shapes)
    nk = grid[k_axis] if k_axis is not None else 1

    def body(*refs):
        in_refs, out_refs, acc_refs = refs[:n_in], refs[n_in:n_in + n_out], refs[n_in + n_out:]
        parts = [None] * n_acc
        for ai, bi, mode, ci in pairs:
            d = _dot(in_refs[ai][...], in_refs[bi][...], mode)
            parts[ci] = d if parts[ci] is None else parts[ci] + d
        if nk == 1:
            epilogue(parts, in_refs, out_refs)
            return
        k = pl.program_id(k_axis)

        @pl.when(k == 0)
        def _():
            for c in range(n_acc):
                acc_refs[c][...] = parts[c]

        @pl.when(k > 0)
        def _():
            for c in range(n_acc):
                acc_refs[c][...] += parts[c]

        @pl.when(k == nk - 1)
        def _():
            epilogue([r[...] for r in acc_refs], in_refs, out_refs)

    scratch = [pltpu.VMEM(acc_shape, F32) for _ in range(n_acc)] if nk > 1 else []
    return pl.pallas_call(body, name=name, grid=grid, in_specs=in_specs, out_specs=out_specs,
                          out_shape=out_shapes, scratch_shapes=scratch, compiler_params=_cp(len(grid)))(*ins)


def _mm2d(name, a, b, mode, tm, tn, tk, out_dtypes=(F32,), epilogue=None, extras=()):
    if mode == "nn":
        (M, K), N = a.shape, b.shape[1]
    elif mode == "nt":
        (M, K), N = a.shape, b.shape[0]
    else:
        (K, M), N = a.shape, b.shape[1]
    tm, tn, tk = _tile(M, tm), _tile(N, tn), _tile(K, tk)
    a_spec = pl.BlockSpec((tk, tm), lambda i, j, k: (k, i)) if mode == "tn" else pl.BlockSpec((tm, tk), lambda i, j, k: (i, k))
    b_spec = pl.BlockSpec((tn, tk), lambda i, j, k: (j, k)) if mode == "nt" else pl.BlockSpec((tk, tn), lambda i, j, k: (k, j))
    mn_spec = pl.BlockSpec((tm, tn), lambda i, j, k: (i, j))
    n_ex = len(extras)

    def default_epilogue(acc, ex_refs, out_refs):
        out_refs[0][...] = acc.astype(out_refs[0].dtype)

    epi = epilogue or default_epilogue

    def wrapped(accs, in_refs, out_refs):
        epi(accs[0], in_refs[2:2 + n_ex], out_refs)

    outs = _mm(name, (M // tm, N // tn, K // tk), [a, b, *extras], [a_spec, b_spec] + [mn_spec] * n_ex,
               [(0, 1, mode, 0)], 1, (tm, tn), wrapped,
               [jax.ShapeDtypeStruct((M, N), dt) for dt in out_dtypes], [mn_spec] * len(out_dtypes), 2)
    return outs[0] if len(out_dtypes) == 1 else outs


def _rms_stats(x, n):
    return lax.rsqrt(jnp.sum(x * x, axis=-1, keepdims=True) * (1.0 / n) + EPS)


def _rmsnorm_fwd(name, x, g, tm=512):
    T, D = x.shape
    tm = _tile(T, tm)

    def body(x_ref, g_ref, o_ref):
        xv = x_ref[...]
        o_ref[...] = (xv * _rms_stats(xv, D) * g_ref[...]).astype(BF16)

    return pl.pallas_call(body, name=name, grid=(T // tm,),
                          in_specs=[pl.BlockSpec((tm, D), lambda i: (i, 0)), pl.BlockSpec((1, D), lambda i: (0, 0))],
                          out_specs=pl.BlockSpec((tm, D), lambda i: (i, 0)),
                          out_shape=jax.ShapeDtypeStruct((T, D), BF16), compiler_params=_cp(1))(x, g)


def _rms_bwd_rows(dy, x, g, n):
    r = _rms_stats(x, n)
    xh = x * r
    gd = dy * g
    mean = jnp.sum(gd * xh, axis=-1, keepdims=True) * (1.0 / n)
    return r * (gd - xh * mean), dy * xh


def _rmsnorm_bwd(name, dn, x, g, res, tm=256):
    T, D = x.shape
    tm = _tile(T, tm)

    def body(dn_ref, x_ref, g_ref, res_ref, dx_ref, dxb_ref, dg_ref):
        dx, dgp = _rms_bwd_rows(dn_ref[...].astype(F32), x_ref[...], g_ref[...], D)
        dx = dx + res_ref[...]
        dx_ref[...] = dx
        dxb_ref[...] = dx.astype(BF16)

        @pl.when(pl.program_id(0) == 0)
        def _():
            dg_ref[...] = jnp.zeros_like(dg_ref)

        dg_ref[...] += jnp.sum(dgp, axis=0, keepdims=True)

    row = pl.BlockSpec((tm, D), lambda i: (i, 0))
    vec = pl.BlockSpec((1, D), lambda i: (0, 0))
    return pl.pallas_call(body, name=name, grid=(T // tm,), in_specs=[row, row, vec, row],
                          out_specs=[row, row, vec],
                          out_shape=[jax.ShapeDtypeStruct((T, D), F32), jax.ShapeDtypeStruct((T, D), BF16),
                                     jax.ShapeDtypeStruct((1, D), F32)],
                          compiler_params=_cp(1))(dn, x, g, res)


def _rope_tables(pos_col, inv_mla, inv_dil, tm=512):
    T = pos_col.shape[0]
    tm = _tile(T, tm)

    def body(p_ref, im_ref, id_ref, cm, sam, sbm, cd, sad, sbd):
        lane = lax.broadcasted_iota(jnp.int32, (tm, 128), 1)
        p = p_ref[...]
        am = p * im_ref[...]
        c, s = jnp.cos(am), jnp.sin(am)
        cm[...] = jnp.where(lane < 64, c, 0.0)
        sam[...] = jnp.where(lane < 32, -s, 0.0)
        sbm[...] = jnp.where((lane >= 32) & (lane < 64), s, 0.0)
        ad = p * id_ref[...]
        c, s = jnp.cos(ad), jnp.sin(ad)
        cd[...] = jnp.where(lane < 32, c, 1.0)
        sad[...] = jnp.where(lane < 16, -s, 0.0)
        sbd[...] = jnp.where((lane >= 16) & (lane < 32), s, 0.0)

    row = pl.BlockSpec((tm, 128), lambda i: (i, 0))
    vec = pl.BlockSpec((1, 128), lambda i: (0, 0))
    return pl.pallas_call(body, name="rope_tables", grid=(T // tm,),
                          in_specs=[pl.BlockSpec((tm, 1), lambda i: (i, 0)), vec, vec], out_specs=[row] * 6,
                          out_shape=[jax.ShapeDtypeStruct((T, 128), F32)] * 6, compiler_params=_cp(1))(pos_col, inv_mla, inv_dil)


def _rope(v, c, sa, sb, sh):
    return v * c + pltpu.roll(v, 128 - sh, 1) * sa + pltpu.roll(v, sh, 1) * sb


def _rope_t(d, c, sa, sb, sh):
    return d * c + pltpu.roll(d * sa, sh, 1) + pltpu.roll(d * sb, 128 - sh, 1)


def _lora_fwd(pa, g_cq, g_ckv, nq, nkv, tm=512):
    T, W = pa.shape
    tm = _tile(T, tm)

    def body(pa_ref, gq_ref, gk_ref, cq_ref, ckv_ref):
        a = pa_ref[:, :nq]
        cq_ref[...] = (a * _rms_stats(a, nq) * gq_ref[...]).astype(BF16)
        b = pa_ref[:, nq:nq + nkv]
        ckv_ref[...] = (b * _rms_stats(b, nkv) * gk_ref[...]).astype(BF16)

    return pl.pallas_call(body, name="lora_fwd", grid=(T // tm,),
                          in_specs=[pl.BlockSpec((tm, W), lambda i: (i, 0)), pl.BlockSpec((1, nq), lambda i: (0, 0)),
                                    pl.BlockSpec((1, nkv), lambda i: (0, 0))],
                          out_specs=[pl.BlockSpec((tm, nq), lambda i: (i, 0)), pl.BlockSpec((tm, nkv), lambda i: (i, 0))],
                          out_shape=[jax.ShapeDtypeStruct((T, nq), BF16), jax.ShapeDtypeStruct((T, nkv), BF16)],
                          compiler_params=_cp(1))(pa, g_cq, g_ckv)


def _lora_bwd(dcq, dckv, dkr, pa, g_cq, g_ckv, tm=512):
    T, W = pa.shape
    nq, nkv = dcq.shape[1], dckv.shape[1]
    tm = _tile(T, tm)

    def body(dcq_ref, dckv_ref, dkr_ref, pa_ref, gq_ref, gk_ref, dpa_ref, dgq_ref, dgk_ref):
        dx, dgp = _rms_bwd_rows(dcq_ref[...], pa_ref[:, :nq], gq_ref[...], nq)
        dpa_ref[:, :nq] = dx.astype(BF16)
        dx2, dgp2 = _rms_bwd_rows(dckv_ref[...], pa_ref[:, nq:nq + nkv], gk_ref[...], nkv)
        dpa_ref[:, nq:nq + nkv] = dx2.astype(BF16)
        dpa_ref[:, nq + nkv:] = dkr_ref[...].astype(BF16)

        @pl.when(pl.program_id(0) == 0)
        def _():
            dgq_ref[...] = jnp.zeros_like(dgq_ref)
            dgk_ref[...] = jnp.zeros_like(dgk_ref)

        dgq_ref[...] += jnp.sum(dgp, axis=0, keepdims=True)
        dgk_ref[...] += jnp.sum(dgp2, axis=0, keepdims=True)

    def row(n):
        return pl.BlockSpec((tm, n), lambda i: (i, 0))

    def vec(n):
        return pl.BlockSpec((1, n), lambda i: (0, 0))

    return pl.pallas_call(body, name="lora_bwd", grid=(T // tm,),
                          in_specs=[row(nq), row(nkv), row(128), row(W), vec(nq), vec(nkv)],
                          out_specs=[row(W), vec(nq), vec(nkv)],
                          out_shape=[jax.ShapeDtypeStruct((T, W), BF16), jax.ShapeDtypeStruct((1, nq), F32),
                                     jax.ShapeDtypeStruct((1, nkv), F32)],
                          compiler_params=_cp(1))(dcq, dckv, dkr, pa, g_cq, g_ckv)


def _sumsq(v):
    return jnp.sum(v * v, axis=-1, keepdims=True)


def _mla_prep_fwd(q_raw, kv, pa, tabs, gq, gk, H, kr_block, tm=256):
    T = q_raw.shape[0]
    tm = _tile(T, tm)
    scale = MLA_QK ** -0.5
    P = MLA_PAD

    def body(q_ref, kv_ref, kr_ref, c_ref, sa_ref, sb_ref, gq_ref, gk_ref, qo, ko, vo):
        c, sa, sb = c_ref[...], sa_ref[...], sb_ref[...]
        kr = kr_ref[...]
        kr2 = _sumsq(kr)
        for h in range(H):
            lo, hi = q_ref[:, h * P:h * P + 128], q_ref[:, h * P + 128:(h + 1) * P]
            r = lax.rsqrt((_sumsq(lo) + _sumsq(hi)) * (1.0 / MLA_QK) + EPS)
            qo[:, h * P:h * P + 128] = (lo * r * gq_ref[:, :128] * scale).astype(BF16)
            qo[:, h * P + 128:(h + 1) * P] = (_rope(hi * r * gq_ref[:, 128:], c, sa, sb, 32) * scale).astype(BF16)
            kn = kv_ref[:, h * P:h * P + 128]
            r = lax.rsqrt((_sumsq(kn) + kr2) * (1.0 / MLA_QK) + EPS)
            ko[:, h * P:h * P + 128] = (kn * r * gk_ref[:, :128]).astype(BF16)
            ko[:, h * P + 128:(h + 1) * P] = _rope(kr * r * gk_ref[:, 128:], c, sa, sb, 32).astype(BF16)
            vo[:, h * 128:(h + 1) * 128] = kv_ref[:, h * P + 128:(h + 1) * P].astype(BF16)

    wide = pl.BlockSpec((tm, H * P), lambda i: (i, 0))
    lane = pl.BlockSpec((tm, 128), lambda i: (i, 0))
    vec = pl.BlockSpec((1, P), lambda i: (0, 0))
    return pl.pallas_call(body, name="mla_prep_fwd", grid=(T // tm,),
                          in_specs=[wide, wide, pl.BlockSpec((tm, 128), lambda i: (i, kr_block)), lane, lane, lane, vec, vec],
                          out_specs=[wide, wide, pl.BlockSpec((tm, H * 128), lambda i: (i, 0))],
                          out_shape=[jax.ShapeDtypeStruct((T, H * P), BF16), jax.ShapeDtypeStruct((T, H * P), BF16),
                                     jax.ShapeDtypeStruct((T, H * 128), BF16)],
                          compiler_params=_cp(1))(q_raw, kv, pa, *tabs, gq, gk)


def _mla_prep_bwd(dq, dk, dv, q_raw, kv, pa, tabs, gq, gk, H, kr_block, tm=256):
    T = q_raw.shape[0]
    tm = _tile(T, tm)
    scale = MLA_QK ** -0.5
    P = MLA_PAD

    def body(dq_ref, dk_ref, dv_ref, q_ref, kv_ref, kr_ref, c_ref, sa_ref, sb_ref, gq_ref, gk_ref,
             dqr, dkv, dkr, dgq, dgk):
        c, sa, sb = c_ref[...], sa_ref[...], sb_ref[...]
        kr = kr_ref[...]
        kr2 = _sumsq(kr)
        gql, gqh, gkl, gkh = gq_ref[:, :128], gq_ref[:, 128:], gk_ref[:, :128], gk_ref[:, 128:]
        dkr_acc = jnp.zeros((tm, 128), F32)
        sums = [jnp.zeros((1, 128), F32) for _ in range(4)]
        for h in range(H):
            lo_s, hi_s = slice(h * P, h * P + 128), slice(h * P + 128, (h + 1) * P)
            lo, hi = q_ref[:, lo_s], q_ref[:, hi_s]
            r = lax.rsqrt((_sumsq(lo) + _sumsq(hi)) * (1.0 / MLA_QK) + EPS)
            ql, qh = lo * r, hi * r
            dyl = dq_ref[:, lo_s] * scale
            dyh = _rope_t(dq_ref[:, hi_s] * scale, c, sa, sb, 32)
            gl, gh = dyl * gql, dyh * gqh
            mean = (jnp.sum(gl * ql, axis=-1, keepdims=True) + jnp.sum(gh * qh, axis=-1, keepdims=True)) * (1.0 / MLA_QK)
            dqr[:, lo_s] = (r * (gl - ql * mean)).astype(BF16)
            dqr[:, hi_s] = (r * (gh - qh * mean)).astype(BF16)
            sums[0] += jnp.sum(dyl * ql, axis=0, keepdims=True)
            sums[1] += jnp.sum(dyh * qh, axis=0, keepdims=True)
            kn = kv_ref[:, lo_s]
            r = lax.rsqrt((_sumsq(kn) + kr2) * (1.0 / MLA_QK) + EPS)
            kl, kh = kn * r, kr * r
            dkl = dk_ref[:, lo_s]
            dkh = _rope_t(dk_ref[:, hi_s], c, sa, sb, 32)
            gl, gh = dkl * gkl, dkh * gkh
            mean = (jnp.sum(gl * kl, axis=-1, keepdims=True) + jnp.sum(gh * kh, axis=-1, keepdims=True)) * (1.0 / MLA_QK)
            dkv[:, lo_s] = (r * (gl - kl * mean)).astype(BF16)
            dkr_acc += r * (gh - kh * mean)
            dkv[:, hi_s] = dv_ref[:, h * 128:(h + 1) * 128].astype(BF16)
            sums[2] += jnp.sum(dkl * kl, axis=0, keepdims=True)
            sums[3] += jnp.sum(dkh * kh, axis=0, keepdims=True)
        dkr[...] = dkr_acc

        @pl.when(pl.program_id(0) == 0)
        def _():
            dgq[...] = jnp.zeros_like(dgq)
            dgk[...] = jnp.zeros_like(dgk)

        dgq[:, :128] += sums[0]
        dgq[:, 128:] += sums[1]
        dgk[:, :128] += sums[2]
        dgk[:, 128:] += sums[3]

    wide = pl.BlockSpec((tm, H * P), lambda i: (i, 0))
    lane = pl.BlockSpec((tm, 128), lambda i: (i, 0))
    vec = pl.BlockSpec((1, P), lambda i: (0, 0))
    return pl.pallas_call(body, name="mla_prep_bwd", grid=(T // tm,),
                          in_specs=[wide, wide, pl.BlockSpec((tm, H * 128), lambda i: (i, 0)), wide, wide,
                                    pl.BlockSpec((tm, 128), lambda i: (i, kr_block)), lane, lane, lane, vec, vec],
                          out_specs=[wide, wide, lane, vec, vec],
                          out_shape=[jax.ShapeDtypeStruct((T, H * P), BF16), jax.ShapeDtypeStruct((T, H * P), BF16),
                                     jax.ShapeDtypeStruct((T, 128), F32), jax.ShapeDtypeStruct((1, P), F32),
                                     jax.ShapeDtypeStruct((1, P), F32)],
                          compiler_params=_cp(1))(dq, dk, dv, q_raw, kv, pa, *tabs, gq, gk)


def _dil_prep_fwd(pdil, tabs, gq, gk, G, H, tm=256):
    T = pdil.shape[0]
    tm = _tile(T, tm)
    scale = HEAD ** -0.5

    def body(p_ref, c_ref, sa_ref, sb_ref, gq_ref, gk_ref, qo, ko):
        c, sa, sb = c_ref[...], sa_ref[...], sb_ref[...]
        for h in range(H):
            q = p_ref[:, h * 128:(h + 1) * 128].astype(F32)
            qo[:, h * 128:(h + 1) * 128] = (_rope(q * _rms_stats(q, HEAD) * gq_ref[...], c, sa, sb, 16) * scale).astype(BF16)
            k = p_ref[:, (H + h) * 128:(H + h + 1) * 128].astype(F32)
            ko[:, h * 128:(h + 1) * 128] = _rope(k * _rms_stats(k, HEAD) * gk_ref[...], c, sa, sb, 16).astype(BF16)

    lane = pl.BlockSpec((tm, 128), lambda g, i: (i, 0))
    gain = pl.BlockSpec((None, 1, 128), lambda g, i: (g, 0, 0))
    out = pl.BlockSpec((tm, H * 128), lambda g, i: (i, g))
    return pl.pallas_call(body, name="dil_prep_fwd", grid=(G, T // tm),
                          in_specs=[pl.BlockSpec((tm, 3 * H * 128), lambda g, i: (i, g)), lane, lane, lane, gain, gain],
                          out_specs=[out, out],
                          out_shape=[jax.ShapeDtypeStruct((T, G * H * 128), BF16)] * 2,
                          compiler_params=_cp(2))(pdil, *tabs, gq, gk)


def _dil_prep_bwd(dqd, dkd, dvd, pdil, tabs, gq, gk, G, H, tm=256):
    T = pdil.shape[0]
    tm = _tile(T, tm)
    scale = HEAD ** -0.5

    def body(dq_ref, dk_ref, dv_ref, p_ref, c_ref, sa_ref, sb_ref, gq_ref, gk_ref, dp_ref, dgq, dgk):
        c, sa, sb = c_ref[...], sa_ref[...], sb_ref[...]
        sq = jnp.zeros((1, 128), F32)
        sk = jnp.zeros((1, 128), F32)
        for h in range(H):
            hs = slice(h * 128, (h + 1) * 128)
            q = p_ref[:, hs].astype(F32)
            dy = _rope_t(dq_ref[:, hs] * scale, c, sa, sb, 16)
            dx, dgp = _rms_bwd_rows(dy, q, gq_ref[...], HEAD)
            dp_ref[:, hs] = dx.astype(BF16)
            sq += jnp.sum(dgp, axis=0, keepdims=True)
            ks = slice((H + h) * 128, (H + h + 1) * 128)
            k = p_ref[:, ks].astype(F32)
            dy = _rope_t(dk_ref[:, hs], c, sa, sb, 16)
            dx, dgp = _rms_bwd_rows(dy, k, gk_ref[...], HEAD)
            dp_ref[:, ks] = dx.astype(BF16)
            sk += jnp.sum(dgp, axis=0, keepdims=True)
            dp_ref[:, (2 * H + h) * 128:(2 * H + h + 1) * 128] = dv_ref[:, hs].astype(BF16)

        @pl.when(pl.program_id(1) == 0)
        def _():
            dgq[...] = jnp.zeros_like(dgq)
            dgk[...] = jnp.zeros_like(dgk)

        dgq[...] += sq
        dgk[...] += sk

    lane = pl.BlockSpec((tm, 128), lambda g, i: (i, 0))
    gain = pl.BlockSpec((None, 1, 128), lambda g, i: (g, 0, 0))
    grp = pl.BlockSpec((tm, H * 128), lambda g, i: (i, g))
    big = pl.BlockSpec((tm, 3 * H * 128), lambda g, i: (i, g))
    return pl.pallas_call(body, name="dil_prep_bwd", grid=(G, T // tm),
                          in_specs=[grp, grp, grp, big, lane, lane, lane, gain, gain],
                          out_specs=[big, gain, gain],
                          out_shape=[jax.ShapeDtypeStruct(pdil.shape, BF16), jax.ShapeDtypeStruct((G, 1, 128), F32),
                                     jax.ShapeDtypeStruct((G, 1, 128), F32)],
                          compiler_params=_cp(2))(dqd, dkd, dvd, pdil, *tabs, gq, gk)


def _attn_steps(nq, groups, blk):
    steps = []
    for g, (win, dil) in enumerate(groups):
        nb = nq if win is None else min(nq, -(-win // blk) + 1)
        for b in range(nb):
            steps.append((g, b, dil, NO_WINDOW if win is None else win, int(b == 0), int(b == nb - 1)))
    return [jnp.asarray(col, jnp.int32) for col in np.array(steps, np.int32).T]


def _valid(back, dil, win, blk, q_axis):
    qi = lax.broadcasted_iota(jnp.int32, (blk, blk), q_axis)
    ki = lax.broadcasted_iota(jnp.int32, (blk, blk), 1 - q_axis)
    delta = back * blk + (qi - ki)
    return (delta >= 0) & (delta <= win) & ((delta & (dil - 1)) == 0)


def _attn_fwd(name, q, k, v, H, wq, v_base, v_stride, groups, blk):
    T = q.shape[0]
    nq = T // blk
    tabs = _attn_steps(nq, groups, blk)
    ns = int(tabs[0].shape[0])

    def body(sg, sb, sd, sw, sf, sl, q_ref, k_ref, v_ref, o_ref, lse_ref, m_sc, l_sc, acc_sc):
        i, s = pl.program_id(1), pl.program_id(2)

        @pl.when(s == 0)
        def _():
            m_sc[...] = jnp.full_like(m_sc, NEG)
            l_sc[...] = jnp.zeros_like(l_sc)
            acc_sc[...] = jnp.zeros_like(acc_sc)

        back = sb[s]

        @pl.when(i - back >= 0)
        def _():
            sc = _dot(q_ref[...], k_ref[...], "nt")
            ok = _valid(back, sd[s], sw[s], blk, 0)
            sc = jnp.where(ok, sc, NEG)
            m_prev = m_sc[...]
            m_new = jnp.maximum(m_prev, jnp.max(sc, axis=-1, keepdims=True))
            p = jnp.where(ok, jnp.exp(sc - m_new), 0.0)
            alpha = jnp.exp(m_prev - m_new)
            l_sc[...] = alpha * l_sc[...] + jnp.sum(p, axis=-1, keepdims=True)
            acc_sc[...] = alpha * acc_sc[...] + _dot(p, v_ref[...], "nn")
            m_sc[...] = m_new

        @pl.when(s == ns - 1)
        def _():
            o_ref[...] = (acc_sc[...] / l_sc[...]).astype(BF16)
            lse_ref[...] = m_sc[...] + jnp.log(l_sc[...])

    def kv_row(i, s, sb):
        return jnp.maximum(i - sb[s], 0)

    grid_spec = pltpu.PrefetchScalarGridSpec(
        num_scalar_prefetch=6, grid=(H, nq, ns),
        in_specs=[pl.BlockSpec((blk, wq), lambda h, i, s, sg, sb, *_: (i, sg[s] * H + h)),
                  pl.BlockSpec((blk, wq), lambda h, i, s, sg, sb, *_: (kv_row(i, s, sb), sg[s] * H + h)),
                  pl.BlockSpec((blk, HEAD), lambda h, i, s, sg, sb, *_: (kv_row(i, s, sb), v_base + sg[s] * v_stride + h))],
        out_specs=[pl.BlockSpec((blk, HEAD), lambda h, i, s, *_: (i, h)),
                   pl.BlockSpec((None, blk, 1), lambda h, i, s, *_: (h, i, 0))],
        scratch_shapes=[pltpu.VMEM((blk, 1), F32), pltpu.VMEM((blk, 1), F32), pltpu.VMEM((blk, HEAD), F32)])
    return pl.pallas_call(body, name=name, grid_spec=grid_spec,
                          out_shape=[jax.ShapeDtypeStruct((T, H * HEAD), BF16), jax.ShapeDtypeStruct((H, T, 1), F32)],
                          compiler_params=_cp(3))(*tabs, q, k, v)


def _attn_dq(name, q, k, v, do, o, lse, H, wq, v_base, v_stride, groups, blk):
    T = q.shape[0]
    nq = T // blk
    tabs = _attn_steps(nq, groups, blk)
    ns = int(tabs[0].shape[0])

    def body(sg, sb, sd, sw, sf, sl, q_ref, k_ref, v_ref, do_ref, o_ref, lse_ref, dq_ref, dl_ref, acc_sc, dl_sc):
        i, s = pl.program_id(1), pl.program_id(2)

        @pl.when(s == 0)
        def _():
            d = jnp.sum(do_ref[...].astype(F32) * o_ref[...].astype(F32), axis=-1, keepdims=True)
            dl_sc[...] = d
            dl_ref[...] = d

        @pl.when(sf[s] == 1)
        def _():
            acc_sc[...] = jnp.zeros_like(acc_sc)

        back = sb[s]

        @pl.when(i - back >= 0)
        def _():
            sc = _dot(q_ref[...], k_ref[...], "nt")
            ok = _valid(back, sd[s], sw[s], blk, 0)
            p = jnp.where(ok, jnp.exp(sc - lse_ref[...]), 0.0)
            dp = _dot(do_ref[...], v_ref[...], "nt")
            ds = p * (dp - dl_sc[...])
            acc_sc[...] += _dot(ds, k_ref[...], "nn")

        @pl.when(sl[s] == 1)
        def _():
            dq_ref[...] = acc_sc[...]

    def kv_row(i, s, sb):
        return jnp.maximum(i - sb[s], 0)

    col = pl.BlockSpec((None, blk, 1), lambda h, i, s, *_: (h, i, 0))
    oh = pl.BlockSpec((blk, HEAD), lambda h, i, s, *_: (i, h))
    grid_spec = pltpu.PrefetchScalarGridSpec(
        num_scalar_prefetch=6, grid=(H, nq, ns),
        in_specs=[pl.BlockSpec((blk, wq), lambda h, i, s, sg, sb, *_: (i, sg[s] * H + h)),
                  pl.BlockSpec((blk, wq), lambda h, i, s, sg, sb, *_: (kv_row(i, s, sb), sg[s] * H + h)),
                  pl.BlockSpec((blk, HEAD), lambda h, i, s, sg, sb, *_: (kv_row(i, s, sb), v_base + sg[s] * v_stride + h)),
                  oh, oh, col],
        out_specs=[pl.BlockSpec((blk, wq), lambda h, i, s, sg, *_: (i, sg[s] * H + h)), col],
        scratch_shapes=[pltpu.VMEM((blk, wq), F32), pltpu.VMEM((blk, 1), F32)])
    G = len(groups)
    return pl.pallas_call(body, name=name, grid_spec=grid_spec,
                          out_shape=[jax.ShapeDtypeStruct((T, G * H * wq), F32), jax.ShapeDtypeStruct((H, T, 1), F32)],
                          compiler_params=_cp(3))(*tabs, q, k, v, do, o, lse)


def _attn_dkv(name, q, k, v, do, lse_row, dl_row, H, wq, v_base, v_stride, groups, blk):
    T = q.shape[0]
    nq = T // blk
    tabs = _attn_steps(nq, groups, blk)
    ns = int(tabs[0].shape[0])

    def body(sg, sb, sd, sw, sf, sl, q_ref, k_ref, v_ref, do_ref, lse_ref, dl_ref, dk_ref, dv_ref, dk_sc, dv_sc):
        j, s = pl.program_id(1), pl.program_id(2)

        @pl.when(sf[s] == 1)
        def _():
            dk_sc[...] = jnp.zeros_like(dk_sc)
            dv_sc[...] = jnp.zeros_like(dv_sc)

        back = sb[s]

        @pl.when(j + back < nq)
        def _():
            st = _dot(k_ref[...], q_ref[...], "nt")
            ok = _valid(back, sd[s], sw[s], blk, 1)
            pt = jnp.where(ok, jnp.exp(st - lse_ref[...]), 0.0)
            dpt = _dot(v_ref[...], do_ref[...], "nt")
            dst = pt * (dpt - dl_ref[...])
            dv_sc[...] += _dot(pt, do_ref[...], "nn")
            dk_sc[...] += _dot(dst, q_ref[...], "nn")

        @pl.when(sl[s] == 1)
        def _():
            dk_ref[...] = dk_sc[...]
            dv_ref[...] = dv_sc[...]

    def q_row(j, s, sb):
        return jnp.minimum(j + sb[s], nq - 1)

    row = pl.BlockSpec((None, 1, blk), lambda h, j, s, sg, sb, *_: (h, 0, q_row(j, s, sb)))
    grid_spec = pltpu.PrefetchScalarGridSpec(
        num_scalar_prefetch=6, grid=(H, nq, ns),
        in_specs=[pl.BlockSpec((blk, wq), lambda h, j, s, sg, sb, *_: (q_row(j, s, sb), sg[s] * H + h)),
                  pl.BlockSpec((blk, wq), lambda h, j, s, sg, *_: (j, sg[s] * H + h)),
                  pl.BlockSpec((blk, HEAD), lambda h, j, s, sg, *_: (j, v_base + sg[s] * v_stride + h)),
                  pl.BlockSpec((blk, HEAD), lambda h, j, s, sg, sb, *_: (q_row(j, s, sb), h)),
                  row, row],
        out_specs=[pl.BlockSpec((blk, wq), lambda h, j, s, sg, *_: (j, sg[s] * H + h)),
                   pl.BlockSpec((blk, HEAD), lambda h, j, s, sg, *_: (j, sg[s] * H + h))],
        scratch_shapes=[pltpu.VMEM((blk, wq), F32), pltpu.VMEM((blk, HEAD), F32)])
    G = len(groups)
    return pl.pallas_call(body, name=name, grid_spec=grid_spec,
                          out_shape=[jax.ShapeDtypeStruct((T, G * H * wq), F32), jax.ShapeDtypeStruct((T, G * H * HEAD), F32)],
                          compiler_params=_cp(3))(*tabs, q, k, v, do, lse_row, dl_row)


def _ffn_up(name, n, wg, wu, tm=512):
    T, D = n.shape
    nd, _, fc = wg.shape
    tm = _tile(T, tm)

    def epilogue(accs, in_refs, out_refs):
        a, b = accs
        out_refs[0][...] = a.astype(BF16)
        out_refs[1][...] = b.astype(BF16)
        out_refs[2][...] = (a * _sigmoid(a) * b).astype(BF16)

    w_spec = pl.BlockSpec((None, D, fc), lambda j, i: (j, 0, 0))
    o_spec = pl.BlockSpec((None, tm, fc), lambda j, i: (j, i, 0))
    sh = jax.ShapeDtypeStruct((nd, T, fc), BF16)
    return _mm(name, (nd, T // tm), [n, wg, wu], [pl.BlockSpec((tm, D), lambda j, i: (i, 0)), w_spec, w_spec],
               [(0, 1, "nn", 0), (0, 2, "nn", 1)], 2, None, epilogue, [sh, sh, sh], [o_spec] * 3, None)


def _ffn_down(name, s, wd, res, tm=512):
    nd, T, fc = s.shape
    D = wd.shape[2]
    tm = _tile(T, tm)

    def epilogue(accs, in_refs, out_refs):
        out_refs[0][...] = in_refs[2][...] + 0.5 * accs[0]

    row = pl.BlockSpec((tm, D), lambda i, k: (i, 0))
    return _mm(name, (T // tm, nd), [s, wd, res],
               [pl.BlockSpec((None, tm, fc), lambda i, k: (k, i, 0)), pl.BlockSpec((None, fc, D), lambda i, k: (k, 0, 0)), row],
               [(0, 1, "nn", 0)], 1, (tm, D), epilogue, [jax.ShapeDtypeStruct((T, D), F32)], [row], 1)[0]


def _ffn_bwd_act(name, dxb, wd, a, b, tm=512):
    T, D = dxb.shape
    nd, fc, _ = wd.shape
    tm = _tile(T, tm)

    def epilogue(accs, in_refs, out_refs):
        ds = 0.5 * accs[0]
        av, bv = in_refs[2][...].astype(F32), in_refs[3][...].astype(F32)
        sg = _sigmoid(av)
        out_refs[0][...] = (ds * bv * sg * (1.0 + av * (1.0 - sg))).astype(BF16)
        out_refs[1][...] = (ds * av * sg).astype(BF16)

    act = pl.BlockSpec((None, tm, fc), lambda j, i: (j, i, 0))
    sh = jax.ShapeDtypeStruct((nd, T, fc), BF16)
    return _mm(name, (nd, T // tm), [dxb, wd, a, b],
               [pl.BlockSpec((tm, D), lambda j, i: (i, 0)), pl.BlockSpec((None, fc, D), lambda j, i: (j, 0, 0)), act, act],
               [(0, 1, "nt", 0)], 1, None, epilogue, [sh, sh], [act, act], None)


def _ffn_dwd(name, s, dxb, tk=512):
    nd, T, fc = s.shape
    D = dxb.shape[1]
    tk = _tile(T, tk)

    def epilogue(accs, in_refs, out_refs):
        out_refs[0][...] = (0.5 * accs[0]).astype(BF16)

    return _mm(name, (nd, T // tk), [s, dxb],
               [pl.BlockSpec((None, tk, fc), lambda j, k: (j, k, 0)), pl.BlockSpec((tk, D), lambda j, k: (k, 0))],
               [(0, 1, "tn", 0)], 1, (fc, D), epilogue, [jax.ShapeDtypeStruct((nd, fc, D), BF16)],
               [pl.BlockSpec((None, fc, D), lambda j, k: (j, 0, 0))], 1)[0]


def _ffn_dwgu(name, n, da, db, tk=512):
    T, D = n.shape
    nd, _, fc = da.shape
    tk = _tile(T, tk)

    def epilogue(accs, in_refs, out_refs):
        out_refs[0][...] = accs[0].astype(BF16)
        out_refs[1][...] = accs[1].astype(BF16)

    act = pl.BlockSpec((None, tk, fc), lambda j, k: (j, k, 0))
    out = pl.BlockSpec((None, D, fc), lambda j, k: (j, 0, 0))
    sh = jax.ShapeDtypeStruct((nd, D, fc), BF16)
    return _mm(name, (nd, T // tk), [n, da, db], [pl.BlockSpec((tk, D), lambda j, k: (k, 0)), act, act],
               [(0, 1, "tn", 0), (0, 2, "tn", 1)], 2, (D, fc), epilogue, [sh, sh], [out, out], 1)


def _ffn_dn(name, da, db, wg, wu, tm=512):
    nd, T, fc = da.shape
    D = wg.shape[1]
    tm = _tile(T, tm)

    def epilogue(accs, in_refs, out_refs):
        out_refs[0][...] = accs[0]

    act = pl.BlockSpec((None, tm, fc), lambda i, k: (k, i, 0))
    w_spec = pl.BlockSpec((None, D, fc), lambda i, k: (k, 0, 0))
    row = pl.BlockSpec((tm, D), lambda i, k: (i, 0))
    return _mm(name, (T // tm, nd), [da, db, wg, wu], [act, act, w_spec, w_spec],
               [(0, 2, "nt", 0), (1, 3, "nt", 0)], 1, (tm, D), epilogue, [jax.ShapeDtypeStruct((T, D), F32)], [row], 1)[0]


def _ffn_forward(tag, x, g, wg, wu, wd):
    n = _rmsnorm_fwd(tag + "_norm", x, g)
    a, b, s = _ffn_up(tag + "_up", n, wg, wu)
    return _ffn_down(tag + "_down", s, wd, x), (n, a, b, s)


def _ffn_backward(tag, dx, dxb, x, g, wg, wu, wd, saved):
    n, a, b, s = saved
    da, db = _ffn_bwd_act(tag + "_bwd_act", dxb, wd, a, b)
    d_wd = _ffn_dwd(tag + "_dwd", s, dxb)
    d_wg, d_wu = _ffn_dwgu(tag + "_dwgu", n, da, db)
    dn = _ffn_dn(tag + "_dn", da, db, wg, wu)
    dx_in, dxb_in, dg = _rmsnorm_bwd(tag + "_norm_bwd", dn, x, g, dx)
    return dx_in, dxb_in, dg, d_wg, d_wu, d_wd


def _place():
    x, y, c = lax.axis_index("x"), lax.axis_index("y"), lax.axis_index("c")
    return x, y, c, [(1 - x, y), (x, 1 - y), (1 - x, 1 - y)]


def _allgather(name, shards):
    n = len(shards)

    def body(*refs):
        ins, outs = refs[:n], refs[n:2 * n]
        send_sems, recv_sems, local_sems = refs[2 * n:]
        x, y, c, chips = _place()
        me, sibling = (x, y, c), (x, y, 1 - c)

        def slot(a, p):
            return outs[a].at[4 * p[0] + 2 * p[1] + p[2]]

        def copy(a, kk, block, to, src=None):
            return pltpu.make_async_remote_copy(
                src_ref=slot(a, block) if src is None else src, dst_ref=slot(a, block),
                send_sem=send_sems.at[a * 7 + kk], recv_sem=recv_sems.at[a * 7 + kk],
                device_id=to, device_id_type=MESH)

        mine = [pltpu.make_async_copy(ins[a], slot(a, me), local_sems.at[a]) for a in range(n)]
        for cp in mine:
            cp.start()
        first = []
        for a in range(n):
            first.append(copy(a, 0, me, sibling, src=ins[a]))
            first += [copy(a, 1 + j, me, (*chip, c), src=ins[a]) for j, chip in enumerate(chips)]
        for cp in first:
            cp.start()
        passed = []
        for j, chip in enumerate(chips):
            for a in range(n):
                copy(a, 1 + j, (*chip, c), me).wait_recv()
                fwd = copy(a, 4 + j, (*chip, c), sibling)
                fwd.start()
                passed.append(fwd)
        for a in range(n):
            copy(a, 0, sibling, me).wait_recv()
        for j, chip in enumerate(chips):
            for a in range(n):
                copy(a, 4 + j, (*chip, 1 - c), me).wait_recv()
        for cp in first + passed:
            cp.wait_send()
        for cp in mine:
            cp.wait()

    return pl.pallas_call(
        body, name=name, in_specs=[HBM_SPEC] * n, out_specs=[HBM_SPEC] * n,
        out_shape=[jax.ShapeDtypeStruct((N_DEV, *s.shape), s.dtype) for s in shards],
        scratch_shapes=[pltpu.SemaphoreType.DMA((7 * n,)), pltpu.SemaphoreType.DMA((7 * n,)), pltpu.SemaphoreType.DMA((n,))],
    )(*shards)


def _rs_sibling(name, grads):
    n = len(grads)

    def body(*refs):
        ins, outs = refs[:n], refs[n:2 * n]
        send_sems, recv_sems = refs[2 * n:]
        x, y, c, _ = _place()
        copies = [pltpu.make_async_remote_copy(
            src_ref=ins[a].at[:, 1 - c], dst_ref=outs[a], send_sem=send_sems.at[a], recv_sem=recv_sems.at[a],
            device_id=(x, y, 1 - c), device_id_type=MESH) for a in range(n)]
        for cp in copies:
            cp.start()
        for cp in copies:
            cp.wait()

    return pl.pallas_call(
        body, name=name, in_specs=[HBM_SPEC] * n, out_specs=[HBM_SPEC] * n,
        out_shape=[jax.ShapeDtypeStruct((g.shape[0], *g.shape[2:]), g.dtype) for g in grads],
        scratch_shapes=[pltpu.SemaphoreType.DMA((n,)), pltpu.SemaphoreType.DMA((n,))],
    )(*grads)


def _pair_add(name, own, got, core):
    nch, _, K, N = own.shape
    tr = _row_tile(K, N)

    def body(c_ref, own_ref, got_ref, o_ref):
        o_ref[...] = (own_ref[...].astype(F32) + got_ref[...].astype(F32)).astype(o_ref.dtype)

    grid_spec = pltpu.PrefetchScalarGridSpec(
        num_scalar_prefetch=1, grid=(nch, K // tr),
        in_specs=[pl.BlockSpec((None, None, tr, N), lambda k, r, c_ref: (k, c_ref[0], r, 0)),
                  pl.BlockSpec((None, tr, N), lambda k, r, c_ref: (k, r, 0))],
        out_specs=pl.BlockSpec((None, tr, N), lambda k, r, c_ref: (k, r, 0)))
    return pl.pallas_call(body, name=name, grid_spec=grid_spec, out_shape=jax.ShapeDtypeStruct((nch, K, N), own.dtype),
                          compiler_params=_cp(2))(core, own, got)


def _rs_chips(name, sums):
    n = len(sums)

    def body(*refs):
        ins, outs = refs[:n], refs[n:2 * n]
        send_sems, recv_sems, local_sems = refs[2 * n:]
        x, y, c, chips = _place()
        k_me = 2 * x + y
        mine = [pltpu.make_async_copy(ins[a].at[k_me], outs[a].at[k_me], local_sems.at[a]) for a in range(n)]
        for cp in mine:
            cp.start()

        def copy(a, r, slot):
            chip = chips[r]
            k_peer = 2 * chip[0] + chip[1]
            return pltpu.make_async_remote_copy(
                src_ref=ins[a].at[k_peer], dst_ref=outs[a].at[k_me if slot == "theirs" else k_peer],
                send_sem=send_sems.at[3 * a + r], recv_sem=recv_sems.at[3 * a + r],
                device_id=(*chip, c), device_id_type=MESH)

        sends = [copy(a, r, "theirs") for a in range(n) for r in range(3)]
        for cp in sends:
            cp.start()
        for a in range(n):
            for r in range(3):
                copy(a, r, "mine").wait_recv()
        for cp in sends:
            cp.wait_send()
        for cp in mine:
            cp.wait()

    return pl.pallas_call(
        body, name=name, in_specs=[HBM_SPEC] * n, out_specs=[HBM_SPEC] * n,
        out_shape=[jax.ShapeDtypeStruct(s.shape, s.dtype) for s in sums],
        scratch_shapes=[pltpu.SemaphoreType.DMA((3 * n,)), pltpu.SemaphoreType.DMA((3 * n,)), pltpu.SemaphoreType.DMA((n,))],
    )(*sums)


def _row_tile(K, N):
    limit = max(16, 262144 // N)
    t = 1
    while t * 2 <= limit and K % (t * 2) == 0:
        t *= 2
    return t if t >= 16 else K


def _adamw(name, parts, w, m, v):
    P, K, N = parts.shape
    tr = _row_tile(K, N)

    def body(p_ref, w_ref, m_ref, v_ref, g_ref, d_ref, nm_ref, nv_ref):
        g = p_ref[0].astype(F32)
        for i in range(1, P):
            g = g + p_ref[i].astype(F32)
        m_new = ADAM_B1 * m_ref[...] + (1.0 - ADAM_B1) * g
        v_new = ADAM_B2 * v_ref[...] + (1.0 - ADAM_B2) * (g * g)
        m_hat = m_new / (1.0 - ADAM_B1 ** ADAM_STEP)
        v_hat = v_new / (1.0 - ADAM_B2 ** ADAM_STEP)
        g_ref[...] = g
        d_ref[...] = -ADAM_LR * (m_hat / (jnp.sqrt(v_hat) + ADAM_EPS) + ADAM_WD * w_ref[...])
        nm_ref[...] = m_new
        nv_ref[...] = v_new

    row = pl.BlockSpec((tr, N), lambda r: (r, 0))
    sh = jax.ShapeDtypeStruct((K, N), F32)
    return pl.pallas_call(body, name=name, grid=(K // tr,),
                          in_specs=[pl.BlockSpec((P, tr, N), lambda r: (0, r, 0)), row, row, row],
                          out_specs=[row] * 4, out_shape=[sh] * 4, compiler_params=_cp(1))(parts, w, m, v)


def _merge_fwd(o_mla, wbm, o_dil, wbd, g0, g1, tm=512, tn=512):
    T, K1 = o_mla.shape
    K2, D = o_dil.shape[1], wbm.shape[1]
    tm, tn = _tile(T, tm), _tile(D, tn)

    def epilogue(accs, in_refs, out_refs):
        a, b = accs
        out_refs[0][...] = a.astype(BF16)
        out_refs[1][...] = b.astype(BF16)
        out_refs[2][...] = (in_refs[4][...].astype(F32) * a + in_refs[5][...].astype(F32) * b).astype(BF16)

    mn = pl.BlockSpec((tm, tn), lambda i, j: (i, j))
    sh = jax.ShapeDtypeStruct((T, D), BF16)
    return _mm("merge_fwd", (T // tm, D // tn), [o_mla, wbm, o_dil, wbd, g0, g1],
               [pl.BlockSpec((tm, K1), lambda i, j: (i, 0)), pl.BlockSpec((K1, tn), lambda i, j: (0, j)),
                pl.BlockSpec((tm, K2), lambda i, j: (i, 0)), pl.BlockSpec((K2, tn), lambda i, j: (0, j)), mn, mn],
               [(0, 1, "nn", 0), (2, 3, "nn", 1)], 2, None, epilogue, [sh, sh, sh], [mn, mn, mn], None)


def _ple_loss(n4, wpg, pe, wpp, x3, tgt, tm=512, tn=512):
    T, D = x3.shape
    Kp = pe.shape[1]
    tm, tn = _tile(T, tm), _tile(D, tn)

    def epilogue(accs, in_refs, out_refs):
        z, proj = accs
        pg = _sigmoid(z)
        err = in_refs[4][...] + pg * proj - in_refs[5][...]
        dy = err * (1.0 / D)
        out_refs[0][...] = dy
        out_refs[1][...] = (dy * proj * pg * (1.0 - pg)).astype(BF16)
        out_refs[2][...] = (dy * pg).astype(BF16)

        @pl.when(pl.program_id(1) == 0)
        def _():
            out_refs[3][...] = jnp.zeros_like(out_refs[3])

        out_refs[3][...] += jnp.sum(err * err, axis=-1, keepdims=True)

    mn = pl.BlockSpec((tm, tn), lambda i, j: (i, j))
    return _mm("ple_loss", (T // tm, D // tn), [n4, wpg, pe, wpp, x3, tgt],
               [pl.BlockSpec((tm, D), lambda i, j: (i, 0)), pl.BlockSpec((D, tn), lambda i, j: (0, j)),
                pl.BlockSpec((tm, Kp), lambda i, j: (i, 0)), pl.BlockSpec((Kp, tn), lambda i, j: (0, j)), mn, mn],
               [(0, 1, "nn", 0), (2, 3, "nn", 1)], 2, None, epilogue,
               [jax.ShapeDtypeStruct((T, D), F32), jax.ShapeDtypeStruct((T, D), BF16), jax.ShapeDtypeStruct((T, D), BF16),
                jax.ShapeDtypeStruct((T, 1), F32)],
               [mn, mn, mn, pl.BlockSpec((tm, 1), lambda i, j: (i, 0))], None)


def _epi_sigmoid(acc, ex, outs):
    outs[0][...] = _sigmoid(acc).astype(outs[0].dtype)


def _epi_add(acc, ex, outs):
    outs[0][...] = (acc + ex[0][...].astype(F32)).astype(outs[0].dtype)


def _epi_dmerge(acc, ex, outs):
    mp, dp, g0, g1 = [e[...].astype(F32) for e in ex]
    outs[0][...] = (acc * g0).astype(BF16)
    outs[1][...] = (acc * g1).astype(BF16)
    outs[2][...] = (acc * mp * g0 * (1.0 - g0)).astype(BF16)
    outs[3][...] = (acc * dp * g1 * (1.0 - g1)).astype(BF16)


_WEIGHTS = ("g_ffn1", "w1_gate", "w1_up", "w1_down", "g_mix", "w_in", "g_cq", "w_uq", "g_ckv", "w_ukv", "g_q_mla", "g_k_mla",
            "g_q_dil", "g_k_dil", "w_br_mla", "w_br_dil", "w_o", "g_ffn2", "w2_gate", "w2_up", "w2_down", "g_ple",
            "w_ple_gate", "w_ple_proj")
_MATRICES = ("w1_gate", "w1_up", "w1_down", "w_in", "w_uq", "w_ukv", "w_br_mla", "w_br_dil", "w_o", "w2_gate", "w2_up",
             "w2_down", "w_ple_gate", "w_ple_proj")
_GAINS = tuple(n for n in _WEIGHTS if n not in _MATRICES)
MLA_BLOCK = 1024
DIL_BLOCK = 512


def _cols(g3):
    nd, K, n = g3.shape
    return g3.transpose(1, 0, 2).reshape(K, nd * n)


def _uncols(m):
    K, n = m.shape
    return m.reshape(K, N_DEV, n // N_DEV).transpose(1, 0, 2)


def _rows(g3):
    nd, k, N = g3.shape
    return g3.reshape(nd * k, N)


def _unrows(m):
    K, N = m.shape
    return m.reshape(N_DEV, K // N_DEV, N)


def _pack_gains(vals):
    flat = jnp.concatenate([vals[n].reshape(-1) for n in _GAINS])
    pad = (-flat.shape[0]) % 2048
    return jnp.pad(flat, (0, pad)).reshape(-1, 128)


def _unpack_gains(packed, like):
    flat = packed.reshape(-1)
    out, off = {}, 0
    for n in _GAINS:
        size = int(np.prod(like[n].shape))
        out[n] = flat[off:off + size].reshape(like[n].shape)
        off += size
    return out


def _train_step(x, p, positions, loss_target, W, M, V):
    T, D = x.shape[1], x.shape[2]
    xs, tgt, pe = x[0], loss_target[0], p[0, 0]
    pos_col = positions.reshape(T, 1).astype(F32)
    w = {n: (a[0] if n in _MATRICES else a.reshape(1, -1)) for n, a in W.items()}

    gathered = dict(zip(_MATRICES, _allgather("ag_weights", [w[n].astype(BF16) for n in _MATRICES])))
    nq_l, nkv_l = w["g_cq"].shape[-1], w["g_ckv"].shape[-1]
    H = w["w_uq"].shape[1] * N_DEV // MLA_QK
    G = len(DIL_GROUPS)
    off_kr = nq_l + nkv_l
    off_dil = off_kr + MLA_ROPE
    off_gate = off_dil + G * 3 * H * HEAD
    kr_block = off_kr // 128
    w_in = _cols(gathered["w_in"])
    wa = jnp.pad(w_in[:, :off_dil], ((0, 0), (0, 128 - MLA_ROPE)))
    wdil, wg0, wg1 = w_in[:, off_dil:off_gate], w_in[:, off_gate:off_gate + D], w_in[:, off_gate + D:]
    wuq = jnp.pad(_cols(gathered["w_uq"]).reshape(nq_l, H, MLA_QK), ((0, 0), (0, 0), (0, MLA_PAD - MLA_QK))).reshape(nq_l, H * MLA_PAD)
    wukv = _cols(gathered["w_ukv"])
    wbm, wbd, wpp = _cols(gathered["w_br_mla"]), _cols(gathered["w_br_dil"]), _cols(gathered["w_ple_proj"])
    wo, wpg = _rows(gathered["w_o"]), _rows(gathered["w_ple_gate"])
    gq_mla = jnp.pad(w["g_q_mla"], ((0, 0), (0, MLA_PAD - MLA_QK)))
    gk_mla = jnp.pad(w["g_k_mla"], ((0, 0), (0, MLA_PAD - MLA_QK)))
    gq_dil, gk_dil = w["g_q_dil"].reshape(G, 1, HEAD), w["g_k_dil"].reshape(G, 1, HEAD)

    half_m, half_d = MLA_ROPE // 2, DIL_ROT // 2
    inv_m = ROPE_THETA ** (-jnp.arange(half_m, dtype=F32) * 2.0 / MLA_ROPE)
    inv_d = ROPE_THETA ** (-jnp.arange(half_d, dtype=F32) * 2.0 / DIL_ROT)
    inv_m = jnp.tile(inv_m, 128 // half_m).reshape(1, 128)
    inv_d = jnp.tile(inv_d, 128 // half_d).reshape(1, 128)
    tabs = _rope_tables(pos_col, inv_m, inv_d)
    tabs_m, tabs_d = tabs[:3], tabs[3:]

    mla_groups = ((None, 1),)
    mla_blk, dil_blk = _tile(T, MLA_BLOCK), _tile(T, DIL_BLOCK)

    x1, ffn1 = _ffn_forward("ffn1", xs, w["g_ffn1"], gathered["w1_gate"], gathered["w1_up"], gathered["w1_down"])
    h = _rmsnorm_fwd("mix_norm", x1, w["g_mix"])
    pa = _mm2d("proj_a", h, wa, "nn", 512, 1024, 4096)
    pdil = _mm2d("proj_dil", h, wdil, "nn", 1024, 1024, 4096, out_dtypes=(BF16,))
    g0 = _mm2d("proj_gate0", h, wg0, "nn", 1024, 1024, 4096, out_dtypes=(BF16,), epilogue=_epi_sigmoid)
    g1 = _mm2d("proj_gate1", h, wg1, "nn", 1024, 1024, 4096, out_dtypes=(BF16,), epilogue=_epi_sigmoid)
    cq, ckv = _lora_fwd(pa, w["g_cq"], w["g_ckv"], nq_l, nkv_l)
    q_raw = _mm2d("q_up", cq, wuq, "nn", 512, 2048, 4096)
    kv = _mm2d("kv_up", ckv, wukv, "nn", 512, 2048, 4096)
    q_att, k_att, v_mla = _mla_prep_fwd(q_raw, kv, pa, tabs_m, gq_mla, gk_mla, H, kr_block)
    o_mla, lse_mla = _attn_fwd("mla_fwd", q_att, k_att, v_mla, H, MLA_PAD, 0, 0, mla_groups, mla_blk)
    qd, kd = _dil_prep_fwd(pdil, tabs_d, gq_dil, gk_dil, G, H)
    o_dil, lse_dil = _attn_fwd("dil_fwd", qd, kd, pdil, H, HEAD, 2 * H, 3 * H, DIL_GROUPS, dil_blk)
    mla_p, dil_p, merged = _merge_fwd(o_mla, wbm, o_dil, wbd, g0, g1)
    x2 = _mm2d("out_proj", merged, wo, "nn", 512, 1024, 4096, epilogue=_epi_add, extras=(x1,))
    x3, ffn2 = _ffn_forward("ffn2", x2, w["g_ffn2"], gathered["w2_gate"], gathered["w2_up"], gathered["w2_down"])
    n4 = _rmsnorm_fwd("ple_norm", x3, w["g_ple"])
    dy, dz, dproj, loss_rows = _ple_loss(n4, wpg, pe, wpp, x3, tgt)
    loss = lax.psum((0.5 / D) * jnp.sum(loss_rows), ("x", "y", "c"))

    dW, dG = {}, {}
    dW["w_ple_proj"] = _uncols(_mm2d("d_wpp", pe, dproj, "tn", 1024, 2048, 512, out_dtypes=(BF16,)))
    dW["w_ple_gate"] = _unrows(_mm2d("d_wpg", n4, dz, "tn", 1024, 1024, 512, out_dtypes=(BF16,)))
    dn4 = _mm2d("d_n4", dz, wpg, "nt", 512, 1024, 4096)
    dx3, dx3b, dG["g_ple"] = _rmsnorm_bwd("ple_norm_bwd", dn4, x3, w["g_ple"], dy)
    dx2, dx2b, dG["g_ffn2"], dW["w2_gate"], dW["w2_up"], dW["w2_down"] = _ffn_backward(
        "ffn2", dx3, dx3b, x2, w["g_ffn2"], gathered["w2_gate"], gathered["w2_up"], gathered["w2_down"], ffn2)

    d_mla_p, d_dil_p, dpg0, dpg1 = _mm2d("d_merged", dx2b, wo, "nt", 512, 1024, 4096, out_dtypes=(BF16,) * 4,
                                         epilogue=_epi_dmerge, extras=(mla_p, dil_p, g0, g1))
    dW["w_o"] = _unrows(_mm2d("d_wo", merged, dx2b, "tn", 1024, 1024, 512, out_dtypes=(BF16,)))
    dW["w_br_mla"] = _uncols(_mm2d("d_wbm", o_mla, d_mla_p, "tn", 1024, 1024, 512, out_dtypes=(BF16,)))
    dW["w_br_dil"] = _uncols(_mm2d("d_wbd", o_dil, d_dil_p, "tn", 1024, 1024, 512, out_dtypes=(BF16,)))
    do_mla = _mm2d("d_o_mla", d_mla_p, wbm, "nt", 512, 1024, 4096, out_dtypes=(BF16,))
    do_dil = _mm2d("d_o_dil", d_dil_p, wbd, "nt", 512, 1024, 4096, out_dtypes=(BF16,))

    def as_row(a):
        return a.reshape(a.shape[0], 1, a.shape[1])

    dq_att, dl_mla = _attn_dq("mla_dq", q_att, k_att, v_mla, do_mla, o_mla, lse_mla, H, MLA_PAD, 0, 0, mla_groups, mla_blk)
    dk_att, dv_mla = _attn_dkv("mla_dkv", q_att, k_att, v_mla, do_mla, as_row(lse_mla), as_row(dl_mla), H, MLA_PAD, 0, 0,
                               mla_groups, mla_blk)
    dq_raw, dkv, dkr, dgq, dgk = _mla_prep_bwd(dq_att, dk_att, dv_mla, q_raw, kv, pa, tabs_m, gq_mla, gk_mla, H, kr_block)
    dG["g_q_mla"], dG["g_k_mla"] = dgq[:, :MLA_QK], dgk[:, :MLA_QK]
    d_wuq = _mm2d("d_wuq", cq, dq_raw, "tn", 512, 2048, 512, out_dtypes=(BF16,))
    dW["w_uq"] = _uncols(d_wuq.reshape(nq_l, H, MLA_PAD)[:, :, :MLA_QK].reshape(nq_l, H * MLA_QK))
    dW["w_ukv"] = _uncols(_mm2d("d_wukv", ckv, dkv, "tn", 512, 2048, 512, out_dtypes=(BF16,)))
    dcq = _mm2d("d_cq", dq_raw, wuq, "nt", 512, 1024, 4096)
    dckv = _mm2d("d_ckv", dkv, wukv, "nt", 512, 1024, 4096)
    dpa, dG["g_cq"], dG["g_ckv"] = _lora_bwd(dcq, dckv, dkr, pa, w["g_cq"], w["g_ckv"])

    dqd, dl_dil = _attn_dq("dil_dq", qd, kd, pdil, do_dil, o_dil, lse_dil, H, HEAD, 2 * H, 3 * H, DIL_GROUPS, dil_blk)
    dkd, dvd = _attn_dkv("dil_dkv", qd, kd, pdil, do_dil, as_row(lse_dil), as_row(dl_dil), H, HEAD, 2 * H, 3 * H,
                         DIL_GROUPS, dil_blk)
    dpdil, dgqd, dgkd = _dil_prep_bwd(dqd, dkd, dvd, pdil, tabs_d, gq_dil, gk_dil, G, H)
    dG["g_q_dil"], dG["g_k_dil"] = dgqd.reshape(1, G, HEAD), dgkd.reshape(1, G, HEAD)

    d_wa = _mm2d("d_wa", h, dpa, "tn", 1024, 1024, 512, out_dtypes=(BF16,))
    d_wdil = _mm2d("d_wdil", h, dpdil, "tn", 1024, 1024, 512, out_dtypes=(BF16,))
    d_wg0 = _mm2d("d_wg0", h, dpg0, "tn", 1024, 1024, 512, out_dtypes=(BF16,))
    d_wg1 = _mm2d("d_wg1", h, dpg1, "tn", 1024, 1024, 512, out_dtypes=(BF16,))
    dW["w_in"] = _uncols(jnp.concatenate([d_wa[:, :off_dil], d_wdil, d_wg0, d_wg1], axis=1))
    dh = _mm2d("d_h_a", dpa, wa, "nt", 512, 1024, 1024)
    dh = _mm2d("d_h_dil", dpdil, wdil, "nt", 512, 1024, 1024, epilogue=_epi_add, extras=(dh,))
    dh = _mm2d("d_h_g0", dpg0, wg0, "nt", 512, 1024, 1024, epilogue=_epi_add, extras=(dh,))
    dh = _mm2d("d_h_g1", dpg1, wg1, "nt", 512, 1024, 1024, epilogue=_epi_add, extras=(dh,))
    dx1, dx1b, dG["g_mix"] = _rmsnorm_bwd("mix_norm_bwd", dh, x1, w["g_mix"], dx2)
    dx0, _, dG["g_ffn1"], dW["w1_gate"], dW["w1_up"], dW["w1_down"] = _ffn_backward(
        "ffn1", dx1, dx1b, xs, w["g_ffn1"], gathered["w1_gate"], gathered["w1_up"], gathered["w1_down"], ffn1)

    halves = [dW[n].reshape(4, 2, *dW[n].shape[1:]) for n in _MATRICES]
    got = _rs_sibling("rs_sibling", halves)
    core = lax.axis_index("c").astype(jnp.int32).reshape(1)
    sums = [_pair_add("rs_add_" + n, own, rec, core) for n, own, rec in zip(_MATRICES, halves, got)]
    reduced = dict(zip(_MATRICES, _rs_chips("rs_chips", sums)))
    grads, deltas, new_m, new_v = {}, {}, {}, {}
    for n in _MATRICES:
        grads[n], deltas[n], new_m[n], new_v[n] = [
            a[None] for a in _adamw("adamw_" + n, reduced[n], w[n], M[n][0], V[n][0])]

    parts = _allgather("ag_gain_grads", [_pack_gains(dG)])[0]
    packed = _adamw("adamw_gains", parts, _pack_gains(W), _pack_gains(M), _pack_gains(V))
    for out, pk in zip((grads, deltas, new_m, new_v), packed):
        out.update(_unpack_gains(pk, W))

    return (loss, dx0[None], *[grads[n] for n in _WEIGHTS], *[deltas[n] for n in _WEIGHTS],
            *[new_m[n] for n in _WEIGHTS], *[new_v[n] for n in _WEIGHTS])


def kernel(x, p, positions, g_ffn1, w1_gate, w1_up, w1_down, g_mix, w_in, g_cq, w_uq, g_ckv, w_ukv, g_q_mla, g_k_mla, g_q_dil, g_k_dil, w_br_mla, w_br_dil, w_o, g_ffn2, w2_gate, w2_up, w2_down, g_ple, w_ple_gate, w_ple_proj, loss_target, m_g_ffn1, m_w1_gate, m_w1_up, m_w1_down, m_g_mix, m_w_in, m_g_cq, m_w_uq, m_g_ckv, m_w_ukv, m_g_q_mla, m_g_k_mla, m_g_q_dil, m_g_k_dil, m_w_br_mla, m_w_br_dil, m_w_o, m_g_ffn2, m_w2_gate, m_w2_up, m_w2_down, m_g_ple, m_w_ple_gate, m_w_ple_proj, v_g_ffn1, v_w1_gate, v_w1_up, v_w1_down, v_g_mix, v_w_in, v_g_cq, v_w_uq, v_g_ckv, v_w_ukv, v_g_q_mla, v_g_k_mla, v_g_q_dil, v_g_k_dil, v_w_br_mla, v_w_br_dil, v_w_o, v_g_ffn2, v_w2_gate, v_w2_up, v_w2_down, v_g_ple, v_w_ple_gate, v_w_ple_proj):
    W = dict(zip(_WEIGHTS, (g_ffn1, w1_gate, w1_up, w1_down, g_mix, w_in, g_cq, w_uq, g_ckv, w_ukv, g_q_mla, g_k_mla, g_q_dil,
                            g_k_dil, w_br_mla, w_br_dil, w_o, g_ffn2, w2_gate, w2_up, w2_down, g_ple, w_ple_gate, w_ple_proj)))
    M = dict(zip(_WEIGHTS, (m_g_ffn1, m_w1_gate, m_w1_up, m_w1_down, m_g_mix, m_w_in, m_g_cq, m_w_uq, m_g_ckv, m_w_ukv, m_g_q_mla,
                            m_g_k_mla, m_g_q_dil, m_g_k_dil, m_w_br_mla, m_w_br_dil, m_w_o, m_g_ffn2, m_w2_gate, m_w2_up,
                            m_w2_down, m_g_ple, m_w_ple_gate, m_w_ple_proj)))
    V = dict(zip(_WEIGHTS, (v_g_ffn1, v_w1_gate, v_w1_up, v_w1_down, v_g_mix, v_w_in, v_g_cq, v_w_uq, v_g_ckv, v_w_ukv, v_g_q_mla,
                            v_g_k_mla, v_g_q_dil, v_g_k_dil, v_w_br_mla, v_w_br_dil, v_w_o, v_g_ffn2, v_w2_gate, v_w2_up,
                            v_w2_down, v_g_ple, v_w_ple_gate, v_w_ple_proj)))
    return _train_step(x, p, positions, loss_target, W, M, V)
```

```python
import numpy as np
import jax
import jax.numpy as jnp
from jax import lax
from jax.experimental import pallas as pl
from jax.experimental.pallas import tpu as pltpu

F32 = jnp.float32
BF16 = jnp.bfloat16

EPS = 1e-6
ROPE_THETA = 500000.0
MLA_NOPE = 128
MLA_ROPE = 64
MLA_QK = MLA_NOPE + MLA_ROPE
MLA_PAD = 256
HEAD = 128
DIL_ROT = 32
DIL_GROUPS = ((128, 1), (512, 4), (2048, 16))
NEG = -1e30
NO_WINDOW = 1 << 30
N_DEV = 8
ADAM_LR, ADAM_B1, ADAM_B2, ADAM_EPS, ADAM_WD, ADAM_STEP = 0.001, 0.9, 0.999, 1e-08, 0.01, 10
VMEM_LIMIT_V7X = 56 * 1024 * 1024
MESH = pl.DeviceIdType.MESH
HBM_SPEC = pl.BlockSpec(memory_space=pltpu.HBM)


def _cp(n_axes):
    return pltpu.CompilerParams(dimension_semantics=("arbitrary",) * n_axes,
                                vmem_limit_bytes=VMEM_LIMIT_V7X)


def _tile(n, t):
    return t if (n >= t and n % t == 0) else n


def _sigmoid(x):
    return 1.0 / (1.0 + jnp.exp(-x))


_DIMS = {"nn": (((1,), (0,)), ((), ())), "nt": (((1,), (1,)), ((), ())), "tn": (((0,), (0,)), ((), ()))}


def _dot(a, b, mode):
    return lax.dot_general(a.astype(BF16), b.astype(BF16), _DIMS[mode], preferred_element_type=F32)


def _mm(name, grid, ins, in_specs, pairs, n_acc, acc_shape, epilogue, out_shapes, out_specs, k_axis):
    n_in, n_out = len(ins), len(out_shapes)
    nk = grid[k_axis] if k_axis is not None else 1

    def body(*refs):
        in_refs, out_refs, acc_refs = refs[:n_in], refs[n_in:n_in + n_out], refs[n_in + n_out:]
        parts = [None] * n_acc
        for ai, bi, mode, ci in pairs:
            d = _dot(in_refs[ai][...], in_refs[bi][...], mode)
            parts[ci] = d if parts[ci] is None else parts[ci] + d
        if nk == 1:
            epilogue(parts, in_refs, out_refs)
            return
        k = pl.program_id(k_axis)

        @pl.when(k == 0)
        def _():
            for c in range(n_acc):
                acc_refs[c][...] = parts[c]

        @pl.when(k > 0)
        def _():
            for c in range(n_acc):
                acc_refs[c][...] += parts[c]

        @pl.when(k == nk - 1)
        def _():
            epilogue([r[...] for r in acc_refs], in_refs, out_refs)

    scratch = [pltpu.VMEM(acc_shape, F32) for _ in range(n_acc)] if nk > 1 else []
    return pl.pallas_call(body, name=name, grid=grid, in_specs=in_specs, out_specs=out_specs,
                          out_shape=out_shapes, scratch_shapes=scratch, compiler_params=_cp(len(grid)))(*ins)


def _mm2d(name, a, b, mode, tm, tn, tk, out_dtypes=(F32,), epilogue=None, extras=()):
    if mode == "nn":
        (M, K), N = a.shape, b.shape[1]
    elif mode == "nt":
        (M, K), N = a.shape, b.shape[0]
    else:
        (K, M), N = a.shape, b.shape[1]
    tm, tn, tk = _tile(M, tm), _tile(N, tn), _tile(K, tk)
    a_spec = pl.BlockSpec((tk, tm), lambda i, j, k: (k, i)) if mode == "tn" else pl.BlockSpec((tm, tk), lambda i, j, k: (i, k))
    b_spec = pl.BlockSpec((tn, tk), lambda i, j, k: (j, k)) if mode == "nt" else pl.BlockSpec((tk, tn), lambda i, j, k: (k, j))
    mn_spec = pl.BlockSpec((tm, tn), lambda i, j, k: (i, j))
    n_ex = len(extras)

    def default_epilogue(acc, ex_refs, out_refs):
        out_refs[0][...] = acc.astype(out_refs[0].dtype)

    epi = epilogue or default_epilogue

    def wrapped(accs, in_refs, out_refs):
        epi(accs[0], in_refs[2:2 + n_ex], out_refs)

    outs = _mm(name, (M // tm, N // tn, K // tk), [a, b, *extras], [a_spec, b_spec] + [mn_spec] * n_ex,
               [(0, 1, mode, 0)], 1, (tm, tn), wrapped,
               [jax.ShapeDtypeStruct((M, N), dt) for dt in out_dtypes], [mn_spec] * len(out_dtypes), 2)
    return outs[0] if len(out_dtypes) == 1 else outs


def _rms_stats(x, n):
    return lax.rsqrt(jnp.sum(x * x, axis=-1, keepdims=True) * (1.0 / n) + EPS)


def _rmsnorm_fwd(name, x, g, tm=512):
    T, D = x.shape
    tm = _tile(T, tm)

    def body(x_ref, g_ref, o_ref):
        xv = x_ref[...]
        o_ref[...] = (xv * _rms_stats(xv, D) * g_ref[...]).astype(BF16)

    return pl.pallas_call(body, name=name, grid=(T // tm,),
                          in_specs=[pl.BlockSpec((tm, D), lambda i: (i, 0)), pl.BlockSpec((1, D), lambda i: (0, 0))],
                          out_specs=pl.BlockSpec((tm, D), lambda i: (i, 0)),
                          out_shape=jax.ShapeDtypeStruct((T, D), BF16), compiler_params=_cp(1))(x, g)


def _rms_bwd_rows(dy, x, g, n):
    r = _rms_stats(x, n)
    xh = x * r
    gd = dy * g
    mean = jnp.sum(gd * xh, axis=-1, keepdims=True) * (1.0 / n)
    return r * (gd - xh * mean), dy * xh


def _rmsnorm_bwd(name, dn, x, g, res, tm=256):
    T, D = x.shape
    tm = _tile(T, tm)

    def body(dn_ref, x_ref, g_ref, res_ref, dx_ref, dxb_ref, dg_ref):
        dx, dgp = _rms_bwd_rows(dn_ref[...].astype(F32), x_ref[...], g_ref[...], D)
        dx = dx + res_ref[...]
        dx_ref[...] = dx
        dxb_ref[...] = dx.astype(BF16)

        @pl.when(pl.program_id(0) == 0)
        def _():
            dg_ref[...] = jnp.zeros_like(dg_ref)

        dg_ref[...] += jnp.sum(dgp, axis=0, keepdims=True)

    row = pl.BlockSpec((tm, D), lambda i: (i, 0))
    vec = pl.BlockSpec((1, D), lambda i: (0, 0))
    return pl.pallas_call(body, name=name, grid=(T // tm,), in_specs=[row, row, vec, row],
                          out_specs=[row, row, vec],
                          out_shape=[jax.ShapeDtypeStruct((T, D), F32), jax.ShapeDtypeStruct((T, D), BF16),
                                     jax.ShapeDtypeStruct((1, D), F32)],
                          compiler_params=_cp(1))(dn, x, g, res)


def _rope_tables(pos_col, inv_mla, inv_dil, tm=512):
    T = pos_col.shape[0]
    tm = _tile(T, tm)

    def body(p_ref, im_ref, id_ref, cm, sam, sbm, cd, sad, sbd):
        lane = lax.broadcasted_iota(jnp.int32, (tm, 128), 1)
        p = p_ref[...]
        am = p * im_ref[...]
        c, s = jnp.cos(am), jnp.sin(am)
        cm[...] = jnp.where(lane < 64, c, 0.0)
        sam[...] = jnp.where(lane < 32, -s, 0.0)
        sbm[...] = jnp.where((lane >= 32) & (lane < 64), s, 0.0)
        ad = p * id_ref[...]
        c, s = jnp.cos(ad), jnp.sin(ad)
        cd[...] = jnp.where(lane < 32, c, 1.0)
        sad[...] = jnp.where(lane < 16, -s, 0.0)
        sbd[...] = jnp.where((lane >= 16) & (lane < 32), s, 0.0)

    row = pl.BlockSpec((tm, 128), lambda i: (i, 0))
    vec = pl.BlockSpec((1, 128), lambda i: (0, 0))
    return pl.pallas_call(body, name="rope_tables", grid=(T // tm,),
                          in_specs=[pl.BlockSpec((tm, 1), lambda i: (i, 0)), vec, vec], out_specs=[row] * 6,
                          out_shape=[jax.ShapeDtypeStruct((T, 128), F32)] * 6, compiler_params=_cp(1))(pos_col, inv_mla, inv_dil)


def _rope(v, c, sa, sb, sh):
    return v * c + pltpu.roll(v, 128 - sh, 1) * sa + pltpu.roll(v, sh, 1) * sb


def _rope_t(d, c, sa, sb, sh):
    return d * c + pltpu.roll(d * sa, sh, 1) + pltpu.roll(d * sb, 128 - sh, 1)


def _lora_fwd(pa, g_cq, g_ckv, nq, nkv, tm=512):
    T, W = pa.shape
    tm = _tile(T, tm)

    def body(pa_ref, gq_ref, gk_ref, cq_ref, ckv_ref):
        a = pa_ref[:, :nq]
        cq_ref[...] = (a * _rms_stats(a, nq) * gq_ref[...]).astype(BF16)
        b = pa_ref[:, nq:nq + nkv]
        ckv_ref[...] = (b * _rms_stats(b, nkv) * gk_ref[...]).astype(BF16)

    return pl.pallas_call(body, name="lora_fwd", grid=(T // tm,),
                          in_specs=[pl.BlockSpec((tm, W), lambda i: (i, 0)), pl.BlockSpec((1, nq), lambda i: (0, 0)),
                                    pl.BlockSpec((1, nkv), lambda i: (0, 0))],
                          out_specs=[pl.BlockSpec((tm, nq), lambda i: (i, 0)), pl.BlockSpec((tm, nkv), lambda i: (i, 0))],
                          out_shape=[jax.ShapeDtypeStruct((T, nq), BF16), jax.ShapeDtypeStruct((T, nkv), BF16)],
                          compiler_params=_cp(1))(pa, g_cq, g_ckv)


def _lora_bwd(dcq, dckv, dkr, pa, g_cq, g_ckv, tm=512):
    T, W = pa.shape
    nq, nkv = dcq.shape[1], dckv.shape[1]
    tm = _tile(T, tm)

    def body(dcq_ref, dckv_ref, dkr_ref, pa_ref, gq_ref, gk_ref, dpa_ref, dgq_ref, dgk_ref):
        dx, dgp = _rms_bwd_rows(dcq_ref[...], pa_ref[:, :nq], gq_ref[...], nq)
        dpa_ref[:, :nq] = dx.astype(BF16)
        dx2, dgp2 = _rms_bwd_rows(dckv_ref[...], pa_ref[:, nq:nq + nkv], gk_ref[...], nkv)
        dpa_ref[:, nq:nq + nkv] = dx2.astype(BF16)
        dpa_ref[:, nq + nkv:] = dkr_ref[...].astype(BF16)

        @pl.when(pl.program_id(0) == 0)
        def _():
            dgq_ref[...] = jnp.zeros_like(dgq_ref)
            dgk_ref[...] = jnp.zeros_like(dgk_ref)

        dgq_ref[...] += jnp.sum(dgp, axis=0, keepdims=True)
        dgk_ref[...] += jnp.sum(dgp2, axis=0, keepdims=True)

    def row(n):
        return pl.BlockSpec((tm, n), lambda i: (i, 0))

    def vec(n):
        return pl.BlockSpec((1, n), lambda i: (0, 0))

    return pl.pallas_call(body, name="lora_bwd", grid=(T // tm,),
                          in_specs=[row(nq), row(nkv), row(128), row(W), vec(nq), vec(nkv)],
                          out_specs=[row(W), vec(nq), vec(nkv)],
                          out_shape=[jax.ShapeDtypeStruct((T, W), BF16), jax.ShapeDtypeStruct((1, nq), F32),
                                     jax.ShapeDtypeStruct((1, nkv), F32)],
                          compiler_params=_cp(1))(dcq, dckv, dkr, pa, g_cq, g_ckv)


def _sumsq(v):
    return jnp.sum(v * v, axis=-1, keepdims=True)


def _mla_prep_fwd(q_raw, kv, pa, tabs, gq, gk, H, kr_block, tm=256):
    T = q_raw.shape[0]
    tm = _tile(T, tm)
    scale = MLA_QK ** -0.5
    P = MLA_PAD

    def body(q_ref, kv_ref, kr_ref, c_ref, sa_ref, sb_ref, gq_ref, gk_ref, qo, ko, vo):
        c, sa, sb = c_ref[...], sa_ref[...], sb_ref[...]
        kr = kr_ref[...]
        kr2 = _sumsq(kr)
        for h in range(H):
            lo, hi = q_ref[:, h * P:h * P + 128], q_ref[:, h * P + 128:(h + 1) * P]
            r = lax.rsqrt((_sumsq(lo) + _sumsq(hi)) * (1.0 / MLA_QK) + EPS)
            qo[:, h * P:h * P + 128] = (lo * r * gq_ref[:, :128] * scale).astype(BF16)
            qo[:, h * P + 128:(h + 1) * P] = (_rope(hi * r * gq_ref[:, 128:], c, sa, sb, 32) * scale).astype(BF16)
            kn = kv_ref[:, h * P:h * P + 128]
            r = lax.rsqrt((_sumsq(kn) + kr2) * (1.0 / MLA_QK) + EPS)
            ko[:, h * P:h * P + 128] = (kn * r * gk_ref[:, :128]).astype(BF16)
            ko[:, h * P + 128:(h + 1) * P] = _rope(kr * r * gk_ref[:, 128:], c, sa, sb, 32).astype(BF16)
            vo[:, h * 128:(h + 1) * 128] = kv_ref[:, h * P + 128:(h + 1) * P].astype(BF16)

    wide = pl.BlockSpec((tm, H * P), lambda i: (i, 0))
    lane = pl.BlockSpec((tm, 128), lambda i: (i, 0))
    vec = pl.BlockSpec((1, P), lambda i: (0, 0))
    return pl.pallas_call(body, name="mla_prep_fwd", grid=(T // tm,),
                          in_specs=[wide, wide, pl.BlockSpec((tm, 128), lambda i: (i, kr_block)), lane, lane, lane, vec, vec],
                          out_specs=[wide, wide, pl.BlockSpec((tm, H * 128), lambda i: (i, 0))],
                          out_shape=[jax.ShapeDtypeStruct((T, H * P), BF16), jax.ShapeDtypeStruct((T, H * P), BF16),
                                     jax.ShapeDtypeStruct((T, H * 128), BF16)],
                          compiler_params=_cp(1))(q_raw, kv, pa, *tabs, gq, gk)


def _mla_prep_bwd(dq, dk, dv, q_raw, kv, pa, tabs, gq, gk, H, kr_block, tm=256):
    T = q_raw.shape[0]
    tm = _tile(T, tm)
    scale = MLA_QK ** -0.5
    P = MLA_PAD

    def body(dq_ref, dk_ref, dv_ref, q_ref, kv_ref, kr_ref, c_ref, sa_ref, sb_ref, gq_ref, gk_ref,
             dqr, dkv, dkr, dgq, dgk):
        c, sa, sb = c_ref[...], sa_ref[...], sb_ref[...]
        kr = kr_ref[...]
        kr2 = _sumsq(kr)
        gql, gqh, gkl, gkh = gq_ref[:, :128], gq_ref[:, 128:], gk_ref[:, :128], gk_ref[:, 128:]
        dkr_acc = jnp.zeros((tm, 128), F32)
        sums = [jnp.zeros((1, 128), F32) for _ in range(4)]
        for h in range(H):
            lo_s, hi_s = slice(h * P, h * P + 128), slice(h * P + 128, (h + 1) * P)
            lo, hi = q_ref[:, lo_s], q_ref[:, hi_s]
            r = lax.rsqrt((_sumsq(lo) + _sumsq(hi)) * (1.0 / MLA_QK) + EPS)
            ql, qh = lo * r, hi * r
            dyl = dq_ref[:, lo_s] * scale
            dyh = _rope_t(dq_ref[:, hi_s] * scale, c, sa, sb, 32)
            gl, gh = dyl * gql, dyh * gqh
            mean = (jnp.sum(gl * ql, axis=-1, keepdims=True) + jnp.sum(gh * qh, axis=-1, keepdims=True)) * (1.0 / MLA_QK)
            dqr[:, lo_s] = (r * (gl - ql * mean)).astype(BF16)
            dqr[:, hi_s] = (r * (gh - qh * mean)).astype(BF16)
            sums[0] += jnp.sum(dyl * ql, axis=0, keepdims=True)
            sums[1] += jnp.sum(dyh * qh, axis=0, keepdims=True)
            kn = kv_ref[:, lo_s]
            r = lax.rsqrt((_sumsq(kn) + kr2) * (1.0 / MLA_QK) + EPS)
            kl, kh = kn * r, kr * r
            dkl = dk_ref[:, lo_s]
            dkh = _rope_t(dk_ref[:, hi_s], c, sa, sb, 32)
            gl, gh = dkl * gkl, dkh * gkh
            mean = (jnp.sum(gl * kl, axis=-1, keepdims=True) + jnp.sum(gh * kh, axis=-1, keepdims=True)) * (1.0 / MLA_QK)
            dkv[:, lo_s] = (r * (gl - kl * mean)).astype(BF16)
            dkr_acc += r * (gh - kh * mean)
            dkv[:, hi_s] = dv_ref[:, h * 128:(h + 1) * 128].astype(BF16)
            sums[2] += jnp.sum(dkl * kl, axis=0, keepdims=True)
            sums[3] += jnp.sum(dkh * kh, axis=0, keepdims=True)
        dkr[...] = dkr_acc

        @pl.when(pl.program_id(0) == 0)
        def _():
            dgq[...] = jnp.zeros_like(dgq)
            dgk[...] = jnp.zeros_like(dgk)

        dgq[:, :128] += sums[0]
        dgq[:, 128:] += sums[1]
        dgk[:, :128] += sums[2]
        dgk[:, 128:] += sums[3]

    wide = pl.BlockSpec((tm, H * P), lambda i: (i, 0))
    lane = pl.BlockSpec((tm, 128), lambda i: (i, 0))
    vec = pl.BlockSpec((1, P), lambda i: (0, 0))
    return pl.pallas_call(body, name="mla_prep_bwd", grid=(T // tm,),
                          in_specs=[wide, wide, pl.BlockSpec((tm, H * 128), lambda i: (i, 0)), wide, wide,
                                    pl.BlockSpec((tm, 128), lambda i: (i, kr_block)), lane, lane, lane, vec, vec],
                          out_specs=[wide, wide, lane, vec, vec],
                          out_shape=[jax.ShapeDtypeStruct((T, H * P), BF16), jax.ShapeDtypeStruct((T, H * P), BF16),
                                     jax.ShapeDtypeStruct((T, 128), F32), jax.ShapeDtypeStruct((1, P), F32),
                                     jax.ShapeDtypeStruct((1, P), F32)],
                          compiler_params=_cp(1))(dq, dk, dv, q_raw, kv, pa, *tabs, gq, gk)


def _dil_prep_fwd(pdil, tabs, gq, gk, G, H, tm=256):
    T = pdil.shape[0]
    tm = _tile(T, tm)
    scale = HEAD ** -0.5

    def body(p_ref, c_ref, sa_ref, sb_ref, gq_ref, gk_ref, qo, ko):
        c, sa, sb = c_ref[...], sa_ref[...], sb_ref[...]
        for h in range(H):
            q = p_ref[:, h * 128:(h + 1) * 128].astype(F32)
            qo[:, h * 128:(h + 1) * 128] = (_rope(q * _rms_stats(q, HEAD) * gq_ref[...], c, sa, sb, 16) * scale).astype(BF16)
            k = p_ref[:, (H + h) * 128:(H + h + 1) * 128].astype(F32)
            ko[:, h * 128:(h + 1) * 128] = _rope(k * _rms_stats(k, HEAD) * gk_ref[...], c, sa, sb, 16).astype(BF16)

    lane = pl.BlockSpec((tm, 128), lambda g, i: (i, 0))
    gain = pl.BlockSpec((None, 1, 128), lambda g, i: (g, 0, 0))
    out = pl.BlockSpec((tm, H * 128), lambda g, i: (i, g))
    return pl.pallas_call(body, name="dil_prep_fwd", grid=(G, T // tm),
                          in_specs=[pl.BlockSpec((tm, 3 * H * 128), lambda g, i: (i, g)), lane, lane, lane, gain, gain],
                          out_specs=[out, out],
                          out_shape=[jax.ShapeDtypeStruct((T, G * H * 128), BF16)] * 2,
                          compiler_params=_cp(2))(pdil, *tabs, gq, gk)


def _dil_prep_bwd(dqd, dkd, dvd, pdil, tabs, gq, gk, G, H, tm=256):
    T = pdil.shape[0]
    tm = _tile(T, tm)
    scale = HEAD ** -0.5

    def body(dq_ref, dk_ref, dv_ref, p_ref, c_ref, sa_ref, sb_ref, gq_ref, gk_ref, dp_ref, dgq, dgk):
        c, sa, sb = c_ref[...], sa_ref[...], sb_ref[...]
        sq = jnp.zeros((1, 128), F32)
        sk = jnp.zeros((1, 128), F32)
        for h in range(H):
            hs = slice(h * 128, (h + 1) * 128)
            q = p_ref[:, hs].astype(F32)
            dy = _rope_t(dq_ref[:, hs] * scale, c, sa, sb, 16)
            dx, dgp = _rms_bwd_rows(dy, q, gq_ref[...], HEAD)
            dp_ref[:, hs] = dx.astype(BF16)
            sq += jnp.sum(dgp, axis=0, keepdims=True)
            ks = slice((H + h) * 128, (H + h + 1) * 128)
            k = p_ref[:, ks].astype(F32)
            dy = _rope_t(dk_ref[:, hs], c, sa, sb, 16)
            dx, dgp = _rms_bwd_rows(dy, k, gk_ref[...], HEAD)
            dp_ref[:, ks] = dx.astype(BF16)
            sk += jnp.sum(dgp, axis=0, keepdims=True)
            dp_ref[:, (2 * H + h) * 128:(2 * H + h + 1) * 128] = dv_ref[:, hs].astype(BF16)

        @pl.when(pl.program_id(1) == 0)
        def _():
            dgq[...] = jnp.zeros_like(dgq)
            dgk[...] = jnp.zeros_like(dgk)

        dgq[...] += sq
        dgk[...] += sk

    lane = pl.BlockSpec((tm, 128), lambda g, i: (i, 0))
    gain = pl.BlockSpec((None, 1, 128), lambda g, i: (g, 0, 0))
    grp = pl.BlockSpec((tm, H * 128), lambda g, i: (i, g))
    big = pl.BlockSpec((tm, 3 * H * 128), lambda g, i: (i, g))
    return pl.pallas_call(body, name="dil_prep_bwd", grid=(G, T // tm),
                          in_specs=[grp, grp, grp, big, lane, lane, lane, gain, gain],
                          out_specs=[big, gain, gain],
                          out_shape=[jax.ShapeDtypeStruct(pdil.shape, BF16), jax.ShapeDtypeStruct((G, 1, 128), F32),
                                     jax.ShapeDtypeStruct((G, 1, 128), F32)],
                          compiler_params=_cp(2))(dqd, dkd, dvd, pdil, *tabs, gq, gk)


def _attn_steps(nq, groups, blk):
    steps = []
    for g, (win, dil) in enumerate(groups):
        nb = nq if win is None else min(nq, -(-win // blk) + 1)
        for b in range(nb):
            steps.append((g, b, dil, NO_WINDOW if win is None else win, int(b == 0), int(b == nb - 1)))
    return [jnp.asarray(col, jnp.int32) for col in np.array(steps, np.int32).T]


def _valid(back, dil, win, blk, q_axis):
    qi = lax.broadcasted_iota(jnp.int32, (blk, blk), q_axis)
    ki = lax.broadcasted_iota(jnp.int32, (blk, blk), 1 - q_axis)
    delta = back * blk + (qi - ki)
    return (delta >= 0) & (delta <= win) & ((delta & (dil - 1)) == 0)


def _attn_fwd(name, q, k, v, H, wq, v_base, v_stride, groups, blk):
    T = q.shape[0]
    nq = T // blk
    tabs = _attn_steps(nq, groups, blk)
    ns = int(tabs[0].shape[0])

    def body(sg, sb, sd, sw, sf, sl, q_ref, k_ref, v_ref, o_ref, lse_ref, m_sc, l_sc, acc_sc):
        i, s = pl.program_id(1), pl.program_id(2)

        @pl.when(s == 0)
        def _():
            m_sc[...] = jnp.full_like(m_sc, NEG)
            l_sc[...] = jnp.zeros_like(l_sc)
            acc_sc[...] = jnp.zeros_like(acc_sc)

        back = sb[s]

        @pl.when(i - back >= 0)
        def _():
            sc = _dot(q_ref[...], k_ref[...], "nt")
            ok = _valid(back, sd[s], sw[s], blk, 0)
            sc = jnp.where(ok, sc, NEG)
            m_prev = m_sc[...]
            m_new = jnp.maximum(m_prev, jnp.max(sc, axis=-1, keepdims=True))
            p = jnp.where(ok, jnp.exp(sc - m_new), 0.0)
            alpha = jnp.exp(m_prev - m_new)
            l_sc[...] = alpha * l_sc[...] + jnp.sum(p, axis=-1, keepdims=True)
            acc_sc[...] = alpha * acc_sc[...] + _dot(p, v_ref[...], "nn")
            m_sc[...] = m_new

        @pl.when(s == ns - 1)
        def _():
            o_ref[...] = (acc_sc[...] / l_sc[...]).astype(BF16)
            lse_ref[...] = m_sc[...] + jnp.log(l_sc[...])

    def kv_row(i, s, sb):
        return jnp.maximum(i - sb[s], 0)

    grid_spec = pltpu.PrefetchScalarGridSpec(
        num_scalar_prefetch=6, grid=(H, nq, ns),
        in_specs=[pl.BlockSpec((blk, wq), lambda h, i, s, sg, sb, *_: (i, sg[s] * H + h)),
                  pl.BlockSpec((blk, wq), lambda h, i, s, sg, sb, *_: (kv_row(i, s, sb), sg[s] * H + h)),
                  pl.BlockSpec((blk, HEAD), lambda h, i, s, sg, sb, *_: (kv_row(i, s, sb), v_base + sg[s] * v_stride + h))],
        out_specs=[pl.BlockSpec((blk, HEAD), lambda h, i, s, *_: (i, h)),
                   pl.BlockSpec((None, blk, 1), lambda h, i, s, *_: (h, i, 0))],
        scratch_shapes=[pltpu.VMEM((blk, 1), F32), pltpu.VMEM((blk, 1), F32), pltpu.VMEM((blk, HEAD), F32)])
    return pl.pallas_call(body, name=name, grid_spec=grid_spec,
                          out_shape=[jax.ShapeDtypeStruct((T, H * HEAD), BF16), jax.ShapeDtypeStruct((H, T, 1), F32)],
                          compiler_params=_cp(3))(*tabs, q, k, v)


def _attn_dq(name, q, k, v, do, o, lse, H, wq, v_base, v_stride, groups, blk):
    T = q.shape[0]
    nq = T // blk
    tabs = _attn_steps(nq, groups, blk)
    ns = int(tabs[0].shape[0])

    def body(sg, sb, sd, sw, sf, sl, q_ref, k_ref, v_ref, do_ref, o_ref, lse_ref, dq_ref, dl_ref, acc_sc, dl_sc):
        i, s = pl.program_id(1), pl.program_id(2)

        @pl.when(s == 0)
        def _():
            d = jnp.sum(do_ref[...].astype(F32) * o_ref[...].astype(F32), axis=-1, keepdims=True)
            dl_sc[...] = d
            dl_ref[...] = d

        @pl.when(sf[s] == 1)
        def _():
            acc_sc[...] = jnp.zeros_like(acc_sc)

        back = sb[s]

        @pl.when(i - back >= 0)
        def _():
            sc = _dot(q_ref[...], k_ref[...], "nt")
            ok = _valid(back, sd[s], sw[s], blk, 0)
            p = jnp.where(ok, jnp.exp(sc - lse_ref[...]), 0.0)
            dp = _dot(do_ref[...], v_ref[...], "nt")
            ds = p * (dp - dl_sc[...])
            acc_sc[...] += _dot(ds, k_ref[...], "nn")

        @pl.when(sl[s] == 1)
        def _():
            dq_ref[...] = acc_sc[...]

    def kv_row(i, s, sb):
        return jnp.maximum(i - sb[s], 0)

    col = pl.BlockSpec((None, blk, 1), lambda h, i, s, *_: (h, i, 0))
    oh = pl.BlockSpec((blk, HEAD), lambda h, i, s, *_: (i, h))
    grid_spec = pltpu.PrefetchScalarGridSpec(
        num_scalar_prefetch=6, grid=(H, nq, ns),
        in_specs=[pl.BlockSpec((blk, wq), lambda h, i, s, sg, sb, *_: (i, sg[s] * H + h)),
                  pl.BlockSpec((blk, wq), lambda h, i, s, sg, sb, *_: (kv_row(i, s, sb), sg[s] * H + h)),
                  pl.BlockSpec((blk, HEAD), lambda h, i, s, sg, sb, *_: (kv_row(i, s, sb), v_base + sg[s] * v_stride + h)),
                  oh, oh, col],
        out_specs=[pl.BlockSpec((blk, wq), lambda h, i, s, sg, *_: (i, sg[s] * H + h)), col],
        scratch_shapes=[pltpu.VMEM((blk, wq), F32), pltpu.VMEM((blk, 1), F32)])
    G = len(groups)
    return pl.pallas_call(body, name=name, grid_spec=grid_spec,
                          out_shape=[jax.ShapeDtypeStruct((T, G * H * wq), F32), jax.ShapeDtypeStruct((H, T, 1), F32)],
                          compiler_params=_cp(3))(*tabs, q, k, v, do, o, lse)


def _attn_dkv(name, q, k, v, do, lse_row, dl_row, H, wq, v_base, v_stride, groups, blk):
    T = q.shape[0]
    nq = T // blk
    tabs = _attn_steps(nq, groups, blk)
    ns = int(tabs[0].shape[0])

    def body(sg, sb, sd, sw, sf, sl, q_ref, k_ref, v_ref, do_ref, lse_ref, dl_ref, dk_ref, dv_ref, dk_sc, dv_sc):
        j, s = pl.program_id(1), pl.program_id(2)

        @pl.when(sf[s] == 1)
        def _():
            dk_sc[...] = jnp.zeros_like(dk_sc)
            dv_sc[...] = jnp.zeros_like(dv_sc)

        back = sb[s]

        @pl.when(j + back < nq)
        def _():
            st = _dot(k_ref[...], q_ref[...], "nt")
            ok = _valid(back, sd[s], sw[s], blk, 1)
            pt = jnp.where(ok, jnp.exp(st - lse_ref[...]), 0.0)
            dpt = _dot(v_ref[...], do_ref[...], "nt")
            dst = pt * (dpt - dl_ref[...])
            dv_sc[...] += _dot(pt, do_ref[...], "nn")
            dk_sc[...] += _dot(dst, q_ref[...], "nn")

        @pl.when(sl[s] == 1)
        def _():
            dk_ref[...] = dk_sc[...]
            dv_ref[...] = dv_sc[...]

    def q_row(j, s, sb):
        return jnp.minimum(j + sb[s], nq - 1)

    row = pl.BlockSpec((None, 1, blk), lambda h, j, s, sg, sb, *_: (h, 0, q_row(j, s, sb)))
    grid_spec = pltpu.PrefetchScalarGridSpec(
        num_scalar_prefetch=6, grid=(H, nq, ns),
        in_specs=[pl.BlockSpec((blk, wq), lambda h, j, s, sg, sb, *_: (q_row(j, s, sb), sg[s] * H + h)),
                  pl.BlockSpec((blk, wq), lambda h, j, s, sg, *_: (j, sg[s] * H + h)),
                  pl.BlockSpec((blk, HEAD), lambda h, j, s, sg, *_: (j, v_base + sg[s] * v_stride + h)),
                  pl.BlockSpec((blk, HEAD), lambda h, j, s, sg, sb, *_: (q_row(j, s, sb), h)),
                  row, row],
        out_specs=[pl.BlockSpec((blk, wq), lambda h, j, s, sg, *_: (j, sg[s] * H + h)),
                   pl.BlockSpec((blk, HEAD), lambda h, j, s, sg, *_: (j, sg[s] * H + h))],
        scratch_shapes=[pltpu.VMEM((blk, wq), F32), pltpu.VMEM((blk, HEAD), F32)])
    G = len(groups)
    return pl.pallas_call(body, name=name, grid_spec=grid_spec,
                          out_shape=[jax.ShapeDtypeStruct((T, G * H * wq), F32), jax.ShapeDtypeStruct((T, G * H * HEAD), F32)],
                          compiler_params=_cp(3))(*tabs, q, k, v, do, lse_row, dl_row)


def _causal_mask(blk, q_axis):
    return lax.broadcasted_iota(jnp.int32, (blk, blk), q_axis) >= lax.broadcasted_iota(jnp.int32, (blk, blk), 1 - q_axis)


def _on_blocks(s, live, step):
    @pl.when(s == 0)
    def _():
        step(True)

    @pl.when((s > 0) & live)
    def _():
        step(False)


def _mla_fwd(q, k, v, H, blk):
    T, wq = q.shape[0], q.shape[1] // H
    nq = T // blk

    def body(q_ref, k_ref, v_ref, o_ref, lse_ref, m_sc, l_sc, acc_sc):
        i, s = pl.program_id(1), pl.program_id(2)

        @pl.when(s == 0)
        def _():
            m_sc[...] = jnp.full_like(m_sc, NEG)
            l_sc[...] = jnp.zeros_like(l_sc)
            acc_sc[...] = jnp.zeros_like(acc_sc)

        def step(masked):
            sc = _dot(q_ref[...], k_ref[...], "nt")
            if masked:
                sc = jnp.where(_causal_mask(blk, 0), sc, NEG)
            m_prev = m_sc[...]
            m_new = jnp.maximum(m_prev, jnp.max(sc, axis=-1, keepdims=True))
            p = jnp.exp(sc - m_new)
            alpha = jnp.exp(m_prev - m_new)
            l_sc[...] = alpha * l_sc[...] + jnp.sum(p, axis=-1, keepdims=True)
            acc_sc[...] = alpha * acc_sc[...] + _dot(p, v_ref[...], "nn")
            m_sc[...] = m_new

        _on_blocks(s, i - s >= 0, step)

        @pl.when(s == nq - 1)
        def _():
            o_ref[...] = (acc_sc[...] / l_sc[...]).astype(BF16)
            lse_ref[...] = m_sc[...] + jnp.log(l_sc[...])

    kv_spec = lambda w: pl.BlockSpec((blk, w), lambda h, i, s: (jnp.maximum(i - s, 0), h))
    return pl.pallas_call(
        body, name="mla_fwd", grid=(H, nq, nq),
        in_specs=[pl.BlockSpec((blk, wq), lambda h, i, s: (i, h)), kv_spec(wq), kv_spec(HEAD)],
        out_specs=[pl.BlockSpec((blk, HEAD), lambda h, i, s: (i, h)), pl.BlockSpec((None, blk, 1), lambda h, i, s: (h, i, 0))],
        out_shape=[jax.ShapeDtypeStruct((T, H * HEAD), BF16), jax.ShapeDtypeStruct((H, T, 1), F32)],
        scratch_shapes=[pltpu.VMEM((blk, 1), F32), pltpu.VMEM((blk, 1), F32), pltpu.VMEM((blk, HEAD), F32)],
        compiler_params=_cp(3))(q, k, v)


def _mla_dq(q, k, v, do, lse, dl, H, blk):
    T, wq = q.shape[0], q.shape[1] // H
    nq = T // blk

    def body(q_ref, k_ref, v_ref, do_ref, lse_ref, dl_ref, dq_ref, acc_sc):
        i, s = pl.program_id(1), pl.program_id(2)

        @pl.when(s == 0)
        def _():
            acc_sc[...] = jnp.zeros_like(acc_sc)

        def step(masked):
            sc = _dot(q_ref[...], k_ref[...], "nt")
            if masked:
                sc = jnp.where(_causal_mask(blk, 0), sc, NEG)
            p = jnp.exp(sc - lse_ref[...])
            dp = _dot(do_ref[...], v_ref[...], "nt")
            acc_sc[...] += _dot(p * (dp - dl_ref[...]), k_ref[...], "nn")

        _on_blocks(s, i - s >= 0, step)

        @pl.when(s == nq - 1)
        def _():
            dq_ref[...] = acc_sc[...]

    kv_spec = lambda w: pl.BlockSpec((blk, w), lambda h, i, s: (jnp.maximum(i - s, 0), h))
    col = pl.BlockSpec((None, blk, 1), lambda h, i, s: (h, i, 0))
    return pl.pallas_call(
        body, name="mla_dq", grid=(H, nq, nq),
        in_specs=[pl.BlockSpec((blk, wq), lambda h, i, s: (i, h)), kv_spec(wq), kv_spec(HEAD),
                  pl.BlockSpec((blk, HEAD), lambda h, i, s: (i, h)), col, col],
        out_specs=pl.BlockSpec((blk, wq), lambda h, i, s: (i, h)),
        out_shape=jax.ShapeDtypeStruct((T, H * wq), F32),
        scratch_shapes=[pltpu.VMEM((blk, wq), F32)], compiler_params=_cp(3))(q, k, v, do, lse, dl)


def _mla_dkv(q, k, v, do, lse_row, dl_row, H, blk):
    T, wq = q.shape[0], q.shape[1] // H
    nq = T // blk

    def body(q_ref, k_ref, v_ref, do_ref, lse_ref, dl_ref, dk_ref, dv_ref, dk_sc, dv_sc):
        j, s = pl.program_id(1), pl.program_id(2)

        @pl.when(s == 0)
        def _():
            dk_sc[...] = jnp.zeros_like(dk_sc)
            dv_sc[...] = jnp.zeros_like(dv_sc)

        def step(masked):
            st = _dot(k_ref[...], q_ref[...], "nt")
            if masked:
                st = jnp.where(_causal_mask(blk, 1), st, NEG)
            pt = jnp.exp(st - lse_ref[...])
            dpt = _dot(v_ref[...], do_ref[...], "nt")
            dv_sc[...] += _dot(pt, do_ref[...], "nn")
            dk_sc[...] += _dot(pt * (dpt - dl_ref[...]), q_ref[...], "nn")

        _on_blocks(s, j + s < nq, step)

        @pl.when(s == nq - 1)
        def _():
            dk_ref[...] = dk_sc[...]
            dv_ref[...] = dv_sc[...]

    q_row = lambda j, s: jnp.minimum(j + s, nq - 1)
    row = pl.BlockSpec((None, 1, blk), lambda h, j, s: (h, 0, q_row(j, s)))
    return pl.pallas_call(
        body, name="mla_dkv", grid=(H, nq, nq),
        in_specs=[pl.BlockSpec((blk, wq), lambda h, j, s: (q_row(j, s), h)), pl.BlockSpec((blk, wq), lambda h, j, s: (j, h)),
                  pl.BlockSpec((blk, HEAD), lambda h, j, s: (j, h)), pl.BlockSpec((blk, HEAD), lambda h, j, s: (q_row(j, s), h)),
                  row, row],
        out_specs=[pl.BlockSpec((blk, wq), lambda h, j, s: (j, h)), pl.BlockSpec((blk, HEAD), lambda h, j, s: (j, h))],
        out_shape=[jax.ShapeDtypeStruct((T, H * wq), F32), jax.ShapeDtypeStruct((T, H * HEAD), F32)],
        scratch_shapes=[pltpu.VMEM((blk, wq), F32), pltpu.VMEM((blk, HEAD), F32)],
        compiler_params=_cp(3))(q, k, v, do, lse_row, dl_row)


def _row_dot(name, a, b, H, tm=1024):
    T = a.shape[0]
    tm = _tile(T, tm)

    def body(a_ref, b_ref, o_ref):
        o_ref[...] = jnp.sum(a_ref[...].astype(F32) * b_ref[...].astype(F32), axis=-1, keepdims=True)

    blk = pl.BlockSpec((tm, HEAD), lambda h, i: (i, h))
    return pl.pallas_call(body, name=name, grid=(H, T // tm), in_specs=[blk, blk],
                          out_specs=pl.BlockSpec((None, tm, 1), lambda h, i: (h, i, 0)),
                          out_shape=jax.ShapeDtypeStruct((H, T, 1), F32), compiler_params=_cp(2))(a, b)


def _tri(n):
    row = lax.broadcasted_iota(jnp.int32, (n, n), 0)
    col = lax.broadcasted_iota(jnp.int32, (n, n), 1)
    return col <= row, col >= row


def _win_fwd(name, q, k, pd, H, n, L, QB):
    T = q.shape[0]
    U = QB // n

    def body(q_ref, k_ref, v_ref, kp_ref, vp_ref, o_ref, lse_ref):
        i = pl.program_id(1)
        own_ok, before_ok = _tri(n)
        for e in range(U):
            rs = slice(e * n, (e + 1) * n)
            ps = slice((e - 1) * n, e * n)
            k_lo, v_lo = (kp_ref[...], vp_ref[...]) if e == 0 else (k_ref[ps, :], v_ref[ps, :])
            inside = ((i * QB + e * n) % L) != 0
            qu = q_ref[rs, :]
            s_hi = jnp.where(own_ok, _dot(qu, k_ref[rs, :], "nt"), NEG)
            s_lo = jnp.where(before_ok & inside, _dot(qu, k_lo, "nt"), NEG)
            m = jnp.maximum(jnp.max(s_hi, axis=-1, keepdims=True), jnp.max(s_lo, axis=-1, keepdims=True))
            p_hi, p_lo = jnp.exp(s_hi - m), jnp.exp(s_lo - m)
            l = jnp.sum(p_hi, axis=-1, keepdims=True) + jnp.sum(p_lo, axis=-1, keepdims=True)
            acc = _dot(p_hi, v_ref[rs, :], "nn") + _dot(p_lo, v_lo, "nn")
            o_ref[rs, :] = acc / l
            lse_ref[rs, :] = m + jnp.log(l)

    cur = lambda c0: pl.BlockSpec((QB, HEAD), lambda h, i: (i, c0 + h))
    prev = lambda c0: pl.BlockSpec((n, HEAD), lambda h, i: (jnp.maximum(i * U - 1, 0), c0 + h))
    return pl.pallas_call(
        body, name=name, grid=(H, T // QB), in_specs=[cur(0), cur(0), cur(2 * H), prev(0), prev(2 * H)],
        out_specs=[pl.BlockSpec((QB, HEAD), lambda h, i: (i, h)), pl.BlockSpec((None, QB, 1), lambda h, i: (h, i, 0))],
        out_shape=[jax.ShapeDtypeStruct((T, H * HEAD), F32), jax.ShapeDtypeStruct((H, T, 1), F32)],
        compiler_params=_cp(2))(q, k, pd, k, pd)


def _win_dq(name, q, k, pd, do, lse, dl, H, n, L, QB):
    T = q.shape[0]
    U = QB // n

    def body(q_ref, k_ref, v_ref, kp_ref, vp_ref, do_ref, lse_ref, dl_ref, dq_ref):
        i = pl.program_id(1)
        own_ok, before_ok = _tri(n)
        for e in range(U):
            rs = slice(e * n, (e + 1) * n)
            ps = slice((e - 1) * n, e * n)
            k_lo, v_lo = (kp_ref[...], vp_ref[...]) if e == 0 else (k_ref[ps, :], v_ref[ps, :])
            inside = ((i * QB + e * n) % L) != 0
            qu, dou, lse_u, dl_u = q_ref[rs, :], do_ref[rs, :], lse_ref[rs, :], dl_ref[rs, :]
            p_hi = jnp.exp(jnp.where(own_ok, _dot(qu, k_ref[rs, :], "nt"), NEG) - lse_u)
            p_lo = jnp.exp(jnp.where(before_ok & inside, _dot(qu, k_lo, "nt"), NEG) - lse_u)
            ds_hi = p_hi * (_dot(dou, v_ref[rs, :], "nt") - dl_u)
            ds_lo = p_lo * (_dot(dou, v_lo, "nt") - dl_u)
            dq_ref[rs, :] = _dot(ds_hi, k_ref[rs, :], "nn") + _dot(ds_lo, k_lo, "nn")

    cur = lambda c0: pl.BlockSpec((QB, HEAD), lambda h, i: (i, c0 + h))
    prev = lambda c0: pl.BlockSpec((n, HEAD), lambda h, i: (jnp.maximum(i * U - 1, 0), c0 + h))
    flat = pl.BlockSpec((QB, HEAD), lambda h, i: (i, h))
    col = pl.BlockSpec((None, QB, 1), lambda h, i: (h, i, 0))
    return pl.pallas_call(
        body, name=name, grid=(H, T // QB), in_specs=[cur(0), cur(0), cur(2 * H), prev(0), prev(2 * H), flat, col, col],
        out_specs=flat, out_shape=jax.ShapeDtypeStruct((T, H * HEAD), F32), compiler_params=_cp(2))(q, k, pd, k, pd, do, lse, dl)


def _win_dkv(name, q, k, pd, do, lse_row, dl_row, H, n, L, QB):
    T = q.shape[0]
    U = QB // n
    last = T // n - 1

    def body(q_ref, k_ref, v_ref, do_ref, lse_ref, dl_ref, qn_ref, don_ref, lsen_ref, dln_ref, dk_ref, dv_ref):
        i = pl.program_id(1)
        own_ok, after_ok = _tri_t(n)
        for e in range(U):
            rs = slice(e * n, (e + 1) * n)
            ns = slice((e + 1) * n, (e + 2) * n)
            if e < U - 1:
                qn, don, lsen, dln = q_ref[ns, :], do_ref[ns, :], lse_ref[:, ns], dl_ref[:, ns]
            else:
                qn, don, lsen, dln = qn_ref[...], don_ref[...], lsen_ref[...], dln_ref[...]
            nxt = i * QB + (e + 1) * n
            inside = ((nxt % L) != 0) & (nxt < T)
            ku, vu, qu, dou = k_ref[rs, :], v_ref[rs, :], q_ref[rs, :], do_ref[rs, :]
            pt_own = jnp.exp(jnp.where(own_ok, _dot(ku, qu, "nt"), NEG) - lse_ref[:, rs])
            pt_aft = jnp.exp(jnp.where(after_ok & inside, _dot(ku, qn, "nt"), NEG) - lsen)
            dst_own = pt_own * (_dot(vu, dou, "nt") - dl_ref[:, rs])
            dst_aft = pt_aft * (_dot(vu, don, "nt") - dln)
            dv_ref[rs, :] = _dot(pt_own, dou, "nn") + _dot(pt_aft, don, "nn")
            dk_ref[rs, :] = _dot(dst_own, qu, "nn") + _dot(dst_aft, qn, "nn")

    cur = lambda c0: pl.BlockSpec((QB, HEAD), lambda h, i: (i, c0 + h))
    flat = pl.BlockSpec((QB, HEAD), lambda h, i: (i, h))
    row = pl.BlockSpec((None, 1, QB), lambda h, i: (h, 0, i))
    nxt_unit = lambda h, i: jnp.minimum((i + 1) * U, last)
    return pl.pallas_call(
        body, name=name, grid=(H, T // QB),
        in_specs=[cur(0), cur(0), cur(2 * H), flat, row, row,
                  pl.BlockSpec((n, HEAD), lambda h, i: (nxt_unit(h, i), h)),
                  pl.BlockSpec((n, HEAD), lambda h, i: (nxt_unit(h, i), h)),
                  pl.BlockSpec((None, 1, n), lambda h, i: (h, 0, nxt_unit(h, i))),
                  pl.BlockSpec((None, 1, n), lambda h, i: (h, 0, nxt_unit(h, i)))],
        out_specs=[flat, flat], out_shape=[jax.ShapeDtypeStruct((T, H * HEAD), F32)] * 2,
        compiler_params=_cp(2))(q, k, pd, do, lse_row, dl_row, q, do, lse_row, dl_row)


def _tri_t(n):
    key = lax.broadcasted_iota(jnp.int32, (n, n), 0)
    qry = lax.broadcasted_iota(jnp.int32, (n, n), 1)
    return key <= qry, key >= qry


def _merge_groups(os_, lses, H, tm=512):
    G = len(os_)
    T = os_[0].shape[0]
    tm = _tile(T, tm)

    def body(*refs):
        o_refs, l_refs, o_out, lse_out = refs[:G], refs[G:2 * G], refs[2 * G], refs[2 * G + 1]
        ls = [r[...] for r in l_refs]
        m = ls[0]
        for x in ls[1:]:
            m = jnp.maximum(m, x)
        ws = [jnp.exp(x - m) for x in ls]
        tot = ws[0]
        for x in ws[1:]:
            tot = tot + x
        acc = ws[0] * o_refs[0][...]
        for x, r in zip(ws[1:], o_refs[1:]):
            acc = acc + x * r[...]
        o_out[...] = (acc / tot).astype(BF16)
        lse_out[...] = m + jnp.log(tot)

    flat = pl.BlockSpec((tm, HEAD), lambda h, i: (i, h))
    col = pl.BlockSpec((None, tm, 1), lambda h, i: (h, i, 0))
    return pl.pallas_call(body, name="dil_merge", grid=(H, T // tm), in_specs=[flat] * G + [col] * G, out_specs=[flat, col],
                          out_shape=[jax.ShapeDtypeStruct((T, H * HEAD), BF16), jax.ShapeDtypeStruct((H, T, 1), F32)],
                          compiler_params=_cp(2))(*os_, *lses)


def _dil_prep_fwd_pm(name, pd, tabs, gq, gk, H, tm=256):
    T = pd.shape[0]
    tm = _tile(T, tm)
    scale = HEAD ** -0.5

    def body(p_ref, c_ref, sa_ref, sb_ref, gq_ref, gk_ref, qo, ko):
        c, sa, sb = c_ref[...], sa_ref[...], sb_ref[...]
        for h in range(H):
            q = p_ref[:, h * 128:(h + 1) * 128].astype(F32)
            qo[:, h * 128:(h + 1) * 128] = (_rope(q * _rms_stats(q, HEAD) * gq_ref[...], c, sa, sb, 16) * scale).astype(BF16)
            k = p_ref[:, (H + h) * 128:(H + h + 1) * 128].astype(F32)
            ko[:, h * 128:(h + 1) * 128] = _rope(k * _rms_stats(k, HEAD) * gk_ref[...], c, sa, sb, 16).astype(BF16)

    lane = pl.BlockSpec((tm, 128), lambda i: (i, 0))
    gain = pl.BlockSpec((1, 128), lambda i: (0, 0))
    out = pl.BlockSpec((tm, H * 128), lambda i: (i, 0))
    return pl.pallas_call(body, name=name, grid=(T // tm,),
                          in_specs=[pl.BlockSpec((tm, 3 * H * 128), lambda i: (i, 0)), lane, lane, lane, gain, gain],
                          out_specs=[out, out], out_shape=[jax.ShapeDtypeStruct((T, H * 128), BF16)] * 2,
                          compiler_params=_cp(1))(pd, *tabs, gq, gk)


def _dil_prep_bwd_pm(name, dq, dk, dv, pd, tabs, gq, gk, H, tm=256):
    T = pd.shape[0]
    tm = _tile(T, tm)
    scale = HEAD ** -0.5

    def body(dq_ref, dk_ref, dv_ref, p_ref, c_ref, sa_ref, sb_ref, gq_ref, gk_ref, dp_ref, dgq, dgk):
        c, sa, sb = c_ref[...], sa_ref[...], sb_ref[...]
        sq = jnp.zeros((1, 128), F32)
        sk = jnp.zeros((1, 128), F32)
        for h in range(H):
            hs = slice(h * 128, (h + 1) * 128)
            q = p_ref[:, hs].astype(F32)
            dx, dgp = _rms_bwd_rows(_rope_t(dq_ref[:, hs] * scale, c, sa, sb, 16), q, gq_ref[...], HEAD)
            dp_ref[:, hs] = dx.astype(BF16)
            sq += jnp.sum(dgp, axis=0, keepdims=True)
            ks = slice((H + h) * 128, (H + h + 1) * 128)
            k = p_ref[:, ks].astype(F32)
            dx, dgp = _rms_bwd_rows(_rope_t(dk_ref[:, hs], c, sa, sb, 16), k, gk_ref[...], HEAD)
            dp_ref[:, ks] = dx.astype(BF16)
            sk += jnp.sum(dgp, axis=0, keepdims=True)
            dp_ref[:, (2 * H + h) * 128:(2 * H + h + 1) * 128] = dv_ref[:, hs].astype(BF16)

        @pl.when(pl.program_id(0) == 0)
        def _():
            dgq[...] = jnp.zeros_like(dgq)
            dgk[...] = jnp.zeros_like(dgk)

        dgq[...] += sq
        dgk[...] += sk

    lane = pl.BlockSpec((tm, 128), lambda i: (i, 0))
    flat = pl.BlockSpec((tm, H * 128), lambda i: (i, 0))
    vec = pl.BlockSpec((1, 128), lambda i: (0, 0))
    return pl.pallas_call(body, name=name, grid=(T // tm,),
                          in_specs=[flat, flat, flat, pl.BlockSpec((tm, 3 * H * 128), lambda i: (i, 0)),
                                    lane, lane, lane, vec, vec],
                          out_specs=[pl.BlockSpec((tm, 3 * H * 128), lambda i: (i, 0)), vec, vec],
                          out_shape=[jax.ShapeDtypeStruct((T, 3 * H * 128), BF16), jax.ShapeDtypeStruct((1, 128), F32),
                                     jax.ShapeDtypeStruct((1, 128), F32)],
                          compiler_params=_cp(1))(dq, dk, dv, pd, *tabs, gq, gk)


def _to_phase(a, d, axis=0):
    if d == 1:
        return a
    sh = a.shape
    T = sh[axis]
    b = a.reshape(*sh[:axis], T // d, d, *sh[axis + 1:])
    return jnp.swapaxes(b, axis, axis + 1).reshape(sh)


def _from_phase(a, d, axis=0):
    if d == 1:
        return a
    sh = a.shape
    T = sh[axis]
    b = a.reshape(*sh[:axis], d, T // d, *sh[axis + 1:])
    return jnp.swapaxes(b, axis, axis + 1).reshape(sh)


def _ffn_up(name, n, wg, wu, tm=512):
    T, D = n.shape
    nd, _, fc = wg.shape
    tm = _tile(T, tm)

    def epilogue(accs, in_refs, out_refs):
        a, b = accs
        out_refs[0][...] = a.astype(BF16)
        out_refs[1][...] = b.astype(BF16)
        out_refs[2][...] = (a * _sigmoid(a) * b).astype(BF16)

    w_spec = pl.BlockSpec((None, D, fc), lambda j, i: (j, 0, 0))
    o_spec = pl.BlockSpec((None, tm, fc), lambda j, i: (j, i, 0))
    sh = jax.ShapeDtypeStruct((nd, T, fc), BF16)
    return _mm(name, (nd, T // tm), [n, wg, wu], [pl.BlockSpec((tm, D), lambda j, i: (i, 0)), w_spec, w_spec],
               [(0, 1, "nn", 0), (0, 2, "nn", 1)], 2, None, epilogue, [sh, sh, sh], [o_spec] * 3, None)


def _ffn_down(name, s, wd, res, tm=512):
    nd, T, fc = s.shape
    D = wd.shape[2]
    tm = _tile(T, tm)

    def epilogue(accs, in_refs, out_refs):
        out_refs[0][...] = in_refs[2][...] + 0.5 * accs[0]

    row = pl.BlockSpec((tm, D), lambda i, k: (i, 0))
    return _mm(name, (T // tm, nd), [s, wd, res],
               [pl.BlockSpec((None, tm, fc), lambda i, k: (k, i, 0)), pl.BlockSpec((None, fc, D), lambda i, k: (k, 0, 0)), row],
               [(0, 1, "nn", 0)], 1, (tm, D), epilogue, [jax.ShapeDtypeStruct((T, D), F32)], [row], 1)[0]


def _ffn_bwd_act(name, dxb, wd, a, b, tm=512):
    T, D = dxb.shape
    nd, fc, _ = wd.shape
    tm = _tile(T, tm)

    def epilogue(accs, in_refs, out_refs):
        ds = 0.5 * accs[0]
        av, bv = in_refs[2][...].astype(F32), in_refs[3][...].astype(F32)
        sg = _sigmoid(av)
        out_refs[0][...] = (ds * bv * sg * (1.0 + av * (1.0 - sg))).astype(BF16)
        out_refs[1][...] = (ds * av * sg).astype(BF16)

    act = pl.BlockSpec((None, tm, fc), lambda j, i: (j, i, 0))
    sh = jax.ShapeDtypeStruct((nd, T, fc), BF16)
    return _mm(name, (nd, T // tm), [dxb, wd, a, b],
               [pl.BlockSpec((tm, D), lambda j, i: (i, 0)), pl.BlockSpec((None, fc, D), lambda j, i: (j, 0, 0)), act, act],
               [(0, 1, "nt", 0)], 1, None, epilogue, [sh, sh], [act, act], None)


def _ffn_dwd(name, s, dxb, tk=512):
    nd, T, fc = s.shape
    D = dxb.shape[1]
    tk = _tile(T, tk)

    def epilogue(accs, in_refs, out_refs):
        out_refs[0][...] = (0.5 * accs[0]).astype(BF16)

    return _mm(name, (nd, T // tk), [s, dxb],
               [pl.BlockSpec((None, tk, fc), lambda j, k: (j, k, 0)), pl.BlockSpec((tk, D), lambda j, k: (k, 0))],
               [(0, 1, "tn", 0)], 1, (fc, D), epilogue, [jax.ShapeDtypeStruct((nd, fc, D), BF16)],
               [pl.BlockSpec((None, fc, D), lambda j, k: (j, 0, 0))], 1)[0]


def _ffn_dwgu(name, n, da, db, tk=512):
    T, D = n.shape
    nd, _, fc = da.shape
    tk = _tile(T, tk)

    def epilogue(accs, in_refs, out_refs):
        out_refs[0][...] = accs[0].astype(BF16)
        out_refs[1][...] = accs[1].astype(BF16)

    act = pl.BlockSpec((None, tk, fc), lambda j, k: (j, k, 0))
    out = pl.BlockSpec((None, D, fc), lambda j, k: (j, 0, 0))
    sh = jax.ShapeDtypeStruct((nd, D, fc), BF16)
    return _mm(name, (nd, T // tk), [n, da, db], [pl.BlockSpec((tk, D), lambda j, k: (k, 0)), act, act],
               [(0, 1, "tn", 0), (0, 2, "tn", 1)], 2, (D, fc), epilogue, [sh, sh], [out, out], 1)


def _ffn_dn(name, da, db, wg, wu, tm=512):
    nd, T, fc = da.shape
    D = wg.shape[1]
    tm = _tile(T, tm)

    def epilogue(accs, in_refs, out_refs):
        out_refs[0][...] = accs[0]

    act = pl.BlockSpec((None, tm, fc), lambda i, k: (k, i, 0))
    w_spec = pl.BlockSpec((None, D, fc), lambda i, k: (k, 0, 0))
    row = pl.BlockSpec((tm, D), lambda i, k: (i, 0))
    return _mm(name, (T // tm, nd), [da, db, wg, wu], [act, act, w_spec, w_spec],
               [(0, 2, "nt", 0), (1, 3, "nt", 0)], 1, (tm, D), epilogue, [jax.ShapeDtypeStruct((T, D), F32)], [row], 1)[0]


def _ffn_forward(tag, x, g, wg, wu, wd):
    n = _rmsnorm_fwd(tag + "_norm", x, g)
    a, b, s = _ffn_up(tag + "_up", n, wg, wu)
    return _ffn_down(tag + "_down", s, wd, x), (n, a, b, s)


def _ffn_backward(tag, dx, dxb, x, g, wg, wu, wd, saved):
    n, a, b, s = saved
    da, db = _ffn_bwd_act(tag + "_bwd_act", dxb, wd, a, b)
    d_wd = _ffn_dwd(tag + "_dwd", s, dxb)
    d_wg, d_wu = _ffn_dwgu(tag + "_dwgu", n, da, db)
    dn = _ffn_dn(tag + "_dn", da, db, wg, wu)
    dx_in, dxb_in, dg = _rmsnorm_bwd(tag + "_norm_bwd", dn, x, g, dx)
    return dx_in, dxb_in, dg, d_wg, d_wu, d_wd


def _place():
    x, y, c = lax.axis_index("x"), lax.axis_index("y"), lax.axis_index("c")
    return x, y, c, [(1 - x, y), (x, 1 - y), (1 - x, 1 - y)]


def _allgather(name, shards):
    n = len(shards)

    def body(*refs):
        ins, outs = refs[:n], refs[n:2 * n]
        send_sems, recv_sems, local_sems = refs[2 * n:]
        x, y, c, chips = _place()
        me, sibling = (x, y, c), (x, y, 1 - c)

        def slot(a, p):
            return outs[a].at[4 * p[0] + 2 * p[1] + p[2]]

        def copy(a, kk, block, to, src=None):
            return pltpu.make_async_remote_copy(
                src_ref=slot(a, block) if src is None else src, dst_ref=slot(a, block),
                send_sem=send_sems.at[a * 7 + kk], recv_sem=recv_sems.at[a * 7 + kk],
                device_id=to, device_id_type=MESH)

        mine = [pltpu.make_async_copy(ins[a], slot(a, me), local_sems.at[a]) for a in range(n)]
        for cp in mine:
            cp.start()
        first = []
        for a in range(n):
            first.append(copy(a, 0, me, sibling, src=ins[a]))
            first += [copy(a, 1 + j, me, (*chip, c), src=ins[a]) for j, chip in enumerate(chips)]
        for cp in first:
            cp.start()
        passed = []
        for j, chip in enumerate(chips):
            for a in range(n):
                copy(a, 1 + j, (*chip, c), me).wait_recv()
                fwd = copy(a, 4 + j, (*chip, c), sibling)
                fwd.start()
                passed.append(fwd)
        for a in range(n):
            copy(a, 0, sibling, me).wait_recv()
        for j, chip in enumerate(chips):
            for a in range(n):
                copy(a, 4 + j, (*chip, 1 - c), me).wait_recv()
        for cp in first + passed:
            cp.wait_send()
        for cp in mine:
            cp.wait()

    return pl.pallas_call(
        body, name=name, in_specs=[HBM_SPEC] * n, out_specs=[HBM_SPEC] * n,
        out_shape=[jax.ShapeDtypeStruct((N_DEV, *s.shape), s.dtype) for s in shards],
        scratch_shapes=[pltpu.SemaphoreType.DMA((7 * n,)), pltpu.SemaphoreType.DMA((7 * n,)), pltpu.SemaphoreType.DMA((n,))],
    )(*shards)


def _rs_sibling(name, grads):
    n = len(grads)

    def body(*refs):
        ins, outs = refs[:n], refs[n:2 * n]
        send_sems, recv_sems = refs[2 * n:]
        x, y, c, _ = _place()
        copies = [pltpu.make_async_remote_copy(
            src_ref=ins[a].at[:, 1 - c], dst_ref=outs[a], send_sem=send_sems.at[a], recv_sem=recv_sems.at[a],
            device_id=(x, y, 1 - c), device_id_type=MESH) for a in range(n)]
        for cp in copies:
            cp.start()
        for cp in copies:
            cp.wait()

    return pl.pallas_call(
        body, name=name, in_specs=[HBM_SPEC] * n, out_specs=[HBM_SPEC] * n,
        out_shape=[jax.ShapeDtypeStruct((g.shape[0], *g.shape[2:]), g.dtype) for g in grads],
        scratch_shapes=[pltpu.SemaphoreType.DMA((n,)), pltpu.SemaphoreType.DMA((n,))],
    )(*grads)


def _pair_add(name, own, got, core):
    nch, _, K, N = own.shape
    tr = _row_tile(K, N)

    def body(c_ref, own_ref, got_ref, o_ref):
        o_ref[...] = (own_ref[...].astype(F32) + got_ref[...].astype(F32)).astype(o_ref.dtype)

    grid_spec = pltpu.PrefetchScalarGridSpec(
        num_scalar_prefetch=1, grid=(nch, K // tr),
        in_specs=[pl.BlockSpec((None, None, tr, N), lambda k, r, c_ref: (k, c_ref[0], r, 0)),
                  pl.BlockSpec((None, tr, N), lambda k, r, c_ref: (k, r, 0))],
        out_specs=pl.BlockSpec((None, tr, N), lambda k, r, c_ref: (k, r, 0)))
    return pl.pallas_call(body, name=name, grid_spec=grid_spec, out_shape=jax.ShapeDtypeStruct((nch, K, N), own.dtype),
                          compiler_params=_cp(2))(core, own, got)


def _rs_chips(name, sums):
    n = len(sums)

    def body(*refs):
        ins, outs = refs[:n], refs[n:2 * n]
        send_sems, recv_sems, local_sems = refs[2 * n:]
        x, y, c, chips = _place()
        k_me = 2 * x + y
        mine = [pltpu.make_async_copy(ins[a].at[k_me], outs[a].at[k_me], local_sems.at[a]) for a in range(n)]
        for cp in mine:
            cp.start()

        def copy(a, r, slot):
            chip = chips[r]
            k_peer = 2 * chip[0] + chip[1]
            return pltpu.make_async_remote_copy(
                src_ref=ins[a].at[k_peer], dst_ref=outs[a].at[k_me if slot == "theirs" else k_peer],
                send_sem=send_sems.at[3 * a + r], recv_sem=recv_sems.at[3 * a + r],
                device_id=(*chip, c), device_id_type=MESH)

        sends = [copy(a, r, "theirs") for a in range(n) for r in range(3)]
        for cp in sends:
            cp.start()
        for a in range(n):
            for r in range(3):
                copy(a, r, "mine").wait_recv()
        for cp in sends:
            cp.wait_send()
        for cp in mine:
            cp.wait()

    return pl.pallas_call(
        body, name=name, in_specs=[HBM_SPEC] * n, out_specs=[HBM_SPEC] * n,
        out_shape=[jax.ShapeDtypeStruct(s.shape, s.dtype) for s in sums],
        scratch_shapes=[pltpu.SemaphoreType.DMA((3 * n,)), pltpu.SemaphoreType.DMA((3 * n,)), pltpu.SemaphoreType.DMA((n,))],
    )(*sums)


def _row_tile(K, N):
    limit = max(16, 262144 // N)
    t = 1
    while t * 2 <= limit and K % (t * 2) == 0:
        t *= 2
    return t if t >= 16 else K


def _adamw(name, parts, w, m, v):
    P, K, N = parts.shape
    tr = _row_tile(K, N)

    def body(p_ref, w_ref, m_ref, v_ref, g_ref, d_ref, nm_ref, nv_ref):
        g = p_ref[0].astype(F32)
        for i in range(1, P):
            g = g + p_ref[i].astype(F32)
        m_new = ADAM_B1 * m_ref[...] + (1.0 - ADAM_B1) * g
        v_new = ADAM_B2 * v_ref[...] + (1.0 - ADAM_B2) * (g * g)
        m_hat = m_new / (1.0 - ADAM_B1 ** ADAM_STEP)
        v_hat = v_new / (1.0 - ADAM_B2 ** ADAM_STEP)
        g_ref[...] = g
        d_ref[...] = -ADAM_LR * (m_hat / (jnp.sqrt(v_hat) + ADAM_EPS) + ADAM_WD * w_ref[...])
        nm_ref[...] = m_new
        nv_ref[...] = v_new

    row = pl.BlockSpec((tr, N), lambda r: (r, 0))
    sh = jax.ShapeDtypeStruct((K, N), F32)
    return pl.pallas_call(body, name=name, grid=(K // tr,),
                          in_specs=[pl.BlockSpec((P, tr, N), lambda r: (0, r, 0)), row, row, row],
                          out_specs=[row] * 4, out_shape=[sh] * 4, compiler_params=_cp(1))(parts, w, m, v)


def _merge_fwd(o_mla, wbm, o_dil, wbd, g0, g1, tm=512, tn=512):
    T, K1 = o_mla.shape
    K2, D = o_dil.shape[1], wbm.shape[1]
    tm, tn = _tile(T, tm), _tile(D, tn)

    def epilogue(accs, in_refs, out_refs):
        a, b = accs
        out_refs[0][...] = a.astype(BF16)
        out_refs[1][...] = b.astype(BF16)
        out_refs[2][...] = (in_refs[4][...].astype(F32) * a + in_refs[5][...].astype(F32) * b).astype(BF16)

    mn = pl.BlockSpec((tm, tn), lambda i, j: (i, j))
    sh = jax.ShapeDtypeStruct((T, D), BF16)
    return _mm("merge_fwd", (T // tm, D // tn), [o_mla, wbm, o_dil, wbd, g0, g1],
               [pl.BlockSpec((tm, K1), lambda i, j: (i, 0)), pl.BlockSpec((K1, tn), lambda i, j: (0, j)),
                pl.BlockSpec((tm, K2), lambda i, j: (i, 0)), pl.BlockSpec((K2, tn), lambda i, j: (0, j)), mn, mn],
               [(0, 1, "nn", 0), (2, 3, "nn", 1)], 2, None, epilogue, [sh, sh, sh], [mn, mn, mn], None)


def _ple_loss(n4, wpg, pe, wpp, x3, tgt, tm=512, tn=512):
    T, D = x3.shape
    Kp = pe.shape[1]
    tm, tn = _tile(T, tm), _tile(D, tn)

    def epilogue(accs, in_refs, out_refs):
        z, proj = accs
        pg = _sigmoid(z)
        err = in_refs[4][...] + pg * proj - in_refs[5][...]
        dy = err * (1.0 / D)
        out_refs[0][...] = dy
        out_refs[1][...] = (dy * proj * pg * (1.0 - pg)).astype(BF16)
        out_refs[2][...] = (dy * pg).astype(BF16)

        @pl.when(pl.program_id(1) == 0)
        def _():
            out_refs[3][...] = jnp.zeros_like(out_refs[3])

        out_refs[3][...] += jnp.sum(err * err, axis=-1, keepdims=True)

    mn = pl.BlockSpec((tm, tn), lambda i, j: (i, j))
    return _mm("ple_loss", (T // tm, D // tn), [n4, wpg, pe, wpp, x3, tgt],
               [pl.BlockSpec((tm, D), lambda i, j: (i, 0)), pl.BlockSpec((D, tn), lambda i, j: (0, j)),
                pl.BlockSpec((tm, Kp), lambda i, j: (i, 0)), pl.BlockSpec((Kp, tn), lambda i, j: (0, j)), mn, mn],
               [(0, 1, "nn", 0), (2, 3, "nn", 1)], 2, None, epilogue,
               [jax.ShapeDtypeStruct((T, D), F32), jax.ShapeDtypeStruct((T, D), BF16), jax.ShapeDtypeStruct((T, D), BF16),
                jax.ShapeDtypeStruct((T, 1), F32)],
               [mn, mn, mn, pl.BlockSpec((tm, 1), lambda i, j: (i, 0))], None)


def _epi_sigmoid(acc, ex, outs):
    outs[0][...] = _sigmoid(acc).astype(outs[0].dtype)


def _epi_add(acc, ex, outs):
    outs[0][...] = (acc + ex[0][...].astype(F32)).astype(outs[0].dtype)


def _epi_dmerge(acc, ex, outs):
    mp, dp, g0, g1 = [e[...].astype(F32) for e in ex]
    outs[0][...] = (acc * g0).astype(BF16)
    outs[1][...] = (acc * g1).astype(BF16)
    outs[2][...] = (acc * mp * g0 * (1.0 - g0)).astype(BF16)
    outs[3][...] = (acc * dp * g1 * (1.0 - g1)).astype(BF16)


_WEIGHTS = ("g_ffn1", "w1_gate", "w1_up", "w1_down", "g_mix", "w_in", "g_cq", "w_uq", "g_ckv", "w_ukv", "g_q_mla", "g_k_mla",
            "g_q_dil", "g_k_dil", "w_br_mla", "w_br_dil", "w_o", "g_ffn2", "w2_gate", "w2_up", "w2_down", "g_ple",
            "w_ple_gate", "w_ple_proj")
_MATRICES = ("w1_gate", "w1_up", "w1_down", "w_in", "w_uq", "w_ukv", "w_br_mla", "w_br_dil", "w_o", "w2_gate", "w2_up",
             "w2_down", "w_ple_gate", "w_ple_proj")
_GAINS = tuple(n for n in _WEIGHTS if n not in _MATRICES)
MLA_BLOCK = 1024
DIL_ROWS = 2048


def _cols(g3):
    nd, K, n = g3.shape
    return g3.transpose(1, 0, 2).reshape(K, nd * n)


def _uncols(m):
    K, n = m.shape
    return m.reshape(K, N_DEV, n // N_DEV).transpose(1, 0, 2)


def _rows(g3):
    nd, k, N = g3.shape
    return g3.reshape(nd * k, N)


def _unrows(m):
    K, N = m.shape
    return m.reshape(N_DEV, K // N_DEV, N)


def _pack_gains(vals):
    flat = jnp.concatenate([vals[n].reshape(-1) for n in _GAINS])
    pad = (-flat.shape[0]) % 2048
    return jnp.pad(flat, (0, pad)).reshape(-1, 128)


def _unpack_gains(packed, like):
    flat = packed.reshape(-1)
    out, off = {}, 0
    for n in _GAINS:
        size = int(np.prod(like[n].shape))
        out[n] = flat[off:off + size].reshape(like[n].shape)
        off += size
    return out


def _train_step(x, p, positions, loss_target, W, M, V):
    T, D = x.shape[1], x.shape[2]
    xs, tgt, pe = x[0], loss_target[0], p[0, 0]
    pos_col = positions.reshape(T, 1).astype(F32)
    w = {n: (a[0] if n in _MATRICES else a.reshape(1, -1)) for n, a in W.items()}

    gathered = dict(zip(_MATRICES, _allgather("ag_weights", [w[n].astype(BF16) for n in _MATRICES])))
    nq_l, nkv_l = w["g_cq"].shape[-1], w["g_ckv"].shape[-1]
    H = w["w_uq"].shape[1] * N_DEV // MLA_QK
    G = len(DIL_GROUPS)
    off_kr = nq_l + nkv_l
    off_dil = off_kr + MLA_ROPE
    off_gate = off_dil + G * 3 * H * HEAD
    kr_block = off_kr // 128
    w_in = _cols(gathered["w_in"])
    wa = jnp.pad(w_in[:, :off_dil], ((0, 0), (0, 128 - MLA_ROPE)))
    wdil, wg0, wg1 = w_in[:, off_dil:off_gate], w_in[:, off_gate:off_gate + D], w_in[:, off_gate + D:]
    wuq = jnp.pad(_cols(gathered["w_uq"]).reshape(nq_l, H, MLA_QK), ((0, 0), (0, 0), (0, MLA_PAD - MLA_QK))).reshape(nq_l, H * MLA_PAD)
    wukv = _cols(gathered["w_ukv"])
    wbm, wbd, wpp = _cols(gathered["w_br_mla"]), _cols(gathered["w_br_dil"]), _cols(gathered["w_ple_proj"])
    wo, wpg = _rows(gathered["w_o"]), _rows(gathered["w_ple_gate"])
    gq_mla = jnp.pad(w["g_q_mla"], ((0, 0), (0, MLA_PAD - MLA_QK)))
    gk_mla = jnp.pad(w["g_k_mla"], ((0, 0), (0, MLA_PAD - MLA_QK)))
    gq_dil, gk_dil = w["g_q_dil"].reshape(G, 1, HEAD), w["g_k_dil"].reshape(G, 1, HEAD)

    half_m, half_d = MLA_ROPE // 2, DIL_ROT // 2
    inv_m = ROPE_THETA ** (-jnp.arange(half_m, dtype=F32) * 2.0 / MLA_ROPE)
    inv_d = ROPE_THETA ** (-jnp.arange(half_d, dtype=F32) * 2.0 / DIL_ROT)
    inv_m = jnp.tile(inv_m, 128 // half_m).reshape(1, 128)
    inv_d = jnp.tile(inv_d, 128 // half_d).reshape(1, 128)
    tabs = _rope_tables(pos_col, inv_m, inv_d)
    tabs_m, tabs_d = tabs[:3], tabs[3:]

    mla_blk, dil_qb = _tile(T, MLA_BLOCK), _tile(T, DIL_ROWS)
    gw = 3 * H * HEAD
    dils = [d for _, d in DIL_GROUPS]
    units = [win // d for win, d in DIL_GROUPS]
    wdil_g = [wdil[:, g * gw:(g + 1) * gw] for g in range(G)]
    tabs_g = [[_to_phase(t, d) for t in tabs_d] for d in dils]

    x1, ffn1 = _ffn_forward("ffn1", xs, w["g_ffn1"], gathered["w1_gate"], gathered["w1_up"], gathered["w1_down"])
    h = _rmsnorm_fwd("mix_norm", x1, w["g_mix"])
    pa = _mm2d("proj_a", h, wa, "nn", 512, 1024, 4096)
    pd = [_to_phase(_mm2d("proj_dil%d" % g, h, wdil_g[g], "nn", 1024, 1024, 4096, out_dtypes=(BF16,)), dils[g]) for g in range(G)]
    g0 = _mm2d("proj_gate0", h, wg0, "nn", 1024, 1024, 4096, out_dtypes=(BF16,), epilogue=_epi_sigmoid)
    g1 = _mm2d("proj_gate1", h, wg1, "nn", 1024, 1024, 4096, out_dtypes=(BF16,), epilogue=_epi_sigmoid)
    cq, ckv = _lora_fwd(pa, w["g_cq"], w["g_ckv"], nq_l, nkv_l)
    q_raw = _mm2d("q_up", cq, wuq, "nn", 512, 2048, 4096)
    kv = _mm2d("kv_up", ckv, wukv, "nn", 512, 2048, 4096)
    q_att, k_att, v_mla = _mla_prep_fwd(q_raw, kv, pa, tabs_m, gq_mla, gk_mla, H, kr_block)
    o_mla, lse_mla = _mla_fwd(q_att, k_att, v_mla, H, mla_blk)
    qd, kd, o_g, lse_g = [], [], [], []
    for g in range(G):
        qg, kg = _dil_prep_fwd_pm("dil_prep_fwd%d" % g, pd[g], tabs_g[g], gq_dil[g], gk_dil[g], H)
        og, lg = _win_fwd("dil_fwd%d" % g, qg, kg, pd[g], H, units[g], T // dils[g], dil_qb)
        qd.append(qg)
        kd.append(kg)
        o_g.append(_from_phase(og, dils[g]))
        lse_g.append(_from_phase(lg, dils[g], axis=1))
    o_dil, lse_dil = _merge_groups(o_g, lse_g, H)
    mla_p, dil_p, merged = _merge_fwd(o_mla, wbm, o_dil, wbd, g0, g1)
    x2 = _mm2d("out_proj", merged, wo, "nn", 512, 1024, 4096, epilogue=_epi_add, extras=(x1,))
    x3, ffn2 = _ffn_forward("ffn2", x2, w["g_ffn2"], gathered["w2_gate"], gathered["w2_up"], gathered["w2_down"])
    n4 = _rmsnorm_fwd("ple_norm", x3, w["g_ple"])
    dy, dz, dproj, loss_rows = _ple_loss(n4, wpg, pe, wpp, x3, tgt)
    loss = lax.psum((0.5 / D) * jnp.sum(loss_rows), ("x", "y", "c"))

    dW, dG = {}, {}
    dW["w_ple_proj"] = _uncols(_mm2d("d_wpp", pe, dproj, "tn", 1024, 2048, 512, out_dtypes=(BF16,)))
    dW["w_ple_gate"] = _unrows(_mm2d("d_wpg", n4, dz, "tn", 1024, 1024, 512, out_dtypes=(BF16,)))
    dn4 = _mm2d("d_n4", dz, wpg, "nt", 512, 1024, 4096)
    dx3, dx3b, dG["g_ple"] = _rmsnorm_bwd("ple_norm_bwd", dn4, x3, w["g_ple"], dy)
    dx2, dx2b, dG["g_ffn2"], dW["w2_gate"], dW["w2_up"], dW["w2_down"] = _ffn_backward(
        "ffn2", dx3, dx3b, x2, w["g_ffn2"], gathered["w2_gate"], gathered["w2_up"], gathered["w2_down"], ffn2)

    d_mla_p, d_dil_p, dpg0, dpg1 = _mm2d("d_merged", dx2b, wo, "nt", 512, 1024, 4096, out_dtypes=(BF16,) * 4,
                                         epilogue=_epi_dmerge, extras=(mla_p, dil_p, g0, g1))
    dW["w_o"] = _unrows(_mm2d("d_wo", merged, dx2b, "tn", 1024, 1024, 512, out_dtypes=(BF16,)))
    dW["w_br_mla"] = _uncols(_mm2d("d_wbm", o_mla, d_mla_p, "tn", 1024, 1024, 512, out_dtypes=(BF16,)))
    dW["w_br_dil"] = _uncols(_mm2d("d_wbd", o_dil, d_dil_p, "tn", 1024, 1024, 512, out_dtypes=(BF16,)))
    do_mla = _mm2d("d_o_mla", d_mla_p, wbm, "nt", 512, 1024, 4096, out_dtypes=(BF16,))
    do_dil = _mm2d("d_o_dil", d_dil_p, wbd, "nt", 512, 1024, 4096, out_dtypes=(BF16,))

    def as_row(a):
        return a.reshape(a.shape[0], 1, a.shape[1])

    dl_mla = _row_dot("mla_delta", do_mla, o_mla, H)
    dq_att = _mla_dq(q_att, k_att, v_mla, do_mla, lse_mla, dl_mla, H, mla_blk)
    dk_att, dv_mla = _mla_dkv(q_att, k_att, v_mla, do_mla, as_row(lse_mla), as_row(dl_mla), H, mla_blk)
    dq_raw, dkv, dkr, dgq, dgk = _mla_prep_bwd(dq_att, dk_att, dv_mla, q_raw, kv, pa, tabs_m, gq_mla, gk_mla, H, kr_block)
    dG["g_q_mla"], dG["g_k_mla"] = dgq[:, :MLA_QK], dgk[:, :MLA_QK]
    d_wuq = _mm2d("d_wuq", cq, dq_raw, "tn", 512, 2048, 512, out_dtypes=(BF16,))
    dW["w_uq"] = _uncols(d_wuq.reshape(nq_l, H, MLA_PAD)[:, :, :MLA_QK].reshape(nq_l, H * MLA_QK))
    dW["w_ukv"] = _uncols(_mm2d("d_wukv", ckv, dkv, "tn", 512, 2048, 512, out_dtypes=(BF16,)))
    dcq = _mm2d("d_cq", dq_raw, wuq, "nt", 512, 1024, 4096)
    dckv = _mm2d("d_ckv", dkv, wukv, "nt", 512, 1024, 4096)
    dpa, dG["g_cq"], dG["g_ckv"] = _lora_bwd(dcq, dckv, dkr, pa, w["g_cq"], w["g_ckv"])

    dl_dil = _row_dot("dil_delta", do_dil, o_dil, H)
    dpd, dgqd, dgkd = [], [], []
    for g in range(G):
        d, n, L = dils[g], units[g], T // dils[g]
        do_g, lse_pg, dl_pg = _to_phase(do_dil, d), _to_phase(lse_dil, d, axis=1), _to_phase(dl_dil, d, axis=1)
        dq_g = _win_dq("dil_dq%d" % g, qd[g], kd[g], pd[g], do_g, lse_pg, dl_pg, H, n, L, dil_qb)
        dk_g, dv_g = _win_dkv("dil_dkv%d" % g, qd[g], kd[g], pd[g], do_g, as_row(lse_pg), as_row(dl_pg), H, n, L, dil_qb)
        dp_g, dgq_g, dgk_g = _dil_prep_bwd_pm("dil_prep_bwd%d" % g, dq_g, dk_g, dv_g, pd[g], tabs_g[g], gq_dil[g], gk_dil[g], H)
        dpd.append(_from_phase(dp_g, d))
        dgqd.append(dgq_g)
        dgkd.append(dgk_g)
    dG["g_q_dil"], dG["g_k_dil"] = jnp.concatenate(dgqd).reshape(1, G, HEAD), jnp.concatenate(dgkd).reshape(1, G, HEAD)

    d_wa = _mm2d("d_wa", h, dpa, "tn", 1024, 1024, 512, out_dtypes=(BF16,))
    d_wdil = [_mm2d("d_wdil%d" % g, h, dpd[g], "tn", 1024, 1024, 512, out_dtypes=(BF16,)) for g in range(G)]
    d_wg0 = _mm2d("d_wg0", h, dpg0, "tn", 1024, 1024, 512, out_dtypes=(BF16,))
    d_wg1 = _mm2d("d_wg1", h, dpg1, "tn", 1024, 1024, 512, out_dtypes=(BF16,))
    dW["w_in"] = _uncols(jnp.concatenate([d_wa[:, :off_dil], *d_wdil, d_wg0, d_wg1], axis=1))
    dh = _mm2d("d_h_a", dpa, wa, "nt", 512, 1024, 1024)
    for g in range(G):
        dh = _mm2d("d_h_dil%d" % g, dpd[g], wdil_g[g], "nt", 512, 1024, 1024, epilogue=_epi_add, extras=(dh,))
    dh = _mm2d("d_h_g0", dpg0, wg0, "nt", 512, 1024, 1024, epilogue=_epi_add, extras=(dh,))
    dh = _mm2d("d_h_g1", dpg1, wg1, "nt", 512, 1024, 1024, epilogue=_epi_add, extras=(dh,))
    dx1, dx1b, dG["g_mix"] = _rmsnorm_bwd("mix_norm_bwd", dh, x1, w["g_mix"], dx2)
    dx0, _, dG["g_ffn1"], dW["w1_gate"], dW["w1_up"], dW["w1_down"] = _ffn_backward(
        "ffn1", dx1, dx1b, xs, w["g_ffn1"], gathered["w1_gate"], gathered["w1_up"], gathered["w1_down"], ffn1)

    halves = [dW[n].reshape(4, 2, *dW[n].shape[1:]) for n in _MATRICES]
    got = _rs_sibling("rs_sibling", halves)
    core = lax.axis_index("c").astype(jnp.int32).reshape(1)
    sums = [_pair_add("rs_add_" + n, own, rec, core) for n, own, rec in zip(_MATRICES, halves, got)]
    reduced = dict(zip(_MATRICES, _rs_chips("rs_chips", sums)))
    grads, deltas, new_m, new_v = {}, {}, {}, {}
    for n in _MATRICES:
        grads[n], deltas[n], new_m[n], new_v[n] = [
            a[None] for a in _adamw("adamw_" + n, reduced[n], w[n], M[n][0], V[n][0])]

    parts = _allgather("ag_gain_grads", [_pack_gains(dG)])[0]
    packed = _adamw("adamw_gains", parts, _pack_gains(W), _pack_gains(M), _pack_gains(V))
    for out, pk in zip((grads, deltas, new_m, new_v), packed):
        out.update(_unpack_gains(pk, W))

    return (loss, dx0[None], *[grads[n] for n in _WEIGHTS], *[deltas[n] for n in _WEIGHTS],
            *[new_m[n] for n in _WEIGHTS], *[new_v[n] for n in _WEIGHTS])


def kernel(x, p, positions, g_ffn1, w1_gate, w1_up, w1_down, g_mix, w_in, g_cq, w_uq, g_ckv, w_ukv, g_q_mla, g_k_mla, g_q_dil, g_k_dil, w_br_mla, w_br_dil, w_o, g_ffn2, w2_gate, w2_up, w2_down, g_ple, w_ple_gate, w_ple_proj, loss_target, m_g_ffn1, m_w1_gate, m_w1_up, m_w1_down, m_g_mix, m_w_in, m_g_cq, m_w_uq, m_g_ckv, m_w_ukv, m_g_q_mla, m_g_k_mla, m_g_q_dil, m_g_k_dil, m_w_br_mla, m_w_br_dil, m_w_o, m_g_ffn2, m_w2_gate, m_w2_up, m_w2_down, m_g_ple, m_w_ple_gate, m_w_ple_proj, v_g_ffn1, v_w1_gate, v_w1_up, v_w1_down, v_g_mix, v_w_in, v_g_cq, v_w_uq, v_g_ckv, v_w_ukv, v_g_q_mla, v_g_k_mla, v_g_q_dil, v_g_k_dil, v_w_br_mla, v_w_br_dil, v_w_o, v_g_ffn2, v_w2_gate, v_w2_up, v_w2_down, v_g_ple, v_w_ple_gate, v_w_ple_proj):
    W = dict(zip(_WEIGHTS, (g_ffn1, w1_gate, w1_up, w1_down, g_mix, w_in, g_cq, w_uq, g_ckv, w_ukv, g_q_mla, g_k_mla, g_q_dil,
                            g_k_dil, w_br_mla, w_br_dil, w_o, g_ffn2, w2_gate, w2_up, w2_down, g_ple, w_ple_gate, w_ple_proj)))
    M = dict(zip(_WEIGHTS, (m_g_ffn1, m_w1_gate, m_w1_up, m_w1_down, m_g_mix, m_w_in, m_g_cq, m_w_uq, m_g_ckv, m_w_ukv, m_g_q_mla,
                            m_g_k_mla, m_g_q_dil, m_g_k_dil, m_w_br_mla, m_w_br_dil, m_w_o, m_g_ffn2, m_w2_gate, m_w2_up,
                            m_w2_down, m_g_ple, m_w_ple_gate, m_w_ple_proj)))
    V = dict(zip(_WEIGHTS, (v_g_ffn1, v_w1_gate, v_w1_up, v_w1_down, v_g_mix, v_w_in, v_g_cq, v_w_uq, v_g_ckv, v_w_ukv, v_g_q_mla,
                            v_g_k_mla, v_g_q_dil, v_g_k_dil, v_w_br_mla, v_w_br_dil, v_w_o, v_g_ffn2, v_w2_gate, v_w2_up,
                            v_w2_down, v_g_ple, v_w_ple_gate, v_w_ple_proj)))
    return _train_step(x, p, positions, loss_target, W, M, V)
```

```python
import numpy as np
import jax
import jax.numpy as jnp
from jax import lax
from jax.experimental import pallas as pl
from jax.experimental.pallas import tpu as pltpu

F32 = jnp.float32
BF16 = jnp.bfloat16

EPS = 1e-6
ROPE_THETA = 500000.0
MLA_NOPE = 128
MLA_ROPE = 64
MLA_QK = MLA_NOPE + MLA_ROPE
MLA_PAD = 256
HEAD = 128
DIL_ROT = 32
DIL_GROUPS = ((128, 1), (512, 4), (2048, 16))
NEG = -1e30
NO_WINDOW = 1 << 30
N_DEV = 8
ADAM_LR, ADAM_B1, ADAM_B2, ADAM_EPS, ADAM_WD, ADAM_STEP = 0.001, 0.9, 0.999, 1e-08, 0.01, 10
VMEM_LIMIT_V7X = 56 * 1024 * 1024
MESH = pl.DeviceIdType.MESH
HBM_SPEC = pl.BlockSpec(memory_space=pltpu.HBM)


def _cp(n_axes):
    return pltpu.CompilerParams(dimension_semantics=("arbitrary",) * n_axes,
                                vmem_limit_bytes=VMEM_LIMIT_V7X)


def _tile(n, t):
    return t if (n >= t and n % t == 0) else n


def _sigmoid(x):
    return 1.0 / (1.0 + jnp.exp(-x))


_DIMS = {"nn": (((1,), (0,)), ((), ())), "nt": (((1,), (1,)), ((), ())), "tn": (((0,), (0,)), ((), ()))}


def _dot(a, b, mode):
    return lax.dot_general(a.astype(BF16), b.astype(BF16), _DIMS[mode], preferred_element_type=F32)


def _mm(name, grid, ins, in_specs, pairs, n_acc, acc_shape, epilogue, out_shapes, out_specs, k_axis):
    n_in, n_out = len(ins), len(out_shapes)
    nk = grid[k_axis] if k_axis is not None else 1

    def body(*refs):
        in_refs, out_refs, acc_refs = refs[:n_in], refs[n_in:n_in + n_out], refs[n_in + n_out:]
        parts = [None] * n_acc
        for ai, bi, mode, ci in pairs:
            d = _dot(in_refs[ai][...], in_refs[bi][...], mode)
            parts[ci] = d if parts[ci] is None else parts[ci] + d
        if nk == 1:
            epilogue(parts, in_refs, out_refs)
            return
        k = pl.program_id(k_axis)

        @pl.when(k == 0)
        def _():
            for c in range(n_acc):
                acc_refs[c][...] = parts[c]

        @pl.when(k > 0)
        def _():
            for c in range(n_acc):
                acc_refs[c][...] += parts[c]

        @pl.when(k == nk - 1)
        def _():
            epilogue([r[...] for r in acc_refs], in_refs, out_refs)

    scratch = [pltpu.VMEM(acc_shape, F32) for _ in range(n_acc)] if nk > 1 else []
    return pl.pallas_call(body, name=name, grid=grid, in_specs=in_specs, out_specs=out_specs,
                          out_shape=out_shapes, scratch_shapes=scratch, compiler_params=_cp(len(grid)))(*ins)


def _mm2d(name, a, b, mode, tm, tn, tk, out_dtypes=(F32,), epilogue=None, extras=()):
    if mode == "nn":
        (M, K), N = a.shape, b.shape[1]
    elif mode == "nt":
        (M, K), N = a.shape, b.shape[0]
    else:
        (K, M), N = a.shape, b.shape[1]
    tm, tn, tk = _tile(M, tm), _tile(N, tn), _tile(K, tk)
    a_spec = pl.BlockSpec((tk, tm), lambda i, j, k: (k, i)) if mode == "tn" else pl.BlockSpec((tm, tk), lambda i, j, k: (i, k))
    b_spec = pl.BlockSpec((tn, tk), lambda i, j, k: (j, k)) if mode == "nt" else pl.BlockSpec((tk, tn), lambda i, j, k: (k, j))
    mn_spec = pl.BlockSpec((tm, tn), lambda i, j, k: (i, j))
    n_ex = len(extras)

    def default_epilogue(acc, ex_refs, out_refs):
        out_refs[0][...] = acc.astype(out_refs[0].dtype)

    epi = epilogue or default_epilogue

    def wrapped(accs, in_refs, out_refs):
        epi(accs[0], in_refs[2:2 + n_ex], out_refs)

    outs = _mm(name, (M // tm, N // tn, K // tk), [a, b, *extras], [a_spec, b_spec] + [mn_spec] * n_ex,
               [(0, 1, mode, 0)], 1, (tm, tn), wrapped,
               [jax.ShapeDtypeStruct((M, N), dt) for dt in out_dtypes], [mn_spec] * len(out_dtypes), 2)
    return outs[0] if len(out_dtypes) == 1 else outs


def _rms_stats(x, n):
    return lax.rsqrt(jnp.sum(x * x, axis=-1, keepdims=True) * (1.0 / n) + EPS)


def _rmsnorm_fwd(name, x, g, tm=512):
    T, D = x.shape
    tm = _tile(T, tm)

    def body(x_ref, g_ref, o_ref):
        xv = x_ref[...]
        o_ref[...] = (xv * _rms_stats(xv, D) * g_ref[...]).astype(BF16)

    return pl.pallas_call(body, name=name, grid=(T // tm,),
                          in_specs=[pl.BlockSpec((tm, D), lambda i: (i, 0)), pl.BlockSpec((1, D), lambda i: (0, 0))],
                          out_specs=pl.BlockSpec((tm, D), lambda i: (i, 0)),
                          out_shape=jax.ShapeDtypeStruct((T, D), BF16), compiler_params=_cp(1))(x, g)


def _rms_bwd_rows(dy, x, g, n):
    r = _rms_stats(x, n)
    xh = x * r
    gd = dy * g
    mean = jnp.sum(gd * xh, axis=-1, keepdims=True) * (1.0 / n)
    return r * (gd - xh * mean), dy * xh


def _rmsnorm_bwd(name, dn, x, g, res, tm=256):
    T, D = x.shape
    tm = _tile(T, tm)

    def body(dn_ref, x_ref, g_ref, res_ref, dx_ref, dxb_ref, dg_ref):
        dx, dgp = _rms_bwd_rows(dn_ref[...].astype(F32), x_ref[...], g_ref[...], D)
        dx = dx + res_ref[...]
        dx_ref[...] = dx
        dxb_ref[...] = dx.astype(BF16)

        @pl.when(pl.program_id(0) == 0)
        def _():
            dg_ref[...] = jnp.zeros_like(dg_ref)

        dg_ref[...] += jnp.sum(dgp, axis=0, keepdims=True)

    row = pl.BlockSpec((tm, D), lambda i: (i, 0))
    vec = pl.BlockSpec((1, D), lambda i: (0, 0))
    return pl.pallas_call(body, name=name, grid=(T // tm,), in_specs=[row, row, vec, row],
                          out_specs=[row, row, vec],
                          out_shape=[jax.ShapeDtypeStruct((T, D), F32), jax.ShapeDtypeStruct((T, D), BF16),
                                     jax.ShapeDtypeStruct((1, D), F32)],
                          compiler_params=_cp(1))(dn, x, g, res)


def _rope_tables(pos_col, inv_mla, inv_dil, tm=512):
    T = pos_col.shape[0]
    tm = _tile(T, tm)

    def body(p_ref, im_ref, id_ref, cm, sam, sbm, cd, sad, sbd):
        lane = lax.broadcasted_iota(jnp.int32, (tm, 128), 1)
        p = p_ref[...]
        am = p * im_ref[...]
        c, s = jnp.cos(am), jnp.sin(am)
        cm[...] = jnp.where(lane < 64, c, 0.0)
        sam[...] = jnp.where(lane < 32, -s, 0.0)
        sbm[...] = jnp.where((lane >= 32) & (lane < 64), s, 0.0)
        ad = p * id_ref[...]
        c, s = jnp.cos(ad), jnp.sin(ad)
        cd[...] = jnp.where(lane < 32, c, 1.0)
        sad[...] = jnp.where(lane < 16, -s, 0.0)
        sbd[...] = jnp.where((lane >= 16) & (lane < 32), s, 0.0)

    row = pl.BlockSpec((tm, 128), lambda i: (i, 0))
    vec = pl.BlockSpec((1, 128), lambda i: (0, 0))
    return pl.pallas_call(body, name="rope_tables", grid=(T // tm,),
                          in_specs=[pl.BlockSpec((tm, 1), lambda i: (i, 0)), vec, vec], out_specs=[row] * 6,
                          out_shape=[jax.ShapeDtypeStruct((T, 128), F32)] * 6, compiler_params=_cp(1))(pos_col, inv_mla, inv_dil)


def _rope(v, c, sa, sb, sh):
    return v * c + pltpu.roll(v, 128 - sh, 1) * sa + pltpu.roll(v, sh, 1) * sb


def _rope_t(d, c, sa, sb, sh):
    return d * c + pltpu.roll(d * sa, sh, 1) + pltpu.roll(d * sb, 128 - sh, 1)


def _lora_fwd(pa, g_cq, g_ckv, nq, nkv, tm=512):
    T, W = pa.shape
    tm = _tile(T, tm)

    def body(pa_ref, gq_ref, gk_ref, cq_ref, ckv_ref):
        a = pa_ref[:, :nq]
        cq_ref[...] = (a * _rms_stats(a, nq) * gq_ref[...]).astype(BF16)
        b = pa_ref[:, nq:nq + nkv]
        ckv_ref[...] = (b * _rms_stats(b, nkv) * gk_ref[...]).astype(BF16)

    return pl.pallas_call(body, name="lora_fwd", grid=(T // tm,),
                          in_specs=[pl.BlockSpec((tm, W), lambda i: (i, 0)), pl.BlockSpec((1, nq), lambda i: (0, 0)),
                                    pl.BlockSpec((1, nkv), lambda i: (0, 0))],
                          out_specs=[pl.BlockSpec((tm, nq), lambda i: (i, 0)), pl.BlockSpec((tm, nkv), lambda i: (i, 0))],
                          out_shape=[jax.ShapeDtypeStruct((T, nq), BF16), jax.ShapeDtypeStruct((T, nkv), BF16)],
                          compiler_params=_cp(1))(pa, g_cq, g_ckv)


def _lora_bwd(dcq, dckv, dkr, pa, g_cq, g_ckv, tm=512):
    T, W = pa.shape
    nq, nkv = dcq.shape[1], dckv.shape[1]
    tm = _tile(T, tm)

    def body(dcq_ref, dckv_ref, dkr_ref, pa_ref, gq_ref, gk_ref, dpa_ref, dgq_ref, dgk_ref):
        dx, dgp = _rms_bwd_rows(dcq_ref[...], pa_ref[:, :nq], gq_ref[...], nq)
        dpa_ref[:, :nq] = dx.astype(BF16)
        dx2, dgp2 = _rms_bwd_rows(dckv_ref[...], pa_ref[:, nq:nq + nkv], gk_ref[...], nkv)
        dpa_ref[:, nq:nq + nkv] = dx2.astype(BF16)
        dpa_ref[:, nq + nkv:] = dkr_ref[...].astype(BF16)

        @pl.when(pl.program_id(0) == 0)
        def _():
            dgq_ref[...] = jnp.zeros_like(dgq_ref)
            dgk_ref[...] = jnp.zeros_like(dgk_ref)

        dgq_ref[...] += jnp.sum(dgp, axis=0, keepdims=True)
        dgk_ref[...] += jnp.sum(dgp2, axis=0, keepdims=True)

    def row(n):
        return pl.BlockSpec((tm, n), lambda i: (i, 0))

    def vec(n):
        return pl.BlockSpec((1, n), lambda i: (0, 0))

    return pl.pallas_call(body, name="lora_bwd", grid=(T // tm,),
                          in_specs=[row(nq), row(nkv), row(128), row(W), vec(nq), vec(nkv)],
                          out_specs=[row(W), vec(nq), vec(nkv)],
                          out_shape=[jax.ShapeDtypeStruct((T, W), BF16), jax.ShapeDtypeStruct((1, nq), F32),
                                     jax.ShapeDtypeStruct((1, nkv), F32)],
                          compiler_params=_cp(1))(dcq, dckv, dkr, pa, g_cq, g_ckv)


def _sumsq(v):
    return jnp.sum(v * v, axis=-1, keepdims=True)


def _mla_prep_fwd(q_raw, kv, pa, tabs, gq, gk, H, kr_block, tm=256):
    T = q_raw.shape[0]
    tm = _tile(T, tm)
    scale = MLA_QK ** -0.5
    P = MLA_PAD

    def body(q_ref, kv_ref, kr_ref, c_ref, sa_ref, sb_ref, gq_ref, gk_ref, qo, ko, vo):
        c, sa, sb = c_ref[...], sa_ref[...], sb_ref[...]
        kr = kr_ref[...]
        kr2 = _sumsq(kr)
        for h in range(H):
            lo, hi = q_ref[:, h * P:h * P + 128], q_ref[:, h * P + 128:(h + 1) * P]
            r = lax.rsqrt((_sumsq(lo) + _sumsq(hi)) * (1.0 / MLA_QK) + EPS)
            qo[:, h * P:h * P + 128] = (lo * r * gq_ref[:, :128] * scale).astype(BF16)
            qo[:, h * P + 128:(h + 1) * P] = (_rope(hi * r * gq_ref[:, 128:], c, sa, sb, 32) * scale).astype(BF16)
            kn = kv_ref[:, h * P:h * P + 128]
            r = lax.rsqrt((_sumsq(kn) + kr2) * (1.0 / MLA_QK) + EPS)
            ko[:, h * P:h * P + 128] = (kn * r * gk_ref[:, :128]).astype(BF16)
            ko[:, h * P + 128:(h + 1) * P] = _rope(kr * r * gk_ref[:, 128:], c, sa, sb, 32).astype(BF16)
            vo[:, h * 128:(h + 1) * 128] = kv_ref[:, h * P + 128:(h + 1) * P].astype(BF16)

    wide = pl.BlockSpec((tm, H * P), lambda i: (i, 0))
    lane = pl.BlockSpec((tm, 128), lambda i: (i, 0))
    vec = pl.BlockSpec((1, P), lambda i: (0, 0))
    return pl.pallas_call(body, name="mla_prep_fwd", grid=(T // tm,),
                          in_specs=[wide, wide, pl.BlockSpec((tm, 128), lambda i: (i, kr_block)), lane, lane, lane, vec, vec],
                          out_specs=[wide, wide, pl.BlockSpec((tm, H * 128), lambda i: (i, 0))],
                          out_shape=[jax.ShapeDtypeStruct((T, H * P), BF16), jax.ShapeDtypeStruct((T, H * P), BF16),
                                     jax.ShapeDtypeStruct((T, H * 128), BF16)],
                          compiler_params=_cp(1))(q_raw, kv, pa, *tabs, gq, gk)


def _mla_prep_bwd(dq, dk, dv, q_raw, kv, pa, tabs, gq, gk, H, kr_block, tm=256):
    T = q_raw.shape[0]
    tm = _tile(T, tm)
    scale = MLA_QK ** -0.5
    P = MLA_PAD

    def body(dq_ref, dk_ref, dv_ref, q_ref, kv_ref, kr_ref, c_ref, sa_ref, sb_ref, gq_ref, gk_ref,
             dqr, dkv, dkr, dgq, dgk):
        c, sa, sb = c_ref[...], sa_ref[...], sb_ref[...]
        kr = kr_ref[...]
        kr2 = _sumsq(kr)
        gql, gqh, gkl, gkh = gq_ref[:, :128], gq_ref[:, 128:], gk_ref[:, :128], gk_ref[:, 128:]
        dkr_acc = jnp.zeros((tm, 128), F32)
        sums = [jnp.zeros((1, 128), F32) for _ in range(4)]
        for h in range(H):
            lo_s, hi_s = slice(h * P, h * P + 128), slice(h * P + 128, (h + 1) * P)
            lo, hi = q_ref[:, lo_s], q_ref[:, hi_s]
            r = lax.rsqrt((_sumsq(lo) + _sumsq(hi)) * (1.0 / MLA_QK) + EPS)
            ql, qh = lo * r, hi * r
            dyl = dq_ref[:, lo_s] * scale
            dyh = _rope_t(dq_ref[:, hi_s] * scale, c, sa, sb, 32)
            gl, gh = dyl * gql, dyh * gqh
            mean = (jnp.sum(gl * ql, axis=-1, keepdims=True) + jnp.sum(gh * qh, axis=-1, keepdims=True)) * (1.0 / MLA_QK)
            dqr[:, lo_s] = (r * (gl - ql * mean)).astype(BF16)
            dqr[:, hi_s] = (r * (gh - qh * mean)).astype(BF16)
            sums[0] += jnp.sum(dyl * ql, axis=0, keepdims=True)
            sums[1] += jnp.sum(dyh * qh, axis=0, keepdims=True)
            kn = kv_ref[:, lo_s]
            r = lax.rsqrt((_sumsq(kn) + kr2) * (1.0 / MLA_QK) + EPS)
            kl, kh = kn * r, kr * r
            dkl = dk_ref[:, lo_s]
            dkh = _rope_t(dk_ref[:, hi_s], c, sa, sb, 32)
            gl, gh = dkl * gkl, dkh * gkh
            mean = (jnp.sum(gl * kl, axis=-1, keepdims=True) + jnp.sum(gh * kh, axis=-1, keepdims=True)) * (1.0 / MLA_QK)
            dkv[:, lo_s] = (r * (gl - kl * mean)).astype(BF16)
            dkr_acc += r * (gh - kh * mean)
            dkv[:, hi_s] = dv_ref[:, h * 128:(h + 1) * 128].astype(BF16)
            sums[2] += jnp.sum(dkl * kl, axis=0, keepdims=True)
            sums[3] += jnp.sum(dkh * kh, axis=0, keepdims=True)
        dkr[...] = dkr_acc

        @pl.when(pl.program_id(0) == 0)
        def _():
            dgq[...] = jnp.zeros_like(dgq)
            dgk[...] = jnp.zeros_like(dgk)

        dgq[:, :128] += sums[0]
        dgq[:, 128:] += sums[1]
        dgk[:, :128] += sums[2]
        dgk[:, 128:] += sums[3]

    wide = pl.BlockSpec((tm, H * P), lambda i: (i, 0))
    lane = pl.BlockSpec((tm, 128), lambda i: (i, 0))
    vec = pl.BlockSpec((1, P), lambda i: (0, 0))
    return pl.pallas_call(body, name="mla_prep_bwd", grid=(T // tm,),
                          in_specs=[wide, wide, pl.BlockSpec((tm, H * 128), lambda i: (i, 0)), wide, wide,
                                    pl.BlockSpec((tm, 128), lambda i: (i, kr_block)), lane, lane, lane, vec, vec],
                          out_specs=[wide, wide, lane, vec, vec],
                          out_shape=[jax.ShapeDtypeStruct((T, H * P), BF16), jax.ShapeDtypeStruct((T, H * P), BF16),
                                     jax.ShapeDtypeStruct((T, 128), F32), jax.ShapeDtypeStruct((1, P), F32),
                                     jax.ShapeDtypeStruct((1, P), F32)],
                          compiler_params=_cp(1))(dq, dk, dv, q_raw, kv, pa, *tabs, gq, gk)


def _dil_prep_fwd(pdil, tabs, gq, gk, G, H, tm=256):
    T = pdil.shape[0]
    tm = _tile(T, tm)
    scale = HEAD ** -0.5

    def body(p_ref, c_ref, sa_ref, sb_ref, gq_ref, gk_ref, qo, ko):
        c, sa, sb = c_ref[...], sa_ref[...], sb_ref[...]
        for h in range(H):
            q = p_ref[:, h * 128:(h + 1) * 128].astype(F32)
            qo[:, h * 128:(h + 1) * 128] = (_rope(q * _rms_stats(q, HEAD) * gq_ref[...], c, sa, sb, 16) * scale).astype(BF16)
            k = p_ref[:, (H + h) * 128:(H + h + 1) * 128].astype(F32)
            ko[:, h * 128:(h + 1) * 128] = _rope(k * _rms_stats(k, HEAD) * gk_ref[...], c, sa, sb, 16).astype(BF16)

    lane = pl.BlockSpec((tm, 128), lambda g, i: (i, 0))
    gain = pl.BlockSpec((None, 1, 128), lambda g, i: (g, 0, 0))
    out = pl.BlockSpec((tm, H * 128), lambda g, i: (i, g))
    return pl.pallas_call(body, name="dil_prep_fwd", grid=(G, T // tm),
                          in_specs=[pl.BlockSpec((tm, 3 * H * 128), lambda g, i: (i, g)), lane, lane, lane, gain, gain],
                          out_specs=[out, out],
                          out_shape=[jax.ShapeDtypeStruct((T, G * H * 128), BF16)] * 2,
                          compiler_params=_cp(2))(pdil, *tabs, gq, gk)


def _dil_prep_bwd(dqd, dkd, dvd, pdil, tabs, gq, gk, G, H, tm=256):
    T = pdil.shape[0]
    tm = _tile(T, tm)
    scale = HEAD ** -0.5

    def body(dq_ref, dk_ref, dv_ref, p_ref, c_ref, sa_ref, sb_ref, gq_ref, gk_ref, dp_ref, dgq, dgk):
        c, sa, sb = c_ref[...], sa_ref[...], sb_ref[...]
        sq = jnp.zeros((1, 128), F32)
        sk = jnp.zeros((1, 128), F32)
        for h in range(H):
            hs = slice(h * 128, (h + 1) * 128)
            q = p_ref[:, hs].astype(F32)
            dy = _rope_t(dq_ref[:, hs] * scale, c, sa, sb, 16)
            dx, dgp = _rms_bwd_rows(dy, q, gq_ref[...], HEAD)
            dp_ref[:, hs] = dx.astype(BF16)
            sq += jnp.sum(dgp, axis=0, keepdims=True)
            ks = slice((H + h) * 128, (H + h + 1) * 128)
            k = p_ref[:, ks].astype(F32)
            dy = _rope_t(dk_ref[:, hs], c, sa, sb, 16)
            dx, dgp = _rms_bwd_rows(dy, k, gk_ref[...], HEAD)
            dp_ref[:, ks] = dx.astype(BF16)
            sk += jnp.sum(dgp, axis=0, keepdims=True)
            dp_ref[:, (2 * H + h) * 128:(2 * H + h + 1) * 128] = dv_ref[:, hs].astype(BF16)

        @pl.when(pl.program_id(1) == 0)
        def _():
            dgq[...] = jnp.zeros_like(dgq)
            dgk[...] = jnp.zeros_like(dgk)

        dgq[...] += sq
        dgk[...] += sk

    lane = pl.BlockSpec((tm, 128), lambda g, i: (i, 0))
    gain = pl.BlockSpec((None, 1, 128), lambda g, i: (g, 0, 0))
    grp = pl.BlockSpec((tm, H * 128), lambda g, i: (i, g))
    big = pl.BlockSpec((tm, 3 * H * 128), lambda g, i: (i, g))
    return pl.pallas_call(body, name="dil_prep_bwd", grid=(G, T // tm),
                          in_specs=[grp, grp, grp, big, lane, lane, lane, gain, gain],
                          out_specs=[big, gain, gain],
                          out_shape=[jax.ShapeDtypeStruct(pdil.shape, BF16), jax.ShapeDtypeStruct((G, 1, 128), F32),
                                     jax.ShapeDtypeStruct((G, 1, 128), F32)],
                          compiler_params=_cp(2))(dqd, dkd, dvd, pdil, *tabs, gq, gk)


def _attn_steps(nq, groups, blk):
    steps = []
    for g, (win, dil) in enumerate(groups):
        nb = nq if win is None else min(nq, -(-win // blk) + 1)
        for b in range(nb):
            steps.append((g, b, dil, NO_WINDOW if win is None else win, int(b == 0), int(b == nb - 1)))
    return [jnp.asarray(col, jnp.int32) for col in np.array(steps, np.int32).T]


def _valid(back, dil, win, blk, q_axis):
    qi = lax.broadcasted_iota(jnp.int32, (blk, blk), q_axis)
    ki = lax.broadcasted_iota(jnp.int32, (blk, blk), 1 - q_axis)
    delta = back * blk + (qi - ki)
    return (delta >= 0) & (delta <= win) & ((delta & (dil - 1)) == 0)


def _attn_fwd(name, q, k, v, H, wq, v_base, v_stride, groups, blk):
    T = q.shape[0]
    nq = T // blk
    tabs = _attn_steps(nq, groups, blk)
    ns = int(tabs[0].shape[0])

    def body(sg, sb, sd, sw, sf, sl, q_ref, k_ref, v_ref, o_ref, lse_ref, m_sc, l_sc, acc_sc):
        i, s = pl.program_id(1), pl.program_id(2)

        @pl.when(s == 0)
        def _():
            m_sc[...] = jnp.full_like(m_sc, NEG)
            l_sc[...] = jnp.zeros_like(l_sc)
            acc_sc[...] = jnp.zeros_like(acc_sc)

        back = sb[s]

        @pl.when(i - back >= 0)
        def _():
            sc = _dot(q_ref[...], k_ref[...], "nt")
            ok = _valid(back, sd[s], sw[s], blk, 0)
            sc = jnp.where(ok, sc, NEG)
            m_prev = m_sc[...]
            m_new = jnp.maximum(m_prev, jnp.max(sc, axis=-1, keepdims=True))
            p = jnp.where(ok, jnp.exp(sc - m_new), 0.0)
            alpha = jnp.exp(m_prev - m_new)
            l_sc[...] = alpha * l_sc[...] + jnp.sum(p, axis=-1, keepdims=True)
            acc_sc[...] = alpha * acc_sc[...] + _dot(p, v_ref[...], "nn")
            m_sc[...] = m_new

        @pl.when(s == ns - 1)
        def _():
            o_ref[...] = (acc_sc[...] / l_sc[...]).astype(BF16)
            lse_ref[...] = m_sc[...] + jnp.log(l_sc[...])

    def kv_row(i, s, sb):
        return jnp.maximum(i - sb[s], 0)

    grid_spec = pltpu.PrefetchScalarGridSpec(
        num_scalar_prefetch=6, grid=(H, nq, ns),
        in_specs=[pl.BlockSpec((blk, wq), lambda h, i, s, sg, sb, *_: (i, sg[s] * H + h)),
                  pl.BlockSpec((blk, wq), lambda h, i, s, sg, sb, *_: (kv_row(i, s, sb), sg[s] * H + h)),
                  pl.BlockSpec((blk, HEAD), lambda h, i, s, sg, sb, *_: (kv_row(i, s, sb), v_base + sg[s] * v_stride + h))],
        out_specs=[pl.BlockSpec((blk, HEAD), lambda h, i, s, *_: (i, h)),
                   pl.BlockSpec((None, blk, 1), lambda h, i, s, *_: (h, i, 0))],
        scratch_shapes=[pltpu.VMEM((blk, 1), F32), pltpu.VMEM((blk, 1), F32), pltpu.VMEM((blk, HEAD), F32)])
    return pl.pallas_call(body, name=name, grid_spec=grid_spec,
                          out_shape=[jax.ShapeDtypeStruct((T, H * HEAD), BF16), jax.ShapeDtypeStruct((H, T, 1), F32)],
                          compiler_params=_cp(3))(*tabs, q, k, v)


def _attn_dq(name, q, k, v, do, o, lse, H, wq, v_base, v_stride, groups, blk):
    T = q.shape[0]
    nq = T // blk
    tabs = _attn_steps(nq, groups, blk)
    ns = int(tabs[0].shape[0])

    def body(sg, sb, sd, sw, sf, sl, q_ref, k_ref, v_ref, do_ref, o_ref, lse_ref, dq_ref, dl_ref, acc_sc, dl_sc):
        i, s = pl.program_id(1), pl.program_id(2)

        @pl.when(s == 0)
        def _():
            d = jnp.sum(do_ref[...].astype(F32) * o_ref[...].astype(F32), axis=-1, keepdims=True)
            dl_sc[...] = d
            dl_ref[...] = d

        @pl.when(sf[s] == 1)
        def _():
            acc_sc[...] = jnp.zeros_like(acc_sc)

        back = sb[s]

        @pl.when(i - back >= 0)
        def _():
            sc = _dot(q_ref[...], k_ref[...], "nt")
            ok = _valid(back, sd[s], sw[s], blk, 0)
            p = jnp.where(ok, jnp.exp(sc - lse_ref[...]), 0.0)
            dp = _dot(do_ref[...], v_ref[...], "nt")
            ds = p * (dp - dl_sc[...])
            acc_sc[...] += _dot(ds, k_ref[...], "nn")

        @pl.when(sl[s] == 1)
        def _():
            dq_ref[...] = acc_sc[...]

    def kv_row(i, s, sb):
        return jnp.maximum(i - sb[s], 0)

    col = pl.BlockSpec((None, blk, 1), lambda h, i, s, *_: (h, i, 0))
    oh = pl.BlockSpec((blk, HEAD), lambda h, i, s, *_: (i, h))
    grid_spec = pltpu.PrefetchScalarGridSpec(
        num_scalar_prefetch=6, grid=(H, nq, ns),
        in_specs=[pl.BlockSpec((blk, wq), lambda h, i, s, sg, sb, *_: (i, sg[s] * H + h)),
                  pl.BlockSpec((blk, wq), lambda h, i, s, sg, sb, *_: (kv_row(i, s, sb), sg[s] * H + h)),
                  pl.BlockSpec((blk, HEAD), lambda h, i, s, sg, sb, *_: (kv_row(i, s, sb), v_base + sg[s] * v_stride + h)),
                  oh, oh, col],
        out_specs=[pl.BlockSpec((blk, wq), lambda h, i, s, sg, *_: (i, sg[s] * H + h)), col],
        scratch_shapes=[pltpu.VMEM((blk, wq), F32), pltpu.VMEM((blk, 1), F32)])
    G = len(groups)
    return pl.pallas_call(body, name=name, grid_spec=grid_spec,
                          out_shape=[jax.ShapeDtypeStruct((T, G * H * wq), F32), jax.ShapeDtypeStruct((H, T, 1), F32)],
                          compiler_params=_cp(3))(*tabs, q, k, v, do, o, lse)


def _attn_dkv(name, q, k, v, do, lse_row, dl_row, H, wq, v_base, v_stride, groups, blk):
    T = q.shape[0]
    nq = T // blk
    tabs = _attn_steps(nq, groups, blk)
    ns = int(tabs[0].shape[0])

    def body(sg, sb, sd, sw, sf, sl, q_ref, k_ref, v_ref, do_ref, lse_ref, dl_ref, dk_ref, dv_ref, dk_sc, dv_sc):
        j, s = pl.program_id(1), pl.program_id(2)

        @pl.when(sf[s] == 1)
        def _():
            dk_sc[...] = jnp.zeros_like(dk_sc)
            dv_sc[...] = jnp.zeros_like(dv_sc)

        back = sb[s]

        @pl.when(j + back < nq)
        def _():
            st = _dot(k_ref[...], q_ref[...], "nt")
            ok = _valid(back, sd[s], sw[s], blk, 1)
            pt = jnp.where(ok, jnp.exp(st - lse_ref[...]), 0.0)
            dpt = _dot(v_ref[...], do_ref[...], "nt")
            dst = pt * (dpt - dl_ref[...])
            dv_sc[...] += _dot(pt, do_ref[...], "nn")
            dk_sc[...] += _dot(dst, q_ref[...], "nn")

        @pl.when(sl[s] == 1)
        def _():
            dk_ref[...] = dk_sc[...]
            dv_ref[...] = dv_sc[...]

    def q_row(j, s, sb):
        return jnp.minimum(j + sb[s], nq - 1)

    row = pl.BlockSpec((None, 1, blk), lambda h, j, s, sg, sb, *_: (h, 0, q_row(j, s, sb)))
    grid_spec = pltpu.PrefetchScalarGridSpec(
        num_scalar_prefetch=6, grid=(H, nq, ns),
        in_specs=[pl.BlockSpec((blk, wq), lambda h, j, s, sg, sb, *_: (q_row(j, s, sb), sg[s] * H + h)),
                  pl.BlockSpec((blk, wq), lambda h, j, s, sg, *_: (j, sg[s] * H + h)),
                  pl.BlockSpec((blk, HEAD), lambda h, j, s, sg, *_: (j, v_base + sg[s] * v_stride + h)),
                  pl.BlockSpec((blk, HEAD), lambda h, j, s, sg, sb, *_: (q_row(j, s, sb), h)),
                  row, row],
        out_specs=[pl.BlockSpec((blk, wq), lambda h, j, s, sg, *_: (j, sg[s] * H + h)),
                   pl.BlockSpec((blk, HEAD), lambda h, j, s, sg, *_: (j, sg[s] * H + h))],
        scratch_shapes=[pltpu.VMEM((blk, wq), F32), pltpu.VMEM((blk, HEAD), F32)])
    G = len(groups)
    return pl.pallas_call(body, name=name, grid_spec=grid_spec,
                          out_shape=[jax.ShapeDtypeStruct((T, G * H * wq), F32), jax.ShapeDtypeStruct((T, G * H * HEAD), F32)],
                          compiler_params=_cp(3))(*tabs, q, k, v, do, lse_row, dl_row)


def _causal_mask(blk, q_axis):
    return lax.broadcasted_iota(jnp.int32, (blk, blk), q_axis) >= lax.broadcasted_iota(jnp.int32, (blk, blk), 1 - q_axis)


def _on_blocks(s, live, step):
    @pl.when(s == 0)
    def _():
        step(True)

    @pl.when((s > 0) & live)
    def _():
        step(False)


def _mla_fwd(q, k, v, H, blk):
    T, wq = q.shape[0], q.shape[1] // H
    nq = T // blk

    def body(q_ref, k_ref, v_ref, o_ref, lse_ref, m_sc, l_sc, acc_sc):
        i, s = pl.program_id(1), pl.program_id(2)

        @pl.when(s == 0)
        def _():
            m_sc[...] = jnp.full_like(m_sc, NEG)
            l_sc[...] = jnp.zeros_like(l_sc)
            acc_sc[...] = jnp.zeros_like(acc_sc)

        def step(masked):
            sc = _dot(q_ref[...], k_ref[...], "nt")
            if masked:
                sc = jnp.where(_causal_mask(blk, 0), sc, NEG)
            m_prev = m_sc[...]
            m_new = jnp.maximum(m_prev, jnp.max(sc, axis=-1, keepdims=True))
            p = jnp.exp(sc - m_new)
            alpha = jnp.exp(m_prev - m_new)
            l_sc[...] = alpha * l_sc[...] + jnp.sum(p, axis=-1, keepdims=True)
            acc_sc[...] = alpha * acc_sc[...] + _dot(p, v_ref[...], "nn")
            m_sc[...] = m_new

        _on_blocks(s, i - s >= 0, step)

        @pl.when(s == nq - 1)
        def _():
            o_ref[...] = (acc_sc[...] / l_sc[...]).astype(BF16)
            lse_ref[...] = m_sc[...] + jnp.log(l_sc[...])

    kv_spec = lambda w: pl.BlockSpec((blk, w), lambda h, i, s: (jnp.maximum(i - s, 0), h))
    return pl.pallas_call(
        body, name="mla_fwd", grid=(H, nq, nq),
        in_specs=[pl.BlockSpec((blk, wq), lambda h, i, s: (i, h)), kv_spec(wq), kv_spec(HEAD)],
        out_specs=[pl.BlockSpec((blk, HEAD), lambda h, i, s: (i, h)), pl.BlockSpec((None, blk, 1), lambda h, i, s: (h, i, 0))],
        out_shape=[jax.ShapeDtypeStruct((T, H * HEAD), BF16), jax.ShapeDtypeStruct((H, T, 1), F32)],
        scratch_shapes=[pltpu.VMEM((blk, 1), F32), pltpu.VMEM((blk, 1), F32), pltpu.VMEM((blk, HEAD), F32)],
        compiler_params=_cp(3))(q, k, v)


def _mla_dq(q, k, v, do, lse, dl, H, blk):
    T, wq = q.shape[0], q.shape[1] // H
    nq = T // blk

    def body(q_ref, k_ref, v_ref, do_ref, lse_ref, dl_ref, dq_ref, acc_sc):
        i, s = pl.program_id(1), pl.program_id(2)

        @pl.when(s == 0)
        def _():
            acc_sc[...] = jnp.zeros_like(acc_sc)

        def step(masked):
            sc = _dot(q_ref[...], k_ref[...], "nt")
            if masked:
                sc = jnp.where(_causal_mask(blk, 0), sc, NEG)
            p = jnp.exp(sc - lse_ref[...])
            dp = _dot(do_ref[...], v_ref[...], "nt")
            acc_sc[...] += _dot(p * (dp - dl_ref[...]), k_ref[...], "nn")

        _on_blocks(s, i - s >= 0, step)

        @pl.when(s == nq - 1)
        def _():
            dq_ref[...] = acc_sc[...]

    kv_spec = lambda w: pl.BlockSpec((blk, w), lambda h, i, s: (jnp.maximum(i - s, 0), h))
    col = pl.BlockSpec((None, blk, 1), lambda h, i, s: (h, i, 0))
    return pl.pallas_call(
        body, name="mla_dq", grid=(H, nq, nq),
        in_specs=[pl.BlockSpec((blk, wq), lambda h, i, s: (i, h)), kv_spec(wq), kv_spec(HEAD),
                  pl.BlockSpec((blk, HEAD), lambda h, i, s: (i, h)), col, col],
        out_specs=pl.BlockSpec((blk, wq), lambda h, i, s: (i, h)),
        out_shape=jax.ShapeDtypeStruct((T, H * wq), F32),
        scratch_shapes=[pltpu.VMEM((blk, wq), F32)], compiler_params=_cp(3))(q, k, v, do, lse, dl)


def _mla_dkv(q, k, v, do, lse_row, dl_row, H, blk):
    T, wq = q.shape[0], q.shape[1] // H
    nq = T // blk

    def body(q_ref, k_ref, v_ref, do_ref, lse_ref, dl_ref, dk_ref, dv_ref, dk_sc, dv_sc):
        j, s = pl.program_id(1), pl.program_id(2)

        @pl.when(s == 0)
        def _():
            dk_sc[...] = jnp.zeros_like(dk_sc)
            dv_sc[...] = jnp.zeros_like(dv_sc)

        def step(masked):
            st = _dot(k_ref[...], q_ref[...], "nt")
            if masked:
                st = jnp.where(_causal_mask(blk, 1), st, NEG)
            pt = jnp.exp(st - lse_ref[...])
            dpt = _dot(v_ref[...], do_ref[...], "nt")
            dv_sc[...] += _dot(pt, do_ref[...], "nn")
            dk_sc[...] += _dot(pt * (dpt - dl_ref[...]), q_ref[...], "nn")

        _on_blocks(s, j + s < nq, step)

        @pl.when(s == nq - 1)
        def _():
            dk_ref[...] = dk_sc[...]
            dv_ref[...] = dv_sc[...]

    q_row = lambda j, s: jnp.minimum(j + s, nq - 1)
    row = pl.BlockSpec((None, 1, blk), lambda h, j, s: (h, 0, q_row(j, s)))
    return pl.pallas_call(
        body, name="mla_dkv", grid=(H, nq, nq),
        in_specs=[pl.BlockSpec((blk, wq), lambda h, j, s: (q_row(j, s), h)), pl.BlockSpec((blk, wq), lambda h, j, s: (j, h)),
                  pl.BlockSpec((blk, HEAD), lambda h, j, s: (j, h)), pl.BlockSpec((blk, HEAD), lambda h, j, s: (q_row(j, s), h)),
                  row, row],
        out_specs=[pl.BlockSpec((blk, wq), lambda h, j, s: (j, h)), pl.BlockSpec((blk, HEAD), lambda h, j, s: (j, h))],
        out_shape=[jax.ShapeDtypeStruct((T, H * wq), F32), jax.ShapeDtypeStruct((T, H * HEAD), F32)],
        scratch_shapes=[pltpu.VMEM((blk, wq), F32), pltpu.VMEM((blk, HEAD), F32)],
        compiler_params=_cp(3))(q, k, v, do, lse_row, dl_row)


def _row_dot(name, a, b, H, tm=1024):
    T = a.shape[0]
    tm = _tile(T, tm)

    def body(a_ref, b_ref, o_ref):
        o_ref[...] = jnp.sum(a_ref[...].astype(F32) * b_ref[...].astype(F32), axis=-1, keepdims=True)

    blk = pl.BlockSpec((tm, HEAD), lambda h, i: (i, h))
    return pl.pallas_call(body, name=name, grid=(H, T // tm), in_specs=[blk, blk],
                          out_specs=pl.BlockSpec((None, tm, 1), lambda h, i: (h, i, 0)),
                          out_shape=jax.ShapeDtypeStruct((H, T, 1), F32), compiler_params=_cp(2))(a, b)


def _tri(n):
    row = lax.broadcasted_iota(jnp.int32, (n, n), 0)
    col = lax.broadcasted_iota(jnp.int32, (n, n), 1)
    return col <= row, col >= row


def _win_fwd(name, q, k, pd, H, n, L, QB):
    T = q.shape[0]
    U = QB // n

    def body(q_ref, k_ref, v_ref, kp_ref, vp_ref, o_ref, lse_ref):
        i = pl.program_id(1)
        own_ok, before_ok = _tri(n)
        for e in range(U):
            rs = slice(e * n, (e + 1) * n)
            ps = slice((e - 1) * n, e * n)
            k_lo, v_lo = (kp_ref[...], vp_ref[...]) if e == 0 else (k_ref[ps, :], v_ref[ps, :])
            inside = ((i * QB + e * n) % L) != 0
            qu = q_ref[rs, :]
            s_hi = jnp.where(own_ok, _dot(qu, k_ref[rs, :], "nt"), NEG)
            s_lo = jnp.where(before_ok & inside, _dot(qu, k_lo, "nt"), NEG)
            m = jnp.maximum(jnp.max(s_hi, axis=-1, keepdims=True), jnp.max(s_lo, axis=-1, keepdims=True))
            p_hi, p_lo = jnp.exp(s_hi - m), jnp.exp(s_lo - m)
            l = jnp.sum(p_hi, axis=-1, keepdims=True) + jnp.sum(p_lo, axis=-1, keepdims=True)
            acc = _dot(p_hi, v_ref[rs, :], "nn") + _dot(p_lo, v_lo, "nn")
            o_ref[rs, :] = acc / l
            lse_ref[rs, :] = m + jnp.log(l)

    cur = lambda c0: pl.BlockSpec((QB, HEAD), lambda h, i: (i, c0 + h))
    prev = lambda c0: pl.BlockSpec((n, HEAD), lambda h, i: (jnp.maximum(i * U - 1, 0), c0 + h))
    return pl.pallas_call(
        body, name=name, grid=(H, T // QB), in_specs=[cur(0), cur(0), cur(2 * H), prev(0), prev(2 * H)],
        out_specs=[pl.BlockSpec((QB, HEAD), lambda h, i: (i, h)), pl.BlockSpec((None, QB, 1), lambda h, i: (h, i, 0))],
        out_shape=[jax.ShapeDtypeStruct((T, H * HEAD), F32), jax.ShapeDtypeStruct((H, T, 1), F32)],
        compiler_params=_cp(2))(q, k, pd, k, pd)


def _win_dq(name, q, k, pd, do, lse, dl, H, n, L, QB):
    T = q.shape[0]
    U = QB // n

    def body(q_ref, k_ref, v_ref, kp_ref, vp_ref, do_ref, lse_ref, dl_ref, dq_ref):
        i = pl.program_id(1)
        own_ok, before_ok = _tri(n)
        for e in range(U):
            rs = slice(e * n, (e + 1) * n)
            ps = slice((e - 1) * n, e * n)
            k_lo, v_lo = (kp_ref[...], vp_ref[...]) if e == 0 else (k_ref[ps, :], v_ref[ps, :])
            inside = ((i * QB + e * n) % L) != 0
            qu, dou, lse_u, dl_u = q_ref[rs, :], do_ref[rs, :], lse_ref[rs, :], dl_ref[rs, :]
            p_hi = jnp.exp(jnp.where(own_ok, _dot(qu, k_ref[rs, :], "nt"), NEG) - lse_u)
            p_lo = jnp.exp(jnp.where(before_ok & inside, _dot(qu, k_lo, "nt"), NEG) - lse_u)
            ds_hi = p_hi * (_dot(dou, v_ref[rs, :], "nt") - dl_u)
            ds_lo = p_lo * (_dot(dou, v_lo, "nt") - dl_u)
            dq_ref[rs, :] = _dot(ds_hi, k_ref[rs, :], "nn") + _dot(ds_lo, k_lo, "nn")

    cur = lambda c0: pl.BlockSpec((QB, HEAD), lambda h, i: (i, c0 + h))
    prev = lambda c0: pl.BlockSpec((n, HEAD), lambda h, i: (jnp.maximum(i * U - 1, 0), c0 + h))
    flat = pl.BlockSpec((QB, HEAD), lambda h, i: (i, h))
    col = pl.BlockSpec((None, QB, 1), lambda h, i: (h, i, 0))
    return pl.pallas_call(
        body, name=name, grid=(H, T // QB), in_specs=[cur(0), cur(0), cur(2 * H), prev(0), prev(2 * H), flat, col, col],
        out_specs=flat, out_shape=jax.ShapeDtypeStruct((T, H * HEAD), F32), compiler_params=_cp(2))(q, k, pd, k, pd, do, lse, dl)


def _win_dkv(name, q, k, pd, do, lse_row, dl_row, H, n, L, QB):
    T = q.shape[0]
    U = QB // n
    last = T // n - 1

    def body(q_ref, k_ref, v_ref, do_ref, lse_ref, dl_ref, qn_ref, don_ref, lsen_ref, dln_ref, dk_ref, dv_ref):
        i = pl.program_id(1)
        own_ok, after_ok = _tri_t(n)
        for e in range(U):
            rs = slice(e * n, (e + 1) * n)
            ns = slice((e + 1) * n, (e + 2) * n)
            if e < U - 1:
                qn, don, lsen, dln = q_ref[ns, :], do_ref[ns, :], lse_ref[:, ns], dl_ref[:, ns]
            else:
                qn, don, lsen, dln = qn_ref[...], don_ref[...], lsen_ref[...], dln_ref[...]
            nxt = i * QB + (e + 1) * n
            inside = ((nxt % L) != 0) & (nxt < T)
            ku, vu, qu, dou = k_ref[rs, :], v_ref[rs, :], q_ref[rs, :], do_ref[rs, :]
            pt_own = jnp.exp(jnp.where(own_ok, _dot(ku, qu, "nt"), NEG) - lse_ref[:, rs])
            pt_aft = jnp.exp(jnp.where(after_ok & inside, _dot(ku, qn, "nt"), NEG) - lsen)
            dst_own = pt_own * (_dot(vu, dou, "nt") - dl_ref[:, rs])
            dst_aft = pt_aft * (_dot(vu, don, "nt") - dln)
            dv_ref[rs, :] = _dot(pt_own, dou, "nn") + _dot(pt_aft, don, "nn")
            dk_ref[rs, :] = _dot(dst_own, qu, "nn") + _dot(dst_aft, qn, "nn")

    cur = lambda c0: pl.BlockSpec((QB, HEAD), lambda h, i: (i, c0 + h))
    flat = pl.BlockSpec((QB, HEAD), lambda h, i: (i, h))
    row = pl.BlockSpec((None, 1, QB), lambda h, i: (h, 0, i))
    nxt_unit = lambda h, i: jnp.minimum((i + 1) * U, last)
    return pl.pallas_call(
        body, name=name, grid=(H, T // QB),
        in_specs=[cur(0), cur(0), cur(2 * H), flat, row, row,
                  pl.BlockSpec((n, HEAD), lambda h, i: (nxt_unit(h, i), h)),
                  pl.BlockSpec((n, HEAD), lambda h, i: (nxt_unit(h, i), h)),
                  pl.BlockSpec((None, 1, n), lambda h, i: (h, 0, nxt_unit(h, i))),
                  pl.BlockSpec((None, 1, n), lambda h, i: (h, 0, nxt_unit(h, i)))],
        out_specs=[flat, flat], out_shape=[jax.ShapeDtypeStruct((T, H * HEAD), F32)] * 2,
        compiler_params=_cp(2))(q, k, pd, do, lse_row, dl_row, q, do, lse_row, dl_row)


def _tri_t(n):
    key = lax.broadcasted_iota(jnp.int32, (n, n), 0)
    qry = lax.broadcasted_iota(jnp.int32, (n, n), 1)
    return key <= qry, key >= qry


def _merge_groups(os_, lses, H, tm=512):
    G = len(os_)
    T = os_[0].shape[0]
    tm = _tile(T, tm)

    def body(*refs):
        o_refs, l_refs, o_out, lse_out = refs[:G], refs[G:2 * G], refs[2 * G], refs[2 * G + 1]
        ls = [r[...] for r in l_refs]
        m = ls[0]
        for x in ls[1:]:
            m = jnp.maximum(m, x)
        ws = [jnp.exp(x - m) for x in ls]
        tot = ws[0]
        for x in ws[1:]:
            tot = tot + x
        acc = ws[0] * o_refs[0][...]
        for x, r in zip(ws[1:], o_refs[1:]):
            acc = acc + x * r[...]
        o_out[...] = (acc / tot).astype(BF16)
        lse_out[...] = m + jnp.log(tot)

    flat = pl.BlockSpec((tm, HEAD), lambda h, i: (i, h))
    col = pl.BlockSpec((None, tm, 1), lambda h, i: (h, i, 0))
    return pl.pallas_call(body, name="dil_merge", grid=(H, T // tm), in_specs=[flat] * G + [col] * G, out_specs=[flat, col],
                          out_shape=[jax.ShapeDtypeStruct((T, H * HEAD), BF16), jax.ShapeDtypeStruct((H, T, 1), F32)],
                          compiler_params=_cp(2))(*os_, *lses)


def _dil_prep_fwd_pm(name, pd, tabs, gq, gk, H, tm=256):
    T = pd.shape[0]
    tm = _tile(T, tm)
    scale = HEAD ** -0.5

    def body(p_ref, c_ref, sa_ref, sb_ref, gq_ref, gk_ref, qo, ko):
        c, sa, sb = c_ref[...], sa_ref[...], sb_ref[...]
        for h in range(H):
            q = p_ref[:, h * 128:(h + 1) * 128].astype(F32)
            qo[:, h * 128:(h + 1) * 128] = (_rope(q * _rms_stats(q, HEAD) * gq_ref[...], c, sa, sb, 16) * scale).astype(BF16)
            k = p_ref[:, (H + h) * 128:(H + h + 1) * 128].astype(F32)
            ko[:, h * 128:(h + 1) * 128] = _rope(k * _rms_stats(k, HEAD) * gk_ref[...], c, sa, sb, 16).astype(BF16)

    lane = pl.BlockSpec((tm, 128), lambda i: (i, 0))
    gain = pl.BlockSpec((1, 128), lambda i: (0, 0))
    out = pl.BlockSpec((tm, H * 128), lambda i: (i, 0))
    return pl.pallas_call(body, name=name, grid=(T // tm,),
                          in_specs=[pl.BlockSpec((tm, 3 * H * 128), lambda i: (i, 0)), lane, lane, lane, gain, gain],
                          out_specs=[out, out], out_shape=[jax.ShapeDtypeStruct((T, H * 128), BF16)] * 2,
                          compiler_params=_cp(1))(pd, *tabs, gq, gk)


def _dil_prep_bwd_pm(name, dq, dk, dv, pd, tabs, gq, gk, H, tm=256):
    T = pd.shape[0]
    tm = _tile(T, tm)
    scale = HEAD ** -0.5

    def body(dq_ref, dk_ref, dv_ref, p_ref, c_ref, sa_ref, sb_ref, gq_ref, gk_ref, dp_ref, dgq, dgk):
        c, sa, sb = c_ref[...], sa_ref[...], sb_ref[...]
        sq = jnp.zeros((1, 128), F32)
        sk = jnp.zeros((1, 128), F32)
        for h in range(H):
            hs = slice(h * 128, (h + 1) * 128)
            q = p_ref[:, hs].astype(F32)
            dx, dgp = _rms_bwd_rows(_rope_t(dq_ref[:, hs] * scale, c, sa, sb, 16), q, gq_ref[...], HEAD)
            dp_ref[:, hs] = dx.astype(BF16)
            sq += jnp.sum(dgp, axis=0, keepdims=True)
            ks = slice((H + h) * 128, (H + h + 1) * 128)
            k = p_ref[:, ks].astype(F32)
            dx, dgp = _rms_bwd_rows(_rope_t(dk_ref[:, hs], c, sa, sb, 16), k, gk_ref[...], HEAD)
            dp_ref[:, ks] = dx.astype(BF16)
            sk += jnp.sum(dgp, axis=0, keepdims=True)
            dp_ref[:, (2 * H + h) * 128:(2 * H + h + 1) * 128] = dv_ref[:, hs].astype(BF16)

        @pl.when(pl.program_id(0) == 0)
        def _():
            dgq[...] = jnp.zeros_like(dgq)
            dgk[...] = jnp.zeros_like(dgk)

        dgq[...] += sq
        dgk[...] += sk

    lane = pl.BlockSpec((tm, 128), lambda i: (i, 0))
    flat = pl.BlockSpec((tm, H * 128), lambda i: (i, 0))
    vec = pl.BlockSpec((1, 128), lambda i: (0, 0))
    return pl.pallas_call(body, name=name, grid=(T // tm,),
                          in_specs=[flat, flat, flat, pl.BlockSpec((tm, 3 * H * 128), lambda i: (i, 0)),
                                    lane, lane, lane, vec, vec],
                          out_specs=[pl.BlockSpec((tm, 3 * H * 128), lambda i: (i, 0)), vec, vec],
                          out_shape=[jax.ShapeDtypeStruct((T, 3 * H * 128), BF16), jax.ShapeDtypeStruct((1, 128), F32),
                                     jax.ShapeDtypeStruct((1, 128), F32)],
                          compiler_params=_cp(1))(dq, dk, dv, pd, *tabs, gq, gk)


def _to_phase(a, d, axis=0):
    if d == 1:
        return a
    sh = a.shape
    T = sh[axis]
    b = a.reshape(*sh[:axis], T // d, d, *sh[axis + 1:])
    return jnp.swapaxes(b, axis, axis + 1).reshape(sh)


def _from_phase(a, d, axis=0):
    if d == 1:
        return a
    sh = a.shape
    T = sh[axis]
    b = a.reshape(*sh[:axis], d, T // d, *sh[axis + 1:])
    return jnp.swapaxes(b, axis, axis + 1).reshape(sh)


def _ffn_up(name, n, wg, wu, tm=512):
    T, D = n.shape
    nd, _, fc = wg.shape
    tm = _tile(T, tm)

    def epilogue(accs, in_refs, out_refs):
        a, b = accs
        out_refs[0][...] = a.astype(BF16)
        out_refs[1][...] = b.astype(BF16)
        out_refs[2][...] = (a * _sigmoid(a) * b).astype(BF16)

    w_spec = pl.BlockSpec((None, D, fc), lambda j, i: (j, 0, 0))
    o_spec = pl.BlockSpec((None, tm, fc), lambda j, i: (j, i, 0))
    sh = jax.ShapeDtypeStruct((nd, T, fc), BF16)
    return _mm(name, (nd, T // tm), [n, wg, wu], [pl.BlockSpec((tm, D), lambda j, i: (i, 0)), w_spec, w_spec],
               [(0, 1, "nn", 0), (0, 2, "nn", 1)], 2, None, epilogue, [sh, sh, sh], [o_spec] * 3, None)


def _ffn_down(name, s, wd, res, tm=512, tn=512):
    nd, T, fc = s.shape
    D = wd.shape[2]
    tm, tn = _tile(T, tm), _tile(D, tn)

    def body(s_ref, w_ref, r_ref, o_ref):
        acc = _dot(s_ref[0], w_ref[0], "nn")
        for j in range(1, nd):
            acc += _dot(s_ref[j], w_ref[j], "nn")
        o_ref[...] = r_ref[...] + 0.5 * acc

    mn = pl.BlockSpec((tm, tn), lambda i, j: (i, j))
    return pl.pallas_call(body, name=name, grid=(T // tm, D // tn),
                          in_specs=[pl.BlockSpec((nd, tm, fc), lambda i, j: (0, i, 0)),
                                    pl.BlockSpec((nd, fc, tn), lambda i, j: (0, 0, j)), mn],
                          out_specs=mn, out_shape=jax.ShapeDtypeStruct((T, D), F32), compiler_params=_cp(2))(s, wd, res)


def _ffn_bwd_act(name, dxb, wd, a, b, tm=512):
    T, D = dxb.shape
    nd, fc, _ = wd.shape
    tm = _tile(T, tm)

    def epilogue(accs, in_refs, out_refs):
        ds = 0.5 * accs[0]
        av, bv = in_refs[2][...].astype(F32), in_refs[3][...].astype(F32)
        sg = _sigmoid(av)
        out_refs[0][...] = (ds * bv * sg * (1.0 + av * (1.0 - sg))).astype(BF16)
        out_refs[1][...] = (ds * av * sg).astype(BF16)

    act = pl.BlockSpec((None, tm, fc), lambda j, i: (j, i, 0))
    sh = jax.ShapeDtypeStruct((nd, T, fc), BF16)
    return _mm(name, (nd, T // tm), [dxb, wd, a, b],
               [pl.BlockSpec((tm, D), lambda j, i: (i, 0)), pl.BlockSpec((None, fc, D), lambda j, i: (j, 0, 0)), act, act],
               [(0, 1, "nt", 0)], 1, None, epilogue, [sh, sh], [act, act], None)


def _ffn_dwd(name, s, dxb, tk=2048):
    nd, T, fc = s.shape
    D = dxb.shape[1]
    tk = _tile(T, tk)

    def epilogue(accs, in_refs, out_refs):
        out_refs[0][...] = (0.5 * accs[0]).astype(BF16)

    return _mm(name, (nd, T // tk), [s, dxb],
               [pl.BlockSpec((None, tk, fc), lambda j, k: (j, k, 0)), pl.BlockSpec((tk, D), lambda j, k: (k, 0))],
               [(0, 1, "tn", 0)], 1, (fc, D), epilogue, [jax.ShapeDtypeStruct((nd, fc, D), BF16)],
               [pl.BlockSpec((None, fc, D), lambda j, k: (j, 0, 0))], 1)[0]


def _ffn_dw(name, n, dact, tk=2048):
    T, D = n.shape
    nd, _, fc = dact.shape
    tk = _tile(T, tk)

    def epilogue(accs, in_refs, out_refs):
        out_refs[0][...] = accs[0].astype(BF16)

    return _mm(name, (nd, T // tk), [n, dact],
               [pl.BlockSpec((tk, D), lambda j, k: (k, 0)), pl.BlockSpec((None, tk, fc), lambda j, k: (j, k, 0))],
               [(0, 1, "tn", 0)], 1, (D, fc), epilogue, [jax.ShapeDtypeStruct((nd, D, fc), BF16)],
               [pl.BlockSpec((None, D, fc), lambda j, k: (j, 0, 0))], 1)[0]


def _ffn_dn(name, da, db, wg, wu, tm=512, tn=256):
    nd, T, fc = da.shape
    D = wg.shape[1]
    tm, tn = _tile(T, tm), _tile(D, tn)

    def body(da_ref, db_ref, wg_ref, wu_ref, o_ref):
        acc = _dot(da_ref[0], wg_ref[0], "nt") + _dot(db_ref[0], wu_ref[0], "nt")
        for j in range(1, nd):
            acc += _dot(da_ref[j], wg_ref[j], "nt") + _dot(db_ref[j], wu_ref[j], "nt")
        o_ref[...] = acc

    act = pl.BlockSpec((nd, tm, fc), lambda i, j: (0, i, 0))
    w_spec = pl.BlockSpec((nd, tn, fc), lambda i, j: (0, j, 0))
    return pl.pallas_call(body, name=name, grid=(T // tm, D // tn), in_specs=[act, act, w_spec, w_spec],
                          out_specs=pl.BlockSpec((tm, tn), lambda i, j: (i, j)),
                          out_shape=jax.ShapeDtypeStruct((T, D), F32), compiler_params=_cp(2))(da, db, wg, wu)


def _ffn_forward(tag, x, g, wg, wu, wd):
    n = _rmsnorm_fwd(tag + "_norm", x, g)
    a, b, s = _ffn_up(tag + "_up", n, wg, wu)
    return _ffn_down(tag + "_down", s, wd, x), (n, a, b, s)


def _ffn_backward(tag, dx, dxb, x, g, wg, wu, wd, saved):
    n, a, b, s = saved
    da, db = _ffn_bwd_act(tag + "_bwd_act", dxb, wd, a, b)
    d_wd = _ffn_dwd(tag + "_dwd", s, dxb)
    d_wg, d_wu = _ffn_dw(tag + "_dwg", n, da), _ffn_dw(tag + "_dwu", n, db)
    dn = _ffn_dn(tag + "_dn", da, db, wg, wu)
    dx_in, dxb_in, dg = _rmsnorm_bwd(tag + "_norm_bwd", dn, x, g, dx)
    return dx_in, dxb_in, dg, d_wg, d_wu, d_wd


def _place():
    x, y, c = lax.axis_index("x"), lax.axis_index("y"), lax.axis_index("c")
    return x, y, c, [(1 - x, y), (x, 1 - y), (1 - x, 1 - y)]


def _allgather(name, shards):
    n = len(shards)

    def body(*refs):
        ins, outs = refs[:n], refs[n:2 * n]
        send_sems, recv_sems, local_sems = refs[2 * n:]
        x, y, c, chips = _place()
        me, sibling = (x, y, c), (x, y, 1 - c)

        def slot(a, p):
            return outs[a].at[4 * p[0] + 2 * p[1] + p[2]]

        def copy(a, kk, block, to, src=None):
            return pltpu.make_async_remote_copy(
                src_ref=slot(a, block) if src is None else src, dst_ref=slot(a, block),
                send_sem=send_sems.at[a * 7 + kk], recv_sem=recv_sems.at[a * 7 + kk],
                device_id=to, device_id_type=MESH)

        mine = [pltpu.make_async_copy(ins[a], slot(a, me), local_sems.at[a]) for a in range(n)]
        for cp in mine:
            cp.start()
        first = []
        for a in range(n):
            first.append(copy(a, 0, me, sibling, src=ins[a]))
            first += [copy(a, 1 + j, me, (*chip, c), src=ins[a]) for j, chip in enumerate(chips)]
        for cp in first:
            cp.start()
        passed = []
        for j, chip in enumerate(chips):
            for a in range(n):
                copy(a, 1 + j, (*chip, c), me).wait_recv()
                fwd = copy(a, 4 + j, (*chip, c), sibling)
                fwd.start()
                passed.append(fwd)
        for a in range(n):
            copy(a, 0, sibling, me).wait_recv()
        for j, chip in enumerate(chips):
            for a in range(n):
                copy(a, 4 + j, (*chip, 1 - c), me).wait_recv()
        for cp in first + passed:
            cp.wait_send()
        for cp in mine:
            cp.wait()

    return pl.pallas_call(
        body, name=name, in_specs=[HBM_SPEC] * n, out_specs=[HBM_SPEC] * n,
        out_shape=[jax.ShapeDtypeStruct((N_DEV, *s.shape), s.dtype) for s in shards],
        scratch_shapes=[pltpu.SemaphoreType.DMA((7 * n,)), pltpu.SemaphoreType.DMA((7 * n,)), pltpu.SemaphoreType.DMA((n,))],
    )(*shards)


def _rs_sibling(name, grads):
    n = len(grads)

    def body(*refs):
        ins, outs = refs[:n], refs[n:2 * n]
        send_sems, recv_sems = refs[2 * n:]
        x, y, c, _ = _place()
        copies = [pltpu.make_async_remote_copy(
            src_ref=ins[a].at[:, 1 - c], dst_ref=outs[a], send_sem=send_sems.at[a], recv_sem=recv_sems.at[a],
            device_id=(x, y, 1 - c), device_id_type=MESH) for a in range(n)]
        for cp in copies:
            cp.start()
        for cp in copies:
            cp.wait()

    return pl.pallas_call(
        body, name=name, in_specs=[HBM_SPEC] * n, out_specs=[HBM_SPEC] * n,
        out_shape=[jax.ShapeDtypeStruct((g.shape[0], *g.shape[2:]), g.dtype) for g in grads],
        scratch_shapes=[pltpu.SemaphoreType.DMA((n,)), pltpu.SemaphoreType.DMA((n,))],
    )(*grads)


def _pair_add(name, own, got, core):
    nch, _, K, N = own.shape
    tr = _row_tile(K, N)

    def body(c_ref, own_ref, got_ref, o_ref):
        o_ref[...] = (own_ref[...].astype(F32) + got_ref[...].astype(F32)).astype(o_ref.dtype)

    grid_spec = pltpu.PrefetchScalarGridSpec(
        num_scalar_prefetch=1, grid=(nch, K // tr),
        in_specs=[pl.BlockSpec((None, None, tr, N), lambda k, r, c_ref: (k, c_ref[0], r, 0)),
                  pl.BlockSpec((None, tr, N), lambda k, r, c_ref: (k, r, 0))],
        out_specs=pl.BlockSpec((None, tr, N), lambda k, r, c_ref: (k, r, 0)))
    return pl.pallas_call(body, name=name, grid_spec=grid_spec, out_shape=jax.ShapeDtypeStruct((nch, K, N), own.dtype),
                          compiler_params=_cp(2))(core, own, got)


def _rs_chips(name, sums):
    n = len(sums)

    def body(*refs):
        ins, outs = refs[:n], refs[n:2 * n]
        send_sems, recv_sems, local_sems = refs[2 * n:]
        x, y, c, chips = _place()
        k_me = 2 * x + y
        mine = [pltpu.make_async_copy(ins[a].at[k_me], outs[a].at[k_me], local_sems.at[a]) for a in range(n)]
        for cp in mine:
            cp.start()

        def copy(a, r, slot):
            chip = chips[r]
            k_peer = 2 * chip[0] + chip[1]
            return pltpu.make_async_remote_copy(
                src_ref=ins[a].at[k_peer], dst_ref=outs[a].at[k_me if slot == "theirs" else k_peer],
                send_sem=send_sems.at[3 * a + r], recv_sem=recv_sems.at[3 * a + r],
                device_id=(*chip, c), device_id_type=MESH)

        sends = [copy(a, r, "theirs") for a in range(n) for r in range(3)]
        for cp in sends:
            cp.start()
        for a in range(n):
            for r in range(3):
                copy(a, r, "mine").wait_recv()
        for cp in sends:
            cp.wait_send()
        for cp in mine:
            cp.wait()

    return pl.pallas_call(
        body, name=name, in_specs=[HBM_SPEC] * n, out_specs=[HBM_SPEC] * n,
        out_shape=[jax.ShapeDtypeStruct(s.shape, s.dtype) for s in sums],
        scratch_shapes=[pltpu.SemaphoreType.DMA((3 * n,)), pltpu.SemaphoreType.DMA((3 * n,)), pltpu.SemaphoreType.DMA((n,))],
    )(*sums)


def _row_tile(K, N):
    limit = max(16, 262144 // N)
    t = 1
    while t * 2 <= limit and K % (t * 2) == 0:
        t *= 2
    return t if t >= 16 else K


def _adamw(name, parts, w, m, v):
    P, K, N = parts.shape
    tr = _row_tile(K, N)

    def body(p_ref, w_ref, m_ref, v_ref, g_ref, d_ref, nm_ref, nv_ref):
        g = p_ref[0].astype(F32)
        for i in range(1, P):
            g = g + p_ref[i].astype(F32)
        m_new = ADAM_B1 * m_ref[...] + (1.0 - ADAM_B1) * g
        v_new = ADAM_B2 * v_ref[...] + (1.0 - ADAM_B2) * (g * g)
        m_hat = m_new / (1.0 - ADAM_B1 ** ADAM_STEP)
        v_hat = v_new / (1.0 - ADAM_B2 ** ADAM_STEP)
        g_ref[...] = g
        d_ref[...] = -ADAM_LR * (m_hat / (jnp.sqrt(v_hat) + ADAM_EPS) + ADAM_WD * w_ref[...])
        nm_ref[...] = m_new
        nv_ref[...] = v_new

    row = pl.BlockSpec((tr, N), lambda r: (r, 0))
    sh = jax.ShapeDtypeStruct((K, N), F32)
    return pl.pallas_call(body, name=name, grid=(K // tr,),
                          in_specs=[pl.BlockSpec((P, tr, N), lambda r: (0, r, 0)), row, row, row],
                          out_specs=[row] * 4, out_shape=[sh] * 4, compiler_params=_cp(1))(parts, w, m, v)


def _merge_fwd(o_mla, wbm, o_dil, wbd, g0, g1, tm=512, tn=512):
    T, K1 = o_mla.shape
    K2, D = o_dil.shape[1], wbm.shape[1]
    tm, tn = _tile(T, tm), _tile(D, tn)

    def epilogue(accs, in_refs, out_refs):
        a, b = accs
        out_refs[0][...] = a.astype(BF16)
        out_refs[1][...] = b.astype(BF16)
        out_refs[2][...] = (in_refs[4][...].astype(F32) * a + in_refs[5][...].astype(F32) * b).astype(BF16)

    mn = pl.BlockSpec((tm, tn), lambda i, j: (i, j))
    sh = jax.ShapeDtypeStruct((T, D), BF16)
    return _mm("merge_fwd", (T // tm, D // tn), [o_mla, wbm, o_dil, wbd, g0, g1],
               [pl.BlockSpec((tm, K1), lambda i, j: (i, 0)), pl.BlockSpec((K1, tn), lambda i, j: (0, j)),
                pl.BlockSpec((tm, K2), lambda i, j: (i, 0)), pl.BlockSpec((K2, tn), lambda i, j: (0, j)), mn, mn],
               [(0, 1, "nn", 0), (2, 3, "nn", 1)], 2, None, epilogue, [sh, sh, sh], [mn, mn, mn], None)


def _ple_loss(n4, wpg, pe, wpp, x3, tgt, tm=512, tn=512):
    T, D = x3.shape
    Kp = pe.shape[1]
    tm, tn = _tile(T, tm), _tile(D, tn)

    def epilogue(accs, in_refs, out_refs):
        z, proj = accs
        pg = _sigmoid(z)
        err = in_refs[4][...] + pg * proj - in_refs[5][...]
        dy = err * (1.0 / D)
        out_refs[0][...] = dy
        out_refs[1][...] = (dy * proj * pg * (1.0 - pg)).astype(BF16)
        out_refs[2][...] = (dy * pg).astype(BF16)

        @pl.when(pl.program_id(1) == 0)
        def _():
            out_refs[3][...] = jnp.zeros_like(out_refs[3])

        out_refs[3][...] += jnp.sum(err * err, axis=-1, keepdims=True)

    mn = pl.BlockSpec((tm, tn), lambda i, j: (i, j))
    return _mm("ple_loss", (T // tm, D // tn), [n4, wpg, pe, wpp, x3, tgt],
               [pl.BlockSpec((tm, D), lambda i, j: (i, 0)), pl.BlockSpec((D, tn), lambda i, j: (0, j)),
                pl.BlockSpec((tm, Kp), lambda i, j: (i, 0)), pl.BlockSpec((Kp, tn), lambda i, j: (0, j)), mn, mn],
               [(0, 1, "nn", 0), (2, 3, "nn", 1)], 2, None, epilogue,
               [jax.ShapeDtypeStruct((T, D), F32), jax.ShapeDtypeStruct((T, D), BF16), jax.ShapeDtypeStruct((T, D), BF16),
                jax.ShapeDtypeStruct((T, 1), F32)],
               [mn, mn, mn, pl.BlockSpec((tm, 1), lambda i, j: (i, 0))], None)


def _epi_sigmoid(acc, ex, outs):
    outs[0][...] = _sigmoid(acc).astype(outs[0].dtype)


def _epi_add(acc, ex, outs):
    outs[0][...] = (acc + ex[0][...].astype(F32)).astype(outs[0].dtype)


def _epi_dmerge(acc, ex, outs):
    mp, dp, g0, g1 = [e[...].astype(F32) for e in ex]
    outs[0][...] = (acc * g0).astype(BF16)
    outs[1][...] = (acc * g1).astype(BF16)
    outs[2][...] = (acc * mp * g0 * (1.0 - g0)).astype(BF16)
    outs[3][...] = (acc * dp * g1 * (1.0 - g1)).astype(BF16)


_WEIGHTS = ("g_ffn1", "w1_gate", "w1_up", "w1_down", "g_mix", "w_in", "g_cq", "w_uq", "g_ckv", "w_ukv", "g_q_mla", "g_k_mla",
            "g_q_dil", "g_k_dil", "w_br_mla", "w_br_dil", "w_o", "g_ffn2", "w2_gate", "w2_up", "w2_down", "g_ple",
            "w_ple_gate", "w_ple_proj")
_MATRICES = ("w1_gate", "w1_up", "w1_down", "w_in", "w_uq", "w_ukv", "w_br_mla", "w_br_dil", "w_o", "w2_gate", "w2_up",
             "w2_down", "w_ple_gate", "w_ple_proj")
_GAINS = tuple(n for n in _WEIGHTS if n not in _MATRICES)
MLA_BLOCK = 1024
DIL_ROWS = 2048


def _cols(g3):
    nd, K, n = g3.shape
    return g3.transpose(1, 0, 2).reshape(K, nd * n)


def _uncols(m):
    K, n = m.shape
    return m.reshape(K, N_DEV, n // N_DEV).transpose(1, 0, 2)


def _rows(g3):
    nd, k, N = g3.shape
    return g3.reshape(nd * k, N)


def _unrows(m):
    K, N = m.shape
    return m.reshape(N_DEV, K // N_DEV, N)


def _pack_gains(vals):
    flat = jnp.concatenate([vals[n].reshape(-1) for n in _GAINS])
    pad = (-flat.shape[0]) % 2048
    return jnp.pad(flat, (0, pad)).reshape(-1, 128)


def _unpack_gains(packed, like):
    flat = packed.reshape(-1)
    out, off = {}, 0
    for n in _GAINS:
        size = int(np.prod(like[n].shape))
        out[n] = flat[off:off + size].reshape(like[n].shape)
        off += size
    return out


def _train_step(x, p, positions, loss_target, W, M, V):
    T, D = x.shape[1], x.shape[2]
    xs, tgt, pe = x[0], loss_target[0], p[0, 0]
    pos_col = positions.reshape(T, 1).astype(F32)
    w = {n: (a[0] if n in _MATRICES else a.reshape(1, -1)) for n, a in W.items()}

    gathered = dict(zip(_MATRICES, _allgather("ag_weights", [w[n].astype(BF16) for n in _MATRICES])))
    nq_l, nkv_l = w["g_cq"].shape[-1], w["g_ckv"].shape[-1]
    H = w["w_uq"].shape[1] * N_DEV // MLA_QK
    G = len(DIL_GROUPS)
    off_kr = nq_l + nkv_l
    off_dil = off_kr + MLA_ROPE
    off_gate = off_dil + G * 3 * H * HEAD
    kr_block = off_kr // 128
    w_in = _cols(gathered["w_in"])
    wa = jnp.pad(w_in[:, :off_dil], ((0, 0), (0, 128 - MLA_ROPE)))
    wdil, wg0, wg1 = w_in[:, off_dil:off_gate], w_in[:, off_gate:off_gate + D], w_in[:, off_gate + D:]
    wuq = jnp.pad(_cols(gathered["w_uq"]).reshape(nq_l, H, MLA_QK), ((0, 0), (0, 0), (0, MLA_PAD - MLA_QK))).reshape(nq_l, H * MLA_PAD)
    wukv = _cols(gathered["w_ukv"])
    wbm, wbd, wpp = _cols(gathered["w_br_mla"]), _cols(gathered["w_br_dil"]), _cols(gathered["w_ple_proj"])
    wo, wpg = _rows(gathered["w_o"]), _rows(gathered["w_ple_gate"])
    gq_mla = jnp.pad(w["g_q_mla"], ((0, 0), (0, MLA_PAD - MLA_QK)))
    gk_mla = jnp.pad(w["g_k_mla"], ((0, 0), (0, MLA_PAD - MLA_QK)))
    gq_dil, gk_dil = w["g_q_dil"].reshape(G, 1, HEAD), w["g_k_dil"].reshape(G, 1, HEAD)

    half_m, half_d = MLA_ROPE // 2, DIL_ROT // 2
    inv_m = ROPE_THETA ** (-jnp.arange(half_m, dtype=F32) * 2.0 / MLA_ROPE)
    inv_d = ROPE_THETA ** (-jnp.arange(half_d, dtype=F32) * 2.0 / DIL_ROT)
    inv_m = jnp.tile(inv_m, 128 // half_m).reshape(1, 128)
    inv_d = jnp.tile(inv_d, 128 // half_d).reshape(1, 128)
    tabs = _rope_tables(pos_col, inv_m, inv_d)
    tabs_m, tabs_d = tabs[:3], tabs[3:]

    mla_blk, dil_qb = _tile(T, MLA_BLOCK), _tile(T, DIL_ROWS)
    gw = 3 * H * HEAD
    dils = [d for _, d in DIL_GROUPS]
    units = [win // d for win, d in DIL_GROUPS]
    wdil_g = [wdil[:, g * gw:(g + 1) * gw] for g in range(G)]
    tabs_g = [[_to_phase(t, d) for t in tabs_d] for d in dils]

    x1, ffn1 = _ffn_forward("ffn1", xs, w["g_ffn1"], gathered["w1_gate"], gathered["w1_up"], gathered["w1_down"])
    h = _rmsnorm_fwd("mix_norm", x1, w["g_mix"])
    pa = _mm2d("proj_a", h, wa, "nn", 512, 1024, 4096)
    pd = [_to_phase(_mm2d("proj_dil%d" % g, h, wdil_g[g], "nn", 1024, 1024, 4096, out_dtypes=(BF16,)), dils[g]) for g in range(G)]
    g0 = _mm2d("proj_gate0", h, wg0, "nn", 1024, 1024, 4096, out_dtypes=(BF16,), epilogue=_epi_sigmoid)
    g1 = _mm2d("proj_gate1", h, wg1, "nn", 1024, 1024, 4096, out_dtypes=(BF16,), epilogue=_epi_sigmoid)
    cq, ckv = _lora_fwd(pa, w["g_cq"], w["g_ckv"], nq_l, nkv_l)
    q_raw = _mm2d("q_up", cq, wuq, "nn", 512, 2048, 4096)
    kv = _mm2d("kv_up", ckv, wukv, "nn", 512, 2048, 4096)
    q_att, k_att, v_mla = _mla_prep_fwd(q_raw, kv, pa, tabs_m, gq_mla, gk_mla, H, kr_block)
    o_mla, lse_mla = _mla_fwd(q_att, k_att, v_mla, H, mla_blk)
    qd, kd, o_g, lse_g = [], [], [], []
    for g in range(G):
        qg, kg = _dil_prep_fwd_pm("dil_prep_fwd%d" % g, pd[g], tabs_g[g], gq_dil[g], gk_dil[g], H)
        og, lg = _win_fwd("dil_fwd%d" % g, qg, kg, pd[g], H, units[g], T // dils[g], dil_qb)
        qd.append(qg)
        kd.append(kg)
        o_g.append(_from_phase(og, dils[g]))
        lse_g.append(_from_phase(lg, dils[g], axis=1))
    o_dil, lse_dil = _merge_groups(o_g, lse_g, H)
    mla_p, dil_p, merged = _merge_fwd(o_mla, wbm, o_dil, wbd, g0, g1)
    x2 = _mm2d("out_proj", merged, wo, "nn", 512, 1024, 4096, epilogue=_epi_add, extras=(x1,))
    x3, ffn2 = _ffn_forward("ffn2", x2, w["g_ffn2"], gathered["w2_gate"], gathered["w2_up"], gathered["w2_down"])
    n4 = _rmsnorm_fwd("ple_norm", x3, w["g_ple"])
    dy, dz, dproj, loss_rows = _ple_loss(n4, wpg, pe, wpp, x3, tgt)
    loss = lax.psum((0.5 / D) * jnp.sum(loss_rows), ("x", "y", "c"))

    dW, dG = {}, {}
    dW["w_ple_proj"] = _uncols(_mm2d("d_wpp", pe, dproj, "tn", 1024, 2048, 2048, out_dtypes=(BF16,)))
    dW["w_ple_gate"] = _unrows(_mm2d("d_wpg", n4, dz, "tn", 1024, 1024, 2048, out_dtypes=(BF16,)))
    dn4 = _mm2d("d_n4", dz, wpg, "nt", 512, 1024, 4096)
    dx3, dx3b, dG["g_ple"] = _rmsnorm_bwd("ple_norm_bwd", dn4, x3, w["g_ple"], dy)
    dx2, dx2b, dG["g_ffn2"], dW["w2_gate"], dW["w2_up"], dW["w2_down"] = _ffn_backward(
        "ffn2", dx3, dx3b, x2, w["g_ffn2"], gathered["w2_gate"], gathered["w2_up"], gathered["w2_down"], ffn2)

    d_mla_p, d_dil_p, dpg0, dpg1 = _mm2d("d_merged", dx2b, wo, "nt", 512, 1024, 4096, out_dtypes=(BF16,) * 4,
                                         epilogue=_epi_dmerge, extras=(mla_p, dil_p, g0, g1))
    dW["w_o"] = _unrows(_mm2d("d_wo", merged, dx2b, "tn", 1024, 1024, 2048, out_dtypes=(BF16,)))
    dW["w_br_mla"] = _uncols(_mm2d("d_wbm", o_mla, d_mla_p, "tn", 1024, 1024, 2048, out_dtypes=(BF16,)))
    dW["w_br_dil"] = _uncols(_mm2d("d_wbd", o_dil, d_dil_p, "tn", 1024, 1024, 2048, out_dtypes=(BF16,)))
    do_mla = _mm2d("d_o_mla", d_mla_p, wbm, "nt", 512, 1024, 4096, out_dtypes=(BF16,))
    do_dil = _mm2d("d_o_dil", d_dil_p, wbd, "nt", 512, 1024, 4096, out_dtypes=(BF16,))

    def as_row(a):
        return a.reshape(a.shape[0], 1, a.shape[1])

    dl_mla = _row_dot("mla_delta", do_mla, o_mla, H)
    dq_att = _mla_dq(q_att, k_att, v_mla, do_mla, lse_mla, dl_mla, H, mla_blk)
    dk_att, dv_mla = _mla_dkv(q_att, k_att, v_mla, do_mla, as_row(lse_mla), as_row(dl_mla), H, mla_blk)
    dq_raw, dkv, dkr, dgq, dgk = _mla_prep_bwd(dq_att, dk_att, dv_mla, q_raw, kv, pa, tabs_m, gq_mla, gk_mla, H, kr_block)
    dG["g_q_mla"], dG["g_k_mla"] = dgq[:, :MLA_QK], dgk[:, :MLA_QK]
    d_wuq = _mm2d("d_wuq", cq, dq_raw, "tn", 512, 2048, 2048, out_dtypes=(BF16,))
    dW["w_uq"] = _uncols(d_wuq.reshape(nq_l, H, MLA_PAD)[:, :, :MLA_QK].reshape(nq_l, H * MLA_QK))
    dW["w_ukv"] = _uncols(_mm2d("d_wukv", ckv, dkv, "tn", 512, 2048, 2048, out_dtypes=(BF16,)))
    dcq = _mm2d("d_cq", dq_raw, wuq, "nt", 512, 1024, 4096)
    dckv = _mm2d("d_ckv", dkv, wukv, "nt", 512, 1024, 4096)
    dpa, dG["g_cq"], dG["g_ckv"] = _lora_bwd(dcq, dckv, dkr, pa, w["g_cq"], w["g_ckv"])

    dl_dil = _row_dot("dil_delta", do_dil, o_dil, H)
    dpd, dgqd, dgkd = [], [], []
    for g in range(G):
        d, n, L = dils[g], units[g], T // dils[g]
        do_g, lse_pg, dl_pg = _to_phase(do_dil, d), _to_phase(lse_dil, d, axis=1), _to_phase(dl_dil, d, axis=1)
        dq_g = _win_dq("dil_dq%d" % g, qd[g], kd[g], pd[g], do_g, lse_pg, dl_pg, H, n, L, dil_qb)
        dk_g, dv_g = _win_dkv("dil_dkv%d" % g, qd[g], kd[g], pd[g], do_g, as_row(lse_pg), as_row(dl_pg), H, n, L, dil_qb)
        dp_g, dgq_g, dgk_g = _dil_prep_bwd_pm("dil_prep_bwd%d" % g, dq_g, dk_g, dv_g, pd[g], tabs_g[g], gq_dil[g], gk_dil[g], H)
        dpd.append(_from_phase(dp_g, d))
        dgqd.append(dgq_g)
        dgkd.append(dgk_g)
    dG["g_q_dil"], dG["g_k_dil"] = jnp.concatenate(dgqd).reshape(1, G, HEAD), jnp.concatenate(dgkd).reshape(1, G, HEAD)

    d_wa = _mm2d("d_wa", h, dpa, "tn", 1024, 1024, 2048, out_dtypes=(BF16,))
    d_wdil = [_mm2d("d_wdil%d" % g, h, dpd[g], "tn", 1024, 1024, 2048, out_dtypes=(BF16,)) for g in range(G)]
    d_wg0 = _mm2d("d_wg0", h, dpg0, "tn", 1024, 1024, 2048, out_dtypes=(BF16,))
    d_wg1 = _mm2d("d_wg1", h, dpg1, "tn", 1024, 1024, 2048, out_dtypes=(BF16,))
    dW["w_in"] = _uncols(jnp.concatenate([d_wa[:, :off_dil], *d_wdil, d_wg0, d_wg1], axis=1))
    dh = _mm2d("d_h_a", dpa, wa, "nt", 512, 1024, 4096)
    for g in range(G):
        dh = _mm2d("d_h_dil%d" % g, dpd[g], wdil_g[g], "nt", 512, 1024, 4096, epilogue=_epi_add, extras=(dh,))
    dh = _mm2d("d_h_g0", dpg0, wg0, "nt", 512, 1024, 4096, epilogue=_epi_add, extras=(dh,))
    dh = _mm2d("d_h_g1", dpg1, wg1, "nt", 512, 1024, 4096, epilogue=_epi_add, extras=(dh,))
    dx1, dx1b, dG["g_mix"] = _rmsnorm_bwd("mix_norm_bwd", dh, x1, w["g_mix"], dx2)
    dx0, _, dG["g_ffn1"], dW["w1_gate"], dW["w1_up"], dW["w1_down"] = _ffn_backward(
        "ffn1", dx1, dx1b, xs, w["g_ffn1"], gathered["w1_gate"], gathered["w1_up"], gathered["w1_down"], ffn1)

    halves = [dW[n].reshape(4, 2, *dW[n].shape[1:]) for n in _MATRICES]
    got = _rs_sibling("rs_sibling", halves)
    core = lax.axis_index("c").astype(jnp.int32).reshape(1)
    sums = [_pair_add("rs_add_" + n, own, rec, core) for n, own, rec in zip(_MATRICES, halves, got)]
    reduced = dict(zip(_MATRICES, _rs_chips("rs_chips", sums)))
    grads, deltas, new_m, new_v = {}, {}, {}, {}
    for n in _MATRICES:
        grads[n], deltas[n], new_m[n], new_v[n] = [
            a[None] for a in _adamw("adamw_" + n, reduced[n], w[n], M[n][0], V[n][0])]

    parts = _allgather("ag_gain_grads", [_pack_gains(dG)])[0]
    packed = _adamw("adamw_gains", parts, _pack_gains(W), _pack_gains(M), _pack_gains(V))
    for out, pk in zip((grads, deltas, new_m, new_v), packed):
        out.update(_unpack_gains(pk, W))

    return (loss, dx0[None], *[grads[n] for n in _WEIGHTS], *[deltas[n] for n in _WEIGHTS],
            *[new_m[n] for n in _WEIGHTS], *[new_v[n] for n in _WEIGHTS])


def kernel(x, p, positions, g_ffn1, w1_gate, w1_up, w1_down, g_mix, w_in, g_cq, w_uq, g_ckv, w_ukv, g_q_mla, g_k_mla, g_q_dil, g_k_dil, w_br_mla, w_br_dil, w_o, g_ffn2, w2_gate, w2_up, w2_down, g_ple, w_ple_gate, w_ple_proj, loss_target, m_g_ffn1, m_w1_gate, m_w1_up, m_w1_down, m_g_mix, m_w_in, m_g_cq, m_w_uq, m_g_ckv, m_w_ukv, m_g_q_mla, m_g_k_mla, m_g_q_dil, m_g_k_dil, m_w_br_mla, m_w_br_dil, m_w_o, m_g_ffn2, m_w2_gate, m_w2_up, m_w2_down, m_g_ple, m_w_ple_gate, m_w_ple_proj, v_g_ffn1, v_w1_gate, v_w1_up, v_w1_down, v_g_mix, v_w_in, v_g_cq, v_w_uq, v_g_ckv, v_w_ukv, v_g_q_mla, v_g_k_mla, v_g_q_dil, v_g_k_dil, v_w_br_mla, v_w_br_dil, v_w_o, v_g_ffn2, v_w2_gate, v_w2_up, v_w2_down, v_g_ple, v_w_ple_gate, v_w_ple_proj):
    W = dict(zip(_WEIGHTS, (g_ffn1, w1_gate, w1_up, w1_down, g_mix, w_in, g_cq, w_uq, g_ckv, w_ukv, g_q_mla, g_k_mla, g_q_dil,
                            g_k_dil, w_br_mla, w_br_dil, w_o, g_ffn2, w2_gate, w2_up, w2_down, g_ple, w_ple_gate, w_ple_proj)))
    M = dict(zip(_WEIGHTS, (m_g_ffn1, m_w1_gate, m_w1_up, m_w1_down, m_g_mix, m_w_in, m_g_cq, m_w_uq, m_g_ckv, m_w_ukv, m_g_q_mla,
                            m_g_k_mla, m_g_q_dil, m_g_k_dil, m_w_br_mla, m_w_br_dil, m_w_o, m_g_ffn2, m_w2_gate, m_w2_up,
                            m_w2_down, m_g_ple, m_w_ple_gate, m_w_ple_proj)))
    V = dict(zip(_WEIGHTS, (v_g_ffn1, v_w1_gate, v_w1_up, v_w1_down, v_g_mix, v_w_in, v_g_cq, v_w_uq, v_g_ckv, v_w_ukv, v_g_q_mla,
                            v_g_k_mla, v_g_q_dil, v_g_k_dil, v_w_br_mla, v_w_br_dil, v_w_o, v_g_ffn2, v_w2_gate, v_w2_up,
                            v_w2_down, v_g_ple, v_w_ple_gate, v_w_ple_proj)))
    return _train_step(x, p, positions, loss_target, W, M, V)
```

```python
import numpy as np
import jax
import jax.numpy as jnp
from jax import lax
from jax.experimental import pallas as pl
from jax.experimental.pallas import tpu as pltpu

F32 = jnp.float32
BF16 = jnp.bfloat16

EPS = 1e-6
ROPE_THETA = 500000.0
MLA_NOPE = 128
MLA_ROPE = 64
MLA_QK = MLA_NOPE + MLA_ROPE
MLA_PAD = 256
HEAD = 128
DIL_ROT = 32
DIL_GROUPS = ((128, 1), (512, 4), (2048, 16))
NEG = -1e30
NO_WINDOW = 1 << 30
N_DEV = 8
ADAM_LR, ADAM_B1, ADAM_B2, ADAM_EPS, ADAM_WD, ADAM_STEP = 0.001, 0.9, 0.999, 1e-08, 0.01, 10
VMEM_LIMIT_V7X = 56 * 1024 * 1024
MESH = pl.DeviceIdType.MESH
HBM_SPEC = pl.BlockSpec(memory_space=pltpu.HBM)


def _cp(n_axes):
    return pltpu.CompilerParams(dimension_semantics=("arbitrary",) * n_axes,
                                vmem_limit_bytes=VMEM_LIMIT_V7X)


def _tile(n, t):
    return t if (n >= t and n % t == 0) else n


def _sigmoid(x):
    return 1.0 / (1.0 + jnp.exp(-x))


_DIMS = {"nn": (((1,), (0,)), ((), ())), "nt": (((1,), (1,)), ((), ())), "tn": (((0,), (0,)), ((), ()))}


def _dot(a, b, mode):
    return lax.dot_general(a.astype(BF16), b.astype(BF16), _DIMS[mode], preferred_element_type=F32)


class _Exchange:
    def __init__(self, ins, out_shapes, plan, n_remote, n_local, aliases=None):
        self.ins, self.out_shapes, self.plan = list(ins), list(out_shapes), plan
        self.n_remote, self.n_local, self.aliases = n_remote, n_local, dict(aliases or {})
        self.results = None

    def sems(self):
        return [pltpu.SemaphoreType.DMA((self.n_remote,)), pltpu.SemaphoreType.DMA((self.n_remote,)),
                pltpu.SemaphoreType.DMA((max(self.n_local, 1),))]

    def start(self, in_refs, out_refs, sems):
        remote, local = self.plan(in_refs, out_refs)
        for i, (src, dst_there, _, peer) in enumerate(remote):
            pltpu.make_async_remote_copy(src_ref=src, dst_ref=dst_there, send_sem=sems[0].at[i], recv_sem=sems[1].at[i],
                                         device_id=peer, device_id_type=MESH).start()
        for i, (src, dst) in enumerate(local):
            pltpu.make_async_copy(src, dst, sems[2].at[i]).start()

    def finish(self, in_refs, out_refs, sems):
        remote, local = self.plan(in_refs, out_refs)
        for i, (src, _, dst_here, peer) in enumerate(remote):
            pltpu.make_async_remote_copy(src_ref=src, dst_ref=dst_here, send_sem=sems[0].at[i], recv_sem=sems[1].at[i],
                                         device_id=peer, device_id_type=MESH).wait_recv()
        for i, (src, dst_there, _, peer) in enumerate(remote):
            pltpu.make_async_remote_copy(src_ref=src, dst_ref=dst_there, send_sem=sems[0].at[i], recv_sem=sems[1].at[i],
                                         device_id=peer, device_id_type=MESH).wait_send()
        for i, (src, dst) in enumerate(local):
            pltpu.make_async_copy(src, dst, sems[2].at[i]).wait()


def _run_exchange(name, ex):
    ci = len(ex.ins)

    def body(*refs):
        ins, outs, sems = refs[:ci], refs[ci:ci + len(ex.out_shapes)], refs[ci + len(ex.out_shapes):]
        ex.start(ins, outs, sems)
        ex.finish(ins, outs, sems)

    ex.results = pl.pallas_call(body, name=name, in_specs=[HBM_SPEC] * ci, out_specs=[HBM_SPEC] * len(ex.out_shapes),
                                out_shape=ex.out_shapes, scratch_shapes=ex.sems(), input_output_aliases=ex.aliases)(*ex.ins)
    return ex.results


def _call(body, *, name, grid, in_specs, out_specs, out_shape, operands, scratch=(), comm=None):
    multi = isinstance(out_shape, (list, tuple))
    outs = list(out_shape) if multi else [out_shape]
    ospecs = list(out_specs) if multi else [out_specs]
    if comm is None:
        res = pl.pallas_call(body, name=name, grid=grid, in_specs=list(in_specs), out_specs=ospecs, out_shape=outs,
                             scratch_shapes=list(scratch), compiler_params=_cp(len(grid)))(*operands)
        return res if multi else res[0]
    n_in, n_out, n_scr = len(in_specs), len(outs), len(scratch)
    ci, co = len(comm.ins), len(comm.out_shapes)

    def hosted(*refs):
        bounds = np.cumsum([0, n_in, ci, n_out, co, n_scr])
        ins, cins, os_, cos, scr = (refs[bounds[i]:bounds[i + 1]] for i in range(5))
        sems = refs[bounds[5]:]
        ids = [pl.program_id(a) for a in range(len(grid))]
        first, last = ids[0] == 0, ids[0] == grid[0] - 1
        for a in range(1, len(grid)):
            first, last = first & (ids[a] == 0), last & (ids[a] == grid[a] - 1)

        @pl.when(first)
        def _():
            comm.start(cins, cos, sems)

        body(*ins, *os_, *scr)

        @pl.when(last)
        def _():
            comm.finish(cins, cos, sems)

    res = pl.pallas_call(hosted, name=name, grid=grid, in_specs=[*in_specs, *[HBM_SPEC] * ci],
                         out_specs=[*ospecs, *[HBM_SPEC] * co], out_shape=[*outs, *comm.out_shapes],
                         scratch_shapes=[*scratch, *comm.sems()],
                         input_output_aliases={n_in + i: n_out + o for i, o in comm.aliases.items()},
                         compiler_params=_cp(len(grid)))(*operands, *comm.ins)
    comm.results = res[n_out:]
    return res[:n_out] if multi else res[0]


def _mm(name, grid, ins, in_specs, pairs, n_acc, acc_shape, epilogue, out_shapes, out_specs, k_axis, comm=None):
    n_in, n_out = len(ins), len(out_shapes)
    nk = grid[k_axis] if k_axis is not None else 1

    def body(*refs):
        in_refs, out_refs, acc_refs = refs[:n_in], refs[n_in:n_in + n_out], refs[n_in + n_out:]
        parts = [None] * n_acc
        for ai, bi, mode, ci in pairs:
            d = _dot(in_refs[ai][...], in_refs[bi][...], mode)
            parts[ci] = d if parts[ci] is None else parts[ci] + d
        if nk == 1:
            epilogue(parts, in_refs, out_refs)
            return
        k = pl.program_id(k_axis)

        @pl.when(k == 0)
        def _():
            for c in range(n_acc):
                acc_refs[c][...] = parts[c]

        @pl.when(k > 0)
        def _():
            for c in range(n_acc):
                acc_refs[c][...] += parts[c]

        @pl.when(k == nk - 1)
        def _():
            epilogue([r[...] for r in acc_refs], in_refs, out_refs)

    scratch = [pltpu.VMEM(acc_shape, F32) for _ in range(n_acc)] if nk > 1 else []
    return _call(body, name=name, grid=grid, in_specs=in_specs, out_specs=list(out_specs), out_shape=list(out_shapes),
                 operands=ins, scratch=scratch, comm=comm)


def _mm2d(name, a, b, mode, tm, tn, tk, out_dtypes=(F32,), epilogue=None, extras=()):
    if mode == "nn":
        (M, K), N = a.shape, b.shape[1]
    elif mode == "nt":
        (M, K), N = a.shape, b.shape[0]
    else:
        (K, M), N = a.shape, b.shape[1]
    tm, tn, tk = _tile(M, tm), _tile(N, tn), _tile(K, tk)
    a_spec = pl.BlockSpec((tk, tm), lambda i, j, k: (k, i)) if mode == "tn" else pl.BlockSpec((tm, tk), lambda i, j, k: (i, k))
    b_spec = pl.BlockSpec((tn, tk), lambda i, j, k: (j, k)) if mode == "nt" else pl.BlockSpec((tk, tn), lambda i, j, k: (k, j))
    mn_spec = pl.BlockSpec((tm, tn), lambda i, j, k: (i, j))
    n_ex = len(extras)

    def default_epilogue(acc, ex_refs, out_refs):
        out_refs[0][...] = acc.astype(out_refs[0].dtype)

    epi = epilogue or default_epilogue

    def wrapped(accs, in_refs, out_refs):
        epi(accs[0], in_refs[2:2 + n_ex], out_refs)

    outs = _mm(name, (M // tm, N // tn, K // tk), [a, b, *extras], [a_spec, b_spec] + [mn_spec] * n_ex,
               [(0, 1, mode, 0)], 1, (tm, tn), wrapped,
               [jax.ShapeDtypeStruct((M, N), dt) for dt in out_dtypes], [mn_spec] * len(out_dtypes), 2)
    return outs[0] if len(out_dtypes) == 1 else outs


def _rms_stats(x, n):
    return lax.rsqrt(jnp.sum(x * x, axis=-1, keepdims=True) * (1.0 / n) + EPS)


def _rmsnorm_fwd(name, x, g, tm=512):
    T, D = x.shape
    tm = _tile(T, tm)

    def body(x_ref, g_ref, o_ref):
        xv = x_ref[...]
        o_ref[...] = (xv * _rms_stats(xv, D) * g_ref[...]).astype(BF16)

    return pl.pallas_call(body, name=name, grid=(T // tm,),
                          in_specs=[pl.BlockSpec((tm, D), lambda i: (i, 0)), pl.BlockSpec((1, D), lambda i: (0, 0))],
                          out_specs=pl.BlockSpec((tm, D), lambda i: (i, 0)),
                          out_shape=jax.ShapeDtypeStruct((T, D), BF16), compiler_params=_cp(1))(x, g)


def _rms_bwd_rows(dy, x, g, n):
    r = _rms_stats(x, n)
    xh = x * r
    gd = dy * g
    mean = jnp.sum(gd * xh, axis=-1, keepdims=True) * (1.0 / n)
    return r * (gd - xh * mean), dy * xh


def _rmsnorm_bwd(name, dn, x, g, res, tm=256):
    T, D = x.shape
    tm = _tile(T, tm)

    def body(dn_ref, x_ref, g_ref, res_ref, dx_ref, dxb_ref, dg_ref):
        dx, dgp = _rms_bwd_rows(dn_ref[...].astype(F32), x_ref[...], g_ref[...], D)
        dx = dx + res_ref[...]
        dx_ref[...] = dx
        dxb_ref[...] = dx.astype(BF16)

        @pl.when(pl.program_id(0) == 0)
        def _():
            dg_ref[...] = jnp.zeros_like(dg_ref)

        dg_ref[...] += jnp.sum(dgp, axis=0, keepdims=True)

    row = pl.BlockSpec((tm, D), lambda i: (i, 0))
    vec = pl.BlockSpec((1, D), lambda i: (0, 0))
    return pl.pallas_call(body, name=name, grid=(T // tm,), in_specs=[row, row, vec, row],
                          out_specs=[row, row, vec],
                          out_shape=[jax.ShapeDtypeStruct((T, D), F32), jax.ShapeDtypeStruct((T, D), BF16),
                                     jax.ShapeDtypeStruct((1, D), F32)],
                          compiler_params=_cp(1))(dn, x, g, res)


def _rope_tables(pos_col, inv_mla, inv_dil, tm=512):
    T = pos_col.shape[0]
    tm = _tile(T, tm)

    def body(p_ref, im_ref, id_ref, cm, sam, sbm, cd, sad, sbd):
        lane = lax.broadcasted_iota(jnp.int32, (tm, 128), 1)
        p = p_ref[...]
        am = p * im_ref[...]
        c, s = jnp.cos(am), jnp.sin(am)
        cm[...] = jnp.where(lane < 64, c, 0.0)
        sam[...] = jnp.where(lane < 32, -s, 0.0)
        sbm[...] = jnp.where((lane >= 32) & (lane < 64), s, 0.0)
        ad = p * id_ref[...]
        c, s = jnp.cos(ad), jnp.sin(ad)
        cd[...] = jnp.where(lane < 32, c, 1.0)
        sad[...] = jnp.where(lane < 16, -s, 0.0)
        sbd[...] = jnp.where((lane >= 16) & (lane < 32), s, 0.0)

    row = pl.BlockSpec((tm, 128), lambda i: (i, 0))
    vec = pl.BlockSpec((1, 128), lambda i: (0, 0))
    return pl.pallas_call(body, name="rope_tables", grid=(T // tm,),
                          in_specs=[pl.BlockSpec((tm, 1), lambda i: (i, 0)), vec, vec], out_specs=[row] * 6,
                          out_shape=[jax.ShapeDtypeStruct((T, 128), F32)] * 6, compiler_params=_cp(1))(pos_col, inv_mla, inv_dil)


def _rope(v, c, sa, sb, sh):
    return v * c + pltpu.roll(v, 128 - sh, 1) * sa + pltpu.roll(v, sh, 1) * sb


def _rope_t(d, c, sa, sb, sh):
    return d * c + pltpu.roll(d * sa, sh, 1) + pltpu.roll(d * sb, 128 - sh, 1)


def _lora_fwd(pa, g_cq, g_ckv, nq, nkv, tm=512):
    T, W = pa.shape
    tm = _tile(T, tm)

    def body(pa_ref, gq_ref, gk_ref, cq_ref, ckv_ref):
        a = pa_ref[:, :nq]
        cq_ref[...] = (a * _rms_stats(a, nq) * gq_ref[...]).astype(BF16)
        b = pa_ref[:, nq:nq + nkv]
        ckv_ref[...] = (b * _rms_stats(b, nkv) * gk_ref[...]).astype(BF16)

    return pl.pallas_call(body, name="lora_fwd", grid=(T // tm,),
                          in_specs=[pl.BlockSpec((tm, W), lambda i: (i, 0)), pl.BlockSpec((1, nq), lambda i: (0, 0)),
                                    pl.BlockSpec((1, nkv), lambda i: (0, 0))],
                          out_specs=[pl.BlockSpec((tm, nq), lambda i: (i, 0)), pl.BlockSpec((tm, nkv), lambda i: (i, 0))],
                          out_shape=[jax.ShapeDtypeStruct((T, nq), BF16), jax.ShapeDtypeStruct((T, nkv), BF16)],
                          compiler_params=_cp(1))(pa, g_cq, g_ckv)


def _lora_bwd(dcq, dckv, dkr, pa, g_cq, g_ckv, tm=512):
    T, W = pa.shape
    nq, nkv = dcq.shape[1], dckv.shape[1]
    tm = _tile(T, tm)

    def body(dcq_ref, dckv_ref, dkr_ref, pa_ref, gq_ref, gk_ref, dpa_ref, dgq_ref, dgk_ref):
        dx, dgp = _rms_bwd_rows(dcq_ref[...], pa_ref[:, :nq], gq_ref[...], nq)
        dpa_ref[:, :nq] = dx.astype(BF16)
        dx2, dgp2 = _rms_bwd_rows(dckv_ref[...], pa_ref[:, nq:nq + nkv], gk_ref[...], nkv)
        dpa_ref[:, nq:nq + nkv] = dx2.astype(BF16)
        dpa_ref[:, nq + nkv:] = dkr_ref[...].astype(BF16)

        @pl.when(pl.program_id(0) == 0)
        def _():
            dgq_ref[...] = jnp.zeros_like(dgq_ref)
            dgk_ref[...] = jnp.zeros_like(dgk_ref)

        dgq_ref[...] += jnp.sum(dgp, axis=0, keepdims=True)
        dgk_ref[...] += jnp.sum(dgp2, axis=0, keepdims=True)

    def row(n):
        return pl.BlockSpec((tm, n), lambda i: (i, 0))

    def vec(n):
        return pl.BlockSpec((1, n), lambda i: (0, 0))

    return pl.pallas_call(body, name="lora_bwd", grid=(T // tm,),
                          in_specs=[row(nq), row(nkv), row(128), row(W), vec(nq), vec(nkv)],
                          out_specs=[row(W), vec(nq), vec(nkv)],
                          out_shape=[jax.ShapeDtypeStruct((T, W), BF16), jax.ShapeDtypeStruct((1, nq), F32),
                                     jax.ShapeDtypeStruct((1, nkv), F32)],
                          compiler_params=_cp(1))(dcq, dckv, dkr, pa, g_cq, g_ckv)


def _sumsq(v):
    return jnp.sum(v * v, axis=-1, keepdims=True)


def _mla_prep_fwd(q_raw, kv, pa, tabs, gq, gk, H, kr_block, tm=256):
    T = q_raw.shape[0]
    tm = _tile(T, tm)
    scale = MLA_QK ** -0.5
    P = MLA_PAD

    def body(q_ref, kv_ref, kr_ref, c_ref, sa_ref, sb_ref, gq_ref, gk_ref, qo, ko, vo):
        c, sa, sb = c_ref[...], sa_ref[...], sb_ref[...]
        kr = kr_ref[...]
        kr2 = _sumsq(kr)
        for h in range(H):
            lo, hi = q_ref[:, h * P:h * P + 128], q_ref[:, h * P + 128:(h + 1) * P]
            r = lax.rsqrt((_sumsq(lo) + _sumsq(hi)) * (1.0 / MLA_QK) + EPS)
            qo[:, h * P:h * P + 128] = (lo * r * gq_ref[:, :128] * scale).astype(BF16)
            qo[:, h * P + 128:(h + 1) * P] = (_rope(hi * r * gq_ref[:, 128:], c, sa, sb, 32) * scale).astype(BF16)
            kn = kv_ref[:, h * P:h * P + 128]
            r = lax.rsqrt((_sumsq(kn) + kr2) * (1.0 / MLA_QK) + EPS)
            ko[:, h * P:h * P + 128] = (kn * r * gk_ref[:, :128]).astype(BF16)
            ko[:, h * P + 128:(h + 1) * P] = _rope(kr * r * gk_ref[:, 128:], c, sa, sb, 32).astype(BF16)
            vo[:, h * 128:(h + 1) * 128] = kv_ref[:, h * P + 128:(h + 1) * P].astype(BF16)

    wide = pl.BlockSpec((tm, H * P), lambda i: (i, 0))
    lane = pl.BlockSpec((tm, 128), lambda i: (i, 0))
    vec = pl.BlockSpec((1, P), lambda i: (0, 0))
    return pl.pallas_call(body, name="mla_prep_fwd", grid=(T // tm,),
                          in_specs=[wide, wide, pl.BlockSpec((tm, 128), lambda i: (i, kr_block)), lane, lane, lane, vec, vec],
                          out_specs=[wide, wide, pl.BlockSpec((tm, H * 128), lambda i: (i, 0))],
                          out_shape=[jax.ShapeDtypeStruct((T, H * P), BF16), jax.ShapeDtypeStruct((T, H * P), BF16),
                                     jax.ShapeDtypeStruct((T, H * 128), BF16)],
                          compiler_params=_cp(1))(q_raw, kv, pa, *tabs, gq, gk)


def _mla_prep_bwd(dq, dk, dv, q_raw, kv, pa, tabs, gq, gk, H, kr_block, tm=256):
    T = q_raw.shape[0]
    tm = _tile(T, tm)
    scale = MLA_QK ** -0.5
    P = MLA_PAD

    def body(dq_ref, dk_ref, dv_ref, q_ref, kv_ref, kr_ref, c_ref, sa_ref, sb_ref, gq_ref, gk_ref,
             dqr, dkv, dkr, dgq, dgk):
        c, sa, sb = c_ref[...], sa_ref[...], sb_ref[...]
        kr = kr_ref[...]
        kr2 = _sumsq(kr)
        gql, gqh, gkl, gkh = gq_ref[:, :128], gq_ref[:, 128:], gk_ref[:, :128], gk_ref[:, 128:]
        dkr_acc = jnp.zeros((tm, 128), F32)
        sums = [jnp.zeros((1, 128), F32) for _ in range(4)]
        for h in range(H):
            lo_s, hi_s = slice(h * P, h * P + 128), slice(h * P + 128, (h + 1) * P)
            lo, hi = q_ref[:, lo_s], q_ref[:, hi_s]
            r = lax.rsqrt((_sumsq(lo) + _sumsq(hi)) * (1.0 / MLA_QK) + EPS)
            ql, qh = lo * r, hi * r
            dyl = dq_ref[:, lo_s] * scale
            dyh = _rope_t(dq_ref[:, hi_s] * scale, c, sa, sb, 32)
            gl, gh = dyl * gql, dyh * gqh
            mean = (jnp.sum(gl * ql, axis=-1, keepdims=True) + jnp.sum(gh * qh, axis=-1, keepdims=True)) * (1.0 / MLA_QK)
            dqr[:, lo_s] = (r * (gl - ql * mean)).astype(BF16)
            dqr[:, hi_s] = (r * (gh - qh * mean)).astype(BF16)
            sums[0] += jnp.sum(dyl * ql, axis=0, keepdims=True)
            sums[1] += jnp.sum(dyh * qh, axis=0, keepdims=True)
            kn = kv_ref[:, lo_s]
            r = lax.rsqrt((_sumsq(kn) + kr2) * (1.0 / MLA_QK) + EPS)
            kl, kh = kn * r, kr * r
            dkl = dk_ref[:, lo_s]
            dkh = _rope_t(dk_ref[:, hi_s], c, sa, sb, 32)
            gl, gh = dkl * gkl, dkh * gkh
            mean = (jnp.sum(gl * kl, axis=-1, keepdims=True) + jnp.sum(gh * kh, axis=-1, keepdims=True)) * (1.0 / MLA_QK)
            dkv[:, lo_s] = (r * (gl - kl * mean)).astype(BF16)
            dkr_acc += r * (gh - kh * mean)
            dkv[:, hi_s] = dv_ref[:, h * 128:(h + 1) * 128].astype(BF16)
            sums[2] += jnp.sum(dkl * kl, axis=0, keepdims=True)
            sums[3] += jnp.sum(dkh * kh, axis=0, keepdims=True)
        dkr[...] = dkr_acc

        @pl.when(pl.program_id(0) == 0)
        def _():
            dgq[...] = jnp.zeros_like(dgq)
            dgk[...] = jnp.zeros_like(dgk)

        dgq[:, :128] += sums[0]
        dgq[:, 128:] += sums[1]
        dgk[:, :128] += sums[2]
        dgk[:, 128:] += sums[3]

    wide = pl.BlockSpec((tm, H * P), lambda i: (i, 0))
    lane = pl.BlockSpec((tm, 128), lambda i: (i, 0))
    vec = pl.BlockSpec((1, P), lambda i: (0, 0))
    return pl.pallas_call(body, name="mla_prep_bwd", grid=(T // tm,),
                          in_specs=[wide, wide, pl.BlockSpec((tm, H * 128), lambda i: (i, 0)), wide, wide,
                                    pl.BlockSpec((tm, 128), lambda i: (i, kr_block)), lane, lane, lane, vec, vec],
                          out_specs=[wide, wide, lane, vec, vec],
                          out_shape=[jax.ShapeDtypeStruct((T, H * P), BF16), jax.ShapeDtypeStruct((T, H * P), BF16),
                                     jax.ShapeDtypeStruct((T, 128), F32), jax.ShapeDtypeStruct((1, P), F32),
                                     jax.ShapeDtypeStruct((1, P), F32)],
                          compiler_params=_cp(1))(dq, dk, dv, q_raw, kv, pa, *tabs, gq, gk)


def _dil_prep_fwd(pdil, tabs, gq, gk, G, H, tm=256):
    T = pdil.shape[0]
    tm = _tile(T, tm)
    scale = HEAD ** -0.5

    def body(p_ref, c_ref, sa_ref, sb_ref, gq_ref, gk_ref, qo, ko):
        c, sa, sb = c_ref[...], sa_ref[...], sb_ref[...]
        for h in range(H):
            q = p_ref[:, h * 128:(h + 1) * 128].astype(F32)
            qo[:, h * 128:(h + 1) * 128] = (_rope(q * _rms_stats(q, HEAD) * gq_ref[...], c, sa, sb, 16) * scale).astype(BF16)
            k = p_ref[:, (H + h) * 128:(H + h + 1) * 128].astype(F32)
            ko[:, h * 128:(h + 1) * 128] = _rope(k * _rms_stats(k, HEAD) * gk_ref[...], c, sa, sb, 16).astype(BF16)

    lane = pl.BlockSpec((tm, 128), lambda g, i: (i, 0))
    gain = pl.BlockSpec((None, 1, 128), lambda g, i: (g, 0, 0))
    out = pl.BlockSpec((tm, H * 128), lambda g, i: (i, g))
    return pl.pallas_call(body, name="dil_prep_fwd", grid=(G, T // tm),
                          in_specs=[pl.BlockSpec((tm, 3 * H * 128), lambda g, i: (i, g)), lane, lane, lane, gain, gain],
                          out_specs=[out, out],
                          out_shape=[jax.ShapeDtypeStruct((T, G * H * 128), BF16)] * 2,
                          compiler_params=_cp(2))(pdil, *tabs, gq, gk)


def _dil_prep_bwd(dqd, dkd, dvd, pdil, tabs, gq, gk, G, H, tm=256):
    T = pdil.shape[0]
    tm = _tile(T, tm)
    scale = HEAD ** -0.5

    def body(dq_ref, dk_ref, dv_ref, p_ref, c_ref, sa_ref, sb_ref, gq_ref, gk_ref, dp_ref, dgq, dgk):
        c, sa, sb = c_ref[...], sa_ref[...], sb_ref[...]
        sq = jnp.zeros((1, 128), F32)
        sk = jnp.zeros((1, 128), F32)
        for h in range(H):
            hs = slice(h * 128, (h + 1) * 128)
            q = p_ref[:, hs].astype(F32)
            dy = _rope_t(dq_ref[:, hs] * scale, c, sa, sb, 16)
            dx, dgp = _rms_bwd_rows(dy, q, gq_ref[...], HEAD)
            dp_ref[:, hs] = dx.astype(BF16)
            sq += jnp.sum(dgp, axis=0, keepdims=True)
            ks = slice((H + h) * 128, (H + h + 1) * 128)
            k = p_ref[:, ks].astype(F32)
            dy = _rope_t(dk_ref[:, hs], c, sa, sb, 16)
            dx, dgp = _rms_bwd_rows(dy, k, gk_ref[...], HEAD)
            dp_ref[:, ks] = dx.astype(BF16)
            sk += jnp.sum(dgp, axis=0, keepdims=True)
            dp_ref[:, (2 * H + h) * 128:(2 * H + h + 1) * 128] = dv_ref[:, hs].astype(BF16)

        @pl.when(pl.program_id(1) == 0)
        def _():
            dgq[...] = jnp.zeros_like(dgq)
            dgk[...] = jnp.zeros_like(dgk)

        dgq[...] += sq
        dgk[...] += sk

    lane = pl.BlockSpec((tm, 128), lambda g, i: (i, 0))
    gain = pl.BlockSpec((None, 1, 128), lambda g, i: (g, 0, 0))
    grp = pl.BlockSpec((tm, H * 128), lambda g, i: (i, g))
    big = pl.BlockSpec((tm, 3 * H * 128), lambda g, i: (i, g))
    return pl.pallas_call(body, name="dil_prep_bwd", grid=(G, T // tm),
                          in_specs=[grp, grp, grp, big, lane, lane, lane, gain, gain],
                          out_specs=[big, gain, gain],
                          out_shape=[jax.ShapeDtypeStruct(pdil.shape, BF16), jax.ShapeDtypeStruct((G, 1, 128), F32),
                                     jax.ShapeDtypeStruct((G, 1, 128), F32)],
                          compiler_params=_cp(2))(dqd, dkd, dvd, pdil, *tabs, gq, gk)


def _attn_steps(nq, groups, blk):
    steps = []
    for g, (win, dil) in enumerate(groups):
        nb = nq if win is None else min(nq, -(-win // blk) + 1)
        for b in range(nb):
            steps.append((g, b, dil, NO_WINDOW if win is None else win, int(b == 0), int(b == nb - 1)))
    return [jnp.asarray(col, jnp.int32) for col in np.array(steps, np.int32).T]


def _valid(back, dil, win, blk, q_axis):
    qi = lax.broadcasted_iota(jnp.int32, (blk, blk), q_axis)
    ki = lax.broadcasted_iota(jnp.int32, (blk, blk), 1 - q_axis)
    delta = back * blk + (qi - ki)
    return (delta >= 0) & (delta <= win) & ((delta & (dil - 1)) == 0)


def _attn_fwd(name, q, k, v, H, wq, v_base, v_stride, groups, blk):
    T = q.shape[0]
    nq = T // blk
    tabs = _attn_steps(nq, groups, blk)
    ns = int(tabs[0].shape[0])

    def body(sg, sb, sd, sw, sf, sl, q_ref, k_ref, v_ref, o_ref, lse_ref, m_sc, l_sc, acc_sc):
        i, s = pl.program_id(1), pl.program_id(2)

        @pl.when(s == 0)
        def _():
            m_sc[...] = jnp.full_like(m_sc, NEG)
            l_sc[...] = jnp.zeros_like(l_sc)
            acc_sc[...] = jnp.zeros_like(acc_sc)

        back = sb[s]

        @pl.when(i - back >= 0)
        def _():
            sc = _dot(q_ref[...], k_ref[...], "nt")
            ok = _valid(back, sd[s], sw[s], blk, 0)
            sc = jnp.where(ok, sc, NEG)
            m_prev = m_sc[...]
            m_new = jnp.maximum(m_prev, jnp.max(sc, axis=-1, keepdims=True))
            p = jnp.where(ok, jnp.exp(sc - m_new), 0.0)
            alpha = jnp.exp(m_prev - m_new)
            l_sc[...] = alpha * l_sc[...] + jnp.sum(p, axis=-1, keepdims=True)
            acc_sc[...] = alpha * acc_sc[...] + _dot(p, v_ref[...], "nn")
            m_sc[...] = m_new

        @pl.when(s == ns - 1)
        def _():
            o_ref[...] = (acc_sc[...] / l_sc[...]).astype(BF16)
            lse_ref[...] = m_sc[...] + jnp.log(l_sc[...])

    def kv_row(i, s, sb):
        return jnp.maximum(i - sb[s], 0)

    grid_spec = pltpu.PrefetchScalarGridSpec(
        num_scalar_prefetch=6, grid=(H, nq, ns),
        in_specs=[pl.BlockSpec((blk, wq), lambda h, i, s, sg, sb, *_: (i, sg[s] * H + h)),
                  pl.BlockSpec((blk, wq), lambda h, i, s, sg, sb, *_: (kv_row(i, s, sb), sg[s] * H + h)),
                  pl.BlockSpec((blk, HEAD), lambda h, i, s, sg, sb, *_: (kv_row(i, s, sb), v_base + sg[s] * v_stride + h))],
        out_specs=[pl.BlockSpec((blk, HEAD), lambda h, i, s, *_: (i, h)),
                   pl.BlockSpec((None, blk, 1), lambda h, i, s, *_: (h, i, 0))],
        scratch_shapes=[pltpu.VMEM((blk, 1), F32), pltpu.VMEM((blk, 1), F32), pltpu.VMEM((blk, HEAD), F32)])
    return pl.pallas_call(body, name=name, grid_spec=grid_spec,
                          out_shape=[jax.ShapeDtypeStruct((T, H * HEAD), BF16), jax.ShapeDtypeStruct((H, T, 1), F32)],
                          compiler_params=_cp(3))(*tabs, q, k, v)


def _attn_dq(name, q, k, v, do, o, lse, H, wq, v_base, v_stride, groups, blk):
    T = q.shape[0]
    nq = T // blk
    tabs = _attn_steps(nq, groups, blk)
    ns = int(tabs[0].shape[0])

    def body(sg, sb, sd, sw, sf, sl, q_ref, k_ref, v_ref, do_ref, o_ref, lse_ref, dq_ref, dl_ref, acc_sc, dl_sc):
        i, s = pl.program_id(1), pl.program_id(2)

        @pl.when(s == 0)
        def _():
            d = jnp.sum(do_ref[...].astype(F32) * o_ref[...].astype(F32), axis=-1, keepdims=True)
            dl_sc[...] = d
            dl_ref[...] = d

        @pl.when(sf[s] == 1)
        def _():
            acc_sc[...] = jnp.zeros_like(acc_sc)

        back = sb[s]

        @pl.when(i - back >= 0)
        def _():
            sc = _dot(q_ref[...], k_ref[...], "nt")
            ok = _valid(back, sd[s], sw[s], blk, 0)
            p = jnp.where(ok, jnp.exp(sc - lse_ref[...]), 0.0)
            dp = _dot(do_ref[...], v_ref[...], "nt")
            ds = p * (dp - dl_sc[...])
            acc_sc[...] += _dot(ds, k_ref[...], "nn")

        @pl.when(sl[s] == 1)
        def _():
            dq_ref[...] = acc_sc[...]

    def kv_row(i, s, sb):
        return jnp.maximum(i - sb[s], 0)

    col = pl.BlockSpec((None, blk, 1), lambda h, i, s, *_: (h, i, 0))
    oh = pl.BlockSpec((blk, HEAD), lambda h, i, s, *_: (i, h))
    grid_spec = pltpu.PrefetchScalarGridSpec(
        num_scalar_prefetch=6, grid=(H, nq, ns),
        in_specs=[pl.BlockSpec((blk, wq), lambda h, i, s, sg, sb, *_: (i, sg[s] * H + h)),
                  pl.BlockSpec((blk, wq), lambda h, i, s, sg, sb, *_: (kv_row(i, s, sb), sg[s] * H + h)),
                  pl.BlockSpec((blk, HEAD), lambda h, i, s, sg, sb, *_: (kv_row(i, s, sb), v_base + sg[s] * v_stride + h)),
                  oh, oh, col],
        out_specs=[pl.BlockSpec((blk, wq), lambda h, i, s, sg, *_: (i, sg[s] * H + h)), col],
        scratch_shapes=[pltpu.VMEM((blk, wq), F32), pltpu.VMEM((blk, 1), F32)])
    G = len(groups)
    return pl.pallas_call(body, name=name, grid_spec=grid_spec,
                          out_shape=[jax.ShapeDtypeStruct((T, G * H * wq), F32), jax.ShapeDtypeStruct((H, T, 1), F32)],
                          compiler_params=_cp(3))(*tabs, q, k, v, do, o, lse)


def _attn_dkv(name, q, k, v, do, lse_row, dl_row, H, wq, v_base, v_stride, groups, blk):
    T = q.shape[0]
    nq = T // blk
    tabs = _attn_steps(nq, groups, blk)
    ns = int(tabs[0].shape[0])

    def body(sg, sb, sd, sw, sf, sl, q_ref, k_ref, v_ref, do_ref, lse_ref, dl_ref, dk_ref, dv_ref, dk_sc, dv_sc):
        j, s = pl.program_id(1), pl.program_id(2)

        @pl.when(sf[s] == 1)
        def _():
            dk_sc[...] = jnp.zeros_like(dk_sc)
            dv_sc[...] = jnp.zeros_like(dv_sc)

        back = sb[s]

        @pl.when(j + back < nq)
        def _():
            st = _dot(k_ref[...], q_ref[...], "nt")
            ok = _valid(back, sd[s], sw[s], blk, 1)
            pt = jnp.where(ok, jnp.exp(st - lse_ref[...]), 0.0)
            dpt = _dot(v_ref[...], do_ref[...], "nt")
            dst = pt * (dpt - dl_ref[...])
            dv_sc[...] += _dot(pt, do_ref[...], "nn")
            dk_sc[...] += _dot(dst, q_ref[...], "nn")

        @pl.when(sl[s] == 1)
        def _():
            dk_ref[...] = dk_sc[...]
            dv_ref[...] = dv_sc[...]

    def q_row(j, s, sb):
        return jnp.minimum(j + sb[s], nq - 1)

    row = pl.BlockSpec((None, 1, blk), lambda h, j, s, sg, sb, *_: (h, 0, q_row(j, s, sb)))
    grid_spec = pltpu.PrefetchScalarGridSpec(
        num_scalar_prefetch=6, grid=(H, nq, ns),
        in_specs=[pl.BlockSpec((blk, wq), lambda h, j, s, sg, sb, *_: (q_row(j, s, sb), sg[s] * H + h)),
                  pl.BlockSpec((blk, wq), lambda h, j, s, sg, *_: (j, sg[s] * H + h)),
                  pl.BlockSpec((blk, HEAD), lambda h, j, s, sg, *_: (j, v_base + sg[s] * v_stride + h)),
                  pl.BlockSpec((blk, HEAD), lambda h, j, s, sg, sb, *_: (q_row(j, s, sb), h)),
                  row, row],
        out_specs=[pl.BlockSpec((blk, wq), lambda h, j, s, sg, *_: (j, sg[s] * H + h)),
                   pl.BlockSpec((blk, HEAD), lambda h, j, s, sg, *_: (j, sg[s] * H + h))],
        scratch_shapes=[pltpu.VMEM((blk, wq), F32), pltpu.VMEM((blk, HEAD), F32)])
    G = len(groups)
    return pl.pallas_call(body, name=name, grid_spec=grid_spec,
                          out_shape=[jax.ShapeDtypeStruct((T, G * H * wq), F32), jax.ShapeDtypeStruct((T, G * H * HEAD), F32)],
                          compiler_params=_cp(3))(*tabs, q, k, v, do, lse_row, dl_row)


def _causal_mask(blk, q_axis):
    return lax.broadcasted_iota(jnp.int32, (blk, blk), q_axis) >= lax.broadcasted_iota(jnp.int32, (blk, blk), 1 - q_axis)


def _on_blocks(s, live, step):
    @pl.when(s == 0)
    def _():
        step(True)

    @pl.when((s > 0) & live)
    def _():
        step(False)


def _sub_mask(sub, width, row0, col0, q_axis):
    r = lax.broadcasted_iota(jnp.int32, (sub, width), 0) + row0
    c = lax.broadcasted_iota(jnp.int32, (sub, width), 1) + col0
    return (r >= c) if q_axis == 0 else (c >= r)


def _mla_fwd(q, k, v, H, blk, comm=None):
    T, wq = q.shape[0], q.shape[1] // H
    nq = T // blk
    sub = _tile(blk, MLA_SUB)

    def body(q_ref, k_ref, v_ref, o_ref, lse_ref, m_sc, l_sc, acc_sc):
        i, s = pl.program_id(1), pl.program_id(2)

        @pl.when(s == 0)
        def _():
            m_sc[...] = jnp.full_like(m_sc, NEG)
            l_sc[...] = jnp.zeros_like(l_sc)
            acc_sc[...] = jnp.zeros_like(acc_sc)

        def step(masked):
            for r in range(blk // sub):
                rs = slice(r * sub, (r + 1) * sub)
                nk = (r + 1) * sub if masked else blk
                sc = _dot(q_ref[rs, :], k_ref[:nk, :], "nt")
                if masked:
                    sc = jnp.where(_sub_mask(sub, nk, r * sub, 0, 0), sc, NEG)
                m_prev = m_sc[rs, :]
                m_new = jnp.maximum(m_prev, jnp.max(sc, axis=-1, keepdims=True))
                p = jnp.exp(sc - m_new)
                alpha = jnp.exp(m_prev - m_new)
                l_sc[rs, :] = alpha * l_sc[rs, :] + jnp.sum(p, axis=-1, keepdims=True)
                acc_sc[rs, :] = alpha * acc_sc[rs, :] + _dot(p, v_ref[:nk, :], "nn")
                m_sc[rs, :] = m_new

        _on_blocks(s, i - s >= 0, step)

        @pl.when(s == nq - 1)
        def _():
            o_ref[...] = (acc_sc[...] / l_sc[...]).astype(BF16)
            lse_ref[...] = m_sc[...] + jnp.log(l_sc[...])

    kv_spec = lambda w: pl.BlockSpec((blk, w), lambda h, i, s: (jnp.maximum(i - s, 0), h))
    return _call(
        body, name="mla_fwd", grid=(H, nq, nq),
        in_specs=[pl.BlockSpec((blk, wq), lambda h, i, s: (i, h)), kv_spec(wq), kv_spec(HEAD)],
        out_specs=[pl.BlockSpec((blk, HEAD), lambda h, i, s: (i, h)), pl.BlockSpec((None, blk, 1), lambda h, i, s: (h, i, 0))],
        out_shape=[jax.ShapeDtypeStruct((T, H * HEAD), BF16), jax.ShapeDtypeStruct((H, T, 1), F32)],
        scratch=[pltpu.VMEM((blk, 1), F32), pltpu.VMEM((blk, 1), F32), pltpu.VMEM((blk, HEAD), F32)],
        operands=[q, k, v], comm=comm)


def _mla_dq(q, k, v, do, lse, dl, H, blk, comm=None):
    T, wq = q.shape[0], q.shape[1] // H
    nq = T // blk
    sub = _tile(blk, MLA_SUB)

    def body(q_ref, k_ref, v_ref, do_ref, lse_ref, dl_ref, dq_ref, acc_sc):
        i, s = pl.program_id(1), pl.program_id(2)

        @pl.when(s == 0)
        def _():
            acc_sc[...] = jnp.zeros_like(acc_sc)

        def step(masked):
            for r in range(blk // sub):
                rs = slice(r * sub, (r + 1) * sub)
                nk = (r + 1) * sub if masked else blk
                sc = _dot(q_ref[rs, :], k_ref[:nk, :], "nt")
                if masked:
                    sc = jnp.where(_sub_mask(sub, nk, r * sub, 0, 0), sc, NEG)
                p = jnp.exp(sc - lse_ref[rs, :])
                dp = _dot(do_ref[rs, :], v_ref[:nk, :], "nt")
                acc_sc[rs, :] += _dot(p * (dp - dl_ref[rs, :]), k_ref[:nk, :], "nn")

        _on_blocks(s, i - s >= 0, step)

        @pl.when(s == nq - 1)
        def _():
            dq_ref[...] = acc_sc[...]

    kv_spec = lambda w: pl.BlockSpec((blk, w), lambda h, i, s: (jnp.maximum(i - s, 0), h))
    col = pl.BlockSpec((None, blk, 1), lambda h, i, s: (h, i, 0))
    return _call(
        body, name="mla_dq", grid=(H, nq, nq),
        in_specs=[pl.BlockSpec((blk, wq), lambda h, i, s: (i, h)), kv_spec(wq), kv_spec(HEAD),
                  pl.BlockSpec((blk, HEAD), lambda h, i, s: (i, h)), col, col],
        out_specs=pl.BlockSpec((blk, wq), lambda h, i, s: (i, h)), out_shape=jax.ShapeDtypeStruct((T, H * wq), F32),
        scratch=[pltpu.VMEM((blk, wq), F32)], operands=[q, k, v, do, lse, dl], comm=comm)


def _mla_dkv(q, k, v, do, lse_row, dl_row, H, blk):
    T, wq = q.shape[0], q.shape[1] // H
    nq = T // blk
    sub = _tile(blk, MLA_SUB)

    def body(q_ref, k_ref, v_ref, do_ref, lse_ref, dl_ref, dk_ref, dv_ref, dk_sc, dv_sc):
        j, s = pl.program_id(1), pl.program_id(2)

        @pl.when(s == 0)
        def _():
            dk_sc[...] = jnp.zeros_like(dk_sc)
            dv_sc[...] = jnp.zeros_like(dv_sc)

        def step(masked):
            for r in range(blk // sub):
                rs = slice(r * sub, (r + 1) * sub)
                q0 = r * sub if masked else 0
                st = _dot(k_ref[rs, :], q_ref[q0:, :], "nt")
                if masked:
                    st = jnp.where(_sub_mask(sub, blk - q0, r * sub, q0, 1), st, NEG)
                pt = jnp.exp(st - lse_ref[:, q0:])
                dpt = _dot(v_ref[rs, :], do_ref[q0:, :], "nt")
                dv_sc[rs, :] += _dot(pt, do_ref[q0:, :], "nn")
                dk_sc[rs, :] += _dot(pt * (dpt - dl_ref[:, q0:]), q_ref[q0:, :], "nn")

        _on_blocks(s, j + s < nq, step)

        @pl.when(s == nq - 1)
        def _():
            dk_ref[...] = dk_sc[...]
            dv_ref[...] = dv_sc[...]

    q_row = lambda j, s: jnp.minimum(j + s, nq - 1)
    row = pl.BlockSpec((None, 1, blk), lambda h, j, s: (h, 0, q_row(j, s)))
    return pl.pallas_call(
        body, name="mla_dkv", grid=(H, nq, nq),
        in_specs=[pl.BlockSpec((blk, wq), lambda h, j, s: (q_row(j, s), h)), pl.BlockSpec((blk, wq), lambda h, j, s: (j, h)),
                  pl.BlockSpec((blk, HEAD), lambda h, j, s: (j, h)), pl.BlockSpec((blk, HEAD), lambda h, j, s: (q_row(j, s), h)),
                  row, row],
        out_specs=[pl.BlockSpec((blk, wq), lambda h, j, s: (j, h)), pl.BlockSpec((blk, HEAD), lambda h, j, s: (j, h))],
        out_shape=[jax.ShapeDtypeStruct((T, H * wq), F32), jax.ShapeDtypeStruct((T, H * HEAD), F32)],
        scratch_shapes=[pltpu.VMEM((blk, wq), F32), pltpu.VMEM((blk, HEAD), F32)],
        compiler_params=_cp(3))(q, k, v, do, lse_row, dl_row)


def _row_dot(name, a, b, H, tm=1024):
    T = a.shape[0]
    tm = _tile(T, tm)

    def body(a_ref, b_ref, o_ref):
        o_ref[...] = jnp.sum(a_ref[...].astype(F32) * b_ref[...].astype(F32), axis=-1, keepdims=True)

    blk = pl.BlockSpec((tm, HEAD), lambda h, i: (i, h))
    return pl.pallas_call(body, name=name, grid=(H, T // tm), in_specs=[blk, blk],
                          out_specs=pl.BlockSpec((None, tm, 1), lambda h, i: (h, i, 0)),
                          out_shape=jax.ShapeDtypeStruct((H, T, 1), F32), compiler_params=_cp(2))(a, b)


def _tri(n):
    row = lax.broadcasted_iota(jnp.int32, (n, n), 0)
    col = lax.broadcasted_iota(jnp.int32, (n, n), 1)
    return col <= row, col >= row


def _win_fwd(name, q, k, pd, H, n, L, QB):
    T = q.shape[0]
    U = QB // n

    def body(q_ref, k_ref, v_ref, kp_ref, vp_ref, o_ref, lse_ref):
        i = pl.program_id(1)
        own_ok, before_ok = _tri(n)
        for e in range(U):
            rs = slice(e * n, (e + 1) * n)
            ps = slice((e - 1) * n, e * n)
            k_lo, v_lo = (kp_ref[...], vp_ref[...]) if e == 0 else (k_ref[ps, :], v_ref[ps, :])
            inside = ((i * QB + e * n) % L) != 0
            qu = q_ref[rs, :]
            s_hi = jnp.where(own_ok, _dot(qu, k_ref[rs, :], "nt"), NEG)
            s_lo = jnp.where(before_ok & inside, _dot(qu, k_lo, "nt"), NEG)
            m = jnp.maximum(jnp.max(s_hi, axis=-1, keepdims=True), jnp.max(s_lo, axis=-1, keepdims=True))
            p_hi, p_lo = jnp.exp(s_hi - m), jnp.exp(s_lo - m)
            l = jnp.sum(p_hi, axis=-1, keepdims=True) + jnp.sum(p_lo, axis=-1, keepdims=True)
            acc = _dot(p_hi, v_ref[rs, :], "nn") + _dot(p_lo, v_lo, "nn")
            o_ref[rs, :] = acc / l
            lse_ref[rs, :] = m + jnp.log(l)

    cur = lambda c0: pl.BlockSpec((QB, HEAD), lambda h, i: (i, c0 + h))
    prev = lambda c0: pl.BlockSpec((n, HEAD), lambda h, i: (jnp.maximum(i * U - 1, 0), c0 + h))
    return pl.pallas_call(
        body, name=name, grid=(H, T // QB), in_specs=[cur(0), cur(0), cur(2 * H), prev(0), prev(2 * H)],
        out_specs=[pl.BlockSpec((QB, HEAD), lambda h, i: (i, h)), pl.BlockSpec((None, QB, 1), lambda h, i: (h, i, 0))],
        out_shape=[jax.ShapeDtypeStruct((T, H * HEAD), F32), jax.ShapeDtypeStruct((H, T, 1), F32)],
        compiler_params=_cp(2))(q, k, pd, k, pd)


def _win_dq(name, q, k, pd, do, lse, dl, H, n, L, QB):
    T = q.shape[0]
    U = QB // n

    def body(q_ref, k_ref, v_ref, kp_ref, vp_ref, do_ref, lse_ref, dl_ref, dq_ref):
        i = pl.program_id(1)
        own_ok, before_ok = _tri(n)
        for e in range(U):
            rs = slice(e * n, (e + 1) * n)
            ps = slice((e - 1) * n, e * n)
            k_lo, v_lo = (kp_ref[...], vp_ref[...]) if e == 0 else (k_ref[ps, :], v_ref[ps, :])
            inside = ((i * QB + e * n) % L) != 0
            qu, dou, lse_u, dl_u = q_ref[rs, :], do_ref[rs, :], lse_ref[rs, :], dl_ref[rs, :]
            p_hi = jnp.exp(jnp.where(own_ok, _dot(qu, k_ref[rs, :], "nt"), NEG) - lse_u)
            p_lo = jnp.exp(jnp.where(before_ok & inside, _dot(qu, k_lo, "nt"), NEG) - lse_u)
            ds_hi = p_hi * (_dot(dou, v_ref[rs, :], "nt") - dl_u)
            ds_lo = p_lo * (_dot(dou, v_lo, "nt") - dl_u)
            dq_ref[rs, :] = _dot(ds_hi, k_ref[rs, :], "nn") + _dot(ds_lo, k_lo, "nn")

    cur = lambda c0: pl.BlockSpec((QB, HEAD), lambda h, i: (i, c0 + h))
    prev = lambda c0: pl.BlockSpec((n, HEAD), lambda h, i: (jnp.maximum(i * U - 1, 0), c0 + h))
    flat = pl.BlockSpec((QB, HEAD), lambda h, i: (i, h))
    col = pl.BlockSpec((None, QB, 1), lambda h, i: (h, i, 0))
    return pl.pallas_call(
        body, name=name, grid=(H, T // QB), in_specs=[cur(0), cur(0), cur(2 * H), prev(0), prev(2 * H), flat, col, col],
        out_specs=flat, out_shape=jax.ShapeDtypeStruct((T, H * HEAD), F32), compiler_params=_cp(2))(q, k, pd, k, pd, do, lse, dl)


def _win_dkv(name, q, k, pd, do, lse_row, dl_row, H, n, L, QB):
    T = q.shape[0]
    U = QB // n
    last = T // n - 1

    def body(q_ref, k_ref, v_ref, do_ref, lse_ref, dl_ref, qn_ref, don_ref, lsen_ref, dln_ref, dk_ref, dv_ref):
        i = pl.program_id(1)
        own_ok, after_ok = _tri_t(n)
        for e in range(U):
            rs = slice(e * n, (e + 1) * n)
            ns = slice((e + 1) * n, (e + 2) * n)
            if e < U - 1:
                qn, don, lsen, dln = q_ref[ns, :], do_ref[ns, :], lse_ref[:, ns], dl_ref[:, ns]
            else:
                qn, don, lsen, dln = qn_ref[...], don_ref[...], lsen_ref[...], dln_ref[...]
            nxt = i * QB + (e + 1) * n
            inside = ((nxt % L) != 0) & (nxt < T)
            ku, vu, qu, dou = k_ref[rs, :], v_ref[rs, :], q_ref[rs, :], do_ref[rs, :]
            pt_own = jnp.exp(jnp.where(own_ok, _dot(ku, qu, "nt"), NEG) - lse_ref[:, rs])
            pt_aft = jnp.exp(jnp.where(after_ok & inside, _dot(ku, qn, "nt"), NEG) - lsen)
            dst_own = pt_own * (_dot(vu, dou, "nt") - dl_ref[:, rs])
            dst_aft = pt_aft * (_dot(vu, don, "nt") - dln)
            dv_ref[rs, :] = _dot(pt_own, dou, "nn") + _dot(pt_aft, don, "nn")
            dk_ref[rs, :] = _dot(dst_own, qu, "nn") + _dot(dst_aft, qn, "nn")

    cur = lambda c0: pl.BlockSpec((QB, HEAD), lambda h, i: (i, c0 + h))
    flat = pl.BlockSpec((QB, HEAD), lambda h, i: (i, h))
    row = pl.BlockSpec((None, 1, QB), lambda h, i: (h, 0, i))
    nxt_unit = lambda h, i: jnp.minimum((i + 1) * U, last)
    return pl.pallas_call(
        body, name=name, grid=(H, T // QB),
        in_specs=[cur(0), cur(0), cur(2 * H), flat, row, row,
                  pl.BlockSpec((n, HEAD), lambda h, i: (nxt_unit(h, i), h)),
                  pl.BlockSpec((n, HEAD), lambda h, i: (nxt_unit(h, i), h)),
                  pl.BlockSpec((None, 1, n), lambda h, i: (h, 0, nxt_unit(h, i))),
                  pl.BlockSpec((None, 1, n), lambda h, i: (h, 0, nxt_unit(h, i)))],
        out_specs=[flat, flat], out_shape=[jax.ShapeDtypeStruct((T, H * HEAD), F32)] * 2,
        compiler_params=_cp(2))(q, k, pd, do, lse_row, dl_row, q, do, lse_row, dl_row)


def _tri_t(n):
    key = lax.broadcasted_iota(jnp.int32, (n, n), 0)
    qry = lax.broadcasted_iota(jnp.int32, (n, n), 1)
    return key <= qry, key >= qry


def _merge_groups(os_, lses, H, tm=512):
    G = len(os_)
    T = os_[0].shape[0]
    tm = _tile(T, tm)

    def body(*refs):
        o_refs, l_refs, o_out, lse_out = refs[:G], refs[G:2 * G], refs[2 * G], refs[2 * G + 1]
        ls = [r[...] for r in l_refs]
        m = ls[0]
        for x in ls[1:]:
            m = jnp.maximum(m, x)
        ws = [jnp.exp(x - m) for x in ls]
        tot = ws[0]
        for x in ws[1:]:
            tot = tot + x
        acc = ws[0] * o_refs[0][...]
        for x, r in zip(ws[1:], o_refs[1:]):
            acc = acc + x * r[...]
        o_out[...] = (acc / tot).astype(BF16)
        lse_out[...] = m + jnp.log(tot)

    flat = pl.BlockSpec((tm, HEAD), lambda h, i: (i, h))
    col = pl.BlockSpec((None, tm, 1), lambda h, i: (h, i, 0))
    return pl.pallas_call(body, name="dil_merge", grid=(H, T // tm), in_specs=[flat] * G + [col] * G, out_specs=[flat, col],
                          out_shape=[jax.ShapeDtypeStruct((T, H * HEAD), BF16), jax.ShapeDtypeStruct((H, T, 1), F32)],
                          compiler_params=_cp(2))(*os_, *lses)


def _dil_prep_fwd_pm(name, pd, tabs, gq, gk, H, tm=256):
    T = pd.shape[0]
    tm = _tile(T, tm)
    scale = HEAD ** -0.5

    def body(p_ref, c_ref, sa_ref, sb_ref, gq_ref, gk_ref, qo, ko):
        c, sa, sb = c_ref[...], sa_ref[...], sb_ref[...]
        for h in range(H):
            q = p_ref[:, h * 128:(h + 1) * 128].astype(F32)
            qo[:, h * 128:(h + 1) * 128] = (_rope(q * _rms_stats(q, HEAD) * gq_ref[...], c, sa, sb, 16) * scale).astype(BF16)
            k = p_ref[:, (H + h) * 128:(H + h + 1) * 128].astype(F32)
            ko[:, h * 128:(h + 1) * 128] = _rope(k * _rms_stats(k, HEAD) * gk_ref[...], c, sa, sb, 16).astype(BF16)

    lane = pl.BlockSpec((tm, 128), lambda i: (i, 0))
    gain = pl.BlockSpec((1, 128), lambda i: (0, 0))
    out = pl.BlockSpec((tm, H * 128), lambda i: (i, 0))
    return pl.pallas_call(body, name=name, grid=(T // tm,),
                          in_specs=[pl.BlockSpec((tm, 3 * H * 128), lambda i: (i, 0)), lane, lane, lane, gain, gain],
                          out_specs=[out, out], out_shape=[jax.ShapeDtypeStruct((T, H * 128), BF16)] * 2,
                          compiler_params=_cp(1))(pd, *tabs, gq, gk)


def _dil_prep_bwd_pm(name, dq, dk, dv, pd, tabs, gq, gk, H, tm=256):
    T = pd.shape[0]
    tm = _tile(T, tm)
    scale = HEAD ** -0.5

    def body(dq_ref, dk_ref, dv_ref, p_ref, c_ref, sa_ref, sb_ref, gq_ref, gk_ref, dp_ref, dgq, dgk):
        c, sa, sb = c_ref[...], sa_ref[...], sb_ref[...]
        sq = jnp.zeros((1, 128), F32)
        sk = jnp.zeros((1, 128), F32)
        for h in range(H):
            hs = slice(h * 128, (h + 1) * 128)
            q = p_ref[:, hs].astype(F32)
            dx, dgp = _rms_bwd_rows(_rope_t(dq_ref[:, hs] * scale, c, sa, sb, 16), q, gq_ref[...], HEAD)
            dp_ref[:, hs] = dx.astype(BF16)
            sq += jnp.sum(dgp, axis=0, keepdims=True)
            ks = slice((H + h) * 128, (H + h + 1) * 128)
            k = p_ref[:, ks].astype(F32)
            dx, dgp = _rms_bwd_rows(_rope_t(dk_ref[:, hs], c, sa, sb, 16), k, gk_ref[...], HEAD)
            dp_ref[:, ks] = dx.astype(BF16)
            sk += jnp.sum(dgp, axis=0, keepdims=True)
            dp_ref[:, (2 * H + h) * 128:(2 * H + h + 1) * 128] = dv_ref[:, hs].astype(BF16)

        @pl.when(pl.program_id(0) == 0)
        def _():
            dgq[...] = jnp.zeros_like(dgq)
            dgk[...] = jnp.zeros_like(dgk)

        dgq[...] += sq
        dgk[...] += sk

    lane = pl.BlockSpec((tm, 128), lambda i: (i, 0))
    flat = pl.BlockSpec((tm, H * 128), lambda i: (i, 0))
    vec = pl.BlockSpec((1, 128), lambda i: (0, 0))
    return pl.pallas_call(body, name=name, grid=(T // tm,),
                          in_specs=[flat, flat, flat, pl.BlockSpec((tm, 3 * H * 128), lambda i: (i, 0)),
                                    lane, lane, lane, vec, vec],
                          out_specs=[pl.BlockSpec((tm, 3 * H * 128), lambda i: (i, 0)), vec, vec],
                          out_shape=[jax.ShapeDtypeStruct((T, 3 * H * 128), BF16), jax.ShapeDtypeStruct((1, 128), F32),
                                     jax.ShapeDtypeStruct((1, 128), F32)],
                          compiler_params=_cp(1))(dq, dk, dv, pd, *tabs, gq, gk)


def _to_phase(a, d, axis=0):
    if d == 1:
        return a
    sh = a.shape
    T = sh[axis]
    b = a.reshape(*sh[:axis], T // d, d, *sh[axis + 1:])
    return jnp.swapaxes(b, axis, axis + 1).reshape(sh)


def _from_phase(a, d, axis=0):
    if d == 1:
        return a
    sh = a.shape
    T = sh[axis]
    b = a.reshape(*sh[:axis], d, T // d, *sh[axis + 1:])
    return jnp.swapaxes(b, axis, axis + 1).reshape(sh)


def _ffn_up(name, n, wg, wu, tm=512, comm=None):
    T, D = n.shape
    nd, _, fc = wg.shape
    tm = _tile(T, tm)

    def epilogue(accs, in_refs, out_refs):
        a, b = accs
        out_refs[0][...] = a.astype(BF16)
        out_refs[1][...] = b.astype(BF16)
        out_refs[2][...] = (a * _sigmoid(a) * b).astype(BF16)

    w_spec = pl.BlockSpec((None, D, fc), lambda j, i: (j, 0, 0))
    o_spec = pl.BlockSpec((None, tm, fc), lambda j, i: (j, i, 0))
    sh = jax.ShapeDtypeStruct((nd, T, fc), BF16)
    return _mm(name, (nd, T // tm), [n, wg, wu], [pl.BlockSpec((tm, D), lambda j, i: (i, 0)), w_spec, w_spec],
               [(0, 1, "nn", 0), (0, 2, "nn", 1)], 2, None, epilogue, [sh, sh, sh], [o_spec] * 3, None, comm=comm)


def _ffn_down(name, s, wd, res, tm=512, tn=512, comm=None):
    nd, T, fc = s.shape
    D = wd.shape[2]
    tm, tn = _tile(T, tm), _tile(D, tn)

    def body(s_ref, w_ref, r_ref, o_ref):
        acc = _dot(s_ref[0], w_ref[0], "nn")
        for j in range(1, nd):
            acc += _dot(s_ref[j], w_ref[j], "nn")
        o_ref[...] = r_ref[...] + 0.5 * acc

    mn = pl.BlockSpec((tm, tn), lambda i, j: (i, j))
    return _call(body, name=name, grid=(T // tm, D // tn),
                 in_specs=[pl.BlockSpec((nd, tm, fc), lambda i, j: (0, i, 0)), pl.BlockSpec((nd, fc, tn), lambda i, j: (0, 0, j)), mn],
                 out_specs=mn, out_shape=jax.ShapeDtypeStruct((T, D), F32), operands=[s, wd, res], comm=comm)


def _ffn_bwd_act(name, dxb, wd, a, b, tm=512, comm=None):
    T, D = dxb.shape
    nd, fc, _ = wd.shape
    tm = _tile(T, tm)

    def epilogue(accs, in_refs, out_refs):
        ds = 0.5 * accs[0]
        av, bv = in_refs[2][...].astype(F32), in_refs[3][...].astype(F32)
        sg = _sigmoid(av)
        out_refs[0][...] = (ds * bv * sg * (1.0 + av * (1.0 - sg))).astype(BF16)
        out_refs[1][...] = (ds * av * sg).astype(BF16)

    act = pl.BlockSpec((None, tm, fc), lambda j, i: (j, i, 0))
    sh = jax.ShapeDtypeStruct((nd, T, fc), BF16)
    return _mm(name, (nd, T // tm), [dxb, wd, a, b],
               [pl.BlockSpec((tm, D), lambda j, i: (i, 0)), pl.BlockSpec((None, fc, D), lambda j, i: (j, 0, 0)), act, act],
               [(0, 1, "nt", 0)], 1, None, epilogue, [sh, sh], [act, act], None, comm=comm)


def _ffn_dwd(name, s, dxb, tk=2048):
    nd, T, fc = s.shape
    D = dxb.shape[1]
    tk = _tile(T, tk)

    def epilogue(accs, in_refs, out_refs):
        out_refs[0][...] = (0.5 * accs[0]).astype(BF16)

    return _mm(name, (nd, T // tk), [s, dxb],
               [pl.BlockSpec((None, tk, fc), lambda j, k: (j, k, 0)), pl.BlockSpec((tk, D), lambda j, k: (k, 0))],
               [(0, 1, "tn", 0)], 1, (fc, D), epilogue, [jax.ShapeDtypeStruct((nd, fc, D), BF16)],
               [pl.BlockSpec((None, fc, D), lambda j, k: (j, 0, 0))], 1)[0]


def _ffn_dw(name, n, dact, tk=2048):
    T, D = n.shape
    nd, _, fc = dact.shape
    tk = _tile(T, tk)

    def epilogue(accs, in_refs, out_refs):
        out_refs[0][...] = accs[0].astype(BF16)

    return _mm(name, (nd, T // tk), [n, dact],
               [pl.BlockSpec((tk, D), lambda j, k: (k, 0)), pl.BlockSpec((None, tk, fc), lambda j, k: (j, k, 0))],
               [(0, 1, "tn", 0)], 1, (D, fc), epilogue, [jax.ShapeDtypeStruct((nd, D, fc), BF16)],
               [pl.BlockSpec((None, D, fc), lambda j, k: (j, 0, 0))], 1)[0]


def _ffn_dn(name, da, db, wg, wu, tm=512, tn=256, comm=None):
    nd, T, fc = da.shape
    D = wg.shape[1]
    tm, tn = _tile(T, tm), _tile(D, tn)

    def body(da_ref, db_ref, wg_ref, wu_ref, o_ref):
        acc = _dot(da_ref[0], wg_ref[0], "nt") + _dot(db_ref[0], wu_ref[0], "nt")
        for j in range(1, nd):
            acc += _dot(da_ref[j], wg_ref[j], "nt") + _dot(db_ref[j], wu_ref[j], "nt")
        o_ref[...] = acc

    act = pl.BlockSpec((nd, tm, fc), lambda i, j: (0, i, 0))
    w_spec = pl.BlockSpec((nd, tn, fc), lambda i, j: (0, j, 0))
    return _call(body, name=name, grid=(T // tm, D // tn), in_specs=[act, act, w_spec, w_spec],
                 out_specs=pl.BlockSpec((tm, tn), lambda i, j: (i, j)), out_shape=jax.ShapeDtypeStruct((T, D), F32),
                 operands=[da, db, wg, wu], comm=comm)


def _ffn_forward(tag, x, g, wg, wu, wd, comm_up=None, comm_down=None):
    n = _rmsnorm_fwd(tag + "_norm", x, g)
    a, b, s = _ffn_up(tag + "_up", n, wg, wu, comm=comm_up)
    return _ffn_down(tag + "_down", s, wd, x, comm=comm_down() if comm_down else None), (n, a, b, s)


def _ffn_backward(tag, dx, dxb, x, g, wg, wu, wd, saved, comm_act=None, comm_dn=None, weights_first=True):
    n, a, b, s = saved
    da, db = _ffn_bwd_act(tag + "_bwd_act", dxb, wd, a, b, comm=comm_act)

    def weight_grads():
        return _ffn_dw(tag + "_dwg", n, da), _ffn_dw(tag + "_dwu", n, db), _ffn_dwd(tag + "_dwd", s, dxb)

    dws = weight_grads() if weights_first else None
    dn = _ffn_dn(tag + "_dn", da, db, wg, wu, comm=comm_dn(dws) if comm_dn else None)
    d_wg, d_wu, d_wd = dws if weights_first else weight_grads()
    dx_in, dxb_in, dg = _rmsnorm_bwd(tag + "_norm_bwd", dn, x, g, dx)
    return dx_in, dxb_in, dg, d_wg, d_wu, d_wd


def _place():
    x, y, c = lax.axis_index("x"), lax.axis_index("y"), lax.axis_index("c")
    return x, y, c, [(1 - x, y), (x, 1 - y), (1 - x, 1 - y)]


def _allgather(name, shards):
    n = len(shards)

    def body(*refs):
        ins, outs = refs[:n], refs[n:2 * n]
        send_sems, recv_sems, local_sems = refs[2 * n:]
        x, y, c, chips = _place()
        me, sibling = (x, y, c), (x, y, 1 - c)

        def slot(a, p):
            return outs[a].at[4 * p[0] + 2 * p[1] + p[2]]

        def copy(a, kk, block, to, src=None):
            return pltpu.make_async_remote_copy(
                src_ref=slot(a, block) if src is None else src, dst_ref=slot(a, block),
                send_sem=send_sems.at[a * 7 + kk], recv_sem=recv_sems.at[a * 7 + kk],
                device_id=to, device_id_type=MESH)

        mine = [pltpu.make_async_copy(ins[a], slot(a, me), local_sems.at[a]) for a in range(n)]
        for cp in mine:
            cp.start()
        first = []
        for a in range(n):
            first.append(copy(a, 0, me, sibling, src=ins[a]))
            first += [copy(a, 1 + j, me, (*chip, c), src=ins[a]) for j, chip in enumerate(chips)]
        for cp in first:
            cp.start()
        passed = []
        for j, chip in enumerate(chips):
            for a in range(n):
                copy(a, 1 + j, (*chip, c), me).wait_recv()
                fwd = copy(a, 4 + j, (*chip, c), sibling)
                fwd.start()
                passed.append(fwd)
        for a in range(n):
            copy(a, 0, sibling, me).wait_recv()
        for j, chip in enumerate(chips):
            for a in range(n):
                copy(a, 4 + j, (*chip, 1 - c), me).wait_recv()
        for cp in first + passed:
            cp.wait_send()
        for cp in mine:
            cp.wait()

    return pl.pallas_call(
        body, name=name, in_specs=[HBM_SPEC] * n, out_specs=[HBM_SPEC] * n,
        out_shape=[jax.ShapeDtypeStruct((N_DEV, *s.shape), s.dtype) for s in shards],
        scratch_shapes=[pltpu.SemaphoreType.DMA((7 * n,)), pltpu.SemaphoreType.DMA((7 * n,)), pltpu.SemaphoreType.DMA((n,))],
    )(*shards)


def _slot(ref, p):
    return ref.at[4 * p[0] + 2 * p[1] + p[2]]


def _ag_first(shards):
    n = len(shards)

    def plan(ins, outs):
        x, y, c, chips = _place()
        me = (x, y, c)
        remote, local = [], []
        for a in range(n):
            local.append((ins[a], _slot(outs[a], me)))
            for peer in [(x, y, 1 - c)] + [(*chip, c) for chip in chips]:
                remote.append((ins[a], _slot(outs[a], me), _slot(outs[a], peer), peer))
        return remote, local

    return _Exchange(shards, [jax.ShapeDtypeStruct((N_DEV, *s.shape), s.dtype) for s in shards], plan, 4 * n, n)


def _ag_second(partial):
    n = len(partial)

    def plan(ins, outs):
        x, y, c, chips = _place()
        remote = []
        for a in range(n):
            for chip in chips:
                remote.append((_slot(ins[a], (*chip, c)), _slot(outs[a], (*chip, c)), _slot(outs[a], (*chip, 1 - c)), (x, y, 1 - c)))
        return remote, []

    return _Exchange(partial, [jax.ShapeDtypeStruct(p.shape, p.dtype) for p in partial], plan, 3 * n, 0,
                     aliases={a: a for a in range(n)})


def _rs_first(halves):
    n = len(halves)

    def plan(ins, outs):
        x, y, c, _ = _place()
        return [(ins[a].at[:, 1 - c], outs[a], outs[a], (x, y, 1 - c)) for a in range(n)], []

    return _Exchange(halves, [jax.ShapeDtypeStruct((h.shape[0], *h.shape[2:]), h.dtype) for h in halves], plan, n, 0)


def _rs_second(sums):
    n = len(sums)

    def plan(ins, outs):
        x, y, c, chips = _place()
        k_me = 2 * x + y
        remote = []
        for a in range(n):
            for chip in chips:
                k_peer = 2 * chip[0] + chip[1]
                remote.append((ins[a].at[k_peer], outs[a].at[k_me], outs[a].at[k_peer], (*chip, c)))
        return remote, [(ins[a].at[k_me], outs[a].at[k_me]) for a in range(n)]

    return _Exchange(sums, [jax.ShapeDtypeStruct(s.shape, s.dtype) for s in sums], plan, 3 * n, n)


def _pair_add(name, own, got, core):
    nch, _, K, N = own.shape
    tr = _row_tile(K, N)

    def body(c_ref, own_ref, got_ref, o_ref):
        o_ref[...] = (own_ref[...].astype(F32) + got_ref[...].astype(F32)).astype(o_ref.dtype)

    grid_spec = pltpu.PrefetchScalarGridSpec(
        num_scalar_prefetch=1, grid=(nch, K // tr),
        in_specs=[pl.BlockSpec((None, None, tr, N), lambda k, r, c_ref: (k, c_ref[0], r, 0)),
                  pl.BlockSpec((None, tr, N), lambda k, r, c_ref: (k, r, 0))],
        out_specs=pl.BlockSpec((None, tr, N), lambda k, r, c_ref: (k, r, 0)))
    return pl.pallas_call(body, name=name, grid_spec=grid_spec, out_shape=jax.ShapeDtypeStruct((nch, K, N), own.dtype),
                          compiler_params=_cp(2))(core, own, got)


def _row_tile(K, N):
    limit = max(16, 262144 // N)
    t = 1
    while t * 2 <= limit and K % (t * 2) == 0:
        t *= 2
    return t if t >= 16 else K


def _adamw(name, parts, w, m, v):
    P, K, N = parts.shape
    tr = _row_tile(K, N)

    def body(p_ref, w_ref, m_ref, v_ref, g_ref, d_ref, nm_ref, nv_ref):
        g = p_ref[0].astype(F32)
        for i in range(1, P):
            g = g + p_ref[i].astype(F32)
        m_new = ADAM_B1 * m_ref[...] + (1.0 - ADAM_B1) * g
        v_new = ADAM_B2 * v_ref[...] + (1.0 - ADAM_B2) * (g * g)
        m_hat = m_new / (1.0 - ADAM_B1 ** ADAM_STEP)
        v_hat = v_new / (1.0 - ADAM_B2 ** ADAM_STEP)
        g_ref[...] = g
        d_ref[...] = -ADAM_LR * (m_hat / (jnp.sqrt(v_hat) + ADAM_EPS) + ADAM_WD * w_ref[...])
        nm_ref[...] = m_new
        nv_ref[...] = v_new

    row = pl.BlockSpec((tr, N), lambda r: (r, 0))
    sh = jax.ShapeDtypeStruct((K, N), F32)
    return pl.pallas_call(body, name=name, grid=(K // tr,),
                          in_specs=[pl.BlockSpec((P, tr, N), lambda r: (0, r, 0)), row, row, row],
                          out_specs=[row] * 4, out_shape=[sh] * 4, compiler_params=_cp(1))(parts, w, m, v)


def _merge_fwd(o_mla, wbm, o_dil, wbd, g0, g1, tm=512, tn=512, comm=None):
    T, K1 = o_mla.shape
    K2, D = o_dil.shape[1], wbm.shape[1]
    tm, tn = _tile(T, tm), _tile(D, tn)

    def epilogue(accs, in_refs, out_refs):
        a, b = accs
        out_refs[0][...] = a.astype(BF16)
        out_refs[1][...] = b.astype(BF16)
        out_refs[2][...] = (in_refs[4][...].astype(F32) * a + in_refs[5][...].astype(F32) * b).astype(BF16)

    mn = pl.BlockSpec((tm, tn), lambda i, j: (i, j))
    sh = jax.ShapeDtypeStruct((T, D), BF16)
    return _mm("merge_fwd", (T // tm, D // tn), [o_mla, wbm, o_dil, wbd, g0, g1],
               [pl.BlockSpec((tm, K1), lambda i, j: (i, 0)), pl.BlockSpec((K1, tn), lambda i, j: (0, j)),
                pl.BlockSpec((tm, K2), lambda i, j: (i, 0)), pl.BlockSpec((K2, tn), lambda i, j: (0, j)), mn, mn],
               [(0, 1, "nn", 0), (2, 3, "nn", 1)], 2, None, epilogue, [sh, sh, sh], [mn, mn, mn], None, comm=comm)


def _ple_loss(n4, wpg, pe, wpp, x3, tgt, tm=512, tn=512):
    T, D = x3.shape
    Kp = pe.shape[1]
    tm, tn = _tile(T, tm), _tile(D, tn)

    def epilogue(accs, in_refs, out_refs):
        z, proj = accs
        pg = _sigmoid(z)
        err = in_refs[4][...] + pg * proj - in_refs[5][...]
        dy = err * (1.0 / D)
        out_refs[0][...] = dy
        out_refs[1][...] = (dy * proj * pg * (1.0 - pg)).astype(BF16)
        out_refs[2][...] = (dy * pg).astype(BF16)

        @pl.when(pl.program_id(1) == 0)
        def _():
            out_refs[3][...] = jnp.zeros_like(out_refs[3])

        out_refs[3][...] += jnp.sum(err * err, axis=-1, keepdims=True)

    mn = pl.BlockSpec((tm, tn), lambda i, j: (i, j))
    return _mm("ple_loss", (T // tm, D // tn), [n4, wpg, pe, wpp, x3, tgt],
               [pl.BlockSpec((tm, D), lambda i, j: (i, 0)), pl.BlockSpec((D, tn), lambda i, j: (0, j)),
                pl.BlockSpec((tm, Kp), lambda i, j: (i, 0)), pl.BlockSpec((Kp, tn), lambda i, j: (0, j)), mn, mn],
               [(0, 1, "nn", 0), (2, 3, "nn", 1)], 2, None, epilogue,
               [jax.ShapeDtypeStruct((T, D), F32), jax.ShapeDtypeStruct((T, D), BF16), jax.ShapeDtypeStruct((T, D), BF16),
                jax.ShapeDtypeStruct((T, 1), F32)],
               [mn, mn, mn, pl.BlockSpec((tm, 1), lambda i, j: (i, 0))], None)


def _epi_sigmoid(acc, ex, outs):
    outs[0][...] = _sigmoid(acc).astype(outs[0].dtype)


def _epi_add(acc, ex, outs):
    outs[0][...] = (acc + ex[0][...].astype(F32)).astype(outs[0].dtype)


def _epi_dmerge(acc, ex, outs):
    mp, dp, g0, g1 = [e[...].astype(F32) for e in ex]
    outs[0][...] = (acc * g0).astype(BF16)
    outs[1][...] = (acc * g1).astype(BF16)
    outs[2][...] = (acc * mp * g0 * (1.0 - g0)).astype(BF16)
    outs[3][...] = (acc * dp * g1 * (1.0 - g1)).astype(BF16)


_WEIGHTS = ("g_ffn1", "w1_gate", "w1_up", "w1_down", "g_mix", "w_in", "g_cq", "w_uq", "g_ckv", "w_ukv", "g_q_mla", "g_k_mla",
            "g_q_dil", "g_k_dil", "w_br_mla", "w_br_dil", "w_o", "g_ffn2", "w2_gate", "w2_up", "w2_down", "g_ple",
            "w_ple_gate", "w_ple_proj")
_MATRICES = ("w1_gate", "w1_up", "w1_down", "w_in", "w_uq", "w_ukv", "w_br_mla", "w_br_dil", "w_o", "w2_gate", "w2_up",
             "w2_down", "w_ple_gate", "w_ple_proj")
_GAINS = tuple(n for n in _WEIGHTS if n not in _MATRICES)
MLA_BLOCK = 1024
MLA_SUB = 256
DIL_ROWS = 2048
_FIRST = ("w1_gate", "w1_up", "w1_down")
_MID = ("w_in", "w_uq", "w_ukv", "w_br_mla", "w_br_dil", "w_o", "w_ple_gate", "w_ple_proj")
_LATE = ("w2_gate", "w2_up", "w2_down")
_RS_GROUPS = (("w_ple_proj", "w_ple_gate", "w2_gate", "w2_up", "w2_down"),
              ("w_o", "w_br_mla", "w_br_dil", "w_uq", "w_ukv", "w_in"),
              ("w1_gate", "w1_up", "w1_down"))


def _cols(g3):
    nd, K, n = g3.shape
    return g3.transpose(1, 0, 2).reshape(K, nd * n)


def _uncols(m):
    K, n = m.shape
    return m.reshape(K, N_DEV, n // N_DEV).transpose(1, 0, 2)


def _rows(g3):
    nd, k, N = g3.shape
    return g3.reshape(nd * k, N)


def _unrows(m):
    K, N = m.shape
    return m.reshape(N_DEV, K // N_DEV, N)


def _pack_gains(vals):
    flat = jnp.concatenate([vals[n].reshape(-1) for n in _GAINS])
    pad = (-flat.shape[0]) % 2048
    return jnp.pad(flat, (0, pad)).reshape(-1, 128)


def _unpack_gains(packed, like):
    flat = packed.reshape(-1)
    out, off = {}, 0
    for n in _GAINS:
        size = int(np.prod(like[n].shape))
        out[n] = flat[off:off + size].reshape(like[n].shape)
        off += size
    return out


def _train_step(x, p, positions, loss_target, W, M, V):
    T, D = x.shape[1], x.shape[2]
    xs, tgt, pe = x[0], loss_target[0], p[0, 0]
    pos_col = positions.reshape(T, 1).astype(F32)
    w = {n: (a[0] if n in _MATRICES else a.reshape(1, -1)) for n, a in W.items()}

    shard = {n: w[n].astype(BF16) for n in _MATRICES}
    gathered = dict(zip(_FIRST, _allgather("ag_first", [shard[n] for n in _FIRST])))
    ag_mid = _ag_first([shard[n] for n in _MID])
    ag_mid2 = []

    def pass_on_mid():
        ag_mid2.append(_ag_second(ag_mid.results))
        return ag_mid2[0]

    x1, ffn1 = _ffn_forward("ffn1", xs, w["g_ffn1"], gathered["w1_gate"], gathered["w1_up"], gathered["w1_down"],
                            comm_up=ag_mid, comm_down=pass_on_mid)
    gathered.update(zip(_MID, ag_mid2[0].results))
    nq_l, nkv_l = w["g_cq"].shape[-1], w["g_ckv"].shape[-1]
    H = w["w_uq"].shape[1] * N_DEV // MLA_QK
    G = len(DIL_GROUPS)
    off_kr = nq_l + nkv_l
    off_dil = off_kr + MLA_ROPE
    off_gate = off_dil + G * 3 * H * HEAD
    kr_block = off_kr // 128
    w_in = _cols(gathered["w_in"])
    wa = jnp.pad(w_in[:, :off_dil], ((0, 0), (0, 128 - MLA_ROPE)))
    wdil, wg0, wg1 = w_in[:, off_dil:off_gate], w_in[:, off_gate:off_gate + D], w_in[:, off_gate + D:]
    wuq = jnp.pad(_cols(gathered["w_uq"]).reshape(nq_l, H, MLA_QK), ((0, 0), (0, 0), (0, MLA_PAD - MLA_QK))).reshape(nq_l, H * MLA_PAD)
    wukv = _cols(gathered["w_ukv"])
    wbm, wbd, wpp = _cols(gathered["w_br_mla"]), _cols(gathered["w_br_dil"]), _cols(gathered["w_ple_proj"])
    wo, wpg = _rows(gathered["w_o"]), _rows(gathered["w_ple_gate"])
    gq_mla = jnp.pad(w["g_q_mla"], ((0, 0), (0, MLA_PAD - MLA_QK)))
    gk_mla = jnp.pad(w["g_k_mla"], ((0, 0), (0, MLA_PAD - MLA_QK)))
    gq_dil, gk_dil = w["g_q_dil"].reshape(G, 1, HEAD), w["g_k_dil"].reshape(G, 1, HEAD)

    half_m, half_d = MLA_ROPE // 2, DIL_ROT // 2
    inv_m = ROPE_THETA ** (-jnp.arange(half_m, dtype=F32) * 2.0 / MLA_ROPE)
    inv_d = ROPE_THETA ** (-jnp.arange(half_d, dtype=F32) * 2.0 / DIL_ROT)
    inv_m = jnp.tile(inv_m, 128 // half_m).reshape(1, 128)
    inv_d = jnp.tile(inv_d, 128 // half_d).reshape(1, 128)
    tabs = _rope_tables(pos_col, inv_m, inv_d)
    tabs_m, tabs_d = tabs[:3], tabs[3:]

    mla_blk, dil_qb = _tile(T, MLA_BLOCK), _tile(T, DIL_ROWS)
    gw = 3 * H * HEAD
    dils = [d for _, d in DIL_GROUPS]
    units = [win // d for win, d in DIL_GROUPS]
    wdil_g = [wdil[:, g * gw:(g + 1) * gw] for g in range(G)]
    tabs_g = [[_to_phase(t, d) for t in tabs_d] for d in dils]

    h = _rmsnorm_fwd("mix_norm", x1, w["g_mix"])
    pa = _mm2d("proj_a", h, wa, "nn", 512, 1024, 4096)
    pd = [_to_phase(_mm2d("proj_dil%d" % g, h, wdil_g[g], "nn", 1024, 1024, 4096, out_dtypes=(BF16,)), dils[g]) for g in range(G)]
    g0 = _mm2d("proj_gate0", h, wg0, "nn", 1024, 1024, 4096, out_dtypes=(BF16,), epilogue=_epi_sigmoid)
    g1 = _mm2d("proj_gate1", h, wg1, "nn", 1024, 1024, 4096, out_dtypes=(BF16,), epilogue=_epi_sigmoid)
    cq, ckv = _lora_fwd(pa, w["g_cq"], w["g_ckv"], nq_l, nkv_l)
    q_raw = _mm2d("q_up", cq, wuq, "nn", 512, 2048, 4096)
    kv = _mm2d("kv_up", ckv, wukv, "nn", 512, 2048, 4096)
    q_att, k_att, v_mla = _mla_prep_fwd(q_raw, kv, pa, tabs_m, gq_mla, gk_mla, H, kr_block)
    ag_late = _ag_first([shard[n] for n in _LATE])
    o_mla, lse_mla = _mla_fwd(q_att, k_att, v_mla, H, mla_blk, comm=ag_late)
    qd, kd, o_g, lse_g = [], [], [], []
    for g in range(G):
        qg, kg = _dil_prep_fwd_pm("dil_prep_fwd%d" % g, pd[g], tabs_g[g], gq_dil[g], gk_dil[g], H)
        og, lg = _win_fwd("dil_fwd%d" % g, qg, kg, pd[g], H, units[g], T // dils[g], dil_qb)
        qd.append(qg)
        kd.append(kg)
        o_g.append(_from_phase(og, dils[g]))
        lse_g.append(_from_phase(lg, dils[g], axis=1))
    o_dil, lse_dil = _merge_groups(o_g, lse_g, H)
    ag_late2 = _ag_second(ag_late.results)
    mla_p, dil_p, merged = _merge_fwd(o_mla, wbm, o_dil, wbd, g0, g1, comm=ag_late2)
    gathered.update(zip(_LATE, ag_late2.results))
    x2 = _mm2d("out_proj", merged, wo, "nn", 512, 1024, 4096, epilogue=_epi_add, extras=(x1,))
    x3, ffn2 = _ffn_forward("ffn2", x2, w["g_ffn2"], gathered["w2_gate"], gathered["w2_up"], gathered["w2_down"])
    n4 = _rmsnorm_fwd("ple_norm", x3, w["g_ple"])
    dy, dz, dproj, loss_rows = _ple_loss(n4, wpg, pe, wpp, x3, tgt)
    loss = lax.psum((0.5 / D) * jnp.sum(loss_rows), ("x", "y", "c"))

    dW, dG = {}, {}
    dW["w_ple_proj"] = _uncols(_mm2d("d_wpp", pe, dproj, "tn", 1024, 2048, 2048, out_dtypes=(BF16,)))
    dW["w_ple_gate"] = _unrows(_mm2d("d_wpg", n4, dz, "tn", 1024, 1024, 2048, out_dtypes=(BF16,)))
    dn4 = _mm2d("d_n4", dz, wpg, "nt", 512, 1024, 4096)
    dx3, dx3b, dG["g_ple"] = _rmsnorm_bwd("ple_norm_bwd", dn4, x3, w["g_ple"], dy)

    core = lax.axis_index("c").astype(jnp.int32).reshape(1)
    grads, deltas, new_m, new_v = {}, {}, {}, {}

    def sibling_exchange(names):
        return _rs_first([dW[n].reshape(4, 2, *dW[n].shape[1:]) for n in names])

    def chip_exchange(names, first):
        return _rs_second([_pair_add("rs_add_" + n, own, rec, core) for n, own, rec in zip(names, first.ins, first.results)])

    def update(names, second):
        for n, parts in zip(names, second.results):
            grads[n], deltas[n], new_m[n], new_v[n] = [a[None] for a in _adamw("adamw_" + n, parts, w[n], M[n][0], V[n][0])]

    rs_a = []

    def ffn2_grads_done(dws):
        dW["w2_gate"], dW["w2_up"], dW["w2_down"] = dws
        rs_a.append(sibling_exchange(_RS_GROUPS[0]))
        return rs_a[0]

    dx2, dx2b, dG["g_ffn2"], _, _, _ = _ffn_backward(
        "ffn2", dx3, dx3b, x2, w["g_ffn2"], gathered["w2_gate"], gathered["w2_up"], gathered["w2_down"], ffn2,
        comm_dn=ffn2_grads_done)
    rs_a2 = chip_exchange(_RS_GROUPS[0], rs_a[0])

    d_mla_p, d_dil_p, dpg0, dpg1 = _mm2d("d_merged", dx2b, wo, "nt", 512, 1024, 4096, out_dtypes=(BF16,) * 4,
                                         epilogue=_epi_dmerge, extras=(mla_p, dil_p, g0, g1))
    dW["w_o"] = _unrows(_mm2d("d_wo", merged, dx2b, "tn", 1024, 1024, 2048, out_dtypes=(BF16,)))
    dW["w_br_mla"] = _uncols(_mm2d("d_wbm", o_mla, d_mla_p, "tn", 1024, 1024, 2048, out_dtypes=(BF16,)))
    dW["w_br_dil"] = _uncols(_mm2d("d_wbd", o_dil, d_dil_p, "tn", 1024, 1024, 2048, out_dtypes=(BF16,)))
    do_mla = _mm2d("d_o_mla", d_mla_p, wbm, "nt", 512, 1024, 4096, out_dtypes=(BF16,))
    do_dil = _mm2d("d_o_dil", d_dil_p, wbd, "nt", 512, 1024, 4096, out_dtypes=(BF16,))

    def as_row(a):
        return a.reshape(a.shape[0], 1, a.shape[1])

    dl_mla = _row_dot("mla_delta", do_mla, o_mla, H)
    dq_att = _mla_dq(q_att, k_att, v_mla, do_mla, lse_mla, dl_mla, H, mla_blk, comm=rs_a2)
    update(_RS_GROUPS[0], rs_a2)
    dk_att, dv_mla = _mla_dkv(q_att, k_att, v_mla, do_mla, as_row(lse_mla), as_row(dl_mla), H, mla_blk)
    dq_raw, dkv, dkr, dgq, dgk = _mla_prep_bwd(dq_att, dk_att, dv_mla, q_raw, kv, pa, tabs_m, gq_mla, gk_mla, H, kr_block)
    dG["g_q_mla"], dG["g_k_mla"] = dgq[:, :MLA_QK], dgk[:, :MLA_QK]
    d_wuq = _mm2d("d_wuq", cq, dq_raw, "tn", 512, 2048, 2048, out_dtypes=(BF16,))
    dW["w_uq"] = _uncols(d_wuq.reshape(nq_l, H, MLA_PAD)[:, :, :MLA_QK].reshape(nq_l, H * MLA_QK))
    dW["w_ukv"] = _uncols(_mm2d("d_wukv", ckv, dkv, "tn", 512, 2048, 2048, out_dtypes=(BF16,)))
    dcq = _mm2d("d_cq", dq_raw, wuq, "nt", 512, 1024, 4096)
    dckv = _mm2d("d_ckv", dkv, wukv, "nt", 512, 1024, 4096)
    dpa, dG["g_cq"], dG["g_ckv"] = _lora_bwd(dcq, dckv, dkr, pa, w["g_cq"], w["g_ckv"])

    dl_dil = _row_dot("dil_delta", do_dil, o_dil, H)
    dpd, dgqd, dgkd = [], [], []
    for g in range(G):
        d, n, L = dils[g], units[g], T // dils[g]
        do_g, lse_pg, dl_pg = _to_phase(do_dil, d), _to_phase(lse_dil, d, axis=1), _to_phase(dl_dil, d, axis=1)
        dq_g = _win_dq("dil_dq%d" % g, qd[g], kd[g], pd[g], do_g, lse_pg, dl_pg, H, n, L, dil_qb)
        dk_g, dv_g = _win_dkv("dil_dkv%d" % g, qd[g], kd[g], pd[g], do_g, as_row(lse_pg), as_row(dl_pg), H, n, L, dil_qb)
        dp_g, dgq_g, dgk_g = _dil_prep_bwd_pm("dil_prep_bwd%d" % g, dq_g, dk_g, dv_g, pd[g], tabs_g[g], gq_dil[g], gk_dil[g], H)
        dpd.append(_from_phase(dp_g, d))
        dgqd.append(dgq_g)
        dgkd.append(dgk_g)
    dG["g_q_dil"], dG["g_k_dil"] = jnp.concatenate(dgqd).reshape(1, G, HEAD), jnp.concatenate(dgkd).reshape(1, G, HEAD)

    d_wa = _mm2d("d_wa", h, dpa, "tn", 1024, 1024, 2048, out_dtypes=(BF16,))
    d_wdil = [_mm2d("d_wdil%d" % g, h, dpd[g], "tn", 1024, 1024, 2048, out_dtypes=(BF16,)) for g in range(G)]
    d_wg0 = _mm2d("d_wg0", h, dpg0, "tn", 1024, 1024, 2048, out_dtypes=(BF16,))
    d_wg1 = _mm2d("d_wg1", h, dpg1, "tn", 1024, 1024, 2048, out_dtypes=(BF16,))
    dW["w_in"] = _uncols(jnp.concatenate([d_wa[:, :off_dil], *d_wdil, d_wg0, d_wg1], axis=1))
    dh = _mm2d("d_h_a", dpa, wa, "nt", 512, 1024, 4096)
    for g in range(G):
        dh = _mm2d("d_h_dil%d" % g, dpd[g], wdil_g[g], "nt", 512, 1024, 4096, epilogue=_epi_add, extras=(dh,))
    dh = _mm2d("d_h_g0", dpg0, wg0, "nt", 512, 1024, 4096, epilogue=_epi_add, extras=(dh,))
    dh = _mm2d("d_h_g1", dpg1, wg1, "nt", 512, 1024, 4096, epilogue=_epi_add, extras=(dh,))
    dx1, dx1b, dG["g_mix"] = _rmsnorm_bwd("mix_norm_bwd", dh, x1, w["g_mix"], dx2)
    rs_b = sibling_exchange(_RS_GROUPS[1])
    rs_b2 = []

    def attention_grads_summed(_):
        rs_b2.append(chip_exchange(_RS_GROUPS[1], rs_b))
        return rs_b2[0]

    dx0, _, dG["g_ffn1"], dW["w1_gate"], dW["w1_up"], dW["w1_down"] = _ffn_backward(
        "ffn1", dx1, dx1b, xs, w["g_ffn1"], gathered["w1_gate"], gathered["w1_up"], gathered["w1_down"], ffn1,
        comm_act=rs_b, comm_dn=attention_grads_summed, weights_first=False)
    update(_RS_GROUPS[1], rs_b2[0])
    rs_c = sibling_exchange(_RS_GROUPS[2])
    _run_exchange("rs_first_w1", rs_c)
    rs_c2 = chip_exchange(_RS_GROUPS[2], rs_c)
    _run_exchange("rs_second_w1", rs_c2)
    update(_RS_GROUPS[2], rs_c2)

    parts = _allgather("ag_gain_grads", [_pack_gains(dG)])[0]
    packed = _adamw("adamw_gains", parts, _pack_gains(W), _pack_gains(M), _pack_gains(V))
    for out, pk in zip((grads, deltas, new_m, new_v), packed):
        out.update(_unpack_gains(pk, W))

    return (loss, dx0[None], *[grads[n] for n in _WEIGHTS], *[deltas[n] for n in _WEIGHTS],
            *[new_m[n] for n in _WEIGHTS], *[new_v[n] for n in _WEIGHTS])


def kernel(x, p, positions, g_ffn1, w1_gate, w1_up, w1_down, g_mix, w_in, g_cq, w_uq, g_ckv, w_ukv, g_q_mla, g_k_mla, g_q_dil, g_k_dil, w_br_mla, w_br_dil, w_o, g_ffn2, w2_gate, w2_up, w2_down, g_ple, w_ple_gate, w_ple_proj, loss_target, m_g_ffn1, m_w1_gate, m_w1_up, m_w1_down, m_g_mix, m_w_in, m_g_cq, m_w_uq, m_g_ckv, m_w_ukv, m_g_q_mla, m_g_k_mla, m_g_q_dil, m_g_k_dil, m_w_br_mla, m_w_br_dil, m_w_o, m_g_ffn2, m_w2_gate, m_w2_up, m_w2_down, m_g_ple, m_w_ple_gate, m_w_ple_proj, v_g_ffn1, v_w1_gate, v_w1_up, v_w1_down, v_g_mix, v_w_in, v_g_cq, v_w_uq, v_g_ckv, v_w_ukv, v_g_q_mla, v_g_k_mla, v_g_q_dil, v_g_k_dil, v_w_br_mla, v_w_br_dil, v_w_o, v_g_ffn2, v_w2_gate, v_w2_up, v_w2_down, v_g_ple, v_w_ple_gate, v_w_ple_proj):
    W = dict(zip(_WEIGHTS, (g_ffn1, w1_gate, w1_up, w1_down, g_mix, w_in, g_cq, w_uq, g_ckv, w_ukv, g_q_mla, g_k_mla, g_q_dil,
                            g_k_dil, w_br_mla, w_br_dil, w_o, g_ffn2, w2_gate, w2_up, w2_down, g_ple, w_ple_gate, w_ple_proj)))
    M = dict(zip(_WEIGHTS, (m_g_ffn1, m_w1_gate, m_w1_up, m_w1_down, m_g_mix, m_w_in, m_g_cq, m_w_uq, m_g_ckv, m_w_ukv, m_g_q_mla,
                            m_g_k_mla, m_g_q_dil, m_g_k_dil, m_w_br_mla, m_w_br_dil, m_w_o, m_g_ffn2, m_w2_gate, m_w2_up,
                            m_w2_down, m_g_ple, m_w_ple_gate, m_w_ple_proj)))
    V = dict(zip(_WEIGHTS, (v_g_ffn1, v_w1_gate, v_w1_up, v_w1_down, v_g_mix, v_w_in, v_g_cq, v_w_uq, v_g_ckv, v_w_ukv, v_g_q_mla,
                            v_g_k_mla, v_g_q_dil, v_g_k_dil, v_w_br_mla, v_w_br_dil, v_w_o, v_g_ffn2, v_w2_gate, v_w2_up,
                            v_w2_down, v_g_ple, v_w_ple_gate, v_w_ple_proj)))
    return _train_step(x, p, positions, loss_target, W, M, V)
```

```python
import numpy as np
import jax
import jax.numpy as jnp
from jax import lax
from jax.experimental import pallas as pl
from jax.experimental.pallas import tpu as pltpu

F32 = jnp.float32
BF16 = jnp.bfloat16

EPS = 1e-6
ROPE_THETA = 500000.0
MLA_NOPE = 128
MLA_ROPE = 64
MLA_QK = MLA_NOPE + MLA_ROPE
MLA_PAD = 256
HEAD = 128
DIL_ROT = 32
DIL_GROUPS = ((128, 1), (512, 4), (2048, 16))
NEG = -1e30
NO_WINDOW = 1 << 30
N_DEV = 8
ADAM_LR, ADAM_B1, ADAM_B2, ADAM_EPS, ADAM_WD, ADAM_STEP = 0.001, 0.9, 0.999, 1e-08, 0.01, 10
VMEM_LIMIT_V7X = 56 * 1024 * 1024
MESH = pl.DeviceIdType.MESH
HBM_SPEC = pl.BlockSpec(memory_space=pltpu.HBM)


def _cp(n_axes):
    return pltpu.CompilerParams(dimension_semantics=("arbitrary",) * n_axes,
                                vmem_limit_bytes=VMEM_LIMIT_V7X)


def _tile(n, t):
    return t if (n >= t and n % t == 0) else n


def _sigmoid(x):
    return 1.0 / (1.0 + jnp.exp(-x))


_DIMS = {"nn": (((1,), (0,)), ((), ())), "nt": (((1,), (1,)), ((), ())), "tn": (((0,), (0,)), ((), ()))}


def _dot(a, b, mode):
    return lax.dot_general(a.astype(BF16), b.astype(BF16), _DIMS[mode], preferred_element_type=F32)


class _Exchange:
    def __init__(self, ins, out_shapes, plan, n_remote, n_local, aliases=None):
        self.ins, self.out_shapes, self.plan = list(ins), list(out_shapes), plan
        self.n_remote, self.n_local, self.aliases = n_remote, n_local, dict(aliases or {})
        self.results = None

    def sems(self):
        return [pltpu.SemaphoreType.DMA((self.n_remote,)), pltpu.SemaphoreType.DMA((self.n_remote,)),
                pltpu.SemaphoreType.DMA((max(self.n_local, 1),))]

    def start(self, in_refs, out_refs, sems):
        remote, local = self.plan(in_refs, out_refs)
        for i, (src, dst_there, _, peer) in enumerate(remote):
            pltpu.make_async_remote_copy(src_ref=src, dst_ref=dst_there, send_sem=sems[0].at[i], recv_sem=sems[1].at[i],
                                         device_id=peer, device_id_type=MESH).start()
        for i, (src, dst) in enumerate(local):
            pltpu.make_async_copy(src, dst, sems[2].at[i]).start()

    def finish(self, in_refs, out_refs, sems):
        remote, local = self.plan(in_refs, out_refs)
        for i, (src, _, dst_here, peer) in enumerate(remote):
            pltpu.make_async_remote_copy(src_ref=src, dst_ref=dst_here, send_sem=sems[0].at[i], recv_sem=sems[1].at[i],
                                         device_id=peer, device_id_type=MESH).wait_recv()
        for i, (src, dst_there, _, peer) in enumerate(remote):
            pltpu.make_async_remote_copy(src_ref=src, dst_ref=dst_there, send_sem=sems[0].at[i], recv_sem=sems[1].at[i],
                                         device_id=peer, device_id_type=MESH).wait_send()
        for i, (src, dst) in enumerate(local):
            pltpu.make_async_copy(src, dst, sems[2].at[i]).wait()


def _run_exchange(name, ex):
    ci = len(ex.ins)

    def body(*refs):
        ins, outs, sems = refs[:ci], refs[ci:ci + len(ex.out_shapes)], refs[ci + len(ex.out_shapes):]
        ex.start(ins, outs, sems)
        ex.finish(ins, outs, sems)

    ex.results = pl.pallas_call(body, name=name, in_specs=[HBM_SPEC] * ci, out_specs=[HBM_SPEC] * len(ex.out_shapes),
                                out_shape=ex.out_shapes, scratch_shapes=ex.sems(), input_output_aliases=ex.aliases)(*ex.ins)
    return ex.results


def _call(body, *, name, grid, in_specs, out_specs, out_shape, operands, scratch=(), comm=None):
    multi = isinstance(out_shape, (list, tuple))
    outs = list(out_shape) if multi else [out_shape]
    ospecs = list(out_specs) if multi else [out_specs]
    if comm is None:
        res = pl.pallas_call(body, name=name, grid=grid, in_specs=list(in_specs), out_specs=ospecs, out_shape=outs,
                             scratch_shapes=list(scratch), compiler_params=_cp(len(grid)))(*operands)
        return res if multi else res[0]
    n_in, n_out, n_scr = len(in_specs), len(outs), len(scratch)
    ci, co = len(comm.ins), len(comm.out_shapes)

    def hosted(*refs):
        bounds = np.cumsum([0, n_in, ci, n_out, co, n_scr])
        ins, cins, os_, cos, scr = (refs[bounds[i]:bounds[i + 1]] for i in range(5))
        sems = refs[bounds[5]:]
        ids = [pl.program_id(a) for a in range(len(grid))]
        first, last = ids[0] == 0, ids[0] == grid[0] - 1
        for a in range(1, len(grid)):
            first, last = first & (ids[a] == 0), last & (ids[a] == grid[a] - 1)

        @pl.when(first)
        def _():
            comm.start(cins, cos, sems)

        body(*ins, *os_, *scr)

        @pl.when(last)
        def _():
            comm.finish(cins, cos, sems)

    res = pl.pallas_call(hosted, name=name, grid=grid, in_specs=[*in_specs, *[HBM_SPEC] * ci],
                         out_specs=[*ospecs, *[HBM_SPEC] * co], out_shape=[*outs, *comm.out_shapes],
                         scratch_shapes=[*scratch, *comm.sems()],
                         input_output_aliases={n_in + i: n_out + o for i, o in comm.aliases.items()},
                         compiler_params=_cp(len(grid)))(*operands, *comm.ins)
    comm.results = res[n_out:]
    return res[:n_out] if multi else res[0]


def _mm(name, grid, ins, in_specs, pairs, n_acc, acc_shape, epilogue, out_shapes, out_specs, k_axis, comm=None):
    n_in, n_out = len(ins), len(out_shapes)
    nk = grid[k_axis] if k_axis is not None else 1

    def body(*refs):
        in_refs, out_refs, acc_refs = refs[:n_in], refs[n_in:n_in + n_out], refs[n_in + n_out:]
        parts = [None] * n_acc
        for ai, bi, mode, ci in pairs:
            d = _dot(in_refs[ai][...], in_refs[bi][...], mode)
            parts[ci] = d if parts[ci] is None else parts[ci] + d
        if nk == 1:
            epilogue(parts, in_refs, out_refs)
            return
        k = pl.program_id(k_axis)

        @pl.when(k == 0)
        def _():
            for c in range(n_acc):
                acc_refs[c][...] = parts[c]

        @pl.when(k > 0)
        def _():
            for c in range(n_acc):
                acc_refs[c][...] += parts[c]

        @pl.when(k == nk - 1)
        def _():
            epilogue([r[...] for r in acc_refs], in_refs, out_refs)

    scratch = [pltpu.VMEM(acc_shape, F32) for _ in range(n_acc)] if nk > 1 else []
    return _call(body, name=name, grid=grid, in_specs=in_specs, out_specs=list(out_specs), out_shape=list(out_shapes),
                 operands=ins, scratch=scratch, comm=comm)


def _mm2d(name, a, b, mode, tm, tn, tk, out_dtypes=(F32,), epilogue=None, extras=()):
    if mode == "nn":
        (M, K), N = a.shape, b.shape[1]
    elif mode == "nt":
        (M, K), N = a.shape, b.shape[0]
    else:
        (K, M), N = a.shape, b.shape[1]
    tm, tn, tk = _tile(M, tm), _tile(N, tn), _tile(K, tk)
    a_spec = pl.BlockSpec((tk, tm), lambda i, j, k: (k, i)) if mode == "tn" else pl.BlockSpec((tm, tk), lambda i, j, k: (i, k))
    b_spec = pl.BlockSpec((tn, tk), lambda i, j, k: (j, k)) if mode == "nt" else pl.BlockSpec((tk, tn), lambda i, j, k: (k, j))
    mn_spec = pl.BlockSpec((tm, tn), lambda i, j, k: (i, j))
    n_ex = len(extras)

    def default_epilogue(acc, ex_refs, out_refs):
        out_refs[0][...] = acc.astype(out_refs[0].dtype)

    epi = epilogue or default_epilogue

    def wrapped(accs, in_refs, out_refs):
        epi(accs[0], in_refs[2:2 + n_ex], out_refs)

    outs = _mm(name, (M // tm, N // tn, K // tk), [a, b, *extras], [a_spec, b_spec] + [mn_spec] * n_ex,
               [(0, 1, mode, 0)], 1, (tm, tn), wrapped,
               [jax.ShapeDtypeStruct((M, N), dt) for dt in out_dtypes], [mn_spec] * len(out_dtypes), 2)
    return outs[0] if len(out_dtypes) == 1 else outs


def _rms_stats(x, n):
    return lax.rsqrt(jnp.sum(x * x, axis=-1, keepdims=True) * (1.0 / n) + EPS)


def _rmsnorm_fwd(name, x, g, tm=512):
    T, D = x.shape
    tm = _tile(T, tm)

    def body(x_ref, g_ref, o_ref):
        xv = x_ref[...]
        o_ref[...] = (xv * _rms_stats(xv, D) * g_ref[...]).astype(BF16)

    return pl.pallas_call(body, name=name, grid=(T // tm,),
                          in_specs=[pl.BlockSpec((tm, D), lambda i: (i, 0)), pl.BlockSpec((1, D), lambda i: (0, 0))],
                          out_specs=pl.BlockSpec((tm, D), lambda i: (i, 0)),
                          out_shape=jax.ShapeDtypeStruct((T, D), BF16), compiler_params=_cp(1))(x, g)


def _rms_bwd_rows(dy, x, g, n):
    r = _rms_stats(x, n)
    xh = x * r
    gd = dy * g
    mean = jnp.sum(gd * xh, axis=-1, keepdims=True) * (1.0 / n)
    return r * (gd - xh * mean), dy * xh


def _rmsnorm_bwd(name, dn, x, g, res, tm=256):
    T, D = x.shape
    tm = _tile(T, tm)

    def body(dn_ref, x_ref, g_ref, res_ref, dx_ref, dxb_ref, dg_ref):
        dx, dgp = _rms_bwd_rows(dn_ref[...].astype(F32), x_ref[...], g_ref[...], D)
        dx = dx + res_ref[...]
        dx_ref[...] = dx
        dxb_ref[...] = dx.astype(BF16)

        @pl.when(pl.program_id(0) == 0)
        def _():
            dg_ref[...] = jnp.zeros_like(dg_ref)

        dg_ref[...] += jnp.sum(dgp, axis=0, keepdims=True)

    row = pl.BlockSpec((tm, D), lambda i: (i, 0))
    vec = pl.BlockSpec((1, D), lambda i: (0, 0))
    return pl.pallas_call(body, name=name, grid=(T // tm,), in_specs=[row, row, vec, row],
                          out_specs=[row, row, vec],
                          out_shape=[jax.ShapeDtypeStruct((T, D), F32), jax.ShapeDtypeStruct((T, D), BF16),
                                     jax.ShapeDtypeStruct((1, D), F32)],
                          compiler_params=_cp(1))(dn, x, g, res)


def _rope_tables(pos_col, inv_mla, inv_dil, tm=512):
    T = pos_col.shape[0]
    tm = _tile(T, tm)

    def body(p_ref, im_ref, id_ref, cm, sam, sbm, cd, sad, sbd):
        lane = lax.broadcasted_iota(jnp.int32, (tm, 128), 1)
        p = p_ref[...]
        am = p * im_ref[...]
        c, s = jnp.cos(am), jnp.sin(am)
        cm[...] = jnp.where(lane < 64, c, 0.0)
        sam[...] = jnp.where(lane < 32, -s, 0.0)
        sbm[...] = jnp.where((lane >= 32) & (lane < 64), s, 0.0)
        ad = p * id_ref[...]
        c, s = jnp.cos(ad), jnp.sin(ad)
        cd[...] = jnp.where(lane < 32, c, 1.0)
        sad[...] = jnp.where(lane < 16, -s, 0.0)
        sbd[...] = jnp.where((lane >= 16) & (lane < 32), s, 0.0)

    row = pl.BlockSpec((tm, 128), lambda i: (i, 0))
    vec = pl.BlockSpec((1, 128), lambda i: (0, 0))
    return pl.pallas_call(body, name="rope_tables", grid=(T // tm,),
                          in_specs=[pl.BlockSpec((tm, 1), lambda i: (i, 0)), vec, vec], out_specs=[row] * 6,
                          out_shape=[jax.ShapeDtypeStruct((T, 128), F32)] * 6, compiler_params=_cp(1))(pos_col, inv_mla, inv_dil)


def _rope(v, c, sa, sb, sh):
    return v * c + pltpu.roll(v, 128 - sh, 1) * sa + pltpu.roll(v, sh, 1) * sb


def _rope_t(d, c, sa, sb, sh):
    return d * c + pltpu.roll(d * sa, sh, 1) + pltpu.roll(d * sb, 128 - sh, 1)


def _lora_fwd(pa, g_cq, g_ckv, nq, nkv, tm=512):
    T, W = pa.shape
    tm = _tile(T, tm)

    def body(pa_ref, gq_ref, gk_ref, cq_ref, ckv_ref):
        a = pa_ref[:, :nq]
        cq_ref[...] = (a * _rms_stats(a, nq) * gq_ref[...]).astype(BF16)
        b = pa_ref[:, nq:nq + nkv]
        ckv_ref[...] = (b * _rms_stats(b, nkv) * gk_ref[...]).astype(BF16)

    return pl.pallas_call(body, name="lora_fwd", grid=(T // tm,),
                          in_specs=[pl.BlockSpec((tm, W), lambda i: (i, 0)), pl.BlockSpec((1, nq), lambda i: (0, 0)),
                                    pl.BlockSpec((1, nkv), lambda i: (0, 0))],
                          out_specs=[pl.BlockSpec((tm, nq), lambda i: (i, 0)), pl.BlockSpec((tm, nkv), lambda i: (i, 0))],
                          out_shape=[jax.ShapeDtypeStruct((T, nq), BF16), jax.ShapeDtypeStruct((T, nkv), BF16)],
                          compiler_params=_cp(1))(pa, g_cq, g_ckv)


def _lora_bwd(dcq, dckv, dkr, pa, g_cq, g_ckv, tm=512):
    T, W = pa.shape
    nq, nkv = dcq.shape[1], dckv.shape[1]
    tm = _tile(T, tm)

    def body(dcq_ref, dckv_ref, dkr_ref, pa_ref, gq_ref, gk_ref, dpa_ref, dgq_ref, dgk_ref):
        dx, dgp = _rms_bwd_rows(dcq_ref[...], pa_ref[:, :nq], gq_ref[...], nq)
        dpa_ref[:, :nq] = dx.astype(BF16)
        dx2, dgp2 = _rms_bwd_rows(dckv_ref[...], pa_ref[:, nq:nq + nkv], gk_ref[...], nkv)
        dpa_ref[:, nq:nq + nkv] = dx2.astype(BF16)
        dpa_ref[:, nq + nkv:] = dkr_ref[...].astype(BF16)

        @pl.when(pl.program_id(0) == 0)
        def _():
            dgq_ref[...] = jnp.zeros_like(dgq_ref)
            dgk_ref[...] = jnp.zeros_like(dgk_ref)

        dgq_ref[...] += jnp.sum(dgp, axis=0, keepdims=True)
        dgk_ref[...] += jnp.sum(dgp2, axis=0, keepdims=True)

    def row(n):
        return pl.BlockSpec((tm, n), lambda i: (i, 0))

    def vec(n):
        return pl.BlockSpec((1, n), lambda i: (0, 0))

    return pl.pallas_call(body, name="lora_bwd", grid=(T // tm,),
                          in_specs=[row(nq), row(nkv), row(128), row(W), vec(nq), vec(nkv)],
                          out_specs=[row(W), vec(nq), vec(nkv)],
                          out_shape=[jax.ShapeDtypeStruct((T, W), BF16), jax.ShapeDtypeStruct((1, nq), F32),
                                     jax.ShapeDtypeStruct((1, nkv), F32)],
                          compiler_params=_cp(1))(dcq, dckv, dkr, pa, g_cq, g_ckv)


def _sumsq(v):
    return jnp.sum(v * v, axis=-1, keepdims=True)


def _mla_prep_fwd(q_raw, kv, pa, tabs, gq, gk, H, kr_block, tm=256):
    T = q_raw.shape[0]
    tm = _tile(T, tm)
    scale = MLA_QK ** -0.5
    P = MLA_PAD

    def body(q_ref, kv_ref, kr_ref, c_ref, sa_ref, sb_ref, gq_ref, gk_ref, qo, ko, vo):
        c, sa, sb = c_ref[...], sa_ref[...], sb_ref[...]
        kr = kr_ref[...]
        kr2 = _sumsq(kr)
        for h in range(H):
            lo, hi = q_ref[:, h * P:h * P + 128], q_ref[:, h * P + 128:(h + 1) * P]
            r = lax.rsqrt((_sumsq(lo) + _sumsq(hi)) * (1.0 / MLA_QK) + EPS)
            qo[:, h * P:h * P + 128] = (lo * r * gq_ref[:, :128] * scale).astype(BF16)
            qo[:, h * P + 128:(h + 1) * P] = (_rope(hi * r * gq_ref[:, 128:], c, sa, sb, 32) * scale).astype(BF16)
            kn = kv_ref[:, h * P:h * P + 128]
            r = lax.rsqrt((_sumsq(kn) + kr2) * (1.0 / MLA_QK) + EPS)
            ko[:, h * P:h * P + 128] = (kn * r * gk_ref[:, :128]).astype(BF16)
            ko[:, h * P + 128:(h + 1) * P] = _rope(kr * r * gk_ref[:, 128:], c, sa, sb, 32).astype(BF16)
            vo[:, h * 128:(h + 1) * 128] = kv_ref[:, h * P + 128:(h + 1) * P].astype(BF16)

    wide = pl.BlockSpec((tm, H * P), lambda i: (i, 0))
    lane = pl.BlockSpec((tm, 128), lambda i: (i, 0))
    vec = pl.BlockSpec((1, P), lambda i: (0, 0))
    return pl.pallas_call(body, name="mla_prep_fwd", grid=(T // tm,),
                          in_specs=[wide, wide, pl.BlockSpec((tm, 128), lambda i: (i, kr_block)), lane, lane, lane, vec, vec],
                          out_specs=[wide, wide, pl.BlockSpec((tm, H * 128), lambda i: (i, 0))],
                          out_shape=[jax.ShapeDtypeStruct((T, H * P), BF16), jax.ShapeDtypeStruct((T, H * P), BF16),
                                     jax.ShapeDtypeStruct((T, H * 128), BF16)],
                          compiler_params=_cp(1))(q_raw, kv, pa, *tabs, gq, gk)


def _mla_prep_bwd(dq, dk, dv, q_raw, kv, pa, tabs, gq, gk, H, kr_block, tm=256):
    T = q_raw.shape[0]
    tm = _tile(T, tm)
    scale = MLA_QK ** -0.5
    P = MLA_PAD

    def body(dq_ref, dk_ref, dv_ref, q_ref, kv_ref, kr_ref, c_ref, sa_ref, sb_ref, gq_ref, gk_ref,
             dqr, dkv, dkr, dgq, dgk):
        c, sa, sb = c_ref[...], sa_ref[...], sb_ref[...]
        kr = kr_ref[...]
        kr2 = _sumsq(kr)
        gql, gqh, gkl, gkh = gq_ref[:, :128], gq_ref[:, 128:], gk_ref[:, :128], gk_ref[:, 128:]
        dkr_acc = jnp.zeros((tm, 128), F32)
        sums = [jnp.zeros((1, 128), F32) for _ in range(4)]
        for h in range(H):
            lo_s, hi_s = slice(h * P, h * P + 128), slice(h * P + 128, (h + 1) * P)
            lo, hi = q_ref[:, lo_s], q_ref[:, hi_s]
            r = lax.rsqrt((_sumsq(lo) + _sumsq(hi)) * (1.0 / MLA_QK) + EPS)
            ql, qh = lo * r, hi * r
            dyl = dq_ref[:, lo_s] * scale
            dyh = _rope_t(dq_ref[:, hi_s] * scale, c, sa, sb, 32)
            gl, gh = dyl * gql, dyh * gqh
            mean = (jnp.sum(gl * ql, axis=-1, keepdims=True) + jnp.sum(gh * qh, axis=-1, keepdims=True)) * (1.0 / MLA_QK)
            dqr[:, lo_s] = (r * (gl - ql * mean)).astype(BF16)
            dqr[:, hi_s] = (r * (gh - qh * mean)).astype(BF16)
            sums[0] += jnp.sum(dyl * ql, axis=0, keepdims=True)
            sums[1] += jnp.sum(dyh * qh, axis=0, keepdims=True)
            kn = kv_ref[:, lo_s]
            r = lax.rsqrt((_sumsq(kn) + kr2) * (1.0 / MLA_QK) + EPS)
            kl, kh = kn * r, kr * r
            dkl = dk_ref[:, lo_s]
            dkh = _rope_t(dk_ref[:, hi_s], c, sa, sb, 32)
            gl, gh = dkl * gkl, dkh * gkh
            mean = (jnp.sum(gl * kl, axis=-1, keepdims=True) + jnp.sum(gh * kh, axis=-1, keepdims=True)) * (1.0 / MLA_QK)
            dkv[:, lo_s] = (r * (gl - kl * mean)).astype(BF16)
            dkr_acc += r * (gh - kh * mean)
            dkv[:, hi_s] = dv_ref[:, h * 128:(h + 1) * 128].astype(BF16)
            sums[2] += jnp.sum(dkl * kl, axis=0, keepdims=True)
            sums[3] += jnp.sum(dkh * kh, axis=0, keepdims=True)
        dkr[...] = dkr_acc

        @pl.when(pl.program_id(0) == 0)
        def _():
            dgq[...] = jnp.zeros_like(dgq)
            dgk[...] = jnp.zeros_like(dgk)

        dgq[:, :128] += sums[0]
        dgq[:, 128:] += sums[1]
        dgk[:, :128] += sums[2]
        dgk[:, 128:] += sums[3]

    wide = pl.BlockSpec((tm, H * P), lambda i: (i, 0))
    lane = pl.BlockSpec((tm, 128), lambda i: (i, 0))
    vec = pl.BlockSpec((1, P), lambda i: (0, 0))
    return pl.pallas_call(body, name="mla_prep_bwd", grid=(T // tm,),
                          in_specs=[wide, wide, pl.BlockSpec((tm, H * 128), lambda i: (i, 0)), wide, wide,
                                    pl.BlockSpec((tm, 128), lambda i: (i, kr_block)), lane, lane, lane, vec, vec],
                          out_specs=[wide, wide, lane, vec, vec],
                          out_shape=[jax.ShapeDtypeStruct((T, H * P), BF16), jax.ShapeDtypeStruct((T, H * P), BF16),
                                     jax.ShapeDtypeStruct((T, 128), F32), jax.ShapeDtypeStruct((1, P), F32),
                                     jax.ShapeDtypeStruct((1, P), F32)],
                          compiler_params=_cp(1))(dq, dk, dv, q_raw, kv, pa, *tabs, gq, gk)


def _on_blocks(s, live, step):
    @pl.when(s == 0)
    def _():
        step(True)

    @pl.when((s > 0) & live)
    def _():
        step(False)


def _sub_mask(sub, width, row0, col0, q_axis):
    r = lax.broadcasted_iota(jnp.int32, (sub, width), 0) + row0
    c = lax.broadcasted_iota(jnp.int32, (sub, width), 1) + col0
    return (r >= c) if q_axis == 0 else (c >= r)


def _mla_fwd(q, k, v, H, blk, comm=None):
    T, wq = q.shape[0], q.shape[1] // H
    nq = T // blk
    sub = _tile(blk, MLA_SUB)

    def body(q_ref, k_ref, v_ref, o_ref, lse_ref, m_sc, l_sc, acc_sc):
        i, s = pl.program_id(1), pl.program_id(2)

        @pl.when(s == 0)
        def _():
            m_sc[...] = jnp.full_like(m_sc, NEG)
            l_sc[...] = jnp.zeros_like(l_sc)
            acc_sc[...] = jnp.zeros_like(acc_sc)

        def step(masked):
            for r in range(blk // sub):
                rs = slice(r * sub, (r + 1) * sub)
                nk = (r + 1) * sub if masked else blk
                sc = _dot(q_ref[rs, :], k_ref[:nk, :], "nt")
                if masked:
                    sc = jnp.where(_sub_mask(sub, nk, r * sub, 0, 0), sc, NEG)
                m_prev = m_sc[rs, :]
                m_new = jnp.maximum(m_prev, jnp.max(sc, axis=-1, keepdims=True))
                p = jnp.exp(sc - m_new)
                alpha = jnp.exp(m_prev - m_new)
                l_sc[rs, :] = alpha * l_sc[rs, :] + jnp.sum(p, axis=-1, keepdims=True)
                acc_sc[rs, :] = alpha * acc_sc[rs, :] + _dot(p, v_ref[:nk, :], "nn")
                m_sc[rs, :] = m_new

        _on_blocks(s, i - s >= 0, step)

        @pl.when(s == nq - 1)
        def _():
            o_ref[...] = (acc_sc[...] / l_sc[...]).astype(BF16)
            lse_ref[...] = m_sc[...] + jnp.log(l_sc[...])

    kv_spec = lambda w: pl.BlockSpec((blk, w), lambda h, i, s: (jnp.maximum(i - s, 0), h))
    return _call(
        body, name="mla_fwd", grid=(H, nq, nq),
        in_specs=[pl.BlockSpec((blk, wq), lambda h, i, s: (i, h)), kv_spec(wq), kv_spec(HEAD)],
        out_specs=[pl.BlockSpec((blk, HEAD), lambda h, i, s: (i, h)), pl.BlockSpec((None, blk, 1), lambda h, i, s: (h, i, 0))],
        out_shape=[jax.ShapeDtypeStruct((T, H * HEAD), BF16), jax.ShapeDtypeStruct((H, T, 1), F32)],
        scratch=[pltpu.VMEM((blk, 1), F32), pltpu.VMEM((blk, 1), F32), pltpu.VMEM((blk, HEAD), F32)],
        operands=[q, k, v], comm=comm)


def _mla_dq(q, k, v, do, lse, dl, H, blk, comm=None):
    T, wq = q.shape[0], q.shape[1] // H
    nq = T // blk
    sub = _tile(blk, MLA_SUB)

    def body(q_ref, k_ref, v_ref, do_ref, lse_ref, dl_ref, dq_ref, acc_sc):
        i, s = pl.program_id(1), pl.program_id(2)

        @pl.when(s == 0)
        def _():
            acc_sc[...] = jnp.zeros_like(acc_sc)

        def step(masked):
            for r in range(blk // sub):
                rs = slice(r * sub, (r + 1) * sub)
                nk = (r + 1) * sub if masked else blk
                sc = _dot(q_ref[rs, :], k_ref[:nk, :], "nt")
                if masked:
                    sc = jnp.where(_sub_mask(sub, nk, r * sub, 0, 0), sc, NEG)
                p = jnp.exp(sc - lse_ref[rs, :])
                dp = _dot(do_ref[rs, :], v_ref[:nk, :], "nt")
                acc_sc[rs, :] += _dot(p * (dp - dl_ref[rs, :]), k_ref[:nk, :], "nn")

        _on_blocks(s, i - s >= 0, step)

        @pl.when(s == nq - 1)
        def _():
            dq_ref[...] = acc_sc[...]

    kv_spec = lambda w: pl.BlockSpec((blk, w), lambda h, i, s: (jnp.maximum(i - s, 0), h))
    col = pl.BlockSpec((None, blk, 1), lambda h, i, s: (h, i, 0))
    return _call(
        body, name="mla_dq", grid=(H, nq, nq),
        in_specs=[pl.BlockSpec((blk, wq), lambda h, i, s: (i, h)), kv_spec(wq), kv_spec(HEAD),
                  pl.BlockSpec((blk, HEAD), lambda h, i, s: (i, h)), col, col],
        out_specs=pl.BlockSpec((blk, wq), lambda h, i, s: (i, h)), out_shape=jax.ShapeDtypeStruct((T, H * wq), F32),
        scratch=[pltpu.VMEM((blk, wq), F32)], operands=[q, k, v, do, lse, dl], comm=comm)


def _as_lanes(col):
    return jnp.transpose(jnp.broadcast_to(col, (col.shape[0], 128)))[0:1, :]


def _mla_dkv(q, k, v, do, lse, dl, H, blk):
    T, wq = q.shape[0], q.shape[1] // H
    nq = T // blk
    sub = _tile(blk, MLA_SUB)

    def body(q_ref, k_ref, v_ref, do_ref, lse_col, dl_col, dk_ref, dv_ref, dk_sc, dv_sc):
        j, s = pl.program_id(1), pl.program_id(2)

        @pl.when(s == 0)
        def _():
            dk_sc[...] = jnp.zeros_like(dk_sc)
            dv_sc[...] = jnp.zeros_like(dv_sc)

        def step(masked):
            lse_ref, dl_ref = _as_lanes(lse_col[...]), _as_lanes(dl_col[...])
            for r in range(blk // sub):
                rs = slice(r * sub, (r + 1) * sub)
                q0 = r * sub if masked else 0
                st = _dot(k_ref[rs, :], q_ref[q0:, :], "nt")
                if masked:
                    st = jnp.where(_sub_mask(sub, blk - q0, r * sub, q0, 1), st, NEG)
                pt = jnp.exp(st - lse_ref[:, q0:])
                dpt = _dot(v_ref[rs, :], do_ref[q0:, :], "nt")
                dv_sc[rs, :] += _dot(pt, do_ref[q0:, :], "nn")
                dk_sc[rs, :] += _dot(pt * (dpt - dl_ref[:, q0:]), q_ref[q0:, :], "nn")

        _on_blocks(s, j + s < nq, step)

        @pl.when(s == nq - 1)
        def _():
            dk_ref[...] = dk_sc[...]
            dv_ref[...] = dv_sc[...]

    q_row = lambda j, s: jnp.minimum(j + s, nq - 1)
    row = pl.BlockSpec((None, blk, 1), lambda h, j, s: (h, q_row(j, s), 0))
    return pl.pallas_call(
        body, name="mla_dkv", grid=(H, nq, nq),
        in_specs=[pl.BlockSpec((blk, wq), lambda h, j, s: (q_row(j, s), h)), pl.BlockSpec((blk, wq), lambda h, j, s: (j, h)),
                  pl.BlockSpec((blk, HEAD), lambda h, j, s: (j, h)), pl.BlockSpec((blk, HEAD), lambda h, j, s: (q_row(j, s), h)),
                  row, row],
        out_specs=[pl.BlockSpec((blk, wq), lambda h, j, s: (j, h)), pl.BlockSpec((blk, HEAD), lambda h, j, s: (j, h))],
        out_shape=[jax.ShapeDtypeStruct((T, H * wq), F32), jax.ShapeDtypeStruct((T, H * HEAD), F32)],
        scratch_shapes=[pltpu.VMEM((blk, wq), F32), pltpu.VMEM((blk, HEAD), F32)],
        compiler_params=_cp(3))(q, k, v, do, lse, dl)


def _row_dot(name, a, b, H, tm=1024):
    T = a.shape[0]
    tm = _tile(T, tm)

    def body(a_ref, b_ref, o_ref):
        o_ref[...] = jnp.sum(a_ref[...].astype(F32) * b_ref[...].astype(F32), axis=-1, keepdims=True)

    blk = pl.BlockSpec((tm, HEAD), lambda h, i: (i, h))
    return pl.pallas_call(body, name=name, grid=(H, T // tm), in_specs=[blk, blk],
                          out_specs=pl.BlockSpec((None, tm, 1), lambda h, i: (h, i, 0)),
                          out_shape=jax.ShapeDtypeStruct((H, T, 1), F32), compiler_params=_cp(2))(a, b)


def _tri(n):
    row = lax.broadcasted_iota(jnp.int32, (n, n), 0)
    col = lax.broadcasted_iota(jnp.int32, (n, n), 1)
    return col <= row, col >= row


def _win_fwd(name, q, k, pd, H, n, L, QB):
    T = q.shape[0]
    U = QB // n

    def body(q_ref, k_ref, v_ref, kp_ref, vp_ref, o_ref, lse_ref):
        i = pl.program_id(1)
        own_ok, before_ok = _tri(n)
        for e in range(U):
            rs = slice(e * n, (e + 1) * n)
            ps = slice((e - 1) * n, e * n)
            k_lo, v_lo = (kp_ref[...], vp_ref[...]) if e == 0 else (k_ref[ps, :], v_ref[ps, :])
            inside = ((i * QB + e * n) % L) != 0
            qu = q_ref[rs, :]
            s_hi = jnp.where(own_ok, _dot(qu, k_ref[rs, :], "nt"), NEG)
            s_lo = jnp.where(before_ok & inside, _dot(qu, k_lo, "nt"), NEG)
            m = jnp.maximum(jnp.max(s_hi, axis=-1, keepdims=True), jnp.max(s_lo, axis=-1, keepdims=True))
            p_hi, p_lo = jnp.exp(s_hi - m), jnp.exp(s_lo - m)
            l = jnp.sum(p_hi, axis=-1, keepdims=True) + jnp.sum(p_lo, axis=-1, keepdims=True)
            acc = _dot(p_hi, v_ref[rs, :], "nn") + _dot(p_lo, v_lo, "nn")
            o_ref[rs, :] = acc / l
            lse_ref[rs, :] = m + jnp.log(l)

    cur = lambda c0: pl.BlockSpec((QB, HEAD), lambda h, i: (i, c0 + h))
    prev = lambda c0: pl.BlockSpec((n, HEAD), lambda h, i: (jnp.maximum(i * U - 1, 0), c0 + h))
    return pl.pallas_call(
        body, name=name, grid=(H, T // QB), in_specs=[cur(0), cur(0), cur(2 * H), prev(0), prev(2 * H)],
        out_specs=[pl.BlockSpec((QB, HEAD), lambda h, i: (i, h)), pl.BlockSpec((None, QB, 1), lambda h, i: (h, i, 0))],
        out_shape=[jax.ShapeDtypeStruct((T, H * HEAD), F32), jax.ShapeDtypeStruct((H, T, 1), F32)],
        compiler_params=_cp(2))(q, k, pd, k, pd)


def _win_dq(name, q, k, pd, do, lse, dl, H, n, L, QB):
    T = q.shape[0]
    U = QB // n

    def body(q_ref, k_ref, v_ref, kp_ref, vp_ref, do_ref, lse_ref, dl_ref, dq_ref):
        i = pl.program_id(1)
        own_ok, before_ok = _tri(n)
        for e in range(U):
            rs = slice(e * n, (e + 1) * n)
            ps = slice((e - 1) * n, e * n)
            k_lo, v_lo = (kp_ref[...], vp_ref[...]) if e == 0 else (k_ref[ps, :], v_ref[ps, :])
            inside = ((i * QB + e * n) % L) != 0
            qu, dou, lse_u, dl_u = q_ref[rs, :], do_ref[rs, :], lse_ref[rs, :], dl_ref[rs, :]
            p_hi = jnp.exp(jnp.where(own_ok, _dot(qu, k_ref[rs, :], "nt"), NEG) - lse_u)
            p_lo = jnp.exp(jnp.where(before_ok & inside, _dot(qu, k_lo, "nt"), NEG) - lse_u)
            ds_hi = p_hi * (_dot(dou, v_ref[rs, :], "nt") - dl_u)
            ds_lo = p_lo * (_dot(dou, v_lo, "nt") - dl_u)
            dq_ref[rs, :] = _dot(ds_hi, k_ref[rs, :], "nn") + _dot(ds_lo, k_lo, "nn")

    cur = lambda c0: pl.BlockSpec((QB, HEAD), lambda h, i: (i, c0 + h))
    prev = lambda c0: pl.BlockSpec((n, HEAD), lambda h, i: (jnp.maximum(i * U - 1, 0), c0 + h))
    flat = pl.BlockSpec((QB, HEAD), lambda h, i: (i, h))
    col = pl.BlockSpec((None, QB, 1), lambda h, i: (h, i, 0))
    return pl.pallas_call(
        body, name=name, grid=(H, T // QB), in_specs=[cur(0), cur(0), cur(2 * H), prev(0), prev(2 * H), flat, col, col],
        out_specs=flat, out_shape=jax.ShapeDtypeStruct((T, H * HEAD), F32), compiler_params=_cp(2))(q, k, pd, k, pd, do, lse, dl)


def _win_dkv(name, q, k, pd, do, lse, dl, H, n, L, QB):
    T = q.shape[0]
    U = QB // n
    last = T // n - 1

    def body(q_ref, k_ref, v_ref, do_ref, lse_col, dl_col, qn_ref, don_ref, lsen_col, dln_col, dk_ref, dv_ref):
        i = pl.program_id(1)
        own_ok, after_ok = _tri_t(n)
        lse_ref, dl_ref = _as_lanes(lse_col[...]), _as_lanes(dl_col[...])
        for e in range(U):
            rs = slice(e * n, (e + 1) * n)
            ns = slice((e + 1) * n, (e + 2) * n)
            if e < U - 1:
                qn, don, lsen, dln = q_ref[ns, :], do_ref[ns, :], lse_ref[:, ns], dl_ref[:, ns]
            else:
                qn, don, lsen, dln = qn_ref[...], don_ref[...], _as_lanes(lsen_col[...]), _as_lanes(dln_col[...])
            nxt = i * QB + (e + 1) * n
            inside = ((nxt % L) != 0) & (nxt < T)
            ku, vu, qu, dou = k_ref[rs, :], v_ref[rs, :], q_ref[rs, :], do_ref[rs, :]
            pt_own = jnp.exp(jnp.where(own_ok, _dot(ku, qu, "nt"), NEG) - lse_ref[:, rs])
            pt_aft = jnp.exp(jnp.where(after_ok & inside, _dot(ku, qn, "nt"), NEG) - lsen)
            dst_own = pt_own * (_dot(vu, dou, "nt") - dl_ref[:, rs])
            dst_aft = pt_aft * (_dot(vu, don, "nt") - dln)
            dv_ref[rs, :] = _dot(pt_own, dou, "nn") + _dot(pt_aft, don, "nn")
            dk_ref[rs, :] = _dot(dst_own, qu, "nn") + _dot(dst_aft, qn, "nn")

    cur = lambda c0: pl.BlockSpec((QB, HEAD), lambda h, i: (i, c0 + h))
    flat = pl.BlockSpec((QB, HEAD), lambda h, i: (i, h))
    row = pl.BlockSpec((None, QB, 1), lambda h, i: (h, i, 0))
    nxt_unit = lambda h, i: jnp.minimum((i + 1) * U, last)
    return pl.pallas_call(
        body, name=name, grid=(H, T // QB),
        in_specs=[cur(0), cur(0), cur(2 * H), flat, row, row,
                  pl.BlockSpec((n, HEAD), lambda h, i: (nxt_unit(h, i), h)),
                  pl.BlockSpec((n, HEAD), lambda h, i: (nxt_unit(h, i), h)),
                  pl.BlockSpec((None, n, 1), lambda h, i: (h, nxt_unit(h, i), 0)),
                  pl.BlockSpec((None, n, 1), lambda h, i: (h, nxt_unit(h, i), 0))],
        out_specs=[flat, flat], out_shape=[jax.ShapeDtypeStruct((T, H * HEAD), F32)] * 2,
        compiler_params=_cp(2))(q, k, pd, do, lse, dl, q, do, lse, dl)


def _tri_t(n):
    key = lax.broadcasted_iota(jnp.int32, (n, n), 0)
    qry = lax.broadcasted_iota(jnp.int32, (n, n), 1)
    return key <= qry, key >= qry


def _merge_groups(os_, lses, H, tm=512):
    G = len(os_)
    T = os_[0].shape[0]
    tm = _tile(T, tm)

    def body(*refs):
        o_refs, l_refs, o_out, lse_out = refs[:G], refs[G:2 * G], refs[2 * G], refs[2 * G + 1]
        ls = [r[...] for r in l_refs]
        m = ls[0]
        for x in ls[1:]:
            m = jnp.maximum(m, x)
        ws = [jnp.exp(x - m) for x in ls]
        tot = ws[0]
        for x in ws[1:]:
            tot = tot + x
        acc = ws[0] * o_refs[0][...]
        for x, r in zip(ws[1:], o_refs[1:]):
            acc = acc + x * r[...]
        o_out[...] = (acc / tot).astype(BF16)
        lse_out[...] = m + jnp.log(tot)

    flat = pl.BlockSpec((tm, HEAD), lambda h, i: (i, h))
    col = pl.BlockSpec((None, tm, 1), lambda h, i: (h, i, 0))
    return pl.pallas_call(body, name="dil_merge", grid=(H, T // tm), in_specs=[flat] * G + [col] * G, out_specs=[flat, col],
                          out_shape=[jax.ShapeDtypeStruct((T, H * HEAD), BF16), jax.ShapeDtypeStruct((H, T, 1), F32)],
                          compiler_params=_cp(2))(*os_, *lses)


def _dil_prep_fwd_pm(name, pd, tabs, gq, gk, H, tm=256):
    T = pd.shape[0]
    tm = _tile(T, tm)
    scale = HEAD ** -0.5

    def body(p_ref, c_ref, sa_ref, sb_ref, gq_ref, gk_ref, qo, ko):
        c, sa, sb = c_ref[...], sa_ref[...], sb_ref[...]
        for h in range(H):
            q = p_ref[:, h * 128:(h + 1) * 128].astype(F32)
            qo[:, h * 128:(h + 1) * 128] = (_rope(q * _rms_stats(q, HEAD) * gq_ref[...], c, sa, sb, 16) * scale).astype(BF16)
            k = p_ref[:, (H + h) * 128:(H + h + 1) * 128].astype(F32)
            ko[:, h * 128:(h + 1) * 128] = _rope(k * _rms_stats(k, HEAD) * gk_ref[...], c, sa, sb, 16).astype(BF16)

    lane = pl.BlockSpec((tm, 128), lambda i: (i, 0))
    gain = pl.BlockSpec((1, 128), lambda i: (0, 0))
    out = pl.BlockSpec((tm, H * 128), lambda i: (i, 0))
    return pl.pallas_call(body, name=name, grid=(T // tm,),
                          in_specs=[pl.BlockSpec((tm, 3 * H * 128), lambda i: (i, 0)), lane, lane, lane, gain, gain],
                          out_specs=[out, out], out_shape=[jax.ShapeDtypeStruct((T, H * 128), BF16)] * 2,
                          compiler_params=_cp(1))(pd, *tabs, gq, gk)


def _dil_prep_bwd_pm(name, dq, dk, dv, pd, tabs, gq, gk, H, tm=256):
    T = pd.shape[0]
    tm = _tile(T, tm)
    scale = HEAD ** -0.5

    def body(dq_ref, dk_ref, dv_ref, p_ref, c_ref, sa_ref, sb_ref, gq_ref, gk_ref, dp_ref, dgq, dgk):
        c, sa, sb = c_ref[...], sa_ref[...], sb_ref[...]
        sq = jnp.zeros((1, 128), F32)
        sk = jnp.zeros((1, 128), F32)
        for h in range(H):
            hs = slice(h * 128, (h + 1) * 128)
            q = p_ref[:, hs].astype(F32)
            dx, dgp = _rms_bwd_rows(_rope_t(dq_ref[:, hs] * scale, c, sa, sb, 16), q, gq_ref[...], HEAD)
            dp_ref[:, hs] = dx.astype(BF16)
            sq += jnp.sum(dgp, axis=0, keepdims=True)
            ks = slice((H + h) * 128, (H + h + 1) * 128)
            k = p_ref[:, ks].astype(F32)
            dx, dgp = _rms_bwd_rows(_rope_t(dk_ref[:, hs], c, sa, sb, 16), k, gk_ref[...], HEAD)
            dp_ref[:, ks] = dx.astype(BF16)
            sk += jnp.sum(dgp, axis=0, keepdims=True)
            dp_ref[:, (2 * H + h) * 128:(2 * H + h + 1) * 128] = dv_ref[:, hs].astype(BF16)

        @pl.when(pl.program_id(0) == 0)
        def _():
            dgq[...] = jnp.zeros_like(dgq)
            dgk[...] = jnp.zeros_like(dgk)

        dgq[...] += sq
        dgk[...] += sk

    lane = pl.BlockSpec((tm, 128), lambda i: (i, 0))
    flat = pl.BlockSpec((tm, H * 128), lambda i: (i, 0))
    vec = pl.BlockSpec((1, 128), lambda i: (0, 0))
    return pl.pallas_call(body, name=name, grid=(T // tm,),
                          in_specs=[flat, flat, flat, pl.BlockSpec((tm, 3 * H * 128), lambda i: (i, 0)),
                                    lane, lane, lane, vec, vec],
                          out_specs=[pl.BlockSpec((tm, 3 * H * 128), lambda i: (i, 0)), vec, vec],
                          out_shape=[jax.ShapeDtypeStruct((T, 3 * H * 128), BF16), jax.ShapeDtypeStruct((1, 128), F32),
                                     jax.ShapeDtypeStruct((1, 128), F32)],
                          compiler_params=_cp(1))(dq, dk, dv, pd, *tabs, gq, gk)


def _to_phase(a, d, axis=0):
    if d == 1:
        return a
    sh = a.shape
    T = sh[axis]
    b = a.reshape(*sh[:axis], T // d, d, *sh[axis + 1:])
    return jnp.swapaxes(b, axis, axis + 1).reshape(sh)


def _from_phase(a, d, axis=0):
    if d == 1:
        return a
    sh = a.shape
    T = sh[axis]
    b = a.reshape(*sh[:axis], d, T // d, *sh[axis + 1:])
    return jnp.swapaxes(b, axis, axis + 1).reshape(sh)


def _ffn_up(name, n, wg, wu, tm=512, comm=None):
    T, D = n.shape
    nd, _, fc = wg.shape
    tm = _tile(T, tm)

    def epilogue(accs, in_refs, out_refs):
        a, b = accs
        out_refs[0][...] = a.astype(BF16)
        out_refs[1][...] = b.astype(BF16)
        out_refs[2][...] = (a * _sigmoid(a) * b).astype(BF16)

    w_spec = pl.BlockSpec((None, D, fc), lambda j, i: (j, 0, 0))
    o_spec = pl.BlockSpec((None, tm, fc), lambda j, i: (j, i, 0))
    sh = jax.ShapeDtypeStruct((nd, T, fc), BF16)
    return _mm(name, (nd, T // tm), [n, wg, wu], [pl.BlockSpec((tm, D), lambda j, i: (i, 0)), w_spec, w_spec],
               [(0, 1, "nn", 0), (0, 2, "nn", 1)], 2, None, epilogue, [sh, sh, sh], [o_spec] * 3, None, comm=comm)


def _ffn_down(name, s, wd, res, tm=512, tn=512, comm=None):
    nd, T, fc = s.shape
    D = wd.shape[2]
    tm, tn = _tile(T, tm), _tile(D, tn)

    def body(s_ref, w_ref, r_ref, o_ref):
        acc = _dot(s_ref[0], w_ref[0], "nn")
        for j in range(1, nd):
            acc += _dot(s_ref[j], w_ref[j], "nn")
        o_ref[...] = r_ref[...] + 0.5 * acc

    mn = pl.BlockSpec((tm, tn), lambda i, j: (i, j))
    return _call(body, name=name, grid=(T // tm, D // tn),
                 in_specs=[pl.BlockSpec((nd, tm, fc), lambda i, j: (0, i, 0)), pl.BlockSpec((nd, fc, tn), lambda i, j: (0, 0, j)), mn],
                 out_specs=mn, out_shape=jax.ShapeDtypeStruct((T, D), F32), operands=[s, wd, res], comm=comm)


def _ffn_bwd_act(name, dxb, wdt, a, b, tm=512, comm=None):
    T, D = dxb.shape
    nd, _, fc = wdt.shape
    tm = _tile(T, tm)

    def epilogue(accs, in_refs, out_refs):
        ds = 0.5 * accs[0]
        av, bv = in_refs[2][...].astype(F32), in_refs[3][...].astype(F32)
        sg = _sigmoid(av)
        out_refs[0][...] = (ds * bv * sg * (1.0 + av * (1.0 - sg))).astype(BF16)
        out_refs[1][...] = (ds * av * sg).astype(BF16)

    act = pl.BlockSpec((None, tm, fc), lambda j, i: (j, i, 0))
    sh = jax.ShapeDtypeStruct((nd, T, fc), BF16)
    return _mm(name, (nd, T // tm), [dxb, wdt, a, b],
               [pl.BlockSpec((tm, D), lambda j, i: (i, 0)), pl.BlockSpec((None, D, fc), lambda j, i: (j, 0, 0)), act, act],
               [(0, 1, "nn", 0)], 1, None, epilogue, [sh, sh], [act, act], None, comm=comm)


def _ffn_dwd(name, s, dxb, tk=2048):
    nd, T, fc = s.shape
    D = dxb.shape[1]
    tk = _tile(T, tk)

    def epilogue(accs, in_refs, out_refs):
        out_refs[0][...] = (0.5 * accs[0]).astype(BF16)

    return _mm(name, (nd, T // tk), [s, dxb],
               [pl.BlockSpec((None, tk, fc), lambda j, k: (j, k, 0)), pl.BlockSpec((tk, D), lambda j, k: (k, 0))],
               [(0, 1, "tn", 0)], 1, (fc, D), epilogue, [jax.ShapeDtypeStruct((nd, fc, D), BF16)],
               [pl.BlockSpec((None, fc, D), lambda j, k: (j, 0, 0))], 1)[0]


def _ffn_dw(name, n, dact, tk=2048):
    T, D = n.shape
    nd, _, fc = dact.shape
    tk = _tile(T, tk)

    def epilogue(accs, in_refs, out_refs):
        out_refs[0][...] = accs[0].astype(BF16)

    return _mm(name, (nd, T // tk), [n, dact],
               [pl.BlockSpec((tk, D), lambda j, k: (k, 0)), pl.BlockSpec((None, tk, fc), lambda j, k: (j, k, 0))],
               [(0, 1, "tn", 0)], 1, (D, fc), epilogue, [jax.ShapeDtypeStruct((nd, D, fc), BF16)],
               [pl.BlockSpec((None, D, fc), lambda j, k: (j, 0, 0))], 1)[0]


def _ffn_dn(name, da, db, wg, wu, tm=512, tn=256, comm=None):
    nd, T, fc = da.shape
    D = wg.shape[1]
    tm, tn = _tile(T, tm), _tile(D, tn)

    def body(da_ref, db_ref, wg_ref, wu_ref, o_ref):
        acc = _dot(da_ref[0], wg_ref[0], "nt") + _dot(db_ref[0], wu_ref[0], "nt")
        for j in range(1, nd):
            acc += _dot(da_ref[j], wg_ref[j], "nt") + _dot(db_ref[j], wu_ref[j], "nt")
        o_ref[...] = acc

    act = pl.BlockSpec((nd, tm, fc), lambda i, j: (0, i, 0))
    w_spec = pl.BlockSpec((nd, tn, fc), lambda i, j: (0, j, 0))
    return _call(body, name=name, grid=(T // tm, D // tn), in_specs=[act, act, w_spec, w_spec],
                 out_specs=pl.BlockSpec((tm, tn), lambda i, j: (i, j)), out_shape=jax.ShapeDtypeStruct((T, D), F32),
                 operands=[da, db, wg, wu], comm=comm)


def _ffn_forward(tag, x, g, wg, wu, wd, comm_up=None, comm_down=None):
    n = _rmsnorm_fwd(tag + "_norm", x, g)
    a, b, s = _ffn_up(tag + "_up", n, wg, wu, comm=comm_up)
    return _ffn_down(tag + "_down", s, wd, x, comm=comm_down() if comm_down else None), (n, a, b, s)


def _ffn_backward(tag, dx, dxb, x, g, wg, wu, wd, saved, comm_act=None, comm_dn=None, weights_first=True):
    n, a, b, s = saved
    da, db = _ffn_bwd_act(tag + "_bwd_act", dxb, jnp.swapaxes(wd, 1, 2), a, b, comm=comm_act)

    def weight_grads():
        return _ffn_dw(tag + "_dwg", n, da), _ffn_dw(tag + "_dwu", n, db), _ffn_dwd(tag + "_dwd", s, dxb)

    dws = weight_grads() if weights_first else None
    dn = _ffn_dn(tag + "_dn", da, db, wg, wu, comm=comm_dn(dws) if comm_dn else None)
    d_wg, d_wu, d_wd = dws if weights_first else weight_grads()
    dx_in, dxb_in, dg = _rmsnorm_bwd(tag + "_norm_bwd", dn, x, g, dx)
    return dx_in, dxb_in, dg, d_wg, d_wu, d_wd


def _place():
    x, y, c = lax.axis_index("x"), lax.axis_index("y"), lax.axis_index("c")
    return x, y, c, [(1 - x, y), (x, 1 - y), (1 - x, 1 - y)]


def _allgather(name, shards):
    n = len(shards)

    def body(*refs):
        ins, outs = refs[:n], refs[n:2 * n]
        send_sems, recv_sems, local_sems = refs[2 * n:]
        x, y, c, chips = _place()
        me, sibling = (x, y, c), (x, y, 1 - c)

        def slot(a, p):
            return outs[a].at[4 * p[0] + 2 * p[1] + p[2]]

        def copy(a, kk, block, to, src=None):
            return pltpu.make_async_remote_copy(
                src_ref=slot(a, block) if src is None else src, dst_ref=slot(a, block),
                send_sem=send_sems.at[a * 7 + kk], recv_sem=recv_sems.at[a * 7 + kk],
                device_id=to, device_id_type=MESH)

        mine = [pltpu.make_async_copy(ins[a], slot(a, me), local_sems.at[a]) for a in range(n)]
        for cp in mine:
            cp.start()
        first = []
        for a in range(n):
            first.append(copy(a, 0, me, sibling, src=ins[a]))
            first += [copy(a, 1 + j, me, (*chip, c), src=ins[a]) for j, chip in enumerate(chips)]
        for cp in first:
            cp.start()
        passed = []
        for j, chip in enumerate(chips):
            for a in range(n):
                copy(a, 1 + j, (*chip, c), me).wait_recv()
                fwd = copy(a, 4 + j, (*chip, c), sibling)
                fwd.start()
                passed.append(fwd)
        for a in range(n):
            copy(a, 0, sibling, me).wait_recv()
        for j, chip in enumerate(chips):
            for a in range(n):
                copy(a, 4 + j, (*chip, 1 - c), me).wait_recv()
        for cp in first + passed:
            cp.wait_send()
        for cp in mine:
            cp.wait()

    return pl.pallas_call(
        body, name=name, in_specs=[HBM_SPEC] * n, out_specs=[HBM_SPEC] * n,
        out_shape=[jax.ShapeDtypeStruct((N_DEV, *s.shape), s.dtype) for s in shards],
        scratch_shapes=[pltpu.SemaphoreType.DMA((7 * n,)), pltpu.SemaphoreType.DMA((7 * n,)), pltpu.SemaphoreType.DMA((n,))],
    )(*shards)


def _slot(ref, p):
    return ref.at[4 * p[0] + 2 * p[1] + p[2]]


def _ag_first(shards):
    n = len(shards)

    def plan(ins, outs):
        x, y, c, chips = _place()
        me = (x, y, c)
        remote, local = [], []
        for a in range(n):
            local.append((ins[a], _slot(outs[a], me)))
            for peer in [(x, y, 1 - c)] + [(*chip, c) for chip in chips]:
                remote.append((ins[a], _slot(outs[a], me), _slot(outs[a], peer), peer))
        return remote, local

    return _Exchange(shards, [jax.ShapeDtypeStruct((N_DEV, *s.shape), s.dtype) for s in shards], plan, 4 * n, n)


def _ag_second(partial):
    n = len(partial)

    def plan(ins, outs):
        x, y, c, chips = _place()
        remote = []
        for a in range(n):
            for chip in chips:
                remote.append((_slot(ins[a], (*chip, c)), _slot(outs[a], (*chip, c)), _slot(outs[a], (*chip, 1 - c)), (x, y, 1 - c)))
        return remote, []

    return _Exchange(partial, [jax.ShapeDtypeStruct(p.shape, p.dtype) for p in partial], plan, 3 * n, 0,
                     aliases={a: a for a in range(n)})


def _rs_first(halves):
    n = len(halves)

    def plan(ins, outs):
        x, y, c, _ = _place()
        return [(ins[a].at[:, 1 - c], outs[a], outs[a], (x, y, 1 - c)) for a in range(n)], []

    return _Exchange(halves, [jax.ShapeDtypeStruct((h.shape[0], *h.shape[2:]), h.dtype) for h in halves], plan, n, 0)


def _rs_second(sums):
    n = len(sums)

    def plan(ins, outs):
        x, y, c, chips = _place()
        k_me = 2 * x + y
        remote = []
        for a in range(n):
            for chip in chips:
                k_peer = 2 * chip[0] + chip[1]
                remote.append((ins[a].at[k_peer], outs[a].at[k_me], outs[a].at[k_peer], (*chip, c)))
        return remote, [(ins[a].at[k_me], outs[a].at[k_me]) for a in range(n)]

    return _Exchange(sums, [jax.ShapeDtypeStruct(s.shape, s.dtype) for s in sums], plan, 3 * n, n)


def _pair_add(name, own, got, core):
    nch, _, K, N = own.shape
    tr = _row_tile(K, N)

    def body(c_ref, own_ref, got_ref, o_ref):
        o_ref[...] = (own_ref[...].astype(F32) + got_ref[...].astype(F32)).astype(o_ref.dtype)

    grid_spec = pltpu.PrefetchScalarGridSpec(
        num_scalar_prefetch=1, grid=(nch, K // tr),
        in_specs=[pl.BlockSpec((None, None, tr, N), lambda k, r, c_ref: (k, c_ref[0], r, 0)),
                  pl.BlockSpec((None, tr, N), lambda k, r, c_ref: (k, r, 0))],
        out_specs=pl.BlockSpec((None, tr, N), lambda k, r, c_ref: (k, r, 0)))
    return pl.pallas_call(body, name=name, grid_spec=grid_spec, out_shape=jax.ShapeDtypeStruct((nch, K, N), own.dtype),
                          compiler_params=_cp(2))(core, own, got)


def _row_tile(K, N):
    limit = max(16, 262144 // N)
    t = 1
    while t * 2 <= limit and K % (t * 2) == 0:
        t *= 2
    return t if t >= 16 else K


def _adamw(name, parts, w, m, v):
    P, K, N = parts.shape
    tr = _row_tile(K, N)

    def body(p_ref, w_ref, m_ref, v_ref, g_ref, d_ref, nm_ref, nv_ref):
        g = p_ref[0].astype(F32)
        for i in range(1, P):
            g = g + p_ref[i].astype(F32)
        m_new = ADAM_B1 * m_ref[...] + (1.0 - ADAM_B1) * g
        v_new = ADAM_B2 * v_ref[...] + (1.0 - ADAM_B2) * (g * g)
        m_hat = m_new / (1.0 - ADAM_B1 ** ADAM_STEP)
        v_hat = v_new / (1.0 - ADAM_B2 ** ADAM_STEP)
        g_ref[...] = g
        d_ref[...] = -ADAM_LR * (m_hat / (jnp.sqrt(v_hat) + ADAM_EPS) + ADAM_WD * w_ref[...])
        nm_ref[...] = m_new
        nv_ref[...] = v_new

    row = pl.BlockSpec((tr, N), lambda r: (r, 0))
    sh = jax.ShapeDtypeStruct((K, N), F32)
    return pl.pallas_call(body, name=name, grid=(K // tr,),
                          in_specs=[pl.BlockSpec((P, tr, N), lambda r: (0, r, 0)), row, row, row],
                          out_specs=[row] * 4, out_shape=[sh] * 4, compiler_params=_cp(1))(parts, w, m, v)


def _merge_fwd(o_mla, wbm, o_dil, wbd, g0, g1, tm=512, tn=512, comm=None):
    T, K1 = o_mla.shape
    K2, D = o_dil.shape[1], wbm.shape[1]
    tm, tn = _tile(T, tm), _tile(D, tn)

    def epilogue(accs, in_refs, out_refs):
        a, b = accs
        out_refs[0][...] = a.astype(BF16)
        out_refs[1][...] = b.astype(BF16)
        out_refs[2][...] = (in_refs[4][...].astype(F32) * a + in_refs[5][...].astype(F32) * b).astype(BF16)

    mn = pl.BlockSpec((tm, tn), lambda i, j: (i, j))
    sh = jax.ShapeDtypeStruct((T, D), BF16)
    return _mm("merge_fwd", (T // tm, D // tn), [o_mla, wbm, o_dil, wbd, g0, g1],
               [pl.BlockSpec((tm, K1), lambda i, j: (i, 0)), pl.BlockSpec((K1, tn), lambda i, j: (0, j)),
                pl.BlockSpec((tm, K2), lambda i, j: (i, 0)), pl.BlockSpec((K2, tn), lambda i, j: (0, j)), mn, mn],
               [(0, 1, "nn", 0), (2, 3, "nn", 1)], 2, None, epilogue, [sh, sh, sh], [mn, mn, mn], None, comm=comm)


def _ple_loss(n4, wpg, pe, wpp, x3, tgt, tm=512, tn=512):
    T, D = x3.shape
    Kp = pe.shape[1]
    tm, tn = _tile(T, tm), _tile(D, tn)

    def epilogue(accs, in_refs, out_refs):
        z, proj = accs
        pg = _sigmoid(z)
        err = in_refs[4][...] + pg * proj - in_refs[5][...]
        dy = err * (1.0 / D)
        out_refs[0][...] = dy
        out_refs[1][...] = (dy * proj * pg * (1.0 - pg)).astype(BF16)
        out_refs[2][...] = (dy * pg).astype(BF16)

        @pl.when(pl.program_id(1) == 0)
        def _():
            out_refs[3][...] = jnp.zeros_like(out_refs[3])

        out_refs[3][...] += jnp.sum(err * err, axis=-1, keepdims=True)

    mn = pl.BlockSpec((tm, tn), lambda i, j: (i, j))
    return _mm("ple_loss", (T // tm, D // tn), [n4, wpg, pe, wpp, x3, tgt],
               [pl.BlockSpec((tm, D), lambda i, j: (i, 0)), pl.BlockSpec((D, tn), lambda i, j: (0, j)),
                pl.BlockSpec((tm, Kp), lambda i, j: (i, 0)), pl.BlockSpec((Kp, tn), lambda i, j: (0, j)), mn, mn],
               [(0, 1, "nn", 0), (2, 3, "nn", 1)], 2, None, epilogue,
               [jax.ShapeDtypeStruct((T, D), F32), jax.ShapeDtypeStruct((T, D), BF16), jax.ShapeDtypeStruct((T, D), BF16),
                jax.ShapeDtypeStruct((T, 1), F32)],
               [mn, mn, mn, pl.BlockSpec((tm, 1), lambda i, j: (i, 0))], None)


def _epi_sigmoid(acc, ex, outs):
    outs[0][...] = _sigmoid(acc).astype(outs[0].dtype)


def _epi_add(acc, ex, outs):
    outs[0][...] = (acc + ex[0][...].astype(F32)).astype(outs[0].dtype)


def _epi_dmerge(acc, ex, outs):
    mp, dp, g0, g1 = [e[...].astype(F32) for e in ex]
    outs[0][...] = (acc * g0).astype(BF16)
    outs[1][...] = (acc * g1).astype(BF16)
    outs[2][...] = (acc * mp * g0 * (1.0 - g0)).astype(BF16)
    outs[3][...] = (acc * dp * g1 * (1.0 - g1)).astype(BF16)


_WEIGHTS = ("g_ffn1", "w1_gate", "w1_up", "w1_down", "g_mix", "w_in", "g_cq", "w_uq", "g_ckv", "w_ukv", "g_q_mla", "g_k_mla",
            "g_q_dil", "g_k_dil", "w_br_mla", "w_br_dil", "w_o", "g_ffn2", "w2_gate", "w2_up", "w2_down", "g_ple",
            "w_ple_gate", "w_ple_proj")
_MATRICES = ("w1_gate", "w1_up", "w1_down", "w_in", "w_uq", "w_ukv", "w_br_mla", "w_br_dil", "w_o", "w2_gate", "w2_up",
             "w2_down", "w_ple_gate", "w_ple_proj")
_GAINS = tuple(n for n in _WEIGHTS if n not in _MATRICES)
MLA_BLOCK = 1024
MLA_SUB = 256
DIL_ROWS = 2048
_FIRST = ("w1_gate", "w1_up", "w1_down")
_MID = ("w_in", "w_uq", "w_ukv", "w_br_mla", "w_br_dil", "w_o", "w_ple_gate", "w_ple_proj")
_LATE = ("w2_gate", "w2_up", "w2_down")
_RS_GROUPS = (("w_ple_proj", "w_ple_gate", "w2_gate", "w2_up", "w2_down"),
              ("w_o", "w_br_mla", "w_br_dil", "w_uq", "w_ukv", "w_in"),
              ("w1_gate", "w1_up", "w1_down"))


def _cols(g3):
    nd, K, n = g3.shape
    return g3.transpose(1, 0, 2).reshape(K, nd * n)


def _uncols(m):
    K, n = m.shape
    return m.reshape(K, N_DEV, n // N_DEV).transpose(1, 0, 2)


def _rows(g3):
    nd, k, N = g3.shape
    return g3.reshape(nd * k, N)


def _unrows(m):
    K, N = m.shape
    return m.reshape(N_DEV, K // N_DEV, N)


def _pack_gains(vals):
    flat = jnp.concatenate([vals[n].reshape(-1) for n in _GAINS])
    pad = (-flat.shape[0]) % 2048
    return jnp.pad(flat, (0, pad)).reshape(-1, 128)


def _unpack_gains(packed, like):
    flat = packed.reshape(-1)
    out, off = {}, 0
    for n in _GAINS:
        size = int(np.prod(like[n].shape))
        out[n] = flat[off:off + size].reshape(like[n].shape)
        off += size
    return out


def _train_step(x, p, positions, loss_target, W, M, V):
    T, D = x.shape[1], x.shape[2]
    xs, tgt, pe = x[0], loss_target[0], p[0, 0]
    pos_col = positions.reshape(T, 1).astype(F32)
    w = {n: (a[0] if n in _MATRICES else a.reshape(1, -1)) for n, a in W.items()}

    shard = {n: w[n].astype(BF16) for n in _MATRICES}
    gathered = dict(zip(_FIRST, _allgather("ag_first", [shard[n] for n in _FIRST])))
    ag_mid = _ag_first([shard[n] for n in _MID])
    ag_mid2 = []

    def pass_on_mid():
        ag_mid2.append(_ag_second(ag_mid.results))
        return ag_mid2[0]

    x1, ffn1 = _ffn_forward("ffn1", xs, w["g_ffn1"], gathered["w1_gate"], gathered["w1_up"], gathered["w1_down"],
                            comm_up=ag_mid, comm_down=pass_on_mid)
    gathered.update(zip(_MID, ag_mid2[0].results))
    nq_l, nkv_l = w["g_cq"].shape[-1], w["g_ckv"].shape[-1]
    H = w["w_uq"].shape[1] * N_DEV // MLA_QK
    G = len(DIL_GROUPS)
    off_kr = nq_l + nkv_l
    off_dil = off_kr + MLA_ROPE
    off_gate = off_dil + G * 3 * H * HEAD
    kr_block = off_kr // 128
    w_in = _cols(gathered["w_in"])
    wa = jnp.pad(w_in[:, :off_dil], ((0, 0), (0, 128 - MLA_ROPE)))
    wdil, wg0, wg1 = w_in[:, off_dil:off_gate], w_in[:, off_gate:off_gate + D], w_in[:, off_gate + D:]
    wuq = jnp.pad(_cols(gathered["w_uq"]).reshape(nq_l, H, MLA_QK), ((0, 0), (0, 0), (0, MLA_PAD - MLA_QK))).reshape(nq_l, H * MLA_PAD)
    wukv = _cols(gathered["w_ukv"])
    wbm, wbd, wpp = _cols(gathered["w_br_mla"]), _cols(gathered["w_br_dil"]), _cols(gathered["w_ple_proj"])
    wo, wpg = _rows(gathered["w_o"]), _rows(gathered["w_ple_gate"])
    gq_mla = jnp.pad(w["g_q_mla"], ((0, 0), (0, MLA_PAD - MLA_QK)))
    gk_mla = jnp.pad(w["g_k_mla"], ((0, 0), (0, MLA_PAD - MLA_QK)))
    gq_dil, gk_dil = w["g_q_dil"].reshape(G, 1, HEAD), w["g_k_dil"].reshape(G, 1, HEAD)

    half_m, half_d = MLA_ROPE // 2, DIL_ROT // 2
    inv_m = ROPE_THETA ** (-jnp.arange(half_m, dtype=F32) * 2.0 / MLA_ROPE)
    inv_d = ROPE_THETA ** (-jnp.arange(half_d, dtype=F32) * 2.0 / DIL_ROT)
    inv_m = jnp.tile(inv_m, 128 // half_m).reshape(1, 128)
    inv_d = jnp.tile(inv_d, 128 // half_d).reshape(1, 128)
    tabs = _rope_tables(pos_col, inv_m, inv_d)
    tabs_m, tabs_d = tabs[:3], tabs[3:]

    mla_blk, dil_qb = _tile(T, MLA_BLOCK), _tile(T, DIL_ROWS)
    gw = 3 * H * HEAD
    dils = [d for _, d in DIL_GROUPS]
    units = [win // d for win, d in DIL_GROUPS]
    wdil_g = [wdil[:, g * gw:(g + 1) * gw] for g in range(G)]
    tabs_g = [[_to_phase(t, d) for t in tabs_d] for d in dils]

    h = _rmsnorm_fwd("mix_norm", x1, w["g_mix"])
    pa = _mm2d("proj_a", h, wa, "nn", 512, 1024, 4096)
    pd = [_to_phase(_mm2d("proj_dil%d" % g, h, wdil_g[g], "nn", 1024, 1024, 4096, out_dtypes=(BF16,)), dils[g]) for g in range(G)]
    g0 = _mm2d("proj_gate0", h, wg0, "nn", 1024, 1024, 4096, out_dtypes=(BF16,), epilogue=_epi_sigmoid)
    g1 = _mm2d("proj_gate1", h, wg1, "nn", 1024, 1024, 4096, out_dtypes=(BF16,), epilogue=_epi_sigmoid)
    cq, ckv = _lora_fwd(pa, w["g_cq"], w["g_ckv"], nq_l, nkv_l)
    q_raw = _mm2d("q_up", cq, wuq, "nn", 512, 2048, 4096)
    kv = _mm2d("kv_up", ckv, wukv, "nn", 512, 2048, 4096)
    q_att, k_att, v_mla = _mla_prep_fwd(q_raw, kv, pa, tabs_m, gq_mla, gk_mla, H, kr_block)
    ag_late = _ag_first([shard[n] for n in _LATE])
    o_mla, lse_mla = _mla_fwd(q_att, k_att, v_mla, H, mla_blk, comm=ag_late)
    qd, kd, o_g, lse_g = [], [], [], []
    for g in range(G):
        qg, kg = _dil_prep_fwd_pm("dil_prep_fwd%d" % g, pd[g], tabs_g[g], gq_dil[g], gk_dil[g], H)
        og, lg = _win_fwd("dil_fwd%d" % g, qg, kg, pd[g], H, units[g], T // dils[g], dil_qb)
        qd.append(qg)
        kd.append(kg)
        o_g.append(_from_phase(og, dils[g]))
        lse_g.append(_from_phase(lg, dils[g], axis=1))
    o_dil, lse_dil = _merge_groups(o_g, lse_g, H)
    ag_late2 = _ag_second(ag_late.results)
    mla_p, dil_p, merged = _merge_fwd(o_mla, wbm, o_dil, wbd, g0, g1, comm=ag_late2)
    gathered.update(zip(_LATE, ag_late2.results))
    x2 = _mm2d("out_proj", merged, wo, "nn", 512, 1024, 4096, epilogue=_epi_add, extras=(x1,))
    x3, ffn2 = _ffn_forward("ffn2", x2, w["g_ffn2"], gathered["w2_gate"], gathered["w2_up"], gathered["w2_down"])
    n4 = _rmsnorm_fwd("ple_norm", x3, w["g_ple"])
    dy, dz, dproj, loss_rows = _ple_loss(n4, wpg, pe, wpp, x3, tgt)
    loss = lax.psum((0.5 / D) * jnp.sum(loss_rows), ("x", "y", "c"))

    dW, dG = {}, {}
    dW["w_ple_proj"] = _uncols(_mm2d("d_wpp", pe, dproj, "tn", 1024, 2048, 2048, out_dtypes=(BF16,)))
    dW["w_ple_gate"] = _unrows(_mm2d("d_wpg", n4, dz, "tn", 1024, 1024, 2048, out_dtypes=(BF16,)))
    dn4 = _mm2d("d_n4", dz, wpg, "nt", 512, 1024, 4096)
    dx3, dx3b, dG["g_ple"] = _rmsnorm_bwd("ple_norm_bwd", dn4, x3, w["g_ple"], dy)

    core = lax.axis_index("c").astype(jnp.int32).reshape(1)
    grads, deltas, new_m, new_v = {}, {}, {}, {}

    def sibling_exchange(names):
        return _rs_first([dW[n].reshape(4, 2, *dW[n].shape[1:]) for n in names])

    def chip_exchange(names, first):
        return _rs_second([_pair_add("rs_add_" + n, own, rec, core) for n, own, rec in zip(names, first.ins, first.results)])

    def update(names, second):
        for n, parts in zip(names, second.results):
            grads[n], deltas[n], new_m[n], new_v[n] = [a[None] for a in _adamw("adamw_" + n, parts, w[n], M[n][0], V[n][0])]

    rs_a = []

    def ffn2_grads_done(dws):
        dW["w2_gate"], dW["w2_up"], dW["w2_down"] = dws
        rs_a.append(sibling_exchange(_RS_GROUPS[0]))
        return rs_a[0]

    dx2, dx2b, dG["g_ffn2"], _, _, _ = _ffn_backward(
        "ffn2", dx3, dx3b, x2, w["g_ffn2"], gathered["w2_gate"], gathered["w2_up"], gathered["w2_down"], ffn2,
        comm_dn=ffn2_grads_done)
    rs_a2 = chip_exchange(_RS_GROUPS[0], rs_a[0])

    d_mla_p, d_dil_p, dpg0, dpg1 = _mm2d("d_merged", dx2b, wo, "nt", 512, 1024, 4096, out_dtypes=(BF16,) * 4,
                                         epilogue=_epi_dmerge, extras=(mla_p, dil_p, g0, g1))
    dW["w_o"] = _unrows(_mm2d("d_wo", merged, dx2b, "tn", 1024, 1024, 2048, out_dtypes=(BF16,)))
    dW["w_br_mla"] = _uncols(_mm2d("d_wbm", o_mla, d_mla_p, "tn", 1024, 1024, 2048, out_dtypes=(BF16,)))
    dW["w_br_dil"] = _uncols(_mm2d("d_wbd", o_dil, d_dil_p, "tn", 1024, 1024, 2048, out_dtypes=(BF16,)))
    do_mla = _mm2d("d_o_mla", d_mla_p, wbm, "nt", 512, 1024, 4096, out_dtypes=(BF16,))
    do_dil = _mm2d("d_o_dil", d_dil_p, wbd, "nt", 512, 1024, 4096, out_dtypes=(BF16,))

    dl_mla = _row_dot("mla_delta", do_mla, o_mla, H)
    dq_att = _mla_dq(q_att, k_att, v_mla, do_mla, lse_mla, dl_mla, H, mla_blk, comm=rs_a2)
    update(_RS_GROUPS[0], rs_a2)
    dk_att, dv_mla = _mla_dkv(q_att, k_att, v_mla, do_mla, lse_mla, dl_mla, H, mla_blk)
    dq_raw, dkv, dkr, dgq, dgk = _mla_prep_bwd(dq_att, dk_att, dv_mla, q_raw, kv, pa, tabs_m, gq_mla, gk_mla, H, kr_block)
    dG["g_q_mla"], dG["g_k_mla"] = dgq[:, :MLA_QK], dgk[:, :MLA_QK]
    d_wuq = _mm2d("d_wuq", cq, dq_raw, "tn", 512, 2048, 2048, out_dtypes=(BF16,))
    dW["w_uq"] = _uncols(d_wuq.reshape(nq_l, H, MLA_PAD)[:, :, :MLA_QK].reshape(nq_l, H * MLA_QK))
    dW["w_ukv"] = _uncols(_mm2d("d_wukv", ckv, dkv, "tn", 512, 2048, 2048, out_dtypes=(BF16,)))
    dcq = _mm2d("d_cq", dq_raw, wuq, "nt", 512, 1024, 4096)
    dckv = _mm2d("d_ckv", dkv, wukv, "nt", 512, 1024, 4096)
    dpa, dG["g_cq"], dG["g_ckv"] = _lora_bwd(dcq, dckv, dkr, pa, w["g_cq"], w["g_ckv"])

    dl_dil = _row_dot("dil_delta", do_dil, o_dil, H)
    dpd, dgqd, dgkd = [], [], []
    for g in range(G):
        d, n, L = dils[g], units[g], T // dils[g]
        do_g, lse_pg, dl_pg = _to_phase(do_dil, d), _to_phase(lse_dil, d, axis=1), _to_phase(dl_dil, d, axis=1)
        dq_g = _win_dq("dil_dq%d" % g, qd[g], kd[g], pd[g], do_g, lse_pg, dl_pg, H, n, L, dil_qb)
        dk_g, dv_g = _win_dkv("dil_dkv%d" % g, qd[g], kd[g], pd[g], do_g, lse_pg, dl_pg, H, n, L, dil_qb)
        dp_g, dgq_g, dgk_g = _dil_prep_bwd_pm("dil_prep_bwd%d" % g, dq_g, dk_g, dv_g, pd[g], tabs_g[g], gq_dil[g], gk_dil[g], H)
        dpd.append(_from_phase(dp_g, d))
        dgqd.append(dgq_g)
        dgkd.append(dgk_g)
    dG["g_q_dil"], dG["g_k_dil"] = jnp.concatenate(dgqd).reshape(1, G, HEAD), jnp.concatenate(dgkd).reshape(1, G, HEAD)

    d_wa = _mm2d("d_wa", h, dpa, "tn", 1024, 1024, 2048, out_dtypes=(BF16,))
    d_wdil = [_mm2d("d_wdil%d" % g, h, dpd[g], "tn", 1024, 1024, 2048, out_dtypes=(BF16,)) for g in range(G)]
    d_wg0 = _mm2d("d_wg0", h, dpg0, "tn", 1024, 1024, 2048, out_dtypes=(BF16,))
    d_wg1 = _mm2d("d_wg1", h, dpg1, "tn", 1024, 1024, 2048, out_dtypes=(BF16,))
    dW["w_in"] = _uncols(jnp.concatenate([d_wa[:, :off_dil], *d_wdil, d_wg0, d_wg1], axis=1))
    dh = _mm2d("d_h_a", dpa, wa, "nt", 512, 1024, 4096)
    for g in range(G):
        dh = _mm2d("d_h_dil%d" % g, dpd[g], wdil_g[g], "nt", 512, 1024, 4096, epilogue=_epi_add, extras=(dh,))
    dh = _mm2d("d_h_g0", dpg0, wg0, "nt", 512, 1024, 4096, epilogue=_epi_add, extras=(dh,))
    dh = _mm2d("d_h_g1", dpg1, wg1, "nt", 512, 1024, 4096, epilogue=_epi_add, extras=(dh,))
    dx1, dx1b, dG["g_mix"] = _rmsnorm_bwd("mix_norm_bwd", dh, x1, w["g_mix"], dx2)
    rs_b = sibling_exchange(_RS_GROUPS[1])
    rs_b2 = []

    def attention_grads_summed(_):
        rs_b2.append(chip_exchange(_RS_GROUPS[1], rs_b))
        return rs_b2[0]

    dx0, _, dG["g_ffn1"], dW["w1_gate"], dW["w1_up"], dW["w1_down"] = _ffn_backward(
        "ffn1", dx1, dx1b, xs, w["g_ffn1"], gathered["w1_gate"], gathered["w1_up"], gathered["w1_down"], ffn1,
        comm_act=rs_b, comm_dn=attention_grads_summed, weights_first=False)
    update(_RS_GROUPS[1], rs_b2[0])
    rs_c = sibling_exchange(_RS_GROUPS[2])
    _run_exchange("rs_first_w1", rs_c)
    rs_c2 = chip_exchange(_RS_GROUPS[2], rs_c)
    _run_exchange("rs_second_w1", rs_c2)
    update(_RS_GROUPS[2], rs_c2)

    parts = _allgather("ag_gain_grads", [_pack_gains(dG)])[0]
    packed = _adamw("adamw_gains", parts, _pack_gains(W), _pack_gains(M), _pack_gains(V))
    for out, pk in zip((grads, deltas, new_m, new_v), packed):
        out.update(_unpack_gains(pk, W))

    return (loss, dx0[None], *[grads[n] for n in _WEIGHTS], *[deltas[n] for n in _WEIGHTS],
            *[new_m[n] for n in _WEIGHTS], *[new_v[n] for n in _WEIGHTS])


def kernel(x, p, positions, g_ffn1, w1_gate, w1_up, w1_down, g_mix, w_in, g_cq, w_uq, g_ckv, w_ukv, g_q_mla, g_k_mla, g_q_dil, g_k_dil, w_br_mla, w_br_dil, w_o, g_ffn2, w2_gate, w2_up, w2_down, g_ple, w_ple_gate, w_ple_proj, loss_target, m_g_ffn1, m_w1_gate, m_w1_up, m_w1_down, m_g_mix, m_w_in, m_g_cq, m_w_uq, m_g_ckv, m_w_ukv, m_g_q_mla, m_g_k_mla, m_g_q_dil, m_g_k_dil, m_w_br_mla, m_w_br_dil, m_w_o, m_g_ffn2, m_w2_gate, m_w2_up, m_w2_down, m_g_ple, m_w_ple_gate, m_w_ple_proj, v_g_ffn1, v_w1_gate, v_w1_up, v_w1_down, v_g_mix, v_w_in, v_g_cq, v_w_uq, v_g_ckv, v_w_ukv, v_g_q_mla, v_g_k_mla, v_g_q_dil, v_g_k_dil, v_w_br_mla, v_w_br_dil, v_w_o, v_g_ffn2, v_w2_gate, v_w2_up, v_w2_down, v_g_ple, v_w_ple_gate, v_w_ple_proj):
    W = dict(zip(_WEIGHTS, (g_ffn1, w1_gate, w1_up, w1_down, g_mix, w_in, g_cq, w_uq, g_ckv, w_ukv, g_q_mla, g_k_mla, g_q_dil,
                            g_k_dil, w_br_mla, w_br_dil, w_o, g_ffn2, w2_gate, w2_up, w2_down, g_ple, w_ple_gate, w_ple_proj)))
    M = dict(zip(_WEIGHTS, (m_g_ffn1, m_w1_gate, m_w1_up, m_w1_down, m_g_mix, m_w_in, m_g_cq, m_w_uq, m_g_ckv, m_w_ukv, m_g_q_mla,
                            m_g_k_mla, m_g_q_dil, m_g_k_dil, m_w_br_mla, m_w_br_dil, m_w_o, m_g_ffn2, m_w2_gate, m_w2_up,
                            m_w2_down, m_g_ple, m_w_ple_gate, m_w_ple_proj)))
    V = dict(zip(_WEIGHTS, (v_g_ffn1, v_w1_gate, v_w1_up, v_w1_down, v_g_mix, v_w_in, v_g_cq, v_w_uq, v_g_ckv, v_w_ukv, v_g_q_mla,
                            v_g_k_mla, v_g_q_dil, v_g_k_dil, v_w_br_mla, v_w_br_dil, v_w_o, v_g_ffn2, v_w2_gate, v_w2_up,
                            v_w2_down, v_g_ple, v_w_ple_gate, v_w_ple_proj)))
    return _train_step(x, p, positions, loss_target, W, M, V)
```

```python
import numpy as np
import jax
import jax.numpy as jnp
from jax import lax
from jax.experimental import pallas as pl
from jax.experimental.pallas import tpu as pltpu

F32 = jnp.float32
BF16 = jnp.bfloat16

EPS = 1e-6
ROPE_THETA = 500000.0
MLA_NOPE = 128
MLA_ROPE = 64
MLA_QK = MLA_NOPE + MLA_ROPE
MLA_PAD = 256
HEAD = 128
DIL_ROT = 32
DIL_GROUPS = ((128, 1), (512, 4), (2048, 16))
NEG = -1e30
NO_WINDOW = 1 << 30
N_DEV = 8
ADAM_LR, ADAM_B1, ADAM_B2, ADAM_EPS, ADAM_WD, ADAM_STEP = 0.001, 0.9, 0.999, 1e-08, 0.01, 10
VMEM_LIMIT_V7X = 56 * 1024 * 1024
MESH = pl.DeviceIdType.MESH
HBM_SPEC = pl.BlockSpec(memory_space=pltpu.HBM)


def _cp(n_axes):
    return pltpu.CompilerParams(dimension_semantics=("arbitrary",) * n_axes,
                                vmem_limit_bytes=VMEM_LIMIT_V7X)


def _tile(n, t):
    return t if (n >= t and n % t == 0) else n


def _sigmoid(x):
    return 1.0 / (1.0 + jnp.exp(-x))


_DIMS = {"nn": (((1,), (0,)), ((), ())), "nt": (((1,), (1,)), ((), ())), "tn": (((0,), (0,)), ((), ()))}


def _dot(a, b, mode):
    return lax.dot_general(a.astype(BF16), b.astype(BF16), _DIMS[mode], preferred_element_type=F32)


class _Exchange:
    def __init__(self, ins, out_shapes, plan, n_remote, n_local, aliases=None):
        self.ins, self.out_shapes, self.plan = list(ins), list(out_shapes), plan
        self.n_remote, self.n_local, self.aliases = n_remote, n_local, dict(aliases or {})
        self.results = None

    def sems(self):
        return [pltpu.SemaphoreType.DMA((self.n_remote,)), pltpu.SemaphoreType.DMA((self.n_remote,)),
                pltpu.SemaphoreType.DMA((max(self.n_local, 1),))]

    def start(self, in_refs, out_refs, sems):
        remote, local = self.plan(in_refs, out_refs)
        for i, (src, dst_there, _, peer) in enumerate(remote):
            pltpu.make_async_remote_copy(src_ref=src, dst_ref=dst_there, send_sem=sems[0].at[i], recv_sem=sems[1].at[i],
                                         device_id=peer, device_id_type=MESH).start()
        for i, (src, dst) in enumerate(local):
            pltpu.make_async_copy(src, dst, sems[2].at[i]).start()

    def finish(self, in_refs, out_refs, sems):
        remote, local = self.plan(in_refs, out_refs)
        for i, (src, _, dst_here, peer) in enumerate(remote):
            pltpu.make_async_remote_copy(src_ref=src, dst_ref=dst_here, send_sem=sems[0].at[i], recv_sem=sems[1].at[i],
                                         device_id=peer, device_id_type=MESH).wait_recv()
        for i, (src, dst_there, _, peer) in enumerate(remote):
            pltpu.make_async_remote_copy(src_ref=src, dst_ref=dst_there, send_sem=sems[0].at[i], recv_sem=sems[1].at[i],
                                         device_id=peer, device_id_type=MESH).wait_send()
        for i, (src, dst) in enumerate(local):
            pltpu.make_async_copy(src, dst, sems[2].at[i]).wait()


def _run_exchange(name, ex):
    ci = len(ex.ins)

    def body(*refs):
        ins, outs, sems = refs[:ci], refs[ci:ci + len(ex.out_shapes)], refs[ci + len(ex.out_shapes):]
        ex.start(ins, outs, sems)
        ex.finish(ins, outs, sems)

    ex.results = pl.pallas_call(body, name=name, in_specs=[HBM_SPEC] * ci, out_specs=[HBM_SPEC] * len(ex.out_shapes),
                                out_shape=ex.out_shapes, scratch_shapes=ex.sems(), input_output_aliases=ex.aliases)(*ex.ins)
    return ex.results


def _call(body, *, name, grid, in_specs, out_specs, out_shape, operands, scratch=(), comm=None):
    multi = isinstance(out_shape, (list, tuple))
    outs = list(out_shape) if multi else [out_shape]
    ospecs = list(out_specs) if multi else [out_specs]
    if comm is None:
        res = pl.pallas_call(body, name=name, grid=grid, in_specs=list(in_specs), out_specs=ospecs, out_shape=outs,
                             scratch_shapes=list(scratch), compiler_params=_cp(len(grid)))(*operands)
        return res if multi else res[0]
    n_in, n_out, n_scr = len(in_specs), len(outs), len(scratch)
    ci, co = len(comm.ins), len(comm.out_shapes)

    def hosted(*refs):
        bounds = np.cumsum([0, n_in, ci, n_out, co, n_scr])
        ins, cins, os_, cos, scr = (refs[bounds[i]:bounds[i + 1]] for i in range(5))
        sems = refs[bounds[5]:]
        ids = [pl.program_id(a) for a in range(len(grid))]
        first, last = ids[0] == 0, ids[0] == grid[0] - 1
        for a in range(1, len(grid)):
            first, last = first & (ids[a] == 0), last & (ids[a] == grid[a] - 1)

        @pl.when(first)
        def _():
            comm.start(cins, cos, sems)

        body(*ins, *os_, *scr)

        @pl.when(last)
        def _():
            comm.finish(cins, cos, sems)

    res = pl.pallas_call(hosted, name=name, grid=grid, in_specs=[*in_specs, *[HBM_SPEC] * ci],
                         out_specs=[*ospecs, *[HBM_SPEC] * co], out_shape=[*outs, *comm.out_shapes],
                         scratch_shapes=[*scratch, *comm.sems()],
                         input_output_aliases={n_in + i: n_out + o for i, o in comm.aliases.items()},
                         compiler_params=_cp(len(grid)))(*operands, *comm.ins)
    comm.results = res[n_out:]
    return res[:n_out] if multi else res[0]


def _mm(name, grid, ins, in_specs, pairs, n_acc, acc_shape, epilogue, out_shapes, out_specs, k_axis, comm=None):
    n_in, n_out = len(ins), len(out_shapes)
    nk = grid[k_axis] if k_axis is not None else 1

    def body(*refs):
        in_refs, out_refs, acc_refs = refs[:n_in], refs[n_in:n_in + n_out], refs[n_in + n_out:]
        parts = [None] * n_acc
        for ai, bi, mode, ci in pairs:
            d = _dot(in_refs[ai][...], in_refs[bi][...], mode)
            parts[ci] = d if parts[ci] is None else parts[ci] + d
        if nk == 1:
            epilogue(parts, in_refs, out_refs)
            return
        k = pl.program_id(k_axis)

        @pl.when(k == 0)
        def _():
            for c in range(n_acc):
                acc_refs[c][...] = parts[c]

        @pl.when(k > 0)
        def _():
            for c in range(n_acc):
                acc_refs[c][...] += parts[c]

        @pl.when(k == nk - 1)
        def _():
            epilogue([r[...] for r in acc_refs], in_refs, out_refs)

    scratch = [pltpu.VMEM(acc_shape, F32) for _ in range(n_acc)] if nk > 1 else []
    return _call(body, name=name, grid=grid, in_specs=in_specs, out_specs=list(out_specs), out_shape=list(out_shapes),
                 operands=ins, scratch=scratch, comm=comm)


def _mm2d(name, a, b, mode, tm, tn, tk, out_dtypes=(F32,), epilogue=None, extras=()):
    if mode == "nn":
        (M, K), N = a.shape, b.shape[1]
    elif mode == "nt":
        (M, K), N = a.shape, b.shape[0]
    else:
        (K, M), N = a.shape, b.shape[1]
    tm, tn, tk = _tile(M, tm), _tile(N, tn), _tile(K, tk)
    a_spec = pl.BlockSpec((tk, tm), lambda i, j, k: (k, i)) if mode == "tn" else pl.BlockSpec((tm, tk), lambda i, j, k: (i, k))
    b_spec = pl.BlockSpec((tn, tk), lambda i, j, k: (j, k)) if mode == "nt" else pl.BlockSpec((tk, tn), lambda i, j, k: (k, j))
    mn_spec = pl.BlockSpec((tm, tn), lambda i, j, k: (i, j))
    n_ex = len(extras)

    def default_epilogue(acc, ex_refs, out_refs):
        out_refs[0][...] = acc.astype(out_refs[0].dtype)

    epi = epilogue or default_epilogue

    def wrapped(accs, in_refs, out_refs):
        epi(accs[0], in_refs[2:2 + n_ex], out_refs)

    outs = _mm(name, (M // tm, N // tn, K // tk), [a, b, *extras], [a_spec, b_spec] + [mn_spec] * n_ex,
               [(0, 1, mode, 0)], 1, (tm, tn), wrapped,
               [jax.ShapeDtypeStruct((M, N), dt) for dt in out_dtypes], [mn_spec] * len(out_dtypes), 2)
    return outs[0] if len(out_dtypes) == 1 else outs


def _rms_stats(x, n):
    return lax.rsqrt(jnp.sum(x * x, axis=-1, keepdims=True) * (1.0 / n) + EPS)


def _rmsnorm_fwd(name, x, g, tm=512):
    T, D = x.shape
    tm = _tile(T, tm)

    def body(x_ref, g_ref, o_ref):
        xv = x_ref[...]
        o_ref[...] = (xv * _rms_stats(xv, D) * g_ref[...]).astype(BF16)

    return pl.pallas_call(body, name=name, grid=(T // tm,),
                          in_specs=[pl.BlockSpec((tm, D), lambda i: (i, 0)), pl.BlockSpec((1, D), lambda i: (0, 0))],
                          out_specs=pl.BlockSpec((tm, D), lambda i: (i, 0)),
                          out_shape=jax.ShapeDtypeStruct((T, D), BF16), compiler_params=_cp(1))(x, g)


def _rms_bwd_rows(dy, x, g, n):
    r = _rms_stats(x, n)
    xh = x * r
    gd = dy * g
    mean = jnp.sum(gd * xh, axis=-1, keepdims=True) * (1.0 / n)
    return r * (gd - xh * mean), dy * xh


def _rmsnorm_bwd(name, dn, x, g, res, tm=256):
    T, D = x.shape
    tm = _tile(T, tm)

    def body(dn_ref, x_ref, g_ref, res_ref, dx_ref, dxb_ref, dg_ref):
        dx, dgp = _rms_bwd_rows(dn_ref[...].astype(F32), x_ref[...], g_ref[...], D)
        dx = dx + res_ref[...]
        dx_ref[...] = dx
        dxb_ref[...] = dx.astype(BF16)

        @pl.when(pl.program_id(0) == 0)
        def _():
            dg_ref[...] = jnp.zeros_like(dg_ref)

        dg_ref[...] += jnp.sum(dgp, axis=0, keepdims=True)

    row = pl.BlockSpec((tm, D), lambda i: (i, 0))
    vec = pl.BlockSpec((1, D), lambda i: (0, 0))
    return pl.pallas_call(body, name=name, grid=(T // tm,), in_specs=[row, row, vec, row],
                          out_specs=[row, row, vec],
                          out_shape=[jax.ShapeDtypeStruct((T, D), F32), jax.ShapeDtypeStruct((T, D), BF16),
                                     jax.ShapeDtypeStruct((1, D), F32)],
                          compiler_params=_cp(1))(dn, x, g, res)


def _rope_tables(pos_col, inv_mla, inv_dil, tm=512):
    T = pos_col.shape[0]
    tm = _tile(T, tm)

    def body(p_ref, im_ref, id_ref, cm, sam, sbm, cd, sad, sbd):
        lane = lax.broadcasted_iota(jnp.int32, (tm, 128), 1)
        p = p_ref[...]
        am = p * im_ref[...]
        c, s = jnp.cos(am), jnp.sin(am)
        cm[...] = jnp.where(lane < 64, c, 0.0)
        sam[...] = jnp.where(lane < 32, -s, 0.0)
        sbm[...] = jnp.where((lane >= 32) & (lane < 64), s, 0.0)
        ad = p * id_ref[...]
        c, s = jnp.cos(ad), jnp.sin(ad)
        cd[...] = jnp.where(lane < 32, c, 1.0)
        sad[...] = jnp.where(lane < 16, -s, 0.0)
        sbd[...] = jnp.where((lane >= 16) & (lane < 32), s, 0.0)

    row = pl.BlockSpec((tm, 128), lambda i: (i, 0))
    vec = pl.BlockSpec((1, 128), lambda i: (0, 0))
    return pl.pallas_call(body, name="rope_tables", grid=(T // tm,),
                          in_specs=[pl.BlockSpec((tm, 1), lambda i: (i, 0)), vec, vec], out_specs=[row] * 6,
                          out_shape=[jax.ShapeDtypeStruct((T, 128), F32)] * 6, compiler_params=_cp(1))(pos_col, inv_mla, inv_dil)


def _rope(v, c, sa, sb, sh):
    return v * c + pltpu.roll(v, 128 - sh, 1) * sa + pltpu.roll(v, sh, 1) * sb


def _rope_t(d, c, sa, sb, sh):
    return d * c + pltpu.roll(d * sa, sh, 1) + pltpu.roll(d * sb, 128 - sh, 1)


def _lora_fwd(pa, g_cq, g_ckv, nq, nkv, tm=512):
    T, W = pa.shape
    tm = _tile(T, tm)

    def body(pa_ref, gq_ref, gk_ref, cq_ref, ckv_ref):
        a = pa_ref[:, :nq]
        cq_ref[...] = (a * _rms_stats(a, nq) * gq_ref[...]).astype(BF16)
        b = pa_ref[:, nq:nq + nkv]
        ckv_ref[...] = (b * _rms_stats(b, nkv) * gk_ref[...]).astype(BF16)

    return pl.pallas_call(body, name="lora_fwd", grid=(T // tm,),
                          in_specs=[pl.BlockSpec((tm, W), lambda i: (i, 0)), pl.BlockSpec((1, nq), lambda i: (0, 0)),
                                    pl.BlockSpec((1, nkv), lambda i: (0, 0))],
                          out_specs=[pl.BlockSpec((tm, nq), lambda i: (i, 0)), pl.BlockSpec((tm, nkv), lambda i: (i, 0))],
                          out_shape=[jax.ShapeDtypeStruct((T, nq), BF16), jax.ShapeDtypeStruct((T, nkv), BF16)],
                          compiler_params=_cp(1))(pa, g_cq, g_ckv)


def _lora_bwd(dcq, dckv, dkr, pa, g_cq, g_ckv, tm=512):
    T, W = pa.shape
    nq, nkv = dcq.shape[1], dckv.shape[1]
    tm = _tile(T, tm)

    def body(dcq_ref, dckv_ref, dkr_ref, pa_ref, gq_ref, gk_ref, dpa_ref, dgq_ref, dgk_ref):
        dx, dgp = _rms_bwd_rows(dcq_ref[...], pa_ref[:, :nq], gq_ref[...], nq)
        dpa_ref[:, :nq] = dx.astype(BF16)
        dx2, dgp2 = _rms_bwd_rows(dckv_ref[...], pa_ref[:, nq:nq + nkv], gk_ref[...], nkv)
        dpa_ref[:, nq:nq + nkv] = dx2.astype(BF16)
        dpa_ref[:, nq + nkv:] = dkr_ref[...].astype(BF16)

        @pl.when(pl.program_id(0) == 0)
        def _():
            dgq_ref[...] = jnp.zeros_like(dgq_ref)
            dgk_ref[...] = jnp.zeros_like(dgk_ref)

        dgq_ref[...] += jnp.sum(dgp, axis=0, keepdims=True)
        dgk_ref[...] += jnp.sum(dgp2, axis=0, keepdims=True)

    def row(n):
        return pl.BlockSpec((tm, n), lambda i: (i, 0))

    def vec(n):
        return pl.BlockSpec((1, n), lambda i: (0, 0))

    return pl.pallas_call(body, name="lora_bwd", grid=(T // tm,),
                          in_specs=[row(nq), row(nkv), row(128), row(W), vec(nq), vec(nkv)],
                          out_specs=[row(W), vec(nq), vec(nkv)],
                          out_shape=[jax.ShapeDtypeStruct((T, W), BF16), jax.ShapeDtypeStruct((1, nq), F32),
                                     jax.ShapeDtypeStruct((1, nkv), F32)],
                          compiler_params=_cp(1))(dcq, dckv, dkr, pa, g_cq, g_ckv)


def _sumsq(v):
    return jnp.sum(v * v, axis=-1, keepdims=True)


def _mla_prep_fwd(q_raw, kv, pa, tabs, gq, gk, H, kr_block, tm=256):
    T = q_raw.shape[0]
    tm = _tile(T, tm)
    scale = MLA_QK ** -0.5
    P = MLA_PAD

    def body(q_ref, kv_ref, kr_ref, c_ref, sa_ref, sb_ref, gq_ref, gk_ref, qo, ko, vo):
        c, sa, sb = c_ref[...], sa_ref[...], sb_ref[...]
        kr = kr_ref[...]
        kr2 = _sumsq(kr)
        for h in range(H):
            lo, hi = q_ref[:, h * P:h * P + 128], q_ref[:, h * P + 128:(h + 1) * P]
            r = lax.rsqrt((_sumsq(lo) + _sumsq(hi)) * (1.0 / MLA_QK) + EPS)
            qo[:, h * P:h * P + 128] = (lo * r * gq_ref[:, :128] * scale).astype(BF16)
            qo[:, h * P + 128:(h + 1) * P] = (_rope(hi * r * gq_ref[:, 128:], c, sa, sb, 32) * scale).astype(BF16)
            kn = kv_ref[:, h * P:h * P + 128]
            r = lax.rsqrt((_sumsq(kn) + kr2) * (1.0 / MLA_QK) + EPS)
            ko[:, h * P:h * P + 128] = (kn * r * gk_ref[:, :128]).astype(BF16)
            ko[:, h * P + 128:(h + 1) * P] = _rope(kr * r * gk_ref[:, 128:], c, sa, sb, 32).astype(BF16)
            vo[:, h * 128:(h + 1) * 128] = kv_ref[:, h * P + 128:(h + 1) * P].astype(BF16)

    wide = pl.BlockSpec((tm, H * P), lambda i: (i, 0))
    lane = pl.BlockSpec((tm, 128), lambda i: (i, 0))
    vec = pl.BlockSpec((1, P), lambda i: (0, 0))
    return pl.pallas_call(body, name="mla_prep_fwd", grid=(T // tm,),
                          in_specs=[wide, wide, pl.BlockSpec((tm, 128), lambda i: (i, kr_block)), lane, lane, lane, vec, vec],
                          out_specs=[wide, wide, pl.BlockSpec((tm, H * 128), lambda i: (i, 0))],
                          out_shape=[jax.ShapeDtypeStruct((T, H * P), BF16), jax.ShapeDtypeStruct((T, H * P), BF16),
                                     jax.ShapeDtypeStruct((T, H * 128), BF16)],
                          compiler_params=_cp(1))(q_raw, kv, pa, *tabs, gq, gk)


def _mla_prep_bwd(dq, dk, dv, q_raw, kv, pa, tabs, gq, gk, H, kr_block, tm=256):
    T = q_raw.shape[0]
    tm = _tile(T, tm)
    scale = MLA_QK ** -0.5
    P = MLA_PAD

    def body(dq_ref, dk_ref, dv_ref, q_ref, kv_ref, kr_ref, c_ref, sa_ref, sb_ref, gq_ref, gk_ref,
             dqr, dkv, dkr, dgq, dgk):
        c, sa, sb = c_ref[...], sa_ref[...], sb_ref[...]
        kr = kr_ref[...]
        kr2 = _sumsq(kr)
        gql, gqh, gkl, gkh = gq_ref[:, :128], gq_ref[:, 128:], gk_ref[:, :128], gk_ref[:, 128:]
        dkr_acc = jnp.zeros((tm, 128), F32)
        sums = [jnp.zeros((1, 128), F32) for _ in range(4)]
        for h in range(H):
            lo_s, hi_s = slice(h * P, h * P + 128), slice(h * P + 128, (h + 1) * P)
            lo, hi = q_ref[:, lo_s], q_ref[:, hi_s]
            r = lax.rsqrt((_sumsq(lo) + _sumsq(hi)) * (1.0 / MLA_QK) + EPS)
            ql, qh = lo * r, hi * r
            dyl = dq_ref[:, lo_s] * scale
            dyh = _rope_t(dq_ref[:, hi_s] * scale, c, sa, sb, 32)
            gl, gh = dyl * gql, dyh * gqh
            mean = (jnp.sum(gl * ql, axis=-1, keepdims=True) + jnp.sum(gh * qh, axis=-1, keepdims=True)) * (1.0 / MLA_QK)
            dqr[:, lo_s] = (r * (gl - ql * mean)).astype(BF16)
            dqr[:, hi_s] = (r * (gh - qh * mean)).astype(BF16)
            sums[0] += jnp.sum(dyl * ql, axis=0, keepdims=True)
            sums[1] += jnp.sum(dyh * qh, axis=0, keepdims=True)
            kn = kv_ref[:, lo_s]
            r = lax.rsqrt((_sumsq(kn) + kr2) * (1.0 / MLA_QK) + EPS)
            kl, kh = kn * r, kr * r
            dkl = dk_ref[:, lo_s]
            dkh = _rope_t(dk_ref[:, hi_s], c, sa, sb, 32)
            gl, gh = dkl * gkl, dkh * gkh
            mean = (jnp.sum(gl * kl, axis=-1, keepdims=True) + jnp.sum(gh * kh, axis=-1, keepdims=True)) * (1.0 / MLA_QK)
            dkv[:, lo_s] = (r * (gl - kl * mean)).astype(BF16)
            dkr_acc += r * (gh - kh * mean)
            dkv[:, hi_s] = dv_ref[:, h * 128:(h + 1) * 128].astype(BF16)
            sums[2] += jnp.sum(dkl * kl, axis=0, keepdims=True)
            sums[3] += jnp.sum(dkh * kh, axis=0, keepdims=True)
        dkr[...] = dkr_acc

        @pl.when(pl.program_id(0) == 0)
        def _():
            dgq[...] = jnp.zeros_like(dgq)
            dgk[...] = jnp.zeros_like(dgk)

        dgq[:, :128] += sums[0]
        dgq[:, 128:] += sums[1]
        dgk[:, :128] += sums[2]
        dgk[:, 128:] += sums[3]

    wide = pl.BlockSpec((tm, H * P), lambda i: (i, 0))
    lane = pl.BlockSpec((tm, 128), lambda i: (i, 0))
    vec = pl.BlockSpec((1, P), lambda i: (0, 0))
    return pl.pallas_call(body, name="mla_prep_bwd", grid=(T // tm,),
                          in_specs=[wide, wide, pl.BlockSpec((tm, H * 128), lambda i: (i, 0)), wide, wide,
                                    pl.BlockSpec((tm, 128), lambda i: (i, kr_block)), lane, lane, lane, vec, vec],
                          out_specs=[wide, wide, lane, vec, vec],
                          out_shape=[jax.ShapeDtypeStruct((T, H * P), BF16), jax.ShapeDtypeStruct((T, H * P), BF16),
                                     jax.ShapeDtypeStruct((T, 128), F32), jax.ShapeDtypeStruct((1, P), F32),
                                     jax.ShapeDtypeStruct((1, P), F32)],
                          compiler_params=_cp(1))(dq, dk, dv, q_raw, kv, pa, *tabs, gq, gk)


def _on_blocks(s, live, step):
    @pl.when(s == 0)
    def _():
        step(True)

    @pl.when((s > 0) & live)
    def _():
        step(False)


def _sub_mask(sub, width, row0, col0, q_axis):
    r = lax.broadcasted_iota(jnp.int32, (sub, width), 0) + row0
    c = lax.broadcasted_iota(jnp.int32, (sub, width), 1) + col0
    return (r >= c) if q_axis == 0 else (c >= r)


def _mla_fwd(q, k, v, H, blk, comm=None):
    T, wq = q.shape[0], q.shape[1] // H
    nq = T // blk
    sub = _tile(blk, MLA_SUB)

    def body(q_ref, k_ref, v_ref, o_ref, lse_ref, m_sc, l_sc, acc_sc):
        i, s = pl.program_id(1), pl.program_id(2)

        @pl.when(s == 0)
        def _():
            m_sc[...] = jnp.full_like(m_sc, NEG)
            l_sc[...] = jnp.zeros_like(l_sc)
            acc_sc[...] = jnp.zeros_like(acc_sc)

        def step(masked):
            for r in range(blk // sub):
                rs = slice(r * sub, (r + 1) * sub)
                nk = (r + 1) * sub if masked else blk
                sc = _dot(q_ref[rs, :], k_ref[:nk, :], "nt")
                if masked:
                    sc = jnp.where(_sub_mask(sub, nk, r * sub, 0, 0), sc, NEG)
                m_prev = m_sc[rs, :]
                m_new = jnp.maximum(m_prev, jnp.max(sc, axis=-1, keepdims=True))
                p = jnp.exp(sc - m_new)
                alpha = jnp.exp(m_prev - m_new)
                l_sc[rs, :] = alpha * l_sc[rs, :] + jnp.sum(p, axis=-1, keepdims=True)
                acc_sc[rs, :] = alpha * acc_sc[rs, :] + _dot(p, v_ref[:nk, :], "nn")
                m_sc[rs, :] = m_new

        _on_blocks(s, i - s >= 0, step)

        @pl.when(s == nq - 1)
        def _():
            o_ref[...] = (acc_sc[...] / l_sc[...]).astype(BF16)
            lse_ref[...] = m_sc[...] + jnp.log(l_sc[...])

    kv_spec = lambda w: pl.BlockSpec((blk, w), lambda h, i, s: (jnp.maximum(i - s, 0), h))
    return _call(
        body, name="mla_fwd", grid=(H, nq, nq),
        in_specs=[pl.BlockSpec((blk, wq), lambda h, i, s: (i, h)), kv_spec(wq), kv_spec(HEAD)],
        out_specs=[pl.BlockSpec((blk, HEAD), lambda h, i, s: (i, h)), pl.BlockSpec((None, blk, 1), lambda h, i, s: (h, i, 0))],
        out_shape=[jax.ShapeDtypeStruct((T, H * HEAD), BF16), jax.ShapeDtypeStruct((H, T, 1), F32)],
        scratch=[pltpu.VMEM((blk, 1), F32), pltpu.VMEM((blk, 1), F32), pltpu.VMEM((blk, HEAD), F32)],
        operands=[q, k, v], comm=comm)


def _mla_dq(q, k, v, do, lse, dl, H, blk, comm=None):
    T, wq = q.shape[0], q.shape[1] // H
    nq = T // blk
    sub = _tile(blk, MLA_SUB)

    def body(q_ref, k_ref, v_ref, do_ref, lse_ref, dl_ref, dq_ref, acc_sc):
        i, s = pl.program_id(1), pl.program_id(2)

        @pl.when(s == 0)
        def _():
            acc_sc[...] = jnp.zeros_like(acc_sc)

        def step(masked):
            for r in range(blk // sub):
                rs = slice(r * sub, (r + 1) * sub)
                nk = (r + 1) * sub if masked else blk
                sc = _dot(q_ref[rs, :], k_ref[:nk, :], "nt")
                if masked:
                    sc = jnp.where(_sub_mask(sub, nk, r * sub, 0, 0), sc, NEG)
                p = jnp.exp(sc - lse_ref[rs, :])
                dp = _dot(do_ref[rs, :], v_ref[:nk, :], "nt")
                acc_sc[rs, :] += _dot(p * (dp - dl_ref[rs, :]), k_ref[:nk, :], "nn")

        _on_blocks(s, i - s >= 0, step)

        @pl.when(s == nq - 1)
        def _():
            dq_ref[...] = acc_sc[...]

    kv_spec = lambda w: pl.BlockSpec((blk, w), lambda h, i, s: (jnp.maximum(i - s, 0), h))
    col = pl.BlockSpec((None, blk, 1), lambda h, i, s: (h, i, 0))
    return _call(
        body, name="mla_dq", grid=(H, nq, nq),
        in_specs=[pl.BlockSpec((blk, wq), lambda h, i, s: (i, h)), kv_spec(wq), kv_spec(HEAD),
                  pl.BlockSpec((blk, HEAD), lambda h, i, s: (i, h)), col, col],
        out_specs=pl.BlockSpec((blk, wq), lambda h, i, s: (i, h)), out_shape=jax.ShapeDtypeStruct((T, H * wq), F32),
        scratch=[pltpu.VMEM((blk, wq), F32)], operands=[q, k, v, do, lse, dl], comm=comm)


def _as_lanes(col):
    return jnp.transpose(jnp.broadcast_to(col, (col.shape[0], 128)))[0:1, :]


def _mla_dkv(q, k, v, do, lse, dl, H, blk):
    T, wq = q.shape[0], q.shape[1] // H
    nq = T // blk
    sub = _tile(blk, MLA_SUB)

    def body(q_ref, k_ref, v_ref, do_ref, lse_ref, dl_ref, dk_ref, dv_ref, dk_sc, dv_sc):
        j, s = pl.program_id(1), pl.program_id(2)

        @pl.when(s == 0)
        def _():
            dk_sc[...] = jnp.zeros_like(dk_sc)
            dv_sc[...] = jnp.zeros_like(dv_sc)

        def step(masked):
            for r in range(blk // sub):
                rs = slice(r * sub, (r + 1) * sub)
                q0 = r * sub if masked else 0
                st = _dot(k_ref[rs, :], q_ref[q0:, :], "nt")
                if masked:
                    st = jnp.where(_sub_mask(sub, blk - q0, r * sub, q0, 1), st, NEG)
                pt = jnp.exp(st - lse_ref[:, q0:])
                dpt = _dot(v_ref[rs, :], do_ref[q0:, :], "nt")
                dv_sc[rs, :] += _dot(pt, do_ref[q0:, :], "nn")
                dk_sc[rs, :] += _dot(pt * (dpt - dl_ref[:, q0:]), q_ref[q0:, :], "nn")

        _on_blocks(s, j + s < nq, step)

        @pl.when(s == nq - 1)
        def _():
            dk_ref[...] = dk_sc[...]
            dv_ref[...] = dv_sc[...]

    q_row = lambda j, s: jnp.minimum(j + s, nq - 1)
    row = pl.BlockSpec((None, 1, blk), lambda h, j, s: (h, 0, q_row(j, s)))
    return pl.pallas_call(
        body, name="mla_dkv", grid=(H, nq, nq),
        in_specs=[pl.BlockSpec((blk, wq), lambda h, j, s: (q_row(j, s), h)), pl.BlockSpec((blk, wq), lambda h, j, s: (j, h)),
                  pl.BlockSpec((blk, HEAD), lambda h, j, s: (j, h)), pl.BlockSpec((blk, HEAD), lambda h, j, s: (q_row(j, s), h)),
                  row, row],
        out_specs=[pl.BlockSpec((blk, wq), lambda h, j, s: (j, h)), pl.BlockSpec((blk, HEAD), lambda h, j, s: (j, h))],
        out_shape=[jax.ShapeDtypeStruct((T, H * wq), F32), jax.ShapeDtypeStruct((T, H * HEAD), F32)],
        scratch_shapes=[pltpu.VMEM((blk, wq), F32), pltpu.VMEM((blk, HEAD), F32)],
        compiler_params=_cp(3))(q, k, v, do, lse, dl)


def _row_dot(name, a, b, H, tm=1024):
    T = a.shape[0]
    tm = _tile(T, tm)

    def body(a_ref, b_ref, o_ref):
        o_ref[...] = jnp.sum(a_ref[...].astype(F32) * b_ref[...].astype(F32), axis=-1, keepdims=True)

    blk = pl.BlockSpec((tm, HEAD), lambda h, i: (i, h))
    return pl.pallas_call(body, name=name, grid=(H, T // tm), in_specs=[blk, blk],
                          out_specs=pl.BlockSpec((None, tm, 1), lambda h, i: (h, i, 0)),
                          out_shape=jax.ShapeDtypeStruct((H, T, 1), F32), compiler_params=_cp(2))(a, b)


def _tri(n):
    row = lax.broadcasted_iota(jnp.int32, (n, n), 0)
    col = lax.broadcasted_iota(jnp.int32, (n, n), 1)
    return col <= row, col >= row


def _units(ref, U, n):
    return ref[...].reshape(U, n, ref.shape[-1])


def _shift_in(first, units):
    return jnp.concatenate([first[None], units[:-1]], axis=0)


def _shift_out(units, last):
    return jnp.concatenate([units[1:], last[None]], axis=0)


def _starts_inside(row0, U, n, L, limit=None):
    start = row0 + n * lax.broadcasted_iota(jnp.int32, (U, n, n), 0)
    ok = ((start & (L - 1)) if L & (L - 1) == 0 else (start % L)) != 0
    return ok if limit is None else ok & (start < limit)


def _bdot(a, b, mode):
    dims = (((2,), (2,)), ((0,), (0,))) if mode == "nt" else (((2,), (1,)), ((0,), (0,)))
    return lax.dot_general(a.astype(BF16), b.astype(BF16), dims, preferred_element_type=F32)


def _win_fwd(name, q, k, pd, H, n, L, QB):
    T = q.shape[0]
    U = QB // n

    def body(q_ref, k_ref, v_ref, kp_ref, vp_ref, o_ref, lse_ref):
        own_ok, before_ok = _tri(n)
        q3, k3, v3 = _units(q_ref, U, n), _units(k_ref, U, n), _units(v_ref, U, n)
        k_lo, v_lo = _shift_in(kp_ref[...], k3), _shift_in(vp_ref[...], v3)
        inside = _starts_inside(pl.program_id(1) * QB, U, n, L)
        s_hi = jnp.where(own_ok, _bdot(q3, k3, "nt"), NEG)
        s_lo = jnp.where(before_ok & inside, _bdot(q3, k_lo, "nt"), NEG)
        m = jnp.maximum(jnp.max(s_hi, axis=-1, keepdims=True), jnp.max(s_lo, axis=-1, keepdims=True))
        p_hi, p_lo = jnp.exp(s_hi - m), jnp.exp(s_lo - m)
        l = jnp.sum(p_hi, axis=-1, keepdims=True) + jnp.sum(p_lo, axis=-1, keepdims=True)
        acc = _bdot(p_hi, v3, "nn") + _bdot(p_lo, v_lo, "nn")
        o_ref[...] = (acc / l).reshape(QB, HEAD)
        lse_ref[...] = (m + jnp.log(l)).reshape(QB, 1)

    cur = lambda c0: pl.BlockSpec((QB, HEAD), lambda h, i: (i, c0 + h))
    prev = lambda c0: pl.BlockSpec((n, HEAD), lambda h, i: (jnp.maximum(i * U - 1, 0), c0 + h))
    return pl.pallas_call(
        body, name=name, grid=(H, T // QB), in_specs=[cur(0), cur(0), cur(2 * H), prev(0), prev(2 * H)],
        out_specs=[pl.BlockSpec((QB, HEAD), lambda h, i: (i, h)), pl.BlockSpec((None, QB, 1), lambda h, i: (h, i, 0))],
        out_shape=[jax.ShapeDtypeStruct((T, H * HEAD), F32), jax.ShapeDtypeStruct((H, T, 1), F32)],
        compiler_params=_cp(2))(q, k, pd, k, pd)


def _win_dq(name, q, k, pd, do, lse, dl, H, n, L, QB):
    T = q.shape[0]
    U = QB // n

    def body(q_ref, k_ref, v_ref, kp_ref, vp_ref, do_ref, lse_ref, dl_ref, dq_ref):
        own_ok, before_ok = _tri(n)
        q3, k3, v3, do3 = _units(q_ref, U, n), _units(k_ref, U, n), _units(v_ref, U, n), _units(do_ref, U, n)
        lse3, dl3 = _units(lse_ref, U, n), _units(dl_ref, U, n)
        k_lo, v_lo = _shift_in(kp_ref[...], k3), _shift_in(vp_ref[...], v3)
        inside = _starts_inside(pl.program_id(1) * QB, U, n, L)
        p_hi = jnp.exp(jnp.where(own_ok, _bdot(q3, k3, "nt"), NEG) - lse3)
        p_lo = jnp.exp(jnp.where(before_ok & inside, _bdot(q3, k_lo, "nt"), NEG) - lse3)
        ds_hi = p_hi * (_bdot(do3, v3, "nt") - dl3)
        ds_lo = p_lo * (_bdot(do3, v_lo, "nt") - dl3)
        dq_ref[...] = (_bdot(ds_hi, k3, "nn") + _bdot(ds_lo, k_lo, "nn")).reshape(QB, HEAD)

    cur = lambda c0: pl.BlockSpec((QB, HEAD), lambda h, i: (i, c0 + h))
    prev = lambda c0: pl.BlockSpec((n, HEAD), lambda h, i: (jnp.maximum(i * U - 1, 0), c0 + h))
    flat = pl.BlockSpec((QB, HEAD), lambda h, i: (i, h))
    col = pl.BlockSpec((None, QB, 1), lambda h, i: (h, i, 0))
    return pl.pallas_call(
        body, name=name, grid=(H, T // QB), in_specs=[cur(0), cur(0), cur(2 * H), prev(0), prev(2 * H), flat, col, col],
        out_specs=flat, out_shape=jax.ShapeDtypeStruct((T, H * HEAD), F32), compiler_params=_cp(2))(q, k, pd, k, pd, do, lse, dl)


def _win_dkv(name, q, k, pd, do, lse, dl, H, n, L, QB):
    T = q.shape[0]
    U = QB // n
    last = T // n - 1

    def body(q_ref, k_ref, v_ref, do_ref, lse_col, dl_col, qn_ref, don_ref, lsen_col, dln_col, dk_ref, dv_ref):
        own_ok, after_ok = _tri_t(n)
        q3, k3, v3, do3 = _units(q_ref, U, n), _units(k_ref, U, n), _units(v_ref, U, n), _units(do_ref, U, n)
        qn3, don3 = _shift_out(q3, qn_ref[...]), _shift_out(do3, don_ref[...])

        def rows(col_ref, next_col_ref):
            row = _as_lanes(col_ref[...])
            own = [row[:, u * n:(u + 1) * n] for u in range(U)]
            return jnp.stack(own), jnp.stack(own[1:] + [_as_lanes(next_col_ref[...])])

        lse_own, lse_aft = rows(lse_col, lsen_col)
        dl_own, dl_aft = rows(dl_col, dln_col)
        inside = _starts_inside(pl.program_id(1) * QB + n, U, n, L, limit=T)
        pt_own = jnp.exp(jnp.where(own_ok, _bdot(k3, q3, "nt"), NEG) - lse_own)
        pt_aft = jnp.exp(jnp.where(after_ok & inside, _bdot(k3, qn3, "nt"), NEG) - lse_aft)
        dst_own = pt_own * (_bdot(v3, do3, "nt") - dl_own)
        dst_aft = pt_aft * (_bdot(v3, don3, "nt") - dl_aft)
        dv_ref[...] = (_bdot(pt_own, do3, "nn") + _bdot(pt_aft, don3, "nn")).reshape(QB, HEAD)
        dk_ref[...] = (_bdot(dst_own, q3, "nn") + _bdot(dst_aft, qn3, "nn")).reshape(QB, HEAD)

    cur = lambda c0: pl.BlockSpec((QB, HEAD), lambda h, i: (i, c0 + h))
    flat = pl.BlockSpec((QB, HEAD), lambda h, i: (i, h))
    row = pl.BlockSpec((None, QB, 1), lambda h, i: (h, i, 0))
    nxt_unit = lambda h, i: jnp.minimum((i + 1) * U, last)
    return pl.pallas_call(
        body, name=name, grid=(H, T // QB),
        in_specs=[cur(0), cur(0), cur(2 * H), flat, row, row,
                  pl.BlockSpec((n, HEAD), lambda h, i: (nxt_unit(h, i), h)),
                  pl.BlockSpec((n, HEAD), lambda h, i: (nxt_unit(h, i), h)),
                  pl.BlockSpec((None, n, 1), lambda h, i: (h, nxt_unit(h, i), 0)),
                  pl.BlockSpec((None, n, 1), lambda h, i: (h, nxt_unit(h, i), 0))],
        out_specs=[flat, flat], out_shape=[jax.ShapeDtypeStruct((T, H * HEAD), F32)] * 2,
        compiler_params=_cp(2))(q, k, pd, do, lse, dl, q, do, lse, dl)


def _tri_t(n):
    key = lax.broadcasted_iota(jnp.int32, (n, n), 0)
    qry = lax.broadcasted_iota(jnp.int32, (n, n), 1)
    return key <= qry, key >= qry


def _merge_groups(os_, lses, H, tm=512):
    G = len(os_)
    T = os_[0].shape[0]
    tm = _tile(T, tm)

    def body(*refs):
        o_refs, l_refs, o_out, lse_out = refs[:G], refs[G:2 * G], refs[2 * G], refs[2 * G + 1]
        ls = [r[...] for r in l_refs]
        m = ls[0]
        for x in ls[1:]:
            m = jnp.maximum(m, x)
        ws = [jnp.exp(x - m) for x in ls]
        tot = ws[0]
        for x in ws[1:]:
            tot = tot + x
        acc = ws[0] * o_refs[0][...]
        for x, r in zip(ws[1:], o_refs[1:]):
            acc = acc + x * r[...]
        o_out[...] = (acc / tot).astype(BF16)
        lse_out[...] = m + jnp.log(tot)

    flat = pl.BlockSpec((tm, HEAD), lambda h, i: (i, h))
    col = pl.BlockSpec((None, tm, 1), lambda h, i: (h, i, 0))
    return pl.pallas_call(body, name="dil_merge", grid=(H, T // tm), in_specs=[flat] * G + [col] * G, out_specs=[flat, col],
                          out_shape=[jax.ShapeDtypeStruct((T, H * HEAD), BF16), jax.ShapeDtypeStruct((H, T, 1), F32)],
                          compiler_params=_cp(2))(*os_, *lses)


def _dil_prep_fwd_pm(name, pd, tabs, gq, gk, H, tm=256):
    T = pd.shape[0]
    tm = _tile(T, tm)
    scale = HEAD ** -0.5

    def body(p_ref, c_ref, sa_ref, sb_ref, gq_ref, gk_ref, qo, ko):
        c, sa, sb = c_ref[...], sa_ref[...], sb_ref[...]
        for h in range(H):
            q = p_ref[:, h * 128:(h + 1) * 128].astype(F32)
            qo[:, h * 128:(h + 1) * 128] = (_rope(q * _rms_stats(q, HEAD) * gq_ref[...], c, sa, sb, 16) * scale).astype(BF16)
            k = p_ref[:, (H + h) * 128:(H + h + 1) * 128].astype(F32)
            ko[:, h * 128:(h + 1) * 128] = _rope(k * _rms_stats(k, HEAD) * gk_ref[...], c, sa, sb, 16).astype(BF16)

    lane = pl.BlockSpec((tm, 128), lambda i: (i, 0))
    gain = pl.BlockSpec((1, 128), lambda i: (0, 0))
    out = pl.BlockSpec((tm, H * 128), lambda i: (i, 0))
    return pl.pallas_call(body, name=name, grid=(T // tm,),
                          in_specs=[pl.BlockSpec((tm, 3 * H * 128), lambda i: (i, 0)), lane, lane, lane, gain, gain],
                          out_specs=[out, out], out_shape=[jax.ShapeDtypeStruct((T, H * 128), BF16)] * 2,
                          compiler_params=_cp(1))(pd, *tabs, gq, gk)


def _dil_prep_bwd_pm(name, dq, dk, dv, pd, tabs, gq, gk, H, tm=256):
    T = pd.shape[0]
    tm = _tile(T, tm)
    scale = HEAD ** -0.5

    def body(dq_ref, dk_ref, dv_ref, p_ref, c_ref, sa_ref, sb_ref, gq_ref, gk_ref, dp_ref, dgq, dgk):
        c, sa, sb = c_ref[...], sa_ref[...], sb_ref[...]
        sq = jnp.zeros((1, 128), F32)
        sk = jnp.zeros((1, 128), F32)
        for h in range(H):
            hs = slice(h * 128, (h + 1) * 128)
            q = p_ref[:, hs].astype(F32)
            dx, dgp = _rms_bwd_rows(_rope_t(dq_ref[:, hs] * scale, c, sa, sb, 16), q, gq_ref[...], HEAD)
            dp_ref[:, hs] = dx.astype(BF16)
            sq += jnp.sum(dgp, axis=0, keepdims=True)
            ks = slice((H + h) * 128, (H + h + 1) * 128)
            k = p_ref[:, ks].astype(F32)
            dx, dgp = _rms_bwd_rows(_rope_t(dk_ref[:, hs], c, sa, sb, 16), k, gk_ref[...], HEAD)
            dp_ref[:, ks] = dx.astype(BF16)
            sk += jnp.sum(dgp, axis=0, keepdims=True)
            dp_ref[:, (2 * H + h) * 128:(2 * H + h + 1) * 128] = dv_ref[:, hs].astype(BF16)

        @pl.when(pl.program_id(0) == 0)
        def _():
            dgq[...] = jnp.zeros_like(dgq)
            dgk[...] = jnp.zeros_like(dgk)

        dgq[...] += sq
        dgk[...] += sk

    lane = pl.BlockSpec((tm, 128), lambda i: (i, 0))
    flat = pl.BlockSpec((tm, H * 128), lambda i: (i, 0))
    vec = pl.BlockSpec((1, 128), lambda i: (0, 0))
    return pl.pallas_call(body, name=name, grid=(T // tm,),
                          in_specs=[flat, flat, flat, pl.BlockSpec((tm, 3 * H * 128), lambda i: (i, 0)),
                                    lane, lane, lane, vec, vec],
                          out_specs=[pl.BlockSpec((tm, 3 * H * 128), lambda i: (i, 0)), vec, vec],
                          out_shape=[jax.ShapeDtypeStruct((T, 3 * H * 128), BF16), jax.ShapeDtypeStruct((1, 128), F32),
                                     jax.ShapeDtypeStruct((1, 128), F32)],
                          compiler_params=_cp(1))(dq, dk, dv, pd, *tabs, gq, gk)


def _to_phase(a, d, axis=0):
    if d == 1:
        return a
    sh = a.shape
    T = sh[axis]
    b = a.reshape(*sh[:axis], T // d, d, *sh[axis + 1:])
    return jnp.swapaxes(b, axis, axis + 1).reshape(sh)


def _from_phase(a, d, axis=0):
    if d == 1:
        return a
    sh = a.shape
    T = sh[axis]
    b = a.reshape(*sh[:axis], d, T // d, *sh[axis + 1:])
    return jnp.swapaxes(b, axis, axis + 1).reshape(sh)


def _ffn_up(name, n, wg, wu, tm=512, comm=None):
    T, D = n.shape
    nd, _, fc = wg.shape
    tm = _tile(T, tm)

    def epilogue(accs, in_refs, out_refs):
        a, b = accs
        out_refs[0][...] = a.astype(BF16)
        out_refs[1][...] = b.astype(BF16)
        out_refs[2][...] = (a * _sigmoid(a) * b).astype(BF16)

    w_spec = pl.BlockSpec((None, D, fc), lambda j, i: (j, 0, 0))
    o_spec = pl.BlockSpec((None, tm, fc), lambda j, i: (j, i, 0))
    sh = jax.ShapeDtypeStruct((nd, T, fc), BF16)
    return _mm(name, (nd, T // tm), [n, wg, wu], [pl.BlockSpec((tm, D), lambda j, i: (i, 0)), w_spec, w_spec],
               [(0, 1, "nn", 0), (0, 2, "nn", 1)], 2, None, epilogue, [sh, sh, sh], [o_spec] * 3, None, comm=comm)


def _ffn_down(name, s, wd, res, tm=512, tn=512, comm=None):
    nd, T, fc = s.shape
    D = wd.shape[2]
    tm, tn = _tile(T, tm), _tile(D, tn)

    def body(s_ref, w_ref, r_ref, o_ref):
        acc = _dot(s_ref[0], w_ref[0], "nn")
        for j in range(1, nd):
            acc += _dot(s_ref[j], w_ref[j], "nn")
        o_ref[...] = r_ref[...] + 0.5 * acc

    mn = pl.BlockSpec((tm, tn), lambda i, j: (i, j))
    return _call(body, name=name, grid=(T // tm, D // tn),
                 in_specs=[pl.BlockSpec((nd, tm, fc), lambda i, j: (0, i, 0)), pl.BlockSpec((nd, fc, tn), lambda i, j: (0, 0, j)), mn],
                 out_specs=mn, out_shape=jax.ShapeDtypeStruct((T, D), F32), operands=[s, wd, res], comm=comm)


def _ffn_bwd_act(name, dxb, wdt, a, b, tm=512, comm=None):
    T, D = dxb.shape
    nd, _, fc = wdt.shape
    tm = _tile(T, tm)

    def epilogue(accs, in_refs, out_refs):
        ds = 0.5 * accs[0]
        av, bv = in_refs[2][...].astype(F32), in_refs[3][...].astype(F32)
        sg = _sigmoid(av)
        out_refs[0][...] = (ds * bv * sg * (1.0 + av * (1.0 - sg))).astype(BF16)
        out_refs[1][...] = (ds * av * sg).astype(BF16)

    act = pl.BlockSpec((None, tm, fc), lambda j, i: (j, i, 0))
    sh = jax.ShapeDtypeStruct((nd, T, fc), BF16)
    return _mm(name, (nd, T // tm), [dxb, wdt, a, b],
               [pl.BlockSpec((tm, D), lambda j, i: (i, 0)), pl.BlockSpec((None, D, fc), lambda j, i: (j, 0, 0)), act, act],
               [(0, 1, "nn", 0)], 1, None, epilogue, [sh, sh], [act, act], None, comm=comm)


def _ffn_dwd(name, s, dxb, tk=2048):
    nd, T, fc = s.shape
    D = dxb.shape[1]
    tk = _tile(T, tk)

    def epilogue(accs, in_refs, out_refs):
        out_refs[0][...] = (0.5 * accs[0]).astype(BF16)

    return _mm(name, (nd, T // tk), [s, dxb],
               [pl.BlockSpec((None, tk, fc), lambda j, k: (j, k, 0)), pl.BlockSpec((tk, D), lambda j, k: (k, 0))],
               [(0, 1, "tn", 0)], 1, (fc, D), epilogue, [jax.ShapeDtypeStruct((nd, fc, D), BF16)],
               [pl.BlockSpec((None, fc, D), lambda j, k: (j, 0, 0))], 1)[0]


def _ffn_dw(name, n, dact, tk=2048):
    T, D = n.shape
    nd, _, fc = dact.shape
    tk = _tile(T, tk)

    def epilogue(accs, in_refs, out_refs):
        out_refs[0][...] = accs[0].astype(BF16)

    return _mm(name, (nd, T // tk), [n, dact],
               [pl.BlockSpec((tk, D), lambda j, k: (k, 0)), pl.BlockSpec((None, tk, fc), lambda j, k: (j, k, 0))],
               [(0, 1, "tn", 0)], 1, (D, fc), epilogue, [jax.ShapeDtypeStruct((nd, D, fc), BF16)],
               [pl.BlockSpec((None, D, fc), lambda j, k: (j, 0, 0))], 1)[0]


def _ffn_dn(name, da, db, wg, wu, tm=512, tn=256, comm=None):
    nd, T, fc = da.shape
    D = wg.shape[1]
    tm, tn = _tile(T, tm), _tile(D, tn)

    def body(da_ref, db_ref, wg_ref, wu_ref, o_ref):
        acc = _dot(da_ref[0], wg_ref[0], "nt") + _dot(db_ref[0], wu_ref[0], "nt")
        for j in range(1, nd):
            acc += _dot(da_ref[j], wg_ref[j], "nt") + _dot(db_ref[j], wu_ref[j], "nt")
        o_ref[...] = acc

    act = pl.BlockSpec((nd, tm, fc), lambda i, j: (0, i, 0))
    w_spec = pl.BlockSpec((nd, tn, fc), lambda i, j: (0, j, 0))
    return _call(body, name=name, grid=(T // tm, D // tn), in_specs=[act, act, w_spec, w_spec],
                 out_specs=pl.BlockSpec((tm, tn), lambda i, j: (i, j)), out_shape=jax.ShapeDtypeStruct((T, D), F32),
                 operands=[da, db, wg, wu], comm=comm)


def _ffn_forward(tag, x, g, wg, wu, wd, comm_up=None, comm_down=None):
    n = _rmsnorm_fwd(tag + "_norm", x, g)
    a, b, s = _ffn_up(tag + "_up", n, wg, wu, comm=comm_up)
    return _ffn_down(tag + "_down", s, wd, x, comm=comm_down() if comm_down else None), (n, a, b, s)


def _ffn_backward(tag, dx, dxb, x, g, wg, wu, wd, saved, comm_act=None, comm_dn=None, weights_first=True):
    n, a, b, s = saved
    da, db = _ffn_bwd_act(tag + "_bwd_act", dxb, jnp.swapaxes(wd, 1, 2), a, b, comm=comm_act)

    def weight_grads():
        return _ffn_dw(tag + "_dwg", n, da), _ffn_dw(tag + "_dwu", n, db), _ffn_dwd(tag + "_dwd", s, dxb)

    dws = weight_grads() if weights_first else None
    dn = _ffn_dn(tag + "_dn", da, db, wg, wu, comm=comm_dn(dws) if comm_dn else None)
    d_wg, d_wu, d_wd = dws if weights_first else weight_grads()
    dx_in, dxb_in, dg = _rmsnorm_bwd(tag + "_norm_bwd", dn, x, g, dx)
    return dx_in, dxb_in, dg, d_wg, d_wu, d_wd


def _place():
    x, y, c = lax.axis_index("x"), lax.axis_index("y"), lax.axis_index("c")
    return x, y, c, [(1 - x, y), (x, 1 - y), (1 - x, 1 - y)]


def _allgather(name, shards):
    n = len(shards)

    def body(*refs):
        ins, outs = refs[:n], refs[n:2 * n]
        send_sems, recv_sems, local_sems = refs[2 * n:]
        x, y, c, chips = _place()
        me, sibling = (x, y, c), (x, y, 1 - c)

        def slot(a, p):
            return outs[a].at[4 * p[0] + 2 * p[1] + p[2]]

        def copy(a, kk, block, to, src=None):
            return pltpu.make_async_remote_copy(
                src_ref=slot(a, block) if src is None else src, dst_ref=slot(a, block),
                send_sem=send_sems.at[a * 7 + kk], recv_sem=recv_sems.at[a * 7 + kk],
                device_id=to, device_id_type=MESH)

        mine = [pltpu.make_async_copy(ins[a], slot(a, me), local_sems.at[a]) for a in range(n)]
        for cp in mine:
            cp.start()
        first = []
        for a in range(n):
            first.append(copy(a, 0, me, sibling, src=ins[a]))
            first += [copy(a, 1 + j, me, (*chip, c), src=ins[a]) for j, chip in enumerate(chips)]
        for cp in first:
            cp.start()
        passed = []
        for j, chip in enumerate(chips):
            for a in range(n):
                copy(a, 1 + j, (*chip, c), me).wait_recv()
                fwd = copy(a, 4 + j, (*chip, c), sibling)
                fwd.start()
                passed.append(fwd)
        for a in range(n):
            copy(a, 0, sibling, me).wait_recv()
        for j, chip in enumerate(chips):
            for a in range(n):
                copy(a, 4 + j, (*chip, 1 - c), me).wait_recv()
        for cp in first + passed:
            cp.wait_send()
        for cp in mine:
            cp.wait()

    return pl.pallas_call(
        body, name=name, in_specs=[HBM_SPEC] * n, out_specs=[HBM_SPEC] * n,
        out_shape=[jax.ShapeDtypeStruct((N_DEV, *s.shape), s.dtype) for s in shards],
        scratch_shapes=[pltpu.SemaphoreType.DMA((7 * n,)), pltpu.SemaphoreType.DMA((7 * n,)), pltpu.SemaphoreType.DMA((n,))],
    )(*shards)


def _slot(ref, p):
    return ref.at[4 * p[0] + 2 * p[1] + p[2]]


def _ag_first(shards):
    n = len(shards)

    def plan(ins, outs):
        x, y, c, chips = _place()
        me = (x, y, c)
        remote, local = [], []
        for a in range(n):
            local.append((ins[a], _slot(outs[a], me)))
            for peer in [(x, y, 1 - c)] + [(*chip, c) for chip in chips]:
                remote.append((ins[a], _slot(outs[a], me), _slot(outs[a], peer), peer))
        return remote, local

    return _Exchange(shards, [jax.ShapeDtypeStruct((N_DEV, *s.shape), s.dtype) for s in shards], plan, 4 * n, n)


def _ag_second(partial):
    n = len(partial)

    def plan(ins, outs):
        x, y, c, chips = _place()
        remote = []
        for a in range(n):
            for chip in chips:
                remote.append((_slot(ins[a], (*chip, c)), _slot(outs[a], (*chip, c)), _slot(outs[a], (*chip, 1 - c)), (x, y, 1 - c)))
        return remote, []

    return _Exchange(partial, [jax.ShapeDtypeStruct(p.shape, p.dtype) for p in partial], plan, 3 * n, 0,
                     aliases={a: a for a in range(n)})


def _rs_first(halves):
    n = len(halves)

    def plan(ins, outs):
        x, y, c, _ = _place()
        return [(ins[a].at[:, 1 - c], outs[a], outs[a], (x, y, 1 - c)) for a in range(n)], []

    return _Exchange(halves, [jax.ShapeDtypeStruct((h.shape[0], *h.shape[2:]), h.dtype) for h in halves], plan, n, 0)


def _rs_second(sums):
    n = len(sums)

    def plan(ins, outs):
        x, y, c, chips = _place()
        k_me = 2 * x + y
        remote = []
        for a in range(n):
            for chip in chips:
                k_peer = 2 * chip[0] + chip[1]
                remote.append((ins[a].at[k_peer], outs[a].at[k_me], outs[a].at[k_peer], (*chip, c)))
        return remote, [(ins[a].at[k_me], outs[a].at[k_me]) for a in range(n)]

    return _Exchange(sums, [jax.ShapeDtypeStruct(s.shape, s.dtype) for s in sums], plan, 3 * n, n)


def _pair_add(name, own, got, core):
    nch, _, K, N = own.shape
    tr = _row_tile(K, N)

    def body(c_ref, own_ref, got_ref, o_ref):
        o_ref[...] = (own_ref[...].astype(F32) + got_ref[...].astype(F32)).astype(o_ref.dtype)

    grid_spec = pltpu.PrefetchScalarGridSpec(
        num_scalar_prefetch=1, grid=(nch, K // tr),
        in_specs=[pl.BlockSpec((None, None, tr, N), lambda k, r, c_ref: (k, c_ref[0], r, 0)),
                  pl.BlockSpec((None, tr, N), lambda k, r, c_ref: (k, r, 0))],
        out_specs=pl.BlockSpec((None, tr, N), lambda k, r, c_ref: (k, r, 0)))
    return pl.pallas_call(body, name=name, grid_spec=grid_spec, out_shape=jax.ShapeDtypeStruct((nch, K, N), own.dtype),
                          compiler_params=_cp(2))(core, own, got)


def _row_tile(K, N):
    limit = max(16, 262144 // N)
    t = 1
    while t * 2 <= limit and K % (t * 2) == 0:
        t *= 2
    return t if t >= 16 else K


def _adamw(name, parts, w, m, v):
    P, K, N = parts.shape
    tr = _row_tile(K, N)

    def body(p_ref, w_ref, m_ref, v_ref, g_ref, d_ref, nm_ref, nv_ref):
        g = p_ref[0].astype(F32)
        for i in range(1, P):
            g = g + p_ref[i].astype(F32)
        m_new = ADAM_B1 * m_ref[...] + (1.0 - ADAM_B1) * g
        v_new = ADAM_B2 * v_ref[...] + (1.0 - ADAM_B2) * (g * g)
        m_hat = m_new / (1.0 - ADAM_B1 ** ADAM_STEP)
        v_hat = v_new / (1.0 - ADAM_B2 ** ADAM_STEP)
        g_ref[...] = g
        d_ref[...] = -ADAM_LR * (m_hat / (jnp.sqrt(v_hat) + ADAM_EPS) + ADAM_WD * w_ref[...])
        nm_ref[...] = m_new
        nv_ref[...] = v_new

    row = pl.BlockSpec((tr, N), lambda r: (r, 0))
    sh = jax.ShapeDtypeStruct((K, N), F32)
    return pl.pallas_call(body, name=name, grid=(K // tr,),
                          in_specs=[pl.BlockSpec((P, tr, N), lambda r: (0, r, 0)), row, row, row],
                          out_specs=[row] * 4, out_shape=[sh] * 4, compiler_params=_cp(1))(parts, w, m, v)


def _merge_fwd(o_mla, wbm, o_dil, wbd, g0, g1, tm=512, tn=512, comm=None):
    T, K1 = o_mla.shape
    K2, D = o_dil.shape[1], wbm.shape[1]
    tm, tn = _tile(T, tm), _tile(D, tn)

    def epilogue(accs, in_refs, out_refs):
        a, b = accs
        out_refs[0][...] = a.astype(BF16)
        out_refs[1][...] = b.astype(BF16)
        out_refs[2][...] = (in_refs[4][...].astype(F32) * a + in_refs[5][...].astype(F32) * b).astype(BF16)

    mn = pl.BlockSpec((tm, tn), lambda i, j: (i, j))
    sh = jax.ShapeDtypeStruct((T, D), BF16)
    return _mm("merge_fwd", (T // tm, D // tn), [o_mla, wbm, o_dil, wbd, g0, g1],
               [pl.BlockSpec((tm, K1), lambda i, j: (i, 0)), pl.BlockSpec((K1, tn), lambda i, j: (0, j)),
                pl.BlockSpec((tm, K2), lambda i, j: (i, 0)), pl.BlockSpec((K2, tn), lambda i, j: (0, j)), mn, mn],
               [(0, 1, "nn", 0), (2, 3, "nn", 1)], 2, None, epilogue, [sh, sh, sh], [mn, mn, mn], None, comm=comm)


def _ple_loss(n4, wpg, pe, wpp, x3, tgt, tm=512, tn=512):
    T, D = x3.shape
    Kp = pe.shape[1]
    tm, tn = _tile(T, tm), _tile(D, tn)

    def epilogue(accs, in_refs, out_refs):
        z, proj = accs
        pg = _sigmoid(z)
        err = in_refs[4][...] + pg * proj - in_refs[5][...]
        dy = err * (1.0 / D)
        out_refs[0][...] = dy
        out_refs[1][...] = (dy * proj * pg * (1.0 - pg)).astype(BF16)
        out_refs[2][...] = (dy * pg).astype(BF16)

        @pl.when(pl.program_id(1) == 0)
        def _():
            out_refs[3][...] = jnp.zeros_like(out_refs[3])

        out_refs[3][...] += jnp.sum(err * err, axis=-1, keepdims=True)

    mn = pl.BlockSpec((tm, tn), lambda i, j: (i, j))
    return _mm("ple_loss", (T // tm, D // tn), [n4, wpg, pe, wpp, x3, tgt],
               [pl.BlockSpec((tm, D), lambda i, j: (i, 0)), pl.BlockSpec((D, tn), lambda i, j: (0, j)),
                pl.BlockSpec((tm, Kp), lambda i, j: (i, 0)), pl.BlockSpec((Kp, tn), lambda i, j: (0, j)), mn, mn],
               [(0, 1, "nn", 0), (2, 3, "nn", 1)], 2, None, epilogue,
               [jax.ShapeDtypeStruct((T, D), F32), jax.ShapeDtypeStruct((T, D), BF16), jax.ShapeDtypeStruct((T, D), BF16),
                jax.ShapeDtypeStruct((T, 1), F32)],
               [mn, mn, mn, pl.BlockSpec((tm, 1), lambda i, j: (i, 0))], None)


def _epi_sigmoid(acc, ex, outs):
    outs[0][...] = _sigmoid(acc).astype(outs[0].dtype)


def _epi_add(acc, ex, outs):
    outs[0][...] = (acc + ex[0][...].astype(F32)).astype(outs[0].dtype)


def _epi_dmerge(acc, ex, outs):
    mp, dp, g0, g1 = [e[...].astype(F32) for e in ex]
    outs[0][...] = (acc * g0).astype(BF16)
    outs[1][...] = (acc * g1).astype(BF16)
    outs[2][...] = (acc * mp * g0 * (1.0 - g0)).astype(BF16)
    outs[3][...] = (acc * dp * g1 * (1.0 - g1)).astype(BF16)


_WEIGHTS = ("g_ffn1", "w1_gate", "w1_up", "w1_down", "g_mix", "w_in", "g_cq", "w_uq", "g_ckv", "w_ukv", "g_q_mla", "g_k_mla",
            "g_q_dil", "g_k_dil", "w_br_mla", "w_br_dil", "w_o", "g_ffn2", "w2_gate", "w2_up", "w2_down", "g_ple",
            "w_ple_gate", "w_ple_proj")
_MATRICES = ("w1_gate", "w1_up", "w1_down", "w_in", "w_uq", "w_ukv", "w_br_mla", "w_br_dil", "w_o", "w2_gate", "w2_up",
             "w2_down", "w_ple_gate", "w_ple_proj")
_GAINS = tuple(n for n in _WEIGHTS if n not in _MATRICES)
MLA_BLOCK = 1024
MLA_SUB = 256
DIL_ROWS = 2048
_FIRST = ("w1_gate", "w1_up", "w1_down")
_MID = ("w_in", "w_uq", "w_ukv", "w_br_mla", "w_br_dil", "w_o", "w_ple_gate", "w_ple_proj")
_LATE = ("w2_gate", "w2_up", "w2_down")
_RS_GROUPS = (("w_ple_proj", "w_ple_gate", "w2_gate", "w2_up", "w2_down"),
              ("w_o", "w_br_mla", "w_br_dil", "w_uq", "w_ukv", "w_in"),
              ("w1_gate", "w1_up", "w1_down"))


def _cols(g3):
    nd, K, n = g3.shape
    return g3.transpose(1, 0, 2).reshape(K, nd * n)


def _uncols(m):
    K, n = m.shape
    return m.reshape(K, N_DEV, n // N_DEV).transpose(1, 0, 2)


def _rows(g3):
    nd, k, N = g3.shape
    return g3.reshape(nd * k, N)


def _unrows(m):
    K, N = m.shape
    return m.reshape(N_DEV, K // N_DEV, N)


def _pack_gains(vals):
    flat = jnp.concatenate([vals[n].reshape(-1) for n in _GAINS])
    pad = (-flat.shape[0]) % 2048
    return jnp.pad(flat, (0, pad)).reshape(-1, 128)


def _unpack_gains(packed, like):
    flat = packed.reshape(-1)
    out, off = {}, 0
    for n in _GAINS:
        size = int(np.prod(like[n].shape))
        out[n] = flat[off:off + size].reshape(like[n].shape)
        off += size
    return out


def _train_step(x, p, positions, loss_target, W, M, V):
    T, D = x.shape[1], x.shape[2]
    xs, tgt, pe = x[0], loss_target[0], p[0, 0]
    pos_col = positions.reshape(T, 1).astype(F32)
    w = {n: (a[0] if n in _MATRICES else a.reshape(1, -1)) for n, a in W.items()}

    shard = {n: w[n].astype(BF16) for n in _MATRICES}
    gathered = dict(zip(_FIRST, _allgather("ag_first", [shard[n] for n in _FIRST])))
    ag_mid = _ag_first([shard[n] for n in _MID])
    ag_mid2 = []

    def pass_on_mid():
        ag_mid2.append(_ag_second(ag_mid.results))
        return ag_mid2[0]

    x1, ffn1 = _ffn_forward("ffn1", xs, w["g_ffn1"], gathered["w1_gate"], gathered["w1_up"], gathered["w1_down"],
                            comm_up=ag_mid, comm_down=pass_on_mid)
    gathered.update(zip(_MID, ag_mid2[0].results))
    nq_l, nkv_l = w["g_cq"].shape[-1], w["g_ckv"].shape[-1]
    H = w["w_uq"].shape[1] * N_DEV // MLA_QK
    G = len(DIL_GROUPS)
    off_kr = nq_l + nkv_l
    off_dil = off_kr + MLA_ROPE
    off_gate = off_dil + G * 3 * H * HEAD
    kr_block = off_kr // 128
    w_in = _cols(gathered["w_in"])
    wa = jnp.pad(w_in[:, :off_dil], ((0, 0), (0, 128 - MLA_ROPE)))
    wdil, wg0, wg1 = w_in[:, off_dil:off_gate], w_in[:, off_gate:off_gate + D], w_in[:, off_gate + D:]
    wuq = jnp.pad(_cols(gathered["w_uq"]).reshape(nq_l, H, MLA_QK), ((0, 0), (0, 0), (0, MLA_PAD - MLA_QK))).reshape(nq_l, H * MLA_PAD)
    wukv = _cols(gathered["w_ukv"])
    wbm, wbd, wpp = _cols(gathered["w_br_mla"]), _cols(gathered["w_br_dil"]), _cols(gathered["w_ple_proj"])
    wo, wpg = _rows(gathered["w_o"]), _rows(gathered["w_ple_gate"])
    gq_mla = jnp.pad(w["g_q_mla"], ((0, 0), (0, MLA_PAD - MLA_QK)))
    gk_mla = jnp.pad(w["g_k_mla"], ((0, 0), (0, MLA_PAD - MLA_QK)))
    gq_dil, gk_dil = w["g_q_dil"].reshape(G, 1, HEAD), w["g_k_dil"].reshape(G, 1, HEAD)

    half_m, half_d = MLA_ROPE // 2, DIL_ROT // 2
    inv_m = ROPE_THETA ** (-jnp.arange(half_m, dtype=F32) * 2.0 / MLA_ROPE)
    inv_d = ROPE_THETA ** (-jnp.arange(half_d, dtype=F32) * 2.0 / DIL_ROT)
    inv_m = jnp.tile(inv_m, 128 // half_m).reshape(1, 128)
    inv_d = jnp.tile(inv_d, 128 // half_d).reshape(1, 128)
    tabs = _rope_tables(pos_col, inv_m, inv_d)
    tabs_m, tabs_d = tabs[:3], tabs[3:]

    mla_blk, dil_qb = _tile(T, MLA_BLOCK), _tile(T, DIL_ROWS)
    gw = 3 * H * HEAD
    dils = [d for _, d in DIL_GROUPS]
    units = [win // d for win, d in DIL_GROUPS]
    wdil_g = [wdil[:, g * gw:(g + 1) * gw] for g in range(G)]
    tabs_g = [[_to_phase(t, d) for t in tabs_d] for d in dils]

    h = _rmsnorm_fwd("mix_norm", x1, w["g_mix"])
    pa = _mm2d("proj_a", h, wa, "nn", 512, 1024, 4096)
    pd = [_to_phase(_mm2d("proj_dil%d" % g, h, wdil_g[g], "nn", 1024, 1024, 4096, out_dtypes=(BF16,)), dils[g]) for g in range(G)]
    g0 = _mm2d("proj_gate0", h, wg0, "nn", 1024, 1024, 4096, out_dtypes=(BF16,), epilogue=_epi_sigmoid)
    g1 = _mm2d("proj_gate1", h, wg1, "nn", 1024, 1024, 4096, out_dtypes=(BF16,), epilogue=_epi_sigmoid)
    cq, ckv = _lora_fwd(pa, w["g_cq"], w["g_ckv"], nq_l, nkv_l)
    q_raw = _mm2d("q_up", cq, wuq, "nn", 512, 2048, 4096)
    kv = _mm2d("kv_up", ckv, wukv, "nn", 512, 2048, 4096)
    q_att, k_att, v_mla = _mla_prep_fwd(q_raw, kv, pa, tabs_m, gq_mla, gk_mla, H, kr_block)
    ag_late = _ag_first([shard[n] for n in _LATE])
    o_mla, lse_mla = _mla_fwd(q_att, k_att, v_mla, H, mla_blk, comm=ag_late)
    qd, kd, o_g, lse_g = [], [], [], []
    for g in range(G):
        qg, kg = _dil_prep_fwd_pm("dil_prep_fwd%d" % g, pd[g], tabs_g[g], gq_dil[g], gk_dil[g], H)
        og, lg = _win_fwd("dil_fwd%d" % g, qg, kg, pd[g], H, units[g], T // dils[g], dil_qb)
        qd.append(qg)
        kd.append(kg)
        o_g.append(_from_phase(og, dils[g]))
        lse_g.append(_from_phase(lg, dils[g], axis=1))
    o_dil, lse_dil = _merge_groups(o_g, lse_g, H)
    ag_late2 = _ag_second(ag_late.results)
    mla_p, dil_p, merged = _merge_fwd(o_mla, wbm, o_dil, wbd, g0, g1, comm=ag_late2)
    gathered.update(zip(_LATE, ag_late2.results))
    x2 = _mm2d("out_proj", merged, wo, "nn", 512, 1024, 4096, epilogue=_epi_add, extras=(x1,))
    x3, ffn2 = _ffn_forward("ffn2", x2, w["g_ffn2"], gathered["w2_gate"], gathered["w2_up"], gathered["w2_down"])
    n4 = _rmsnorm_fwd("ple_norm", x3, w["g_ple"])
    dy, dz, dproj, loss_rows = _ple_loss(n4, wpg, pe, wpp, x3, tgt)
    loss = lax.psum((0.5 / D) * jnp.sum(loss_rows), ("x", "y", "c"))

    dW, dG = {}, {}
    dW["w_ple_proj"] = _uncols(_mm2d("d_wpp", pe, dproj, "tn", 1024, 2048, 2048, out_dtypes=(BF16,)))
    dW["w_ple_gate"] = _unrows(_mm2d("d_wpg", n4, dz, "tn", 1024, 1024, 2048, out_dtypes=(BF16,)))
    dn4 = _mm2d("d_n4", dz, wpg, "nt", 512, 1024, 4096)
    dx3, dx3b, dG["g_ple"] = _rmsnorm_bwd("ple_norm_bwd", dn4, x3, w["g_ple"], dy)

    core = lax.axis_index("c").astype(jnp.int32).reshape(1)
    grads, deltas, new_m, new_v = {}, {}, {}, {}

    def sibling_exchange(names):
        return _rs_first([dW[n].reshape(4, 2, *dW[n].shape[1:]) for n in names])

    def chip_exchange(names, first):
        return _rs_second([_pair_add("rs_add_" + n, own, rec, core) for n, own, rec in zip(names, first.ins, first.results)])

    def update(names, second):
        for n, parts in zip(names, second.results):
            grads[n], deltas[n], new_m[n], new_v[n] = [a[None] for a in _adamw("adamw_" + n, parts, w[n], M[n][0], V[n][0])]

    rs_a = []

    def ffn2_grads_done(dws):
        dW["w2_gate"], dW["w2_up"], dW["w2_down"] = dws
        rs_a.append(sibling_exchange(_RS_GROUPS[0]))
        return rs_a[0]

    dx2, dx2b, dG["g_ffn2"], _, _, _ = _ffn_backward(
        "ffn2", dx3, dx3b, x2, w["g_ffn2"], gathered["w2_gate"], gathered["w2_up"], gathered["w2_down"], ffn2,
        comm_dn=ffn2_grads_done)
    rs_a2 = chip_exchange(_RS_GROUPS[0], rs_a[0])

    d_mla_p, d_dil_p, dpg0, dpg1 = _mm2d("d_merged", dx2b, wo, "nt", 512, 1024, 4096, out_dtypes=(BF16,) * 4,
                                         epilogue=_epi_dmerge, extras=(mla_p, dil_p, g0, g1))
    dW["w_o"] = _unrows(_mm2d("d_wo", merged, dx2b, "tn", 1024, 1024, 2048, out_dtypes=(BF16,)))
    dW["w_br_mla"] = _uncols(_mm2d("d_wbm", o_mla, d_mla_p, "tn", 1024, 1024, 2048, out_dtypes=(BF16,)))
    dW["w_br_dil"] = _uncols(_mm2d("d_wbd", o_dil, d_dil_p, "tn", 1024, 1024, 2048, out_dtypes=(BF16,)))
    do_mla = _mm2d("d_o_mla", d_mla_p, wbm, "nt", 512, 1024, 4096, out_dtypes=(BF16,))
    do_dil = _mm2d("d_o_dil", d_dil_p, wbd, "nt", 512, 1024, 4096, out_dtypes=(BF16,))

    dl_mla = _row_dot("mla_delta", do_mla, o_mla, H)
    dq_att = _mla_dq(q_att, k_att, v_mla, do_mla, lse_mla, dl_mla, H, mla_blk, comm=rs_a2)
    update(_RS_GROUPS[0], rs_a2)
    dk_att, dv_mla = _mla_dkv(q_att, k_att, v_mla, do_mla, lse_mla.reshape(H, 1, T), dl_mla.reshape(H, 1, T), H, mla_blk)
    dq_raw, dkv, dkr, dgq, dgk = _mla_prep_bwd(dq_att, dk_att, dv_mla, q_raw, kv, pa, tabs_m, gq_mla, gk_mla, H, kr_block)
    dG["g_q_mla"], dG["g_k_mla"] = dgq[:, :MLA_QK], dgk[:, :MLA_QK]
    d_wuq = _mm2d("d_wuq", cq, dq_raw, "tn", 512, 2048, 2048, out_dtypes=(BF16,))
    dW["w_uq"] = _uncols(d_wuq.reshape(nq_l, H, MLA_PAD)[:, :, :MLA_QK].reshape(nq_l, H * MLA_QK))
    dW["w_ukv"] = _uncols(_mm2d("d_wukv", ckv, dkv, "tn", 512, 2048, 2048, out_dtypes=(BF16,)))
    dcq = _mm2d("d_cq", dq_raw, wuq, "nt", 512, 1024, 4096)
    dckv = _mm2d("d_ckv", dkv, wukv, "nt", 512, 1024, 4096)
    dpa, dG["g_cq"], dG["g_ckv"] = _lora_bwd(dcq, dckv, dkr, pa, w["g_cq"], w["g_ckv"])

    dl_dil = _row_dot("dil_delta", do_dil, o_dil, H)
    dpd, dgqd, dgkd = [], [], []
    for g in range(G):
        d, n, L = dils[g], units[g], T // dils[g]
        do_g, lse_pg, dl_pg = _to_phase(do_dil, d), _to_phase(lse_dil, d, axis=1), _to_phase(dl_dil, d, axis=1)
        dq_g = _win_dq("dil_dq%d" % g, qd[g], kd[g], pd[g], do_g, lse_pg, dl_pg, H, n, L, dil_qb)
        dk_g, dv_g = _win_dkv("dil_dkv%d" % g, qd[g], kd[g], pd[g], do_g, lse_pg, dl_pg, H, n, L, dil_qb)
        dp_g, dgq_g, dgk_g = _dil_prep_bwd_pm("dil_prep_bwd%d" % g, dq_g, dk_g, dv_g, pd[g], tabs_g[g], gq_dil[g], gk_dil[g], H)
        dpd.append(_from_phase(dp_g, d))
        dgqd.append(dgq_g)
        dgkd.append(dgk_g)
    dG["g_q_dil"], dG["g_k_dil"] = jnp.concatenate(dgqd).reshape(1, G, HEAD), jnp.concatenate(dgkd).reshape(1, G, HEAD)

    d_wa = _mm2d("d_wa", h, dpa, "tn", 1024, 1024, 2048, out_dtypes=(BF16,))
    d_wdil = [_mm2d("d_wdil%d" % g, h, dpd[g], "tn", 1024, 1024, 2048, out_dtypes=(BF16,)) for g in range(G)]
    d_wg0 = _mm2d("d_wg0", h, dpg0, "tn", 1024, 1024, 2048, out_dtypes=(BF16,))
    d_wg1 = _mm2d("d_wg1", h, dpg1, "tn", 1024, 1024, 2048, out_dtypes=(BF16,))
    dW["w_in"] = _uncols(jnp.concatenate([d_wa[:, :off_dil], *d_wdil, d_wg0, d_wg1], axis=1))
    dh = _mm2d("d_h_a", dpa, wa, "nt", 512, 1024, 4096)
    for g in range(G):
        dh = _mm2d("d_h_dil%d" % g, dpd[g], wdil_g[g], "nt", 512, 1024, 4096, epilogue=_epi_add, extras=(dh,))
    dh = _mm2d("d_h_g0", dpg0, wg0, "nt", 512, 1024, 4096, epilogue=_epi_add, extras=(dh,))
    dh = _mm2d("d_h_g1", dpg1, wg1, "nt", 512, 1024, 4096, epilogue=_epi_add, extras=(dh,))
    dx1, dx1b, dG["g_mix"] = _rmsnorm_bwd("mix_norm_bwd", dh, x1, w["g_mix"], dx2)
    rs_b = sibling_exchange(_RS_GROUPS[1])
    rs_b2 = []

    def attention_grads_summed(_):
        rs_b2.append(chip_exchange(_RS_GROUPS[1], rs_b))
        return rs_b2[0]

    dx0, _, dG["g_ffn1"], dW["w1_gate"], dW["w1_up"], dW["w1_down"] = _ffn_backward(
        "ffn1", dx1, dx1b, xs, w["g_ffn1"], gathered["w1_gate"], gathered["w1_up"], gathered["w1_down"], ffn1,
        comm_act=rs_b, comm_dn=attention_grads_summed, weights_first=False)
    update(_RS_GROUPS[1], rs_b2[0])
    rs_c = sibling_exchange(_RS_GROUPS[2])
    _run_exchange("rs_first_w1", rs_c)
    rs_c2 = chip_exchange(_RS_GROUPS[2], rs_c)
    _run_exchange("rs_second_w1", rs_c2)
    update(_RS_GROUPS[2], rs_c2)

    parts = _allgather("ag_gain_grads", [_pack_gains(dG)])[0]
    packed = _adamw("adamw_gains", parts, _pack_gains(W), _pack_gains(M), _pack_gains(V))
    for out, pk in zip((grads, deltas, new_m, new_v), packed):
        out.update(_unpack_gains(pk, W))

    return (loss, dx0[None], *[grads[n] for n in _WEIGHTS], *[deltas[n] for n in _WEIGHTS],
            *[new_m[n] for n in _WEIGHTS], *[new_v[n] for n in _WEIGHTS])


def kernel(x, p, positions, g_ffn1, w1_gate, w1_up, w1_down, g_mix, w_in, g_cq, w_uq, g_ckv, w_ukv, g_q_mla, g_k_mla, g_q_dil, g_k_dil, w_br_mla, w_br_dil, w_o, g_ffn2, w2_gate, w2_up, w2_down, g_ple, w_ple_gate, w_ple_proj, loss_target, m_g_ffn1, m_w1_gate, m_w1_up, m_w1_down, m_g_mix, m_w_in, m_g_cq, m_w_uq, m_g_ckv, m_w_ukv, m_g_q_mla, m_g_k_mla, m_g_q_dil, m_g_k_dil, m_w_br_mla, m_w_br_dil, m_w_o, m_g_ffn2, m_w2_gate, m_w2_up, m_w2_down, m_g_ple, m_w_ple_gate, m_w_ple_proj, v_g_ffn1, v_w1_gate, v_w1_up, v_w1_down, v_g_mix, v_w_in, v_g_cq, v_w_uq, v_g_ckv, v_w_ukv, v_g_q_mla, v_g_k_mla, v_g_q_dil, v_g_k_dil, v_w_br_mla, v_w_br_dil, v_w_o, v_g_ffn2, v_w2_gate, v_w2_up, v_w2_down, v_g_ple, v_w_ple_gate, v_w_ple_proj):
    W = dict(zip(_WEIGHTS, (g_ffn1, w1_gate, w1_up, w1_down, g_mix, w_in, g_cq, w_uq, g_ckv, w_ukv, g_q_mla, g_k_mla, g_q_dil,
                            g_k_dil, w_br_mla, w_br_dil, w_o, g_ffn2, w2_gate, w2_up, w2_down, g_ple, w_ple_gate, w_ple_proj)))
    M = dict(zip(_WEIGHTS, (m_g_ffn1, m_w1_gate, m_w1_up, m_w1_down, m_g_mix, m_w_in, m_g_cq, m_w_uq, m_g_ckv, m_w_ukv, m_g_q_mla,
                            m_g_k_mla, m_g_q_dil, m_g_k_dil, m_w_br_mla, m_w_br_dil, m_w_o, m_g_ffn2, m_w2_gate, m_w2_up,
                            m_w2_down, m_g_ple, m_w_ple_gate, m_w_ple_proj)))
    V = dict(zip(_WEIGHTS, (v_g_ffn1, v_w1_gate, v_w1_up, v_w1_down, v_g_mix, v_w_in, v_g_cq, v_w_uq, v_g_ckv, v_w_ukv, v_g_q_mla,
                            v_g_k_mla, v_g_q_dil, v_g_k_dil, v_w_br_mla, v_w_br_dil, v_w_o, v_g_ffn2, v_w2_gate, v_w2_up,
                            v_w2_down, v_g_ple, v_w_ple_gate, v_w_ple_proj)))
    return _train_step(x, p, positions, loss_target, W, M, V)
```

```python
import numpy as np
import jax
import jax.numpy as jnp
from jax import lax
from jax.experimental import pallas as pl
from jax.experimental.pallas import tpu as pltpu

F32 = jnp.float32
BF16 = jnp.bfloat16

EPS = 1e-6
ROPE_THETA = 500000.0
MLA_NOPE = 128
MLA_ROPE = 64
MLA_QK = MLA_NOPE + MLA_ROPE
MLA_PAD = 256
HEAD = 128
DIL_ROT = 32
DIL_GROUPS = ((128, 1), (512, 4), (2048, 16))
NEG = -1e30
NO_WINDOW = 1 << 30
N_DEV = 8
ADAM_LR, ADAM_B1, ADAM_B2, ADAM_EPS, ADAM_WD, ADAM_STEP = 0.001, 0.9, 0.999, 1e-08, 0.01, 10
VMEM_LIMIT_V7X = 56 * 1024 * 1024
MESH = pl.DeviceIdType.MESH
HBM_SPEC = pl.BlockSpec(memory_space=pltpu.HBM)


def _cp(n_axes):
    return pltpu.CompilerParams(dimension_semantics=("arbitrary",) * n_axes,
                                vmem_limit_bytes=VMEM_LIMIT_V7X)


def _tile(n, t):
    return t if (n >= t and n % t == 0) else n


def _sigmoid(x):
    return 1.0 / (1.0 + jnp.exp(-x))


_DIMS = {"nn": (((1,), (0,)), ((), ())), "nt": (((1,), (1,)), ((), ())), "tn": (((0,), (0,)), ((), ()))}


def _dot(a, b, mode):
    return lax.dot_general(a.astype(BF16), b.astype(BF16), _DIMS[mode], preferred_element_type=F32)


class _Exchange:
    def __init__(self, ins, out_shapes, plan, n_remote, n_local, aliases=None):
        self.ins, self.out_shapes, self.plan = list(ins), list(out_shapes), plan
        self.n_remote, self.n_local, self.aliases = n_remote, n_local, dict(aliases or {})
        self.results = None

    def sems(self):
        return [pltpu.SemaphoreType.DMA((self.n_remote,)), pltpu.SemaphoreType.DMA((self.n_remote,)),
                pltpu.SemaphoreType.DMA((max(self.n_local, 1),))]

    def start(self, in_refs, out_refs, sems):
        remote, local = self.plan(in_refs, out_refs)
        for i, (src, dst_there, _, peer) in enumerate(remote):
            pltpu.make_async_remote_copy(src_ref=src, dst_ref=dst_there, send_sem=sems[0].at[i], recv_sem=sems[1].at[i],
                                         device_id=peer, device_id_type=MESH).start()
        for i, (src, dst) in enumerate(local):
            pltpu.make_async_copy(src, dst, sems[2].at[i]).start()

    def finish(self, in_refs, out_refs, sems):
        remote, local = self.plan(in_refs, out_refs)
        for i, (src, _, dst_here, peer) in enumerate(remote):
            pltpu.make_async_remote_copy(src_ref=src, dst_ref=dst_here, send_sem=sems[0].at[i], recv_sem=sems[1].at[i],
                                         device_id=peer, device_id_type=MESH).wait_recv()
        for i, (src, dst_there, _, peer) in enumerate(remote):
            pltpu.make_async_remote_copy(src_ref=src, dst_ref=dst_there, send_sem=sems[0].at[i], recv_sem=sems[1].at[i],
                                         device_id=peer, device_id_type=MESH).wait_send()
        for i, (src, dst) in enumerate(local):
            pltpu.make_async_copy(src, dst, sems[2].at[i]).wait()


def _run_exchange(name, ex):
    ci = len(ex.ins)

    def body(*refs):
        ins, outs, sems = refs[:ci], refs[ci:ci + len(ex.out_shapes)], refs[ci + len(ex.out_shapes):]
        ex.start(ins, outs, sems)
        ex.finish(ins, outs, sems)

    ex.results = pl.pallas_call(body, name=name, in_specs=[HBM_SPEC] * ci, out_specs=[HBM_SPEC] * len(ex.out_shapes),
                                out_shape=ex.out_shapes, scratch_shapes=ex.sems(), input_output_aliases=ex.aliases)(*ex.ins)
    return ex.results


def _call(body, *, name, grid, in_specs, out_specs, out_shape, operands, scratch=(), comm=None):
    multi = isinstance(out_shape, (list, tuple))
    outs = list(out_shape) if multi else [out_shape]
    ospecs = list(out_specs) if multi else [out_specs]
    if comm is None:
        res = pl.pallas_call(body, name=name, grid=grid, in_specs=list(in_specs), out_specs=ospecs, out_shape=outs,
                             scratch_shapes=list(scratch), compiler_params=_cp(len(grid)))(*operands)
        return res if multi else res[0]
    n_in, n_out, n_scr = len(in_specs), len(outs), len(scratch)
    ci, co = len(comm.ins), len(comm.out_shapes)

    def hosted(*refs):
        bounds = np.cumsum([0, n_in, ci, n_out, co, n_scr])
        ins, cins, os_, cos, scr = (refs[bounds[i]:bounds[i + 1]] for i in range(5))
        sems = refs[bounds[5]:]
        ids = [pl.program_id(a) for a in range(len(grid))]
        first, last = ids[0] == 0, ids[0] == grid[0] - 1
        for a in range(1, len(grid)):
            first, last = first & (ids[a] == 0), last & (ids[a] == grid[a] - 1)

        @pl.when(first)
        def _():
            comm.start(cins, cos, sems)

        body(*ins, *os_, *scr)

        @pl.when(last)
        def _():
            comm.finish(cins, cos, sems)

    res = pl.pallas_call(hosted, name=name, grid=grid, in_specs=[*in_specs, *[HBM_SPEC] * ci],
                         out_specs=[*ospecs, *[HBM_SPEC] * co], out_shape=[*outs, *comm.out_shapes],
                         scratch_shapes=[*scratch, *comm.sems()],
                         input_output_aliases={n_in + i: n_out + o for i, o in comm.aliases.items()},
                         compiler_params=_cp(len(grid)))(*operands, *comm.ins)
    comm.results = res[n_out:]
    return res[:n_out] if multi else res[0]


def _mm(name, grid, ins, in_specs, pairs, n_acc, acc_shape, epilogue, out_shapes, out_specs, k_axis, comm=None):
    n_in, n_out = len(ins), len(out_shapes)
    nk = grid[k_axis] if k_axis is not None else 1

    def body(*refs):
        in_refs, out_refs, acc_refs = refs[:n_in], refs[n_in:n_in + n_out], refs[n_in + n_out:]
        parts = [None] * n_acc
        for ai, bi, mode, ci in pairs:
            d = _dot(in_refs[ai][...], in_refs[bi][...], mode)
            parts[ci] = d if parts[ci] is None else parts[ci] + d
        if nk == 1:
            epilogue(parts, in_refs, out_refs)
            return
        k = pl.program_id(k_axis)

        @pl.when(k == 0)
        def _():
            for c in range(n_acc):
                acc_refs[c][...] = parts[c]

        @pl.when(k > 0)
        def _():
            for c in range(n_acc):
                acc_refs[c][...] += parts[c]

        @pl.when(k == nk - 1)
        def _():
            epilogue([r[...] for r in acc_refs], in_refs, out_refs)

    scratch = [pltpu.VMEM(acc_shape, F32) for _ in range(n_acc)] if nk > 1 else []
    return _call(body, name=name, grid=grid, in_specs=in_specs, out_specs=list(out_specs), out_shape=list(out_shapes),
                 operands=ins, scratch=scratch, comm=comm)


def _mm2d(name, a, b, mode, tm, tn, tk, out_dtypes=(F32,), epilogue=None, extras=()):
    if mode == "nn":
        (M, K), N = a.shape, b.shape[1]
    elif mode == "nt":
        (M, K), N = a.shape, b.shape[0]
    else:
        (K, M), N = a.shape, b.shape[1]
    tm, tn, tk = _tile(M, tm), _tile(N, tn), _tile(K, tk)
    a_spec = pl.BlockSpec((tk, tm), lambda i, j, k: (k, i)) if mode == "tn" else pl.BlockSpec((tm, tk), lambda i, j, k: (i, k))
    b_spec = pl.BlockSpec((tn, tk), lambda i, j, k: (j, k)) if mode == "nt" else pl.BlockSpec((tk, tn), lambda i, j, k: (k, j))
    mn_spec = pl.BlockSpec((tm, tn), lambda i, j, k: (i, j))
    n_ex = len(extras)

    def default_epilogue(acc, ex_refs, out_refs):
        out_refs[0][...] = acc.astype(out_refs[0].dtype)

    epi = epilogue or default_epilogue

    def wrapped(accs, in_refs, out_refs):
        epi(accs[0], in_refs[2:2 + n_ex], out_refs)

    outs = _mm(name, (M // tm, N // tn, K // tk), [a, b, *extras], [a_spec, b_spec] + [mn_spec] * n_ex,
               [(0, 1, mode, 0)], 1, (tm, tn), wrapped,
               [jax.ShapeDtypeStruct((M, N), dt) for dt in out_dtypes], [mn_spec] * len(out_dtypes), 2)
    return outs[0] if len(out_dtypes) == 1 else outs


def _rms_stats(x, n):
    return lax.rsqrt(jnp.sum(x * x, axis=-1, keepdims=True) * (1.0 / n) + EPS)


def _rmsnorm_fwd(name, x, g, tm=512):
    T, D = x.shape
    tm = _tile(T, tm)

    def body(x_ref, g_ref, o_ref):
        xv = x_ref[...]
        o_ref[...] = (xv * _rms_stats(xv, D) * g_ref[...]).astype(BF16)

    return pl.pallas_call(body, name=name, grid=(T // tm,),
                          in_specs=[pl.BlockSpec((tm, D), lambda i: (i, 0)), pl.BlockSpec((1, D), lambda i: (0, 0))],
                          out_specs=pl.BlockSpec((tm, D), lambda i: (i, 0)),
                          out_shape=jax.ShapeDtypeStruct((T, D), BF16), compiler_params=_cp(1))(x, g)


def _rms_bwd_rows(dy, x, g, n):
    r = _rms_stats(x, n)
    xh = x * r
    gd = dy * g
    mean = jnp.sum(gd * xh, axis=-1, keepdims=True) * (1.0 / n)
    return r * (gd - xh * mean), dy * xh


def _rmsnorm_bwd(name, dn, x, g, res, tm=256, comm=None):
    T, D = x.shape
    tm = _tile(T, tm)

    def body(dn_ref, x_ref, g_ref, res_ref, dx_ref, dxb_ref, dg_ref):
        dx, dgp = _rms_bwd_rows(dn_ref[...].astype(F32), x_ref[...], g_ref[...], D)
        dx = dx + res_ref[...]
        dx_ref[...] = dx
        dxb_ref[...] = dx.astype(BF16)

        @pl.when(pl.program_id(0) == 0)
        def _():
            dg_ref[...] = jnp.zeros_like(dg_ref)

        dg_ref[...] += jnp.sum(dgp, axis=0, keepdims=True)

    row = pl.BlockSpec((tm, D), lambda i: (i, 0))
    vec = pl.BlockSpec((1, D), lambda i: (0, 0))
    return _call(body, name=name, grid=(T // tm,), in_specs=[row, row, vec, row], out_specs=[row, row, vec],
                 out_shape=[jax.ShapeDtypeStruct((T, D), F32), jax.ShapeDtypeStruct((T, D), BF16), jax.ShapeDtypeStruct((1, D), F32)],
                 operands=[dn, x, g, res], comm=comm)


def _rope_tables(pos_col, inv_mla, inv_dil, tm=512):
    T = pos_col.shape[0]
    tm = _tile(T, tm)

    def body(p_ref, im_ref, id_ref, cm, sam, sbm, cd, sad, sbd):
        lane = lax.broadcasted_iota(jnp.int32, (tm, 128), 1)
        p = p_ref[...]
        am = p * im_ref[...]
        c, s = jnp.cos(am), jnp.sin(am)
        cm[...] = jnp.where(lane < 64, c, 0.0)
        sam[...] = jnp.where(lane < 32, -s, 0.0)
        sbm[...] = jnp.where((lane >= 32) & (lane < 64), s, 0.0)
        ad = p * id_ref[...]
        c, s = jnp.cos(ad), jnp.sin(ad)
        cd[...] = jnp.where(lane < 32, c, 1.0)
        sad[...] = jnp.where(lane < 16, -s, 0.0)
        sbd[...] = jnp.where((lane >= 16) & (lane < 32), s, 0.0)

    row = pl.BlockSpec((tm, 128), lambda i: (i, 0))
    vec = pl.BlockSpec((1, 128), lambda i: (0, 0))
    return pl.pallas_call(body, name="rope_tables", grid=(T // tm,),
                          in_specs=[pl.BlockSpec((tm, 1), lambda i: (i, 0)), vec, vec], out_specs=[row] * 6,
                          out_shape=[jax.ShapeDtypeStruct((T, 128), F32)] * 6, compiler_params=_cp(1))(pos_col, inv_mla, inv_dil)


def _rope(v, c, sa, sb, sh):
    return v * c + pltpu.roll(v, 128 - sh, 1) * sa + pltpu.roll(v, sh, 1) * sb


def _rope_t(d, c, sa, sb, sh):
    return d * c + pltpu.roll(d * sa, sh, 1) + pltpu.roll(d * sb, 128 - sh, 1)


def _lora_fwd(pa, g_cq, g_ckv, nq, nkv, tm=512):
    T, W = pa.shape
    tm = _tile(T, tm)

    def body(pa_ref, gq_ref, gk_ref, cq_ref, ckv_ref):
        a = pa_ref[:, :nq]
        cq_ref[...] = (a * _rms_stats(a, nq) * gq_ref[...]).astype(BF16)
        b = pa_ref[:, nq:nq + nkv]
        ckv_ref[...] = (b * _rms_stats(b, nkv) * gk_ref[...]).astype(BF16)

    return pl.pallas_call(body, name="lora_fwd", grid=(T // tm,),
                          in_specs=[pl.BlockSpec((tm, W), lambda i: (i, 0)), pl.BlockSpec((1, nq), lambda i: (0, 0)),
                                    pl.BlockSpec((1, nkv), lambda i: (0, 0))],
                          out_specs=[pl.BlockSpec((tm, nq), lambda i: (i, 0)), pl.BlockSpec((tm, nkv), lambda i: (i, 0))],
                          out_shape=[jax.ShapeDtypeStruct((T, nq), BF16), jax.ShapeDtypeStruct((T, nkv), BF16)],
                          compiler_params=_cp(1))(pa, g_cq, g_ckv)


def _lora_bwd(dcq, dckv, dkr, pa, g_cq, g_ckv, tm=512):
    T, W = pa.shape
    nq, nkv = dcq.shape[1], dckv.shape[1]
    tm = _tile(T, tm)

    def body(dcq_ref, dckv_ref, dkr_ref, pa_ref, gq_ref, gk_ref, dpa_ref, dgq_ref, dgk_ref):
        dx, dgp = _rms_bwd_rows(dcq_ref[...], pa_ref[:, :nq], gq_ref[...], nq)
        dpa_ref[:, :nq] = dx.astype(BF16)
        dx2, dgp2 = _rms_bwd_rows(dckv_ref[...], pa_ref[:, nq:nq + nkv], gk_ref[...], nkv)
        dpa_ref[:, nq:nq + nkv] = dx2.astype(BF16)
        dpa_ref[:, nq + nkv:] = dkr_ref[...].astype(BF16)

        @pl.when(pl.program_id(0) == 0)
        def _():
            dgq_ref[...] = jnp.zeros_like(dgq_ref)
            dgk_ref[...] = jnp.zeros_like(dgk_ref)

        dgq_ref[...] += jnp.sum(dgp, axis=0, keepdims=True)
        dgk_ref[...] += jnp.sum(dgp2, axis=0, keepdims=True)

    def row(n):
        return pl.BlockSpec((tm, n), lambda i: (i, 0))

    def vec(n):
        return pl.BlockSpec((1, n), lambda i: (0, 0))

    return pl.pallas_call(body, name="lora_bwd", grid=(T // tm,),
                          in_specs=[row(nq), row(nkv), row(128), row(W), vec(nq), vec(nkv)],
                          out_specs=[row(W), vec(nq), vec(nkv)],
                          out_shape=[jax.ShapeDtypeStruct((T, W), BF16), jax.ShapeDtypeStruct((1, nq), F32),
                                     jax.ShapeDtypeStruct((1, nkv), F32)],
                          compiler_params=_cp(1))(dcq, dckv, dkr, pa, g_cq, g_ckv)


def _sumsq(v):
    return jnp.sum(v * v, axis=-1, keepdims=True)


def _mla_prep_fwd(q_raw, kv, pa, tabs, gq, gk, H, kr_block, tm=256):
    T = q_raw.shape[0]
    tm = _tile(T, tm)
    scale = MLA_QK ** -0.5
    P = MLA_PAD

    def body(q_ref, kv_ref, kr_ref, c_ref, sa_ref, sb_ref, gq_ref, gk_ref, qo, ko, vo):
        c, sa, sb = c_ref[...], sa_ref[...], sb_ref[...]
        kr = kr_ref[...]
        kr2 = _sumsq(kr)
        for h in range(H):
            lo, hi = q_ref[:, h * P:h * P + 128], q_ref[:, h * P + 128:(h + 1) * P]
            r = lax.rsqrt((_sumsq(lo) + _sumsq(hi)) * (1.0 / MLA_QK) + EPS)
            qo[:, h * P:h * P + 128] = (lo * r * gq_ref[:, :128] * scale).astype(BF16)
            qo[:, h * P + 128:(h + 1) * P] = (_rope(hi * r * gq_ref[:, 128:], c, sa, sb, 32) * scale).astype(BF16)
            kn = kv_ref[:, h * P:h * P + 128]
            r = lax.rsqrt((_sumsq(kn) + kr2) * (1.0 / MLA_QK) + EPS)
            ko[:, h * P:h * P + 128] = (kn * r * gk_ref[:, :128]).astype(BF16)
            ko[:, h * P + 128:(h + 1) * P] = _rope(kr * r * gk_ref[:, 128:], c, sa, sb, 32).astype(BF16)
            vo[:, h * 128:(h + 1) * 128] = kv_ref[:, h * P + 128:(h + 1) * P].astype(BF16)

    wide = pl.BlockSpec((tm, H * P), lambda i: (i, 0))
    lane = pl.BlockSpec((tm, 128), lambda i: (i, 0))
    vec = pl.BlockSpec((1, P), lambda i: (0, 0))
    return pl.pallas_call(body, name="mla_prep_fwd", grid=(T // tm,),
                          in_specs=[wide, wide, pl.BlockSpec((tm, 128), lambda i: (i, kr_block)), lane, lane, lane, vec, vec],
                          out_specs=[wide, wide, pl.BlockSpec((tm, H * 128), lambda i: (i, 0))],
                          out_shape=[jax.ShapeDtypeStruct((T, H * P), BF16), jax.ShapeDtypeStruct((T, H * P), BF16),
                                     jax.ShapeDtypeStruct((T, H * 128), BF16)],
                          compiler_params=_cp(1))(q_raw, kv, pa, *tabs, gq, gk)


def _mla_prep_bwd(dq, dk, dv, q_raw, kv, pa, tabs, gq, gk, H, kr_block, tm=256):
    T = q_raw.shape[0]
    tm = _tile(T, tm)
    scale = MLA_QK ** -0.5
    P = MLA_PAD

    def body(dq_ref, dk_ref, dv_ref, q_ref, kv_ref, kr_ref, c_ref, sa_ref, sb_ref, gq_ref, gk_ref,
             dqr, dkv, dkr, dgq, dgk):
        c, sa, sb = c_ref[...], sa_ref[...], sb_ref[...]
        kr = kr_ref[...]
        kr2 = _sumsq(kr)
        gql, gqh, gkl, gkh = gq_ref[:, :128], gq_ref[:, 128:], gk_ref[:, :128], gk_ref[:, 128:]
        dkr_acc = jnp.zeros((tm, 128), F32)
        sums = [jnp.zeros((1, 128), F32) for _ in range(4)]
        for h in range(H):
            lo_s, hi_s = slice(h * P, h * P + 128), slice(h * P + 128, (h + 1) * P)
            lo, hi = q_ref[:, lo_s], q_ref[:, hi_s]
            r = lax.rsqrt((_sumsq(lo) + _sumsq(hi)) * (1.0 / MLA_QK) + EPS)
            ql, qh = lo * r, hi * r
            dyl = dq_ref[:, lo_s] * scale
            dyh = _rope_t(dq_ref[:, hi_s] * scale, c, sa, sb, 32)
            gl, gh = dyl * gql, dyh * gqh
            mean = (jnp.sum(gl * ql, axis=-1, keepdims=True) + jnp.sum(gh * qh, axis=-1, keepdims=True)) * (1.0 / MLA_QK)
            dqr[:, lo_s] = (r * (gl - ql * mean)).astype(BF16)
            dqr[:, hi_s] = (r * (gh - qh * mean)).astype(BF16)
            sums[0] += jnp.sum(dyl * ql, axis=0, keepdims=True)
            sums[1] += jnp.sum(dyh * qh, axis=0, keepdims=True)
            kn = kv_ref[:, lo_s]
            r = lax.rsqrt((_sumsq(kn) + kr2) * (1.0 / MLA_QK) + EPS)
            kl, kh = kn * r, kr * r
            dkl = dk_ref[:, lo_s]
            dkh = _rope_t(dk_ref[:, hi_s], c, sa, sb, 32)
            gl, gh = dkl * gkl, dkh * gkh
            mean = (jnp.sum(gl * kl, axis=-1, keepdims=True) + jnp.sum(gh * kh, axis=-1, keepdims=True)) * (1.0 / MLA_QK)
            dkv[:, lo_s] = (r * (gl - kl * mean)).astype(BF16)
            dkr_acc += r * (gh - kh * mean)
            dkv[:, hi_s] = dv_ref[:, h * 128:(h + 1) * 128].astype(BF16)
            sums[2] += jnp.sum(dkl * kl, axis=0, keepdims=True)
            sums[3] += jnp.sum(dkh * kh, axis=0, keepdims=True)
        dkr[...] = dkr_acc

        @pl.when(pl.program_id(0) == 0)
        def _():
            dgq[...] = jnp.zeros_like(dgq)
            dgk[...] = jnp.zeros_like(dgk)

        dgq[:, :128] += sums[0]
        dgq[:, 128:] += sums[1]
        dgk[:, :128] += sums[2]
        dgk[:, 128:] += sums[3]

    wide = pl.BlockSpec((tm, H * P), lambda i: (i, 0))
    lane = pl.BlockSpec((tm, 128), lambda i: (i, 0))
    vec = pl.BlockSpec((1, P), lambda i: (0, 0))
    return pl.pallas_call(body, name="mla_prep_bwd", grid=(T // tm,),
                          in_specs=[wide, wide, pl.BlockSpec((tm, H * 128), lambda i: (i, 0)), wide, wide,
                                    pl.BlockSpec((tm, 128), lambda i: (i, kr_block)), lane, lane, lane, vec, vec],
                          out_specs=[wide, wide, lane, vec, vec],
                          out_shape=[jax.ShapeDtypeStruct((T, H * P), BF16), jax.ShapeDtypeStruct((T, H * P), BF16),
                                     jax.ShapeDtypeStruct((T, 128), F32), jax.ShapeDtypeStruct((1, P), F32),
                                     jax.ShapeDtypeStruct((1, P), F32)],
                          compiler_params=_cp(1))(dq, dk, dv, q_raw, kv, pa, *tabs, gq, gk)


def _on_blocks(s, live, step):
    @pl.when(s == 0)
    def _():
        step(True)

    @pl.when((s > 0) & live)
    def _():
        step(False)


def _sub_mask(sub, width, row0, col0, q_axis):
    r = lax.broadcasted_iota(jnp.int32, (sub, width), 0) + row0
    c = lax.broadcasted_iota(jnp.int32, (sub, width), 1) + col0
    return (r >= c) if q_axis == 0 else (c >= r)


def _tri_pos(t, nq):
    row = 0
    for k in range(1, nq):
        row = row + (t >= k * (k + 1) // 2).astype(jnp.int32)
    return row, t - (row * (row + 1)) // 2


def _mla_fwd(q, k, v, H, blk, comm=None):
    T, wq = q.shape[0], q.shape[1] // H
    nq = T // blk
    sub = _tile(blk, MLA_SUB)

    def body(q_ref, k_ref, v_ref, o_ref, lse_ref, m_sc, l_sc, acc_sc):
        i, s = _tri_pos(pl.program_id(1), nq)

        @pl.when(s == 0)
        def _():
            m_sc[...] = jnp.full_like(m_sc, NEG)
            l_sc[...] = jnp.zeros_like(l_sc)
            acc_sc[...] = jnp.zeros_like(acc_sc)

        def step(masked):
            for r in range(blk // sub):
                rs = slice(r * sub, (r + 1) * sub)
                nk = (r + 1) * sub if masked else blk
                sc = _dot(q_ref[rs, :], k_ref[:nk, :], "nt")
                if masked:
                    sc = jnp.where(_sub_mask(sub, nk, r * sub, 0, 0), sc, NEG)
                m_prev = m_sc[rs, :]
                m_new = jnp.maximum(m_prev, jnp.max(sc, axis=-1, keepdims=True))
                p = jnp.exp(sc - m_new)
                alpha = jnp.exp(m_prev - m_new)
                l_sc[rs, :] = alpha * l_sc[rs, :] + jnp.sum(p, axis=-1, keepdims=True)
                acc_sc[rs, :] = alpha * acc_sc[rs, :] + _dot(p, v_ref[:nk, :], "nn")
                m_sc[rs, :] = m_new

        _on_blocks(s, i - s >= 0, step)

        @pl.when(s == i)
        def _():
            o_ref[...] = (acc_sc[...] / l_sc[...]).astype(BF16)
            lse_ref[...] = m_sc[...] + jnp.log(l_sc[...])

    qi = lambda t: _tri_pos(t, nq)[0]
    kj = lambda t: _tri_pos(t, nq)[0] - _tri_pos(t, nq)[1]
    kv_spec = lambda w: pl.BlockSpec((blk, w), lambda h, t: (kj(t), h))
    return _call(
        body, name="mla_fwd", grid=(H, nq * (nq + 1) // 2),
        in_specs=[pl.BlockSpec((blk, wq), lambda h, t: (qi(t), h)), kv_spec(wq), kv_spec(HEAD)],
        out_specs=[pl.BlockSpec((blk, HEAD), lambda h, t: (qi(t), h)), pl.BlockSpec((None, blk, 1), lambda h, t: (h, qi(t), 0))],
        out_shape=[jax.ShapeDtypeStruct((T, H * HEAD), BF16), jax.ShapeDtypeStruct((H, T, 1), F32)],
        scratch=[pltpu.VMEM((blk, 1), F32), pltpu.VMEM((blk, 1), F32), pltpu.VMEM((blk, HEAD), F32)],
        operands=[q, k, v], comm=comm)


def _mla_dq(q, k, v, do, lse, dl, H, blk, comm=None):
    T, wq = q.shape[0], q.shape[1] // H
    nq = T // blk
    sub = _tile(blk, MLA_SUB)

    def body(q_ref, k_ref, v_ref, do_ref, lse_ref, dl_ref, dq_ref, acc_sc):
        i, s = _tri_pos(pl.program_id(1), nq)

        @pl.when(s == 0)
        def _():
            acc_sc[...] = jnp.zeros_like(acc_sc)

        def step(masked):
            for r in range(blk // sub):
                rs = slice(r * sub, (r + 1) * sub)
                nk = (r + 1) * sub if masked else blk
                sc = _dot(q_ref[rs, :], k_ref[:nk, :], "nt")
                if masked:
                    sc = jnp.where(_sub_mask(sub, nk, r * sub, 0, 0), sc, NEG)
                p = jnp.exp(sc - lse_ref[rs, :])
                dp = _dot(do_ref[rs, :], v_ref[:nk, :], "nt")
                acc_sc[rs, :] += _dot(p * (dp - dl_ref[rs, :]), k_ref[:nk, :], "nn")

        _on_blocks(s, i - s >= 0, step)

        @pl.when(s == i)
        def _():
            dq_ref[...] = acc_sc[...]

    qi = lambda t: _tri_pos(t, nq)[0]
    kj = lambda t: _tri_pos(t, nq)[0] - _tri_pos(t, nq)[1]
    kv_spec = lambda w: pl.BlockSpec((blk, w), lambda h, t: (kj(t), h))
    col = pl.BlockSpec((None, blk, 1), lambda h, t: (h, qi(t), 0))
    return _call(
        body, name="mla_dq", grid=(H, nq * (nq + 1) // 2),
        in_specs=[pl.BlockSpec((blk, wq), lambda h, t: (qi(t), h)), kv_spec(wq), kv_spec(HEAD),
                  pl.BlockSpec((blk, HEAD), lambda h, t: (qi(t), h)), col, col],
        out_specs=pl.BlockSpec((blk, wq), lambda h, t: (qi(t), h)), out_shape=jax.ShapeDtypeStruct((T, H * wq), F32),
        scratch=[pltpu.VMEM((blk, wq), F32)], operands=[q, k, v, do, lse, dl], comm=comm)


def _as_lanes(col):
    return jnp.transpose(jnp.broadcast_to(col, (col.shape[0], 128)))[0:1, :]


def _mla_dkv(q, k, v, do, lse, dl, H, blk):
    T, wq = q.shape[0], q.shape[1] // H
    nq = T // blk
    sub = _tile(blk, MLA_SUB)

    def body(q_ref, k_ref, v_ref, do_ref, lse_ref, dl_ref, dk_ref, dv_ref, dk_sc, dv_sc):
        left, s = _tri_pos(pl.program_id(1), nq)

        @pl.when(s == 0)
        def _():
            dk_sc[...] = jnp.zeros_like(dk_sc)
            dv_sc[...] = jnp.zeros_like(dv_sc)

        def step(masked):
            for r in range(blk // sub):
                rs = slice(r * sub, (r + 1) * sub)
                q0 = r * sub if masked else 0
                st = _dot(k_ref[rs, :], q_ref[q0:, :], "nt")
                if masked:
                    st = jnp.where(_sub_mask(sub, blk - q0, r * sub, q0, 1), st, NEG)
                pt = jnp.exp(st - lse_ref[:, q0:])
                dpt = _dot(v_ref[rs, :], do_ref[q0:, :], "nt")
                dv_sc[rs, :] += _dot(pt, do_ref[q0:, :], "nn")
                dk_sc[rs, :] += _dot(pt * (dpt - dl_ref[:, q0:]), q_ref[q0:, :], "nn")

        _on_blocks(s, s <= left, step)

        @pl.when(s == left)
        def _():
            dk_ref[...] = dk_sc[...]
            dv_ref[...] = dv_sc[...]

    kj = lambda t: nq - 1 - _tri_pos(t, nq)[0]
    qi = lambda t: nq - 1 - _tri_pos(t, nq)[0] + _tri_pos(t, nq)[1]
    row = pl.BlockSpec((None, 1, blk), lambda h, t: (h, 0, qi(t)))
    return pl.pallas_call(
        body, name="mla_dkv", grid=(H, nq * (nq + 1) // 2),
        in_specs=[pl.BlockSpec((blk, wq), lambda h, t: (qi(t), h)), pl.BlockSpec((blk, wq), lambda h, t: (kj(t), h)),
                  pl.BlockSpec((blk, HEAD), lambda h, t: (kj(t), h)), pl.BlockSpec((blk, HEAD), lambda h, t: (qi(t), h)),
                  row, row],
        out_specs=[pl.BlockSpec((blk, wq), lambda h, t: (kj(t), h)), pl.BlockSpec((blk, HEAD), lambda h, t: (kj(t), h))],
        out_shape=[jax.ShapeDtypeStruct((T, H * wq), F32), jax.ShapeDtypeStruct((T, H * HEAD), F32)],
        scratch_shapes=[pltpu.VMEM((blk, wq), F32), pltpu.VMEM((blk, HEAD), F32)],
        compiler_params=_cp(2))(q, k, v, do, lse, dl)


def _row_dot(name, a, b, H, tm=1024):
    T = a.shape[0]
    tm = _tile(T, tm)

    def body(a_ref, b_ref, o_ref):
        o_ref[...] = jnp.sum(a_ref[...].astype(F32) * b_ref[...].astype(F32), axis=-1, keepdims=True)

    blk = pl.BlockSpec((tm, HEAD), lambda h, i: (i, h))
    return pl.pallas_call(body, name=name, grid=(H, T // tm), in_specs=[blk, blk],
                          out_specs=pl.BlockSpec((None, tm, 1), lambda h, i: (h, i, 0)),
                          out_shape=jax.ShapeDtypeStruct((H, T, 1), F32), compiler_params=_cp(2))(a, b)


def _tri(n):
    row = lax.broadcasted_iota(jnp.int32, (n, n), 0)
    col = lax.broadcasted_iota(jnp.int32, (n, n), 1)
    return col <= row, col >= row


def _units(ref, U, n):
    return ref[...].reshape(U, n, ref.shape[-1])


def _shift_in(first, units):
    return jnp.concatenate([first[None], units[:-1]], axis=0)


def _shift_out(units, last):
    return jnp.concatenate([units[1:], last[None]], axis=0)


def _starts_inside(row0, U, n, L, limit=None):
    start = row0 + n * lax.broadcasted_iota(jnp.int32, (U, n, n), 0)
    ok = ((start & (L - 1)) if L & (L - 1) == 0 else (start % L)) != 0
    return ok if limit is None else ok & (start < limit)


def _bdot(a, b, mode):
    dims = (((2,), (2,)), ((0,), (0,))) if mode == "nt" else (((2,), (1,)), ((0,), (0,)))
    return lax.dot_general(a.astype(BF16), b.astype(BF16), dims, preferred_element_type=F32)


def _win_fwd(name, q, k, pd, H, n, L, QB):
    T = q.shape[0]
    U = QB // n

    def body(q_ref, k_ref, v_ref, kp_ref, vp_ref, o_ref, lse_ref):
        own_ok, before_ok = _tri(n)
        q3, k3, v3 = _units(q_ref, U, n), _units(k_ref, U, n), _units(v_ref, U, n)
        k_lo, v_lo = _shift_in(kp_ref[...], k3), _shift_in(vp_ref[...], v3)
        inside = _starts_inside(pl.program_id(1) * QB, U, n, L)
        s_hi = jnp.where(own_ok, _bdot(q3, k3, "nt"), NEG)
        s_lo = jnp.where(before_ok & inside, _bdot(q3, k_lo, "nt"), NEG)
        m = jnp.maximum(jnp.max(s_hi, axis=-1, keepdims=True), jnp.max(s_lo, axis=-1, keepdims=True))
        p_hi, p_lo = jnp.exp(s_hi - m), jnp.exp(s_lo - m)
        l = jnp.sum(p_hi, axis=-1, keepdims=True) + jnp.sum(p_lo, axis=-1, keepdims=True)
        acc = _bdot(p_hi, v3, "nn") + _bdot(p_lo, v_lo, "nn")
        o_ref[...] = (acc / l).reshape(QB, HEAD).astype(BF16)
        lse_ref[...] = (m + jnp.log(l)).reshape(QB, 1)

    cur = lambda c0: pl.BlockSpec((QB, HEAD), lambda h, i: (i, c0 + h))
    prev = lambda c0: pl.BlockSpec((n, HEAD), lambda h, i: (jnp.maximum(i * U - 1, 0), c0 + h))
    return pl.pallas_call(
        body, name=name, grid=(H, T // QB), in_specs=[cur(0), cur(0), cur(2 * H), prev(0), prev(2 * H)],
        out_specs=[pl.BlockSpec((QB, HEAD), lambda h, i: (i, h)), pl.BlockSpec((None, QB, 1), lambda h, i: (h, i, 0))],
        out_shape=[jax.ShapeDtypeStruct((T, H * HEAD), BF16), jax.ShapeDtypeStruct((H, T, 1), F32)],
        compiler_params=_cp(2))(q, k, pd, k, pd)


def _win_dq(name, q, k, pd, do, lse, dl, H, n, L, QB):
    T = q.shape[0]
    U = QB // n

    def body(q_ref, k_ref, v_ref, kp_ref, vp_ref, do_ref, lse_ref, dl_ref, dq_ref):
        own_ok, before_ok = _tri(n)
        q3, k3, v3, do3 = _units(q_ref, U, n), _units(k_ref, U, n), _units(v_ref, U, n), _units(do_ref, U, n)
        lse3, dl3 = _units(lse_ref, U, n), _units(dl_ref, U, n)
        k_lo, v_lo = _shift_in(kp_ref[...], k3), _shift_in(vp_ref[...], v3)
        inside = _starts_inside(pl.program_id(1) * QB, U, n, L)
        p_hi = jnp.exp(jnp.where(own_ok, _bdot(q3, k3, "nt"), NEG) - lse3)
        p_lo = jnp.exp(jnp.where(before_ok & inside, _bdot(q3, k_lo, "nt"), NEG) - lse3)
        ds_hi = p_hi * (_bdot(do3, v3, "nt") - dl3)
        ds_lo = p_lo * (_bdot(do3, v_lo, "nt") - dl3)
        dq_ref[...] = (_bdot(ds_hi, k3, "nn") + _bdot(ds_lo, k_lo, "nn")).reshape(QB, HEAD)

    cur = lambda c0: pl.BlockSpec((QB, HEAD), lambda h, i: (i, c0 + h))
    prev = lambda c0: pl.BlockSpec((n, HEAD), lambda h, i: (jnp.maximum(i * U - 1, 0), c0 + h))
    flat = pl.BlockSpec((QB, HEAD), lambda h, i: (i, h))
    col = pl.BlockSpec((None, QB, 1), lambda h, i: (h, i, 0))
    return pl.pallas_call(
        body, name=name, grid=(H, T // QB), in_specs=[cur(0), cur(0), cur(2 * H), prev(0), prev(2 * H), flat, col, col],
        out_specs=flat, out_shape=jax.ShapeDtypeStruct((T, H * HEAD), F32), compiler_params=_cp(2))(q, k, pd, k, pd, do, lse, dl)


def _win_dkv(name, q, k, pd, do, lse, dl, H, n, L, QB):
    T = q.shape[0]
    U = QB // n
    last = T // n - 1

    def body(q_ref, k_ref, v_ref, do_ref, lse_col, dl_col, qn_ref, don_ref, lsen_col, dln_col, dk_ref, dv_ref):
        own_ok, after_ok = _tri_t(n)
        q3, k3, v3, do3 = _units(q_ref, U, n), _units(k_ref, U, n), _units(v_ref, U, n), _units(do_ref, U, n)
        qn3, don3 = _shift_out(q3, qn_ref[...]), _shift_out(do3, don_ref[...])

        def rows(col_ref, next_col_ref):
            row = _as_lanes(col_ref[...])
            own = [row[:, u * n:(u + 1) * n] for u in range(U)]
            return jnp.stack(own), jnp.stack(own[1:] + [_as_lanes(next_col_ref[...])])

        lse_own, lse_aft = rows(lse_col, lsen_col)
        dl_own, dl_aft = rows(dl_col, dln_col)
        inside = _starts_inside(pl.program_id(1) * QB + n, U, n, L, limit=T)
        pt_own = jnp.exp(jnp.where(own_ok, _bdot(k3, q3, "nt"), NEG) - lse_own)
        pt_aft = jnp.exp(jnp.where(after_ok & inside, _bdot(k3, qn3, "nt"), NEG) - lse_aft)
        dst_own = pt_own * (_bdot(v3, do3, "nt") - dl_own)
        dst_aft = pt_aft * (_bdot(v3, don3, "nt") - dl_aft)
        dv_ref[...] = (_bdot(pt_own, do3, "nn") + _bdot(pt_aft, don3, "nn")).reshape(QB, HEAD)
        dk_ref[...] = (_bdot(dst_own, q3, "nn") + _bdot(dst_aft, qn3, "nn")).reshape(QB, HEAD)

    cur = lambda c0: pl.BlockSpec((QB, HEAD), lambda h, i: (i, c0 + h))
    flat = pl.BlockSpec((QB, HEAD), lambda h, i: (i, h))
    row = pl.BlockSpec((None, QB, 1), lambda h, i: (h, i, 0))
    nxt_unit = lambda h, i: jnp.minimum((i + 1) * U, last)
    return pl.pallas_call(
        body, name=name, grid=(H, T // QB),
        in_specs=[cur(0), cur(0), cur(2 * H), flat, row, row,
                  pl.BlockSpec((n, HEAD), lambda h, i: (nxt_unit(h, i), h)),
                  pl.BlockSpec((n, HEAD), lambda h, i: (nxt_unit(h, i), h)),
                  pl.BlockSpec((None, n, 1), lambda h, i: (h, nxt_unit(h, i), 0)),
                  pl.BlockSpec((None, n, 1), lambda h, i: (h, nxt_unit(h, i), 0))],
        out_specs=[flat, flat], out_shape=[jax.ShapeDtypeStruct((T, H * HEAD), F32)] * 2,
        compiler_params=_cp(2))(q, k, pd, do, lse, dl, q, do, lse, dl)


def _tri_t(n):
    key = lax.broadcasted_iota(jnp.int32, (n, n), 0)
    qry = lax.broadcasted_iota(jnp.int32, (n, n), 1)
    return key <= qry, key >= qry


def _merge_groups(os_, lses, H, tm=2048):
    G = len(os_)
    T = os_[0].shape[0]
    tm = _tile(T, tm)

    def body(*refs):
        o_refs, l_refs, o_out, lse_out = refs[:G], refs[G:2 * G], refs[2 * G], refs[2 * G + 1]
        ls = [r[...] for r in l_refs]
        m = ls[0]
        for x in ls[1:]:
            m = jnp.maximum(m, x)
        ws = [jnp.exp(x - m) for x in ls]
        tot = ws[0]
        for x in ws[1:]:
            tot = tot + x
        acc = ws[0] * o_refs[0][...]
        for x, r in zip(ws[1:], o_refs[1:]):
            acc = acc + x * r[...]
        o_out[...] = (acc / tot).astype(BF16)
        lse_out[...] = m + jnp.log(tot)

    flat = pl.BlockSpec((tm, HEAD), lambda h, i: (i, h))
    col = pl.BlockSpec((None, tm, 1), lambda h, i: (h, i, 0))
    return pl.pallas_call(body, name="dil_merge", grid=(H, T // tm), in_specs=[flat] * G + [col] * G, out_specs=[flat, col],
                          out_shape=[jax.ShapeDtypeStruct((T, H * HEAD), BF16), jax.ShapeDtypeStruct((H, T, 1), F32)],
                          compiler_params=_cp(2))(*os_, *lses)


def _dil_prep_fwd_pm(name, pd, tabs, gq, gk, H, tm=256):
    T = pd.shape[0]
    tm = _tile(T, tm)
    scale = HEAD ** -0.5

    def body(p_ref, c_ref, sa_ref, sb_ref, gq_ref, gk_ref, qo, ko):
        c, sa, sb = c_ref[...], sa_ref[...], sb_ref[...]
        for h in range(H):
            q = p_ref[:, h * 128:(h + 1) * 128].astype(F32)
            qo[:, h * 128:(h + 1) * 128] = (_rope(q * _rms_stats(q, HEAD) * gq_ref[...], c, sa, sb, 16) * scale).astype(BF16)
            k = p_ref[:, (H + h) * 128:(H + h + 1) * 128].astype(F32)
            ko[:, h * 128:(h + 1) * 128] = _rope(k * _rms_stats(k, HEAD) * gk_ref[...], c, sa, sb, 16).astype(BF16)

    lane = pl.BlockSpec((tm, 128), lambda i: (i, 0))
    gain = pl.BlockSpec((1, 128), lambda i: (0, 0))
    out = pl.BlockSpec((tm, H * 128), lambda i: (i, 0))
    return pl.pallas_call(body, name=name, grid=(T // tm,),
                          in_specs=[pl.BlockSpec((tm, 3 * H * 128), lambda i: (i, 0)), lane, lane, lane, gain, gain],
                          out_specs=[out, out], out_shape=[jax.ShapeDtypeStruct((T, H * 128), BF16)] * 2,
                          compiler_params=_cp(1))(pd, *tabs, gq, gk)


def _dil_prep_bwd_pm(name, dq, dk, dv, pd, tabs, gq, gk, H, tm=256):
    T = pd.shape[0]
    tm = _tile(T, tm)
    scale = HEAD ** -0.5

    def body(dq_ref, dk_ref, dv_ref, p_ref, c_ref, sa_ref, sb_ref, gq_ref, gk_ref, dp_ref, dgq, dgk):
        c, sa, sb = c_ref[...], sa_ref[...], sb_ref[...]
        sq = jnp.zeros((1, 128), F32)
        sk = jnp.zeros((1, 128), F32)
        for h in range(H):
            hs = slice(h * 128, (h + 1) * 128)
            q = p_ref[:, hs].astype(F32)
            dx, dgp = _rms_bwd_rows(_rope_t(dq_ref[:, hs] * scale, c, sa, sb, 16), q, gq_ref[...], HEAD)
            dp_ref[:, hs] = dx.astype(BF16)
            sq += jnp.sum(dgp, axis=0, keepdims=True)
            ks = slice((H + h) * 128, (H + h + 1) * 128)
            k = p_ref[:, ks].astype(F32)
            dx, dgp = _rms_bwd_rows(_rope_t(dk_ref[:, hs], c, sa, sb, 16), k, gk_ref[...], HEAD)
            dp_ref[:, ks] = dx.astype(BF16)
            sk += jnp.sum(dgp, axis=0, keepdims=True)
            dp_ref[:, (2 * H + h) * 128:(2 * H + h + 1) * 128] = dv_ref[:, hs].astype(BF16)

        @pl.when(pl.program_id(0) == 0)
        def _():
            dgq[...] = jnp.zeros_like(dgq)
            dgk[...] = jnp.zeros_like(dgk)

        dgq[...] += sq
        dgk[...] += sk

    lane = pl.BlockSpec((tm, 128), lambda i: (i, 0))
    flat = pl.BlockSpec((tm, H * 128), lambda i: (i, 0))
    vec = pl.BlockSpec((1, 128), lambda i: (0, 0))
    return pl.pallas_call(body, name=name, grid=(T // tm,),
                          in_specs=[flat, flat, flat, pl.BlockSpec((tm, 3 * H * 128), lambda i: (i, 0)),
                                    lane, lane, lane, vec, vec],
                          out_specs=[pl.BlockSpec((tm, 3 * H * 128), lambda i: (i, 0)), vec, vec],
                          out_shape=[jax.ShapeDtypeStruct((T, 3 * H * 128), BF16), jax.ShapeDtypeStruct((1, 128), F32),
                                     jax.ShapeDtypeStruct((1, 128), F32)],
                          compiler_params=_cp(1))(dq, dk, dv, pd, *tabs, gq, gk)


def _to_phase(a, d, axis=0):
    if d == 1:
        return a
    sh = a.shape
    T = sh[axis]
    b = a.reshape(*sh[:axis], T // d, d, *sh[axis + 1:])
    return jnp.swapaxes(b, axis, axis + 1).reshape(sh)


def _from_phase(a, d, axis=0):
    if d == 1:
        return a
    sh = a.shape
    T = sh[axis]
    b = a.reshape(*sh[:axis], d, T // d, *sh[axis + 1:])
    return jnp.swapaxes(b, axis, axis + 1).reshape(sh)


def _ffn_up(name, n, wg, wu, tm=512, comm=None):
    T, D = n.shape
    nd, _, fc = wg.shape
    tm = _tile(T, tm)

    def epilogue(accs, in_refs, out_refs):
        a, b = accs
        out_refs[0][...] = a.astype(BF16)
        out_refs[1][...] = b.astype(BF16)
        out_refs[2][...] = (a * _sigmoid(a) * b).astype(BF16)

    w_spec = pl.BlockSpec((None, D, fc), lambda j, i: (j, 0, 0))
    o_spec = pl.BlockSpec((None, tm, fc), lambda j, i: (j, i, 0))
    sh = jax.ShapeDtypeStruct((nd, T, fc), BF16)
    return _mm(name, (nd, T // tm), [n, wg, wu], [pl.BlockSpec((tm, D), lambda j, i: (i, 0)), w_spec, w_spec],
               [(0, 1, "nn", 0), (0, 2, "nn", 1)], 2, None, epilogue, [sh, sh, sh], [o_spec] * 3, None, comm=comm)


def _ffn_down(name, s, wd, res, tm=512, tn=512, comm=None):
    nd, T, fc = s.shape
    D = wd.shape[2]
    tm, tn = _tile(T, tm), _tile(D, tn)

    def body(s_ref, w_ref, r_ref, o_ref):
        acc = _dot(s_ref[0], w_ref[0], "nn")
        for j in range(1, nd):
            acc += _dot(s_ref[j], w_ref[j], "nn")
        o_ref[...] = r_ref[...] + 0.5 * acc

    mn = pl.BlockSpec((tm, tn), lambda i, j: (i, j))
    return _call(body, name=name, grid=(T // tm, D // tn),
                 in_specs=[pl.BlockSpec((nd, tm, fc), lambda i, j: (0, i, 0)), pl.BlockSpec((nd, fc, tn), lambda i, j: (0, 0, j)), mn],
                 out_specs=mn, out_shape=jax.ShapeDtypeStruct((T, D), F32), operands=[s, wd, res], comm=comm)


def _ffn_bwd_act(name, dxb, wdt, a, b, tm=512, comm=None):
    T, D = dxb.shape
    nd, _, fc = wdt.shape
    tm = _tile(T, tm)

    def epilogue(accs, in_refs, out_refs):
        ds = 0.5 * accs[0]
        av, bv = in_refs[2][...].astype(F32), in_refs[3][...].astype(F32)
        sg = _sigmoid(av)
        out_refs[0][...] = (ds * bv * sg * (1.0 + av * (1.0 - sg))).astype(BF16)
        out_refs[1][...] = (ds * av * sg).astype(BF16)

    act = pl.BlockSpec((None, tm, fc), lambda j, i: (j, i, 0))
    sh = jax.ShapeDtypeStruct((nd, T, fc), BF16)
    return _mm(name, (nd, T // tm), [dxb, wdt, a, b],
               [pl.BlockSpec((tm, D), lambda j, i: (i, 0)), pl.BlockSpec((None, D, fc), lambda j, i: (j, 0, 0)), act, act],
               [(0, 1, "nn", 0)], 1, None, epilogue, [sh, sh], [act, act], None, comm=comm)


def _ffn_dwd(name, s, dxb, tk=2048):
    nd, T, fc = s.shape
    D = dxb.shape[1]
    tk = _tile(T, tk)

    def epilogue(accs, in_refs, out_refs):
        out_refs[0][...] = (0.5 * accs[0]).astype(BF16)

    return _mm(name, (nd, T // tk), [s, dxb],
               [pl.BlockSpec((None, tk, fc), lambda j, k: (j, k, 0)), pl.BlockSpec((tk, D), lambda j, k: (k, 0))],
               [(0, 1, "tn", 0)], 1, (fc, D), epilogue, [jax.ShapeDtypeStruct((nd, fc, D), BF16)],
               [pl.BlockSpec((None, fc, D), lambda j, k: (j, 0, 0))], 1)[0]


def _ffn_dw(name, n, dact, tk=2048):
    T, D = n.shape
    nd, _, fc = dact.shape
    tk = _tile(T, tk)

    def epilogue(accs, in_refs, out_refs):
        out_refs[0][...] = accs[0].astype(BF16)

    return _mm(name, (nd, T // tk), [n, dact],
               [pl.BlockSpec((tk, D), lambda j, k: (k, 0)), pl.BlockSpec((None, tk, fc), lambda j, k: (j, k, 0))],
               [(0, 1, "tn", 0)], 1, (D, fc), epilogue, [jax.ShapeDtypeStruct((nd, D, fc), BF16)],
               [pl.BlockSpec((None, D, fc), lambda j, k: (j, 0, 0))], 1)[0]


def _ffn_dn(name, da, db, wg, wu, tm=512, tn=256, comm=None):
    nd, T, fc = da.shape
    D = wg.shape[1]
    tm, tn = _tile(T, tm), _tile(D, tn)

    def body(da_ref, db_ref, wg_ref, wu_ref, o_ref):
        acc = _dot(da_ref[0], wg_ref[0], "nt") + _dot(db_ref[0], wu_ref[0], "nt")
        for j in range(1, nd):
            acc += _dot(da_ref[j], wg_ref[j], "nt") + _dot(db_ref[j], wu_ref[j], "nt")
        o_ref[...] = acc

    act = pl.BlockSpec((nd, tm, fc), lambda i, j: (0, i, 0))
    w_spec = pl.BlockSpec((nd, tn, fc), lambda i, j: (0, j, 0))
    return _call(body, name=name, grid=(T // tm, D // tn), in_specs=[act, act, w_spec, w_spec],
                 out_specs=pl.BlockSpec((tm, tn), lambda i, j: (i, j)), out_shape=jax.ShapeDtypeStruct((T, D), F32),
                 operands=[da, db, wg, wu], comm=comm)


def _ffn_forward(tag, x, g, wg, wu, wd, comm_up=None, comm_down=None):
    n = _rmsnorm_fwd(tag + "_norm", x, g)
    a, b, s = _ffn_up(tag + "_up", n, wg, wu, comm=comm_up)
    return _ffn_down(tag + "_down", s, wd, x, comm=comm_down() if comm_down else None), (n, a, b, s)


def _ffn_backward(tag, dx, dxb, x, g, wg, wu, wd, saved, comm_act=None, comm_dn=None, comm_norm=None, weights_first=True):
    n, a, b, s = saved
    da, db = _ffn_bwd_act(tag + "_bwd_act", dxb, jnp.swapaxes(wd, 1, 2), a, b, comm=comm_act)

    def weight_grads():
        return _ffn_dw(tag + "_dwg", n, da), _ffn_dw(tag + "_dwu", n, db), _ffn_dwd(tag + "_dwd", s, dxb)

    dws = weight_grads() if weights_first else None
    dn = _ffn_dn(tag + "_dn", da, db, wg, wu, comm=comm_dn(dws) if comm_dn else None)
    d_wg, d_wu, d_wd = dws if weights_first else weight_grads()
    dx_in, dxb_in, dg = _rmsnorm_bwd(tag + "_norm_bwd", dn, x, g, dx, comm=comm_norm((d_wg, d_wu, d_wd)) if comm_norm else None)
    return dx_in, dxb_in, dg, d_wg, d_wu, d_wd


def _place():
    x, y, c = lax.axis_index("x"), lax.axis_index("y"), lax.axis_index("c")
    return x, y, c, [(1 - x, y), (x, 1 - y), (1 - x, 1 - y)]


def _allgather(name, shards):
    n = len(shards)

    def body(*refs):
        ins, outs = refs[:n], refs[n:2 * n]
        send_sems, recv_sems, local_sems = refs[2 * n:]
        x, y, c, chips = _place()
        me, sibling = (x, y, c), (x, y, 1 - c)

        def slot(a, p):
            return outs[a].at[4 * p[0] + 2 * p[1] + p[2]]

        def copy(a, kk, block, to, src=None):
            return pltpu.make_async_remote_copy(
                src_ref=slot(a, block) if src is None else src, dst_ref=slot(a, block),
                send_sem=send_sems.at[a * 7 + kk], recv_sem=recv_sems.at[a * 7 + kk],
                device_id=to, device_id_type=MESH)

        mine = [pltpu.make_async_copy(ins[a], slot(a, me), local_sems.at[a]) for a in range(n)]
        for cp in mine:
            cp.start()
        first = []
        for a in range(n):
            first.append(copy(a, 0, me, sibling, src=ins[a]))
            first += [copy(a, 1 + j, me, (*chip, c), src=ins[a]) for j, chip in enumerate(chips)]
        for cp in first:
            cp.start()
        passed = []
        for j, chip in enumerate(chips):
            for a in range(n):
                copy(a, 1 + j, (*chip, c), me).wait_recv()
                fwd = copy(a, 4 + j, (*chip, c), sibling)
                fwd.start()
                passed.append(fwd)
        for a in range(n):
            copy(a, 0, sibling, me).wait_recv()
        for j, chip in enumerate(chips):
            for a in range(n):
                copy(a, 4 + j, (*chip, 1 - c), me).wait_recv()
        for cp in first + passed:
            cp.wait_send()
        for cp in mine:
            cp.wait()

    return pl.pallas_call(
        body, name=name, in_specs=[HBM_SPEC] * n, out_specs=[HBM_SPEC] * n,
        out_shape=[jax.ShapeDtypeStruct((N_DEV, *s.shape), s.dtype) for s in shards],
        scratch_shapes=[pltpu.SemaphoreType.DMA((7 * n,)), pltpu.SemaphoreType.DMA((7 * n,)), pltpu.SemaphoreType.DMA((n,))],
    )(*shards)


def _slot(ref, p):
    return ref.at[4 * p[0] + 2 * p[1] + p[2]]


def _ag_first(shards):
    n = len(shards)

    def plan(ins, outs):
        x, y, c, chips = _place()
        me = (x, y, c)
        remote, local = [], []
        for a in range(n):
            local.append((ins[a], _slot(outs[a], me)))
            for peer in [(x, y, 1 - c)] + [(*chip, c) for chip in chips]:
                remote.append((ins[a], _slot(outs[a], me), _slot(outs[a], peer), peer))
        return remote, local

    return _Exchange(shards, [jax.ShapeDtypeStruct((N_DEV, *s.shape), s.dtype) for s in shards], plan, 4 * n, n)


def _ag_second(partial):
    n = len(partial)

    def plan(ins, outs):
        x, y, c, chips = _place()
        remote = []
        for a in range(n):
            for chip in chips:
                remote.append((_slot(ins[a], (*chip, c)), _slot(outs[a], (*chip, c)), _slot(outs[a], (*chip, 1 - c)), (x, y, 1 - c)))
        return remote, []

    return _Exchange(partial, [jax.ShapeDtypeStruct(p.shape, p.dtype) for p in partial], plan, 3 * n, 0,
                     aliases={a: a for a in range(n)})


def _rs_first(halves):
    n = len(halves)

    def plan(ins, outs):
        x, y, c, _ = _place()
        return [(ins[a].at[:, 1 - c], outs[a], outs[a], (x, y, 1 - c)) for a in range(n)], []

    return _Exchange(halves, [jax.ShapeDtypeStruct((h.shape[0], *h.shape[2:]), h.dtype) for h in halves], plan, n, 0)


def _rs_second(sums):
    n = len(sums)

    def plan(ins, outs):
        x, y, c, chips = _place()
        k_me = 2 * x + y
        remote = []
        for a in range(n):
            for chip in chips:
                k_peer = 2 * chip[0] + chip[1]
                remote.append((ins[a].at[k_peer], outs[a].at[k_me], outs[a].at[k_peer], (*chip, c)))
        return remote, [(ins[a].at[k_me], outs[a].at[k_me]) for a in range(n)]

    return _Exchange(sums, [jax.ShapeDtypeStruct(s.shape, s.dtype) for s in sums], plan, 3 * n, n)


def _pair_add(name, own, got, core):
    nch, _, K, N = own.shape
    tr = _row_tile(K, N)

    def body(c_ref, own_ref, got_ref, o_ref):
        o_ref[...] = (own_ref[...].astype(F32) + got_ref[...].astype(F32)).astype(o_ref.dtype)

    grid_spec = pltpu.PrefetchScalarGridSpec(
        num_scalar_prefetch=1, grid=(nch, K // tr),
        in_specs=[pl.BlockSpec((None, None, tr, N), lambda k, r, c_ref: (k, c_ref[0], r, 0)),
                  pl.BlockSpec((None, tr, N), lambda k, r, c_ref: (k, r, 0))],
        out_specs=pl.BlockSpec((None, tr, N), lambda k, r, c_ref: (k, r, 0)))
    return pl.pallas_call(body, name=name, grid_spec=grid_spec, out_shape=jax.ShapeDtypeStruct((nch, K, N), own.dtype),
                          compiler_params=_cp(2))(core, own, got)


def _row_tile(K, N):
    limit = max(16, 262144 // N)
    t = 1
    while t * 2 <= limit and K % (t * 2) == 0:
        t *= 2
    return t if t >= 16 else K


def _adamw(name, parts, w, m, v):
    P, K, N = parts.shape
    tr = _row_tile(K, N)

    def body(p_ref, w_ref, m_ref, v_ref, g_ref, d_ref, nm_ref, nv_ref):
        g = p_ref[0].astype(F32)
        for i in range(1, P):
            g = g + p_ref[i].astype(F32)
        m_new = ADAM_B1 * m_ref[...] + (1.0 - ADAM_B1) * g
        v_new = ADAM_B2 * v_ref[...] + (1.0 - ADAM_B2) * (g * g)
        m_hat = m_new / (1.0 - ADAM_B1 ** ADAM_STEP)
        v_hat = v_new / (1.0 - ADAM_B2 ** ADAM_STEP)
        g_ref[...] = g
        d_ref[...] = -ADAM_LR * (m_hat / (jnp.sqrt(v_hat) + ADAM_EPS) + ADAM_WD * w_ref[...])
        nm_ref[...] = m_new
        nv_ref[...] = v_new

    row = pl.BlockSpec((tr, N), lambda r: (r, 0))
    sh = jax.ShapeDtypeStruct((K, N), F32)
    return pl.pallas_call(body, name=name, grid=(K // tr,),
                          in_specs=[pl.BlockSpec((P, tr, N), lambda r: (0, r, 0)), row, row, row],
                          out_specs=[row] * 4, out_shape=[sh] * 4, compiler_params=_cp(1))(parts, w, m, v)


def _merge_fwd(o_mla, wbm, o_dil, wbd, g0, g1, tm=512, tn=512, comm=None):
    T, K1 = o_mla.shape
    K2, D = o_dil.shape[1], wbm.shape[1]
    tm, tn = _tile(T, tm), _tile(D, tn)

    def epilogue(accs, in_refs, out_refs):
        a, b = accs
        out_refs[0][...] = a.astype(BF16)
        out_refs[1][...] = b.astype(BF16)
        out_refs[2][...] = (in_refs[4][...].astype(F32) * a + in_refs[5][...].astype(F32) * b).astype(BF16)

    mn = pl.BlockSpec((tm, tn), lambda i, j: (i, j))
    sh = jax.ShapeDtypeStruct((T, D), BF16)
    return _mm("merge_fwd", (T // tm, D // tn), [o_mla, wbm, o_dil, wbd, g0, g1],
               [pl.BlockSpec((tm, K1), lambda i, j: (i, 0)), pl.BlockSpec((K1, tn), lambda i, j: (0, j)),
                pl.BlockSpec((tm, K2), lambda i, j: (i, 0)), pl.BlockSpec((K2, tn), lambda i, j: (0, j)), mn, mn],
               [(0, 1, "nn", 0), (2, 3, "nn", 1)], 2, None, epilogue, [sh, sh, sh], [mn, mn, mn], None, comm=comm)


def _ple_loss(n4, wpg, pe, wpp, x3, tgt, tm=512, tn=512):
    T, D = x3.shape
    Kp = pe.shape[1]
    tm, tn = _tile(T, tm), _tile(D, tn)

    def epilogue(accs, in_refs, out_refs):
        z, proj = accs
        pg = _sigmoid(z)
        err = in_refs[4][...] + pg * proj - in_refs[5][...]
        dy = err * (1.0 / D)
        out_refs[0][...] = dy
        out_refs[1][...] = (dy * proj * pg * (1.0 - pg)).astype(BF16)
        out_refs[2][...] = (dy * pg).astype(BF16)

        @pl.when(pl.program_id(1) == 0)
        def _():
            out_refs[3][...] = jnp.zeros_like(out_refs[3])

        out_refs[3][...] += jnp.sum(err * err, axis=-1, keepdims=True)

    mn = pl.BlockSpec((tm, tn), lambda i, j: (i, j))
    return _mm("ple_loss", (T // tm, D // tn), [n4, wpg, pe, wpp, x3, tgt],
               [pl.BlockSpec((tm, D), lambda i, j: (i, 0)), pl.BlockSpec((D, tn), lambda i, j: (0, j)),
                pl.BlockSpec((tm, Kp), lambda i, j: (i, 0)), pl.BlockSpec((Kp, tn), lambda i, j: (0, j)), mn, mn],
               [(0, 1, "nn", 0), (2, 3, "nn", 1)], 2, None, epilogue,
               [jax.ShapeDtypeStruct((T, D), F32), jax.ShapeDtypeStruct((T, D), BF16), jax.ShapeDtypeStruct((T, D), BF16),
                jax.ShapeDtypeStruct((T, 1), F32)],
               [mn, mn, mn, pl.BlockSpec((tm, 1), lambda i, j: (i, 0))], None)


def _epi_sigmoid(acc, ex, outs):
    outs[0][...] = _sigmoid(acc).astype(outs[0].dtype)


def _epi_add(acc, ex, outs):
    outs[0][...] = (acc + ex[0][...].astype(F32)).astype(outs[0].dtype)


def _epi_dmerge(acc, ex, outs):
    mp, dp, g0, g1 = [e[...].astype(F32) for e in ex]
    outs[0][...] = (acc * g0).astype(BF16)
    outs[1][...] = (acc * g1).astype(BF16)
    outs[2][...] = (acc * mp * g0 * (1.0 - g0)).astype(BF16)
    outs[3][...] = (acc * dp * g1 * (1.0 - g1)).astype(BF16)


_WEIGHTS = ("g_ffn1", "w1_gate", "w1_up", "w1_down", "g_mix", "w_in", "g_cq", "w_uq", "g_ckv", "w_ukv", "g_q_mla", "g_k_mla",
            "g_q_dil", "g_k_dil", "w_br_mla", "w_br_dil", "w_o", "g_ffn2", "w2_gate", "w2_up", "w2_down", "g_ple",
            "w_ple_gate", "w_ple_proj")
_MATRICES = ("w1_gate", "w1_up", "w1_down", "w_in", "w_uq", "w_ukv", "w_br_mla", "w_br_dil", "w_o", "w2_gate", "w2_up",
             "w2_down", "w_ple_gate", "w_ple_proj")
_GAINS = tuple(n for n in _WEIGHTS if n not in _MATRICES)
MLA_BLOCK = 1024
MLA_SUB = 256
DIL_ROWS = 2048
_FIRST = ("w1_gate", "w1_up", "w1_down")
_MID = ("w_in", "w_uq", "w_ukv", "w_br_mla", "w_br_dil", "w_o", "w_ple_gate", "w_ple_proj")
_LATE = ("w2_gate", "w2_up", "w2_down")
_RS_GROUPS = (("w_ple_proj", "w_ple_gate", "w2_gate", "w2_up", "w2_down"),
              ("w_o", "w_br_mla", "w_br_dil", "w_uq", "w_ukv", "w_in"),
              ("w1_gate", "w1_up", "w1_down"))


def _cols(g3):
    nd, K, n = g3.shape
    return g3.transpose(1, 0, 2).reshape(K, nd * n)


def _uncols(m):
    K, n = m.shape
    return m.reshape(K, N_DEV, n // N_DEV).transpose(1, 0, 2)


def _rows(g3):
    nd, k, N = g3.shape
    return g3.reshape(nd * k, N)


def _unrows(m):
    K, N = m.shape
    return m.reshape(N_DEV, K // N_DEV, N)


def _pack_gains(vals):
    flat = jnp.concatenate([vals[n].reshape(-1) for n in _GAINS])
    pad = (-flat.shape[0]) % 2048
    return jnp.pad(flat, (0, pad)).reshape(-1, 128)


def _unpack_gains(packed, like):
    flat = packed.reshape(-1)
    out, off = {}, 0
    for n in _GAINS:
        size = int(np.prod(like[n].shape))
        out[n] = flat[off:off + size].reshape(like[n].shape)
        off += size
    return out


def _train_step(x, p, positions, loss_target, W, M, V):
    T, D = x.shape[1], x.shape[2]
    xs, tgt, pe = x[0], loss_target[0], p[0, 0]
    pos_col = positions.reshape(T, 1).astype(F32)
    w = {n: (a[0] if n in _MATRICES else a.reshape(1, -1)) for n, a in W.items()}

    shard = {n: w[n].astype(BF16) for n in _MATRICES}
    gathered = dict(zip(_FIRST, _allgather("ag_first", [shard[n] for n in _FIRST])))
    ag_mid = _ag_first([shard[n] for n in _MID])
    ag_mid2 = []

    def pass_on_mid():
        ag_mid2.append(_ag_second(ag_mid.results))
        return ag_mid2[0]

    x1, ffn1 = _ffn_forward("ffn1", xs, w["g_ffn1"], gathered["w1_gate"], gathered["w1_up"], gathered["w1_down"],
                            comm_up=ag_mid, comm_down=pass_on_mid)
    gathered.update(zip(_MID, ag_mid2[0].results))
    nq_l, nkv_l = w["g_cq"].shape[-1], w["g_ckv"].shape[-1]
    H = w["w_uq"].shape[1] * N_DEV // MLA_QK
    G = len(DIL_GROUPS)
    off_kr = nq_l + nkv_l
    off_dil = off_kr + MLA_ROPE
    off_gate = off_dil + G * 3 * H * HEAD
    kr_block = off_kr // 128
    w_in = _cols(gathered["w_in"])
    wa = jnp.pad(w_in[:, :off_dil], ((0, 0), (0, 128 - MLA_ROPE)))
    wdil, wg0, wg1 = w_in[:, off_dil:off_gate], w_in[:, off_gate:off_gate + D], w_in[:, off_gate + D:]
    wuq = jnp.pad(_cols(gathered["w_uq"]).reshape(nq_l, H, MLA_QK), ((0, 0), (0, 0), (0, MLA_PAD - MLA_QK))).reshape(nq_l, H * MLA_PAD)
    wukv = _cols(gathered["w_ukv"])
    wbm, wbd, wpp = _cols(gathered["w_br_mla"]), _cols(gathered["w_br_dil"]), _cols(gathered["w_ple_proj"])
    wo, wpg = _rows(gathered["w_o"]), _rows(gathered["w_ple_gate"])
    gq_mla = jnp.pad(w["g_q_mla"], ((0, 0), (0, MLA_PAD - MLA_QK)))
    gk_mla = jnp.pad(w["g_k_mla"], ((0, 0), (0, MLA_PAD - MLA_QK)))
    gq_dil, gk_dil = w["g_q_dil"].reshape(G, 1, HEAD), w["g_k_dil"].reshape(G, 1, HEAD)

    half_m, half_d = MLA_ROPE // 2, DIL_ROT // 2
    inv_m = ROPE_THETA ** (-jnp.arange(half_m, dtype=F32) * 2.0 / MLA_ROPE)
    inv_d = ROPE_THETA ** (-jnp.arange(half_d, dtype=F32) * 2.0 / DIL_ROT)
    inv_m = jnp.tile(inv_m, 128 // half_m).reshape(1, 128)
    inv_d = jnp.tile(inv_d, 128 // half_d).reshape(1, 128)
    tabs = _rope_tables(pos_col, inv_m, inv_d)
    tabs_m, tabs_d = tabs[:3], tabs[3:]

    mla_blk, dil_qb = _tile(T, MLA_BLOCK), _tile(T, DIL_ROWS)
    gw = 3 * H * HEAD
    dils = [d for _, d in DIL_GROUPS]
    units = [win // d for win, d in DIL_GROUPS]
    wdil_g = [wdil[:, g * gw:(g + 1) * gw] for g in range(G)]
    tabs_g = [[_to_phase(t, d) for t in tabs_d] for d in dils]

    h = _rmsnorm_fwd("mix_norm", x1, w["g_mix"])
    pa = _mm2d("proj_a", h, wa, "nn", 512, 1024, 4096)
    pd = [_to_phase(_mm2d("proj_dil%d" % g, h, wdil_g[g], "nn", 1024, 1024, 4096, out_dtypes=(BF16,)), dils[g]) for g in range(G)]
    g0 = _mm2d("proj_gate0", h, wg0, "nn", 1024, 1024, 4096, out_dtypes=(BF16,), epilogue=_epi_sigmoid)
    g1 = _mm2d("proj_gate1", h, wg1, "nn", 1024, 1024, 4096, out_dtypes=(BF16,), epilogue=_epi_sigmoid)
    cq, ckv = _lora_fwd(pa, w["g_cq"], w["g_ckv"], nq_l, nkv_l)
    q_raw = _mm2d("q_up", cq, wuq, "nn", 512, 2048, 4096)
    kv = _mm2d("kv_up", ckv, wukv, "nn", 512, 2048, 4096)
    q_att, k_att, v_mla = _mla_prep_fwd(q_raw, kv, pa, tabs_m, gq_mla, gk_mla, H, kr_block)
    ag_late = _ag_first([shard[n] for n in _LATE])
    o_mla, lse_mla = _mla_fwd(q_att, k_att, v_mla, H, mla_blk, comm=ag_late)
    qd, kd, o_g, lse_g = [], [], [], []
    for g in range(G):
        qg, kg = _dil_prep_fwd_pm("dil_prep_fwd%d" % g, pd[g], tabs_g[g], gq_dil[g], gk_dil[g], H)
        og, lg = _win_fwd("dil_fwd%d" % g, qg, kg, pd[g], H, units[g], T // dils[g], dil_qb)
        qd.append(qg)
        kd.append(kg)
        o_g.append(_from_phase(og, dils[g]))
        lse_g.append(_from_phase(lg, dils[g], axis=1))
    o_dil, lse_dil = _merge_groups(o_g, lse_g, H)
    ag_late2 = _ag_second(ag_late.results)
    mla_p, dil_p, merged = _merge_fwd(o_mla, wbm, o_dil, wbd, g0, g1, comm=ag_late2)
    gathered.update(zip(_LATE, ag_late2.results))
    x2 = _mm2d("out_proj", merged, wo, "nn", 512, 1024, 4096, epilogue=_epi_add, extras=(x1,))
    x3, ffn2 = _ffn_forward("ffn2", x2, w["g_ffn2"], gathered["w2_gate"], gathered["w2_up"], gathered["w2_down"])
    n4 = _rmsnorm_fwd("ple_norm", x3, w["g_ple"])
    dy, dz, dproj, loss_rows = _ple_loss(n4, wpg, pe, wpp, x3, tgt)
    loss = lax.psum((0.5 / D) * jnp.sum(loss_rows), ("x", "y", "c"))

    dW, dG = {}, {}
    dW["w_ple_proj"] = _uncols(_mm2d("d_wpp", pe, dproj, "tn", 1024, 2048, 2048, out_dtypes=(BF16,)))
    dW["w_ple_gate"] = _unrows(_mm2d("d_wpg", n4, dz, "tn", 1024, 1024, 2048, out_dtypes=(BF16,)))
    dn4 = _mm2d("d_n4", dz, wpg, "nt", 512, 1024, 4096)
    dx3, dx3b, dG["g_ple"] = _rmsnorm_bwd("ple_norm_bwd", dn4, x3, w["g_ple"], dy)

    core = lax.axis_index("c").astype(jnp.int32).reshape(1)
    grads, deltas, new_m, new_v = {}, {}, {}, {}

    def sibling_exchange(names):
        return _rs_first([dW[n].reshape(4, 2, *dW[n].shape[1:]) for n in names])

    def chip_exchange(names, first):
        return _rs_second([_pair_add("rs_add_" + n, own, rec, core) for n, own, rec in zip(names, first.ins, first.results)])

    def update(names, second):
        for n, parts in zip(names, second.results):
            grads[n], deltas[n], new_m[n], new_v[n] = [a[None] for a in _adamw("adamw_" + n, parts, w[n], M[n][0], V[n][0])]

    rs_a = []

    def ffn2_grads_done(dws):
        dW["w2_gate"], dW["w2_up"], dW["w2_down"] = dws
        rs_a.append(sibling_exchange(_RS_GROUPS[0]))
        return rs_a[0]

    dx2, dx2b, dG["g_ffn2"], _, _, _ = _ffn_backward(
        "ffn2", dx3, dx3b, x2, w["g_ffn2"], gathered["w2_gate"], gathered["w2_up"], gathered["w2_down"], ffn2,
        comm_dn=ffn2_grads_done)
    rs_a2 = chip_exchange(_RS_GROUPS[0], rs_a[0])

    d_mla_p, d_dil_p, dpg0, dpg1 = _mm2d("d_merged", dx2b, wo, "nt", 512, 1024, 4096, out_dtypes=(BF16,) * 4,
                                         epilogue=_epi_dmerge, extras=(mla_p, dil_p, g0, g1))
    dW["w_o"] = _unrows(_mm2d("d_wo", merged, dx2b, "tn", 1024, 1024, 2048, out_dtypes=(BF16,)))
    dW["w_br_mla"] = _uncols(_mm2d("d_wbm", o_mla, d_mla_p, "tn", 1024, 1024, 2048, out_dtypes=(BF16,)))
    dW["w_br_dil"] = _uncols(_mm2d("d_wbd", o_dil, d_dil_p, "tn", 1024, 1024, 2048, out_dtypes=(BF16,)))
    do_mla = _mm2d("d_o_mla", d_mla_p, wbm, "nt", 512, 1024, 4096, out_dtypes=(BF16,))
    do_dil = _mm2d("d_o_dil", d_dil_p, wbd, "nt", 512, 1024, 4096, out_dtypes=(BF16,))

    dl_mla = _row_dot("mla_delta", do_mla, o_mla, H)
    dq_att = _mla_dq(q_att, k_att, v_mla, do_mla, lse_mla, dl_mla, H, mla_blk, comm=rs_a2)
    update(_RS_GROUPS[0], rs_a2)
    dk_att, dv_mla = _mla_dkv(q_att, k_att, v_mla, do_mla, lse_mla.reshape(H, 1, T), dl_mla.reshape(H, 1, T), H, mla_blk)
    dq_raw, dkv, dkr, dgq, dgk = _mla_prep_bwd(dq_att, dk_att, dv_mla, q_raw, kv, pa, tabs_m, gq_mla, gk_mla, H, kr_block)
    dG["g_q_mla"], dG["g_k_mla"] = dgq[:, :MLA_QK], dgk[:, :MLA_QK]
    d_wuq = _mm2d("d_wuq", cq, dq_raw, "tn", 512, 2048, 2048, out_dtypes=(BF16,))
    dW["w_uq"] = _uncols(d_wuq.reshape(nq_l, H, MLA_PAD)[:, :, :MLA_QK].reshape(nq_l, H * MLA_QK))
    dW["w_ukv"] = _uncols(_mm2d("d_wukv", ckv, dkv, "tn", 512, 2048, 2048, out_dtypes=(BF16,)))
    dcq = _mm2d("d_cq", dq_raw, wuq, "nt", 512, 1024, 4096)
    dckv = _mm2d("d_ckv", dkv, wukv, "nt", 512, 1024, 4096)
    dpa, dG["g_cq"], dG["g_ckv"] = _lora_bwd(dcq, dckv, dkr, pa, w["g_cq"], w["g_ckv"])

    dl_dil = _row_dot("dil_delta", do_dil, o_dil, H)
    dpd, dgqd, dgkd = [], [], []
    for g in range(G):
        d, n, L = dils[g], units[g], T // dils[g]
        do_g, lse_pg, dl_pg = _to_phase(do_dil, d), _to_phase(lse_dil, d, axis=1), _to_phase(dl_dil, d, axis=1)
        dq_g = _win_dq("dil_dq%d" % g, qd[g], kd[g], pd[g], do_g, lse_pg, dl_pg, H, n, L, dil_qb)
        dk_g, dv_g = _win_dkv("dil_dkv%d" % g, qd[g], kd[g], pd[g], do_g, lse_pg, dl_pg, H, n, L, dil_qb)
        dp_g, dgq_g, dgk_g = _dil_prep_bwd_pm("dil_prep_bwd%d" % g, dq_g, dk_g, dv_g, pd[g], tabs_g[g], gq_dil[g], gk_dil[g], H)
        dpd.append(_from_phase(dp_g, d))
        dgqd.append(dgq_g)
        dgkd.append(dgk_g)
    dG["g_q_dil"], dG["g_k_dil"] = jnp.concatenate(dgqd).reshape(1, G, HEAD), jnp.concatenate(dgkd).reshape(1, G, HEAD)

    d_wa = _mm2d("d_wa", h, dpa, "tn", 1024, 1024, 2048, out_dtypes=(BF16,))
    d_wdil = [_mm2d("d_wdil%d" % g, h, dpd[g], "tn", 1024, 1024, 2048, out_dtypes=(BF16,)) for g in range(G)]
    d_wg0 = _mm2d("d_wg0", h, dpg0, "tn", 1024, 1024, 2048, out_dtypes=(BF16,))
    d_wg1 = _mm2d("d_wg1", h, dpg1, "tn", 1024, 1024, 2048, out_dtypes=(BF16,))
    dW["w_in"] = _uncols(jnp.concatenate([d_wa[:, :off_dil], *d_wdil, d_wg0, d_wg1], axis=1))
    dh = _mm2d("d_h_a", dpa, wa, "nt", 512, 1024, 4096)
    for g in range(G):
        dh = _mm2d("d_h_dil%d" % g, dpd[g], wdil_g[g], "nt", 512, 1024, 4096, epilogue=_epi_add, extras=(dh,))
    dh = _mm2d("d_h_g0", dpg0, wg0, "nt", 512, 1024, 4096, epilogue=_epi_add, extras=(dh,))
    dh = _mm2d("d_h_g1", dpg1, wg1, "nt", 512, 1024, 4096, epilogue=_epi_add, extras=(dh,))
    dx1, dx1b, dG["g_mix"] = _rmsnorm_bwd("mix_norm_bwd", dh, x1, w["g_mix"], dx2)
    rs_b = sibling_exchange(_RS_GROUPS[1])
    rs_b2 = []

    def attention_grads_summed(_):
        rs_b2.append(chip_exchange(_RS_GROUPS[1], rs_b))
        return rs_b2[0]

    rs_c2 = []

    def ffn1_grads_done(dws):
        dW["w1_gate"], dW["w1_up"], dW["w1_down"] = dws
        rs_c = sibling_exchange(_RS_GROUPS[2])
        _run_exchange("rs_first_w1", rs_c)
        rs_c2.append(chip_exchange(_RS_GROUPS[2], rs_c))
        return rs_c2[0]

    dx0, _, dG["g_ffn1"], _, _, _ = _ffn_backward(
        "ffn1", dx1, dx1b, xs, w["g_ffn1"], gathered["w1_gate"], gathered["w1_up"], gathered["w1_down"], ffn1,
        comm_act=rs_b, comm_dn=attention_grads_summed, comm_norm=ffn1_grads_done, weights_first=False)
    update(_RS_GROUPS[1], rs_b2[0])
    update(_RS_GROUPS[2], rs_c2[0])

    parts = _allgather("ag_gain_grads", [_pack_gains(dG)])[0]
    packed = _adamw("adamw_gains", parts, _pack_gains(W), _pack_gains(M), _pack_gains(V))
    for out, pk in zip((grads, deltas, new_m, new_v), packed):
        out.update(_unpack_gains(pk, W))

    return (loss, dx0[None], *[grads[n] for n in _WEIGHTS], *[deltas[n] for n in _WEIGHTS],
            *[new_m[n] for n in _WEIGHTS], *[new_v[n] for n in _WEIGHTS])


def kernel(x, p, positions, g_ffn1, w1_gate, w1_up, w1_down, g_mix, w_in, g_cq, w_uq, g_ckv, w_ukv, g_q_mla, g_k_mla, g_q_dil, g_k_dil, w_br_mla, w_br_dil, w_o, g_ffn2, w2_gate, w2_up, w2_down, g_ple, w_ple_gate, w_ple_proj, loss_target, m_g_ffn1, m_w1_gate, m_w1_up, m_w1_down, m_g_mix, m_w_in, m_g_cq, m_w_uq, m_g_ckv, m_w_ukv, m_g_q_mla, m_g_k_mla, m_g_q_dil, m_g_k_dil, m_w_br_mla, m_w_br_dil, m_w_o, m_g_ffn2, m_w2_gate, m_w2_up, m_w2_down, m_g_ple, m_w_ple_gate, m_w_ple_proj, v_g_ffn1, v_w1_gate, v_w1_up, v_w1_down, v_g_mix, v_w_in, v_g_cq, v_w_uq, v_g_ckv, v_w_ukv, v_g_q_mla, v_g_k_mla, v_g_q_dil, v_g_k_dil, v_w_br_mla, v_w_br_dil, v_w_o, v_g_ffn2, v_w2_gate, v_w2_up, v_w2_down, v_g_ple, v_w_ple_gate, v_w_ple_proj):
    W = dict(zip(_WEIGHTS, (g_ffn1, w1_gate, w1_up, w1_down, g_mix, w_in, g_cq, w_uq, g_ckv, w_ukv, g_q_mla, g_k_mla, g_q_dil,
                            g_k_dil, w_br_mla, w_br_dil, w_o, g_ffn2, w2_gate, w2_up, w2_down, g_ple, w_ple_gate, w_ple_proj)))
    M = dict(zip(_WEIGHTS, (m_g_ffn1, m_w1_gate, m_w1_up, m_w1_down, m_g_mix, m_w_in, m_g_cq, m_w_uq, m_g_ckv, m_w_ukv, m_g_q_mla,
                            m_g_k_mla, m_g_q_dil, m_g_k_dil, m_w_br_mla, m_w_br_dil, m_w_o, m_g_ffn2, m_w2_gate, m_w2_up,
                            m_w2_down, m_g_ple, m_w_ple_gate, m_w_ple_proj)))
    V = dict(zip(_WEIGHTS, (v_g_ffn1, v_w1_gate, v_w1_up, v_w1_down, v_g_mix, v_w_in, v_g_cq, v_w_uq, v_g_ckv, v_w_ukv, v_g_q_mla,
                            v_g_k_mla, v_g_q_dil, v_g_k_dil, v_w_br_mla, v_w_br_dil, v_w_o, v_g_ffn2, v_w2_gate, v_w2_up,
                            v_w2_down, v_g_ple, v_w_ple_gate, v_w_ple_proj)))
    return _train_step(x, p, positions, loss_target, W, M, V)
```

```python
import numpy as np
import jax
import jax.numpy as jnp
from jax import lax
from jax.experimental import pallas as pl
from jax.experimental.pallas import tpu as pltpu

F32 = jnp.float32
BF16 = jnp.bfloat16

EPS = 1e-6
ROPE_THETA = 500000.0
MLA_NOPE = 128
MLA_ROPE = 64
MLA_QK = MLA_NOPE + MLA_ROPE
MLA_PAD = 256
HEAD = 128
DIL_ROT = 32
DIL_GROUPS = ((128, 1), (512, 4), (2048, 16))
NEG = -1e30
NO_WINDOW = 1 << 30
N_DEV = 8
ADAM_LR, ADAM_B1, ADAM_B2, ADAM_EPS, ADAM_WD, ADAM_STEP = 0.001, 0.9, 0.999, 1e-08, 0.01, 10
VMEM_LIMIT_V7X = 56 * 1024 * 1024
MESH = pl.DeviceIdType.MESH
HBM_SPEC = pl.BlockSpec(memory_space=pltpu.HBM)


def _cp(n_axes):
    return pltpu.CompilerParams(dimension_semantics=("arbitrary",) * n_axes,
                                vmem_limit_bytes=VMEM_LIMIT_V7X)


def _tile(n, t):
    return t if (n >= t and n % t == 0) else n


def _sigmoid(x):
    return 1.0 / (1.0 + jnp.exp(-x))


_DIMS = {"nn": (((1,), (0,)), ((), ())), "nt": (((1,), (1,)), ((), ())), "tn": (((0,), (0,)), ((), ()))}


def _dot(a, b, mode):
    return lax.dot_general(a.astype(BF16), b.astype(BF16), _DIMS[mode], preferred_element_type=F32)


class _Exchange:
    def __init__(self, ins, out_shapes, plan, n_remote, n_local, aliases=None):
        self.ins, self.out_shapes, self.plan = list(ins), list(out_shapes), plan
        self.n_remote, self.n_local, self.aliases = n_remote, n_local, dict(aliases or {})
        self.results = None

    def sems(self):
        return [pltpu.SemaphoreType.DMA((self.n_remote,)), pltpu.SemaphoreType.DMA((self.n_remote,)),
                pltpu.SemaphoreType.DMA((max(self.n_local, 1),))]

    def start(self, in_refs, out_refs, sems):
        remote, local = self.plan(in_refs, out_refs)
        for i, (src, dst_there, _, peer) in enumerate(remote):
            pltpu.make_async_remote_copy(src_ref=src, dst_ref=dst_there, send_sem=sems[0].at[i], recv_sem=sems[1].at[i],
                                         device_id=peer, device_id_type=MESH).start()
        for i, (src, dst) in enumerate(local):
            pltpu.make_async_copy(src, dst, sems[2].at[i]).start()

    def finish(self, in_refs, out_refs, sems):
        remote, local = self.plan(in_refs, out_refs)
        for i, (src, _, dst_here, peer) in enumerate(remote):
            pltpu.make_async_remote_copy(src_ref=src, dst_ref=dst_here, send_sem=sems[0].at[i], recv_sem=sems[1].at[i],
                                         device_id=peer, device_id_type=MESH).wait_recv()
        for i, (src, dst_there, _, peer) in enumerate(remote):
            pltpu.make_async_remote_copy(src_ref=src, dst_ref=dst_there, send_sem=sems[0].at[i], recv_sem=sems[1].at[i],
                                         device_id=peer, device_id_type=MESH).wait_send()
        for i, (src, dst) in enumerate(local):
            pltpu.make_async_copy(src, dst, sems[2].at[i]).wait()


def _run_exchange(name, ex):
    ci = len(ex.ins)

    def body(*refs):
        ins, outs, sems = refs[:ci], refs[ci:ci + len(ex.out_shapes)], refs[ci + len(ex.out_shapes):]
        ex.start(ins, outs, sems)
        ex.finish(ins, outs, sems)

    ex.results = pl.pallas_call(body, name=name, in_specs=[HBM_SPEC] * ci, out_specs=[HBM_SPEC] * len(ex.out_shapes),
                                out_shape=ex.out_shapes, scratch_shapes=ex.sems(), input_output_aliases=ex.aliases)(*ex.ins)
    return ex.results


def _call(body, *, name, grid, in_specs, out_specs, out_shape, operands, scratch=(), comm=None):
    multi = isinstance(out_shape, (list, tuple))
    outs = list(out_shape) if multi else [out_shape]
    ospecs = list(out_specs) if multi else [out_specs]
    if comm is None:
        res = pl.pallas_call(body, name=name, grid=grid, in_specs=list(in_specs), out_specs=ospecs, out_shape=outs,
                             scratch_shapes=list(scratch), compiler_params=_cp(len(grid)))(*operands)
        return res if multi else res[0]
    n_in, n_out, n_scr = len(in_specs), len(outs), len(scratch)
    ci, co = len(comm.ins), len(comm.out_shapes)

    def hosted(*refs):
        bounds = np.cumsum([0, n_in, ci, n_out, co, n_scr])
        ins, cins, os_, cos, scr = (refs[bounds[i]:bounds[i + 1]] for i in range(5))
        sems = refs[bounds[5]:]
        ids = [pl.program_id(a) for a in range(len(grid))]
        first, last = ids[0] == 0, ids[0] == grid[0] - 1
        for a in range(1, len(grid)):
            first, last = first & (ids[a] == 0), last & (ids[a] == grid[a] - 1)

        @pl.when(first)
        def _():
            comm.start(cins, cos, sems)

        body(*ins, *os_, *scr)

        @pl.when(last)
        def _():
            comm.finish(cins, cos, sems)

    res = pl.pallas_call(hosted, name=name, grid=grid, in_specs=[*in_specs, *[HBM_SPEC] * ci],
                         out_specs=[*ospecs, *[HBM_SPEC] * co], out_shape=[*outs, *comm.out_shapes],
                         scratch_shapes=[*scratch, *comm.sems()],
                         input_output_aliases={n_in + i: n_out + o for i, o in comm.aliases.items()},
                         compiler_params=_cp(len(grid)))(*operands, *comm.ins)
    comm.results = res[n_out:]
    return res[:n_out] if multi else res[0]


def _mm(name, grid, ins, in_specs, pairs, n_acc, acc_shape, epilogue, out_shapes, out_specs, k_axis, comm=None):
    n_in, n_out = len(ins), len(out_shapes)
    nk = grid[k_axis] if k_axis is not None else 1

    def body(*refs):
        in_refs, out_refs, acc_refs = refs[:n_in], refs[n_in:n_in + n_out], refs[n_in + n_out:]
        parts = [None] * n_acc
        for ai, bi, mode, ci in pairs:
            d = _dot(in_refs[ai][...], in_refs[bi][...], mode)
            parts[ci] = d if parts[ci] is None else parts[ci] + d
        if nk == 1:
            epilogue(parts, in_refs, out_refs)
            return
        k = pl.program_id(k_axis)

        @pl.when(k == 0)
        def _():
            for c in range(n_acc):
                acc_refs[c][...] = parts[c]

        @pl.when(k > 0)
        def _():
            for c in range(n_acc):
                acc_refs[c][...] += parts[c]

        @pl.when(k == nk - 1)
        def _():
            epilogue([r[...] for r in acc_refs], in_refs, out_refs)

    scratch = [pltpu.VMEM(acc_shape, F32) for _ in range(n_acc)] if nk > 1 else []
    return _call(body, name=name, grid=grid, in_specs=in_specs, out_specs=list(out_specs), out_shape=list(out_shapes),
                 operands=ins, scratch=scratch, comm=comm)


def _mm2d(name, a, b, mode, tm, tn, tk, out_dtypes=(F32,), epilogue=None, extras=()):
    if mode == "nn":
        (M, K), N = a.shape, b.shape[1]
    elif mode == "nt":
        (M, K), N = a.shape, b.shape[0]
    else:
        (K, M), N = a.shape, b.shape[1]
    tm, tn, tk = _tile(M, tm), _tile(N, tn), _tile(K, tk)
    a_spec = pl.BlockSpec((tk, tm), lambda i, j, k: (k, i)) if mode == "tn" else pl.BlockSpec((tm, tk), lambda i, j, k: (i, k))
    b_spec = pl.BlockSpec((tn, tk), lambda i, j, k: (j, k)) if mode == "nt" else pl.BlockSpec((tk, tn), lambda i, j, k: (k, j))
    mn_spec = pl.BlockSpec((tm, tn), lambda i, j, k: (i, j))
    n_ex = len(extras)

    def default_epilogue(acc, ex_refs, out_refs):
        out_refs[0][...] = acc.astype(out_refs[0].dtype)

    epi = epilogue or default_epilogue

    def wrapped(accs, in_refs, out_refs):
        epi(accs[0], in_refs[2:2 + n_ex], out_refs)

    outs = _mm(name, (M // tm, N // tn, K // tk), [a, b, *extras], [a_spec, b_spec] + [mn_spec] * n_ex,
               [(0, 1, mode, 0)], 1, (tm, tn), wrapped,
               [jax.ShapeDtypeStruct((M, N), dt) for dt in out_dtypes], [mn_spec] * len(out_dtypes), 2)
    return outs[0] if len(out_dtypes) == 1 else outs


def _rms_stats(x, n):
    return lax.rsqrt(jnp.sum(x * x, axis=-1, keepdims=True) * (1.0 / n) + EPS)


def _rmsnorm_fwd(name, x, g, tm=512):
    T, D = x.shape
    tm = _tile(T, tm)

    def body(x_ref, g_ref, o_ref):
        xv = x_ref[...]
        o_ref[...] = (xv * _rms_stats(xv, D) * g_ref[...]).astype(BF16)

    return pl.pallas_call(body, name=name, grid=(T // tm,),
                          in_specs=[pl.BlockSpec((tm, D), lambda i: (i, 0)), pl.BlockSpec((1, D), lambda i: (0, 0))],
                          out_specs=pl.BlockSpec((tm, D), lambda i: (i, 0)),
                          out_shape=jax.ShapeDtypeStruct((T, D), BF16), compiler_params=_cp(1))(x, g)


def _rms_bwd_rows(dy, x, g, n):
    r = _rms_stats(x, n)
    xh = x * r
    gd = dy * g
    mean = jnp.sum(gd * xh, axis=-1, keepdims=True) * (1.0 / n)
    return r * (gd - xh * mean), dy * xh


def _rmsnorm_bwd(name, dn, x, g, res, tm=256, comm=None):
    T, D = x.shape
    tm = _tile(T, tm)

    def body(dn_ref, x_ref, g_ref, res_ref, dx_ref, dxb_ref, dg_ref):
        dx, dgp = _rms_bwd_rows(dn_ref[...].astype(F32), x_ref[...], g_ref[...], D)
        dx = dx + res_ref[...]
        dx_ref[...] = dx
        dxb_ref[...] = dx.astype(BF16)

        @pl.when(pl.program_id(0) == 0)
        def _():
            dg_ref[...] = jnp.zeros_like(dg_ref)

        dg_ref[...] += jnp.sum(dgp, axis=0, keepdims=True)

    row = pl.BlockSpec((tm, D), lambda i: (i, 0))
    vec = pl.BlockSpec((1, D), lambda i: (0, 0))
    return _call(body, name=name, grid=(T // tm,), in_specs=[row, row, vec, row], out_specs=[row, row, vec],
                 out_shape=[jax.ShapeDtypeStruct((T, D), F32), jax.ShapeDtypeStruct((T, D), BF16), jax.ShapeDtypeStruct((1, D), F32)],
                 operands=[dn, x, g, res], comm=comm)


def _rope_tables(pos_col, inv_mla, inv_dil, tm=512):
    T = pos_col.shape[0]
    tm = _tile(T, tm)

    def body(p_ref, im_ref, id_ref, cm, sam, sbm, cd, sad, sbd):
        lane = lax.broadcasted_iota(jnp.int32, (tm, 128), 1)
        p = p_ref[...]
        am = p * im_ref[...]
        c, s = jnp.cos(am), jnp.sin(am)
        cm[...] = jnp.where(lane < 64, c, 0.0)
        sam[...] = jnp.where(lane < 32, -s, 0.0)
        sbm[...] = jnp.where((lane >= 32) & (lane < 64), s, 0.0)
        ad = p * id_ref[...]
        c, s = jnp.cos(ad), jnp.sin(ad)
        cd[...] = jnp.where(lane < 32, c, 1.0)
        sad[...] = jnp.where(lane < 16, -s, 0.0)
        sbd[...] = jnp.where((lane >= 16) & (lane < 32), s, 0.0)

    row = pl.BlockSpec((tm, 128), lambda i: (i, 0))
    vec = pl.BlockSpec((1, 128), lambda i: (0, 0))
    return pl.pallas_call(body, name="rope_tables", grid=(T // tm,),
                          in_specs=[pl.BlockSpec((tm, 1), lambda i: (i, 0)), vec, vec], out_specs=[row] * 6,
                          out_shape=[jax.ShapeDtypeStruct((T, 128), F32)] * 6, compiler_params=_cp(1))(pos_col, inv_mla, inv_dil)


def _rope(v, c, sa, sb, sh):
    return v * c + pltpu.roll(v, 128 - sh, 1) * sa + pltpu.roll(v, sh, 1) * sb


def _rope_t(d, c, sa, sb, sh):
    return d * c + pltpu.roll(d * sa, sh, 1) + pltpu.roll(d * sb, 128 - sh, 1)


def _lora_fwd(pa, g_cq, g_ckv, nq, nkv, tm=512):
    T, W = pa.shape
    tm = _tile(T, tm)

    def body(pa_ref, gq_ref, gk_ref, cq_ref, ckv_ref):
        a = pa_ref[:, :nq]
        cq_ref[...] = (a * _rms_stats(a, nq) * gq_ref[...]).astype(BF16)
        b = pa_ref[:, nq:nq + nkv]
        ckv_ref[...] = (b * _rms_stats(b, nkv) * gk_ref[...]).astype(BF16)

    return pl.pallas_call(body, name="lora_fwd", grid=(T // tm,),
                          in_specs=[pl.BlockSpec((tm, W), lambda i: (i, 0)), pl.BlockSpec((1, nq), lambda i: (0, 0)),
                                    pl.BlockSpec((1, nkv), lambda i: (0, 0))],
                          out_specs=[pl.BlockSpec((tm, nq), lambda i: (i, 0)), pl.BlockSpec((tm, nkv), lambda i: (i, 0))],
                          out_shape=[jax.ShapeDtypeStruct((T, nq), BF16), jax.ShapeDtypeStruct((T, nkv), BF16)],
                          compiler_params=_cp(1))(pa, g_cq, g_ckv)


def _lora_bwd(dcq, dckv, dkr, pa, g_cq, g_ckv, tm=512):
    T, W = pa.shape
    nq, nkv = dcq.shape[1], dckv.shape[1]
    tm = _tile(T, tm)

    def body(dcq_ref, dckv_ref, dkr_ref, pa_ref, gq_ref, gk_ref, dpa_ref, dgq_ref, dgk_ref):
        dx, dgp = _rms_bwd_rows(dcq_ref[...], pa_ref[:, :nq], gq_ref[...], nq)
        dpa_ref[:, :nq] = dx.astype(BF16)
        dx2, dgp2 = _rms_bwd_rows(dckv_ref[...], pa_ref[:, nq:nq + nkv], gk_ref[...], nkv)
        dpa_ref[:, nq:nq + nkv] = dx2.astype(BF16)
        dpa_ref[:, nq + nkv:] = dkr_ref[...].astype(BF16)

        @pl.when(pl.program_id(0) == 0)
        def _():
            dgq_ref[...] = jnp.zeros_like(dgq_ref)
            dgk_ref[...] = jnp.zeros_like(dgk_ref)

        dgq_ref[...] += jnp.sum(dgp, axis=0, keepdims=True)
        dgk_ref[...] += jnp.sum(dgp2, axis=0, keepdims=True)

    def row(n):
        return pl.BlockSpec((tm, n), lambda i: (i, 0))

    def vec(n):
        return pl.BlockSpec((1, n), lambda i: (0, 0))

    return pl.pallas_call(body, name="lora_bwd", grid=(T // tm,),
                          in_specs=[row(nq), row(nkv), row(128), row(W), vec(nq), vec(nkv)],
                          out_specs=[row(W), vec(nq), vec(nkv)],
                          out_shape=[jax.ShapeDtypeStruct((T, W), BF16), jax.ShapeDtypeStruct((1, nq), F32),
                                     jax.ShapeDtypeStruct((1, nkv), F32)],
                          compiler_params=_cp(1))(dcq, dckv, dkr, pa, g_cq, g_ckv)


def _sumsq(v):
    return jnp.sum(v * v, axis=-1, keepdims=True)


def _mla_prep_fwd(q_raw, kv, pa, tabs, gq, gk, H, kr_block, tm=256):
    T = q_raw.shape[0]
    tm = _tile(T, tm)
    scale = MLA_QK ** -0.5
    P = MLA_PAD

    def body(q_ref, kv_ref, kr_ref, c_ref, sa_ref, sb_ref, gq_ref, gk_ref, qo, ko, vo):
        c, sa, sb = c_ref[...], sa_ref[...], sb_ref[...]
        kr = kr_ref[...]
        kr2 = _sumsq(kr)
        for h in range(H):
            lo, hi = q_ref[:, h * P:h * P + 128], q_ref[:, h * P + 128:(h + 1) * P]
            r = lax.rsqrt((_sumsq(lo) + _sumsq(hi)) * (1.0 / MLA_QK) + EPS)
            qo[:, h * P:h * P + 128] = (lo * r * gq_ref[:, :128] * scale).astype(BF16)
            qo[:, h * P + 128:(h + 1) * P] = (_rope(hi * r * gq_ref[:, 128:], c, sa, sb, 32) * scale).astype(BF16)
            kn = kv_ref[:, h * P:h * P + 128]
            r = lax.rsqrt((_sumsq(kn) + kr2) * (1.0 / MLA_QK) + EPS)
            ko[:, h * P:h * P + 128] = (kn * r * gk_ref[:, :128]).astype(BF16)
            ko[:, h * P + 128:(h + 1) * P] = _rope(kr * r * gk_ref[:, 128:], c, sa, sb, 32).astype(BF16)
            vo[:, h * 128:(h + 1) * 128] = kv_ref[:, h * P + 128:(h + 1) * P].astype(BF16)

    wide = pl.BlockSpec((tm, H * P), lambda i: (i, 0))
    lane = pl.BlockSpec((tm, 128), lambda i: (i, 0))
    vec = pl.BlockSpec((1, P), lambda i: (0, 0))
    return pl.pallas_call(body, name="mla_prep_fwd", grid=(T // tm,),
                          in_specs=[wide, wide, pl.BlockSpec((tm, 128), lambda i: (i, kr_block)), lane, lane, lane, vec, vec],
                          out_specs=[wide, wide, pl.BlockSpec((tm, H * 128), lambda i: (i, 0))],
                          out_shape=[jax.ShapeDtypeStruct((T, H * P), BF16), jax.ShapeDtypeStruct((T, H * P), BF16),
                                     jax.ShapeDtypeStruct((T, H * 128), BF16)],
                          compiler_params=_cp(1))(q_raw, kv, pa, *tabs, gq, gk)


def _mla_prep_bwd(dq, dk, dv, q_raw, kv, pa, tabs, gq, gk, H, kr_block, tm=256):
    T = q_raw.shape[0]
    tm = _tile(T, tm)
    scale = MLA_QK ** -0.5
    P = MLA_PAD

    def body(dq_ref, dk_ref, dv_ref, q_ref, kv_ref, kr_ref, c_ref, sa_ref, sb_ref, gq_ref, gk_ref,
             dqr, dkv, dkr, dgq, dgk):
        c, sa, sb = c_ref[...], sa_ref[...], sb_ref[...]
        kr = kr_ref[...]
        kr2 = _sumsq(kr)
        gql, gqh, gkl, gkh = gq_ref[:, :128], gq_ref[:, 128:], gk_ref[:, :128], gk_ref[:, 128:]
        dkr_acc = jnp.zeros((tm, 128), F32)
        sums = [jnp.zeros((1, 128), F32) for _ in range(4)]
        for h in range(H):
            lo_s, hi_s = slice(h * P, h * P + 128), slice(h * P + 128, (h + 1) * P)
            lo, hi = q_ref[:, lo_s], q_ref[:, hi_s]
            r = lax.rsqrt((_sumsq(lo) + _sumsq(hi)) * (1.0 / MLA_QK) + EPS)
            ql, qh = lo * r, hi * r
            dyl = dq_ref[:, lo_s] * scale
            dyh = _rope_t(dq_ref[:, hi_s] * scale, c, sa, sb, 32)
            gl, gh = dyl * gql, dyh * gqh
            mean = (jnp.sum(gl * ql, axis=-1, keepdims=True) + jnp.sum(gh * qh, axis=-1, keepdims=True)) * (1.0 / MLA_QK)
            dqr[:, lo_s] = (r * (gl - ql * mean)).astype(BF16)
            dqr[:, hi_s] = (r * (gh - qh * mean)).astype(BF16)
            sums[0] += jnp.sum(dyl * ql, axis=0, keepdims=True)
            sums[1] += jnp.sum(dyh * qh, axis=0, keepdims=True)
            kn = kv_ref[:, lo_s]
            r = lax.rsqrt((_sumsq(kn) + kr2) * (1.0 / MLA_QK) + EPS)
            kl, kh = kn * r, kr * r
            dkl = dk_ref[:, lo_s]
            dkh = _rope_t(dk_ref[:, hi_s], c, sa, sb, 32)
            gl, gh = dkl * gkl, dkh * gkh
            mean = (jnp.sum(gl * kl, axis=-1, keepdims=True) + jnp.sum(gh * kh, axis=-1, keepdims=True)) * (1.0 / MLA_QK)
            dkv[:, lo_s] = (r * (gl - kl * mean)).astype(BF16)
            dkr_acc += r * (gh - kh * mean)
            dkv[:, hi_s] = dv_ref[:, h * 128:(h + 1) * 128].astype(BF16)
            sums[2] += jnp.sum(dkl * kl, axis=0, keepdims=True)
            sums[3] += jnp.sum(dkh * kh, axis=0, keepdims=True)
        dkr[...] = dkr_acc

        @pl.when(pl.program_id(0) == 0)
        def _():
            dgq[...] = jnp.zeros_like(dgq)
            dgk[...] = jnp.zeros_like(dgk)

        dgq[:, :128] += sums[0]
        dgq[:, 128:] += sums[1]
        dgk[:, :128] += sums[2]
        dgk[:, 128:] += sums[3]

    wide = pl.BlockSpec((tm, H * P), lambda i: (i, 0))
    lane = pl.BlockSpec((tm, 128), lambda i: (i, 0))
    vec = pl.BlockSpec((1, P), lambda i: (0, 0))
    return pl.pallas_call(body, name="mla_prep_bwd", grid=(T // tm,),
                          in_specs=[wide, wide, pl.BlockSpec((tm, H * 128), lambda i: (i, 0)), wide, wide,
                                    pl.BlockSpec((tm, 128), lambda i: (i, kr_block)), lane, lane, lane, vec, vec],
                          out_specs=[wide, wide, lane, vec, vec],
                          out_shape=[jax.ShapeDtypeStruct((T, H * P), BF16), jax.ShapeDtypeStruct((T, H * P), BF16),
                                     jax.ShapeDtypeStruct((T, 128), F32), jax.ShapeDtypeStruct((1, P), F32),
                                     jax.ShapeDtypeStruct((1, P), F32)],
                          compiler_params=_cp(1))(dq, dk, dv, q_raw, kv, pa, *tabs, gq, gk)


def _on_blocks(s, live, step):
    @pl.when(s == 0)
    def _():
        step(True)

    @pl.when((s > 0) & live)
    def _():
        step(False)


def _sub_mask(sub, width, row0, col0, q_axis):
    r = lax.broadcasted_iota(jnp.int32, (sub, width), 0) + row0
    c = lax.broadcasted_iota(jnp.int32, (sub, width), 1) + col0
    return (r >= c) if q_axis == 0 else (c >= r)


def _tri_pos(t, nq):
    row = 0
    for k in range(1, nq):
        row = row + (t >= k * (k + 1) // 2).astype(jnp.int32)
    return row, t - (row * (row + 1)) // 2


def _mla_fwd(q, k, v, H, blk, comm=None):
    T, wq = q.shape[0], q.shape[1] // H
    nq = T // blk
    sub = _tile(blk, MLA_SUB)

    def body(q_ref, k_ref, v_ref, o_ref, lse_ref, m_sc, l_sc, acc_sc):
        i, s = _tri_pos(pl.program_id(1), nq)

        @pl.when(s == 0)
        def _():
            m_sc[...] = jnp.full_like(m_sc, NEG)
            l_sc[...] = jnp.zeros_like(l_sc)
            acc_sc[...] = jnp.zeros_like(acc_sc)

        def step(masked):
            for r in range(blk // sub):
                rs = slice(r * sub, (r + 1) * sub)
                nk = (r + 1) * sub if masked else blk
                sc = _dot(q_ref[rs, :], k_ref[:nk, :], "nt")
                if masked:
                    sc = jnp.where(_sub_mask(sub, nk, r * sub, 0, 0), sc, NEG)
                m_prev = m_sc[rs, :]
                m_new = jnp.maximum(m_prev, jnp.max(sc, axis=-1, keepdims=True))
                p = jnp.exp(sc - m_new)
                alpha = jnp.exp(m_prev - m_new)
                l_sc[rs, :] = alpha * l_sc[rs, :] + jnp.sum(p, axis=-1, keepdims=True)
                acc_sc[rs, :] = alpha * acc_sc[rs, :] + _dot(p, v_ref[:nk, :], "nn")
                m_sc[rs, :] = m_new

        _on_blocks(s, i - s >= 0, step)

        @pl.when(s == i)
        def _():
            o_ref[...] = (acc_sc[...] / l_sc[...]).astype(BF16)
            lse_ref[...] = m_sc[...] + jnp.log(l_sc[...])

    qi = lambda t: _tri_pos(t, nq)[0]
    kj = lambda t: _tri_pos(t, nq)[0] - _tri_pos(t, nq)[1]
    kv_spec = lambda w: pl.BlockSpec((blk, w), lambda h, t: (kj(t), h))
    return _call(
        body, name="mla_fwd", grid=(H, nq * (nq + 1) // 2),
        in_specs=[pl.BlockSpec((blk, wq), lambda h, t: (qi(t), h)), kv_spec(wq), kv_spec(HEAD)],
        out_specs=[pl.BlockSpec((blk, HEAD), lambda h, t: (qi(t), h)), pl.BlockSpec((None, blk, 1), lambda h, t: (h, qi(t), 0))],
        out_shape=[jax.ShapeDtypeStruct((T, H * HEAD), BF16), jax.ShapeDtypeStruct((H, T, 1), F32)],
        scratch=[pltpu.VMEM((blk, 1), F32), pltpu.VMEM((blk, 1), F32), pltpu.VMEM((blk, HEAD), F32)],
        operands=[q, k, v], comm=comm)


def _mla_dq(q, k, v, do, lse, dl, H, blk, comm=None):
    T, wq = q.shape[0], q.shape[1] // H
    nq = T // blk
    sub = _tile(blk, MLA_SUB)

    def body(q_ref, k_ref, v_ref, do_ref, lse_ref, dl_ref, dq_ref, acc_sc):
        i, s = _tri_pos(pl.program_id(1), nq)

        @pl.when(s == 0)
        def _():
            acc_sc[...] = jnp.zeros_like(acc_sc)

        def step(masked):
            for r in range(blk // sub):
                rs = slice(r * sub, (r + 1) * sub)
                nk = (r + 1) * sub if masked else blk
                sc = _dot(q_ref[rs, :], k_ref[:nk, :], "nt")
                if masked:
                    sc = jnp.where(_sub_mask(sub, nk, r * sub, 0, 0), sc, NEG)
                p = jnp.exp(sc - lse_ref[rs, :])
                dp = _dot(do_ref[rs, :], v_ref[:nk, :], "nt")
                acc_sc[rs, :] += _dot(p * (dp - dl_ref[rs, :]), k_ref[:nk, :], "nn")

        _on_blocks(s, i - s >= 0, step)

        @pl.when(s == i)
        def _():
            dq_ref[...] = acc_sc[...]

    qi = lambda t: _tri_pos(t, nq)[0]
    kj = lambda t: _tri_pos(t, nq)[0] - _tri_pos(t, nq)[1]
    kv_spec = lambda w: pl.BlockSpec((blk, w), lambda h, t: (kj(t), h))
    col = pl.BlockSpec((None, blk, 1), lambda h, t: (h, qi(t), 0))
    return _call(
        body, name="mla_dq", grid=(H, nq * (nq + 1) // 2),
        in_specs=[pl.BlockSpec((blk, wq), lambda h, t: (qi(t), h)), kv_spec(wq), kv_spec(HEAD),
                  pl.BlockSpec((blk, HEAD), lambda h, t: (qi(t), h)), col, col],
        out_specs=pl.BlockSpec((blk, wq), lambda h, t: (qi(t), h)), out_shape=jax.ShapeDtypeStruct((T, H * wq), F32),
        scratch=[pltpu.VMEM((blk, wq), F32)], operands=[q, k, v, do, lse, dl], comm=comm)


def _as_lanes(col):
    return jnp.transpose(jnp.broadcast_to(col, (col.shape[0], 128)))[0:1, :]


def _mla_dkv(q, k, v, do, lse, dl, H, blk):
    T, wq = q.shape[0], q.shape[1] // H
    nq = T // blk
    sub = _tile(blk, MLA_SUB)

    def body(q_ref, k_ref, v_ref, do_ref, lse_ref, dl_ref, dk_ref, dv_ref, dk_sc, dv_sc):
        left, s = _tri_pos(pl.program_id(1), nq)

        @pl.when(s == 0)
        def _():
            dk_sc[...] = jnp.zeros_like(dk_sc)
            dv_sc[...] = jnp.zeros_like(dv_sc)

        def step(masked):
            for r in range(blk // sub):
                rs = slice(r * sub, (r + 1) * sub)
                q0 = r * sub if masked else 0
                st = _dot(k_ref[rs, :], q_ref[q0:, :], "nt")
                if masked:
                    st = jnp.where(_sub_mask(sub, blk - q0, r * sub, q0, 1), st, NEG)
                pt = jnp.exp(st - lse_ref[:, q0:])
                dpt = _dot(v_ref[rs, :], do_ref[q0:, :], "nt")
                dv_sc[rs, :] += _dot(pt, do_ref[q0:, :], "nn")
                dk_sc[rs, :] += _dot(pt * (dpt - dl_ref[:, q0:]), q_ref[q0:, :], "nn")

        _on_blocks(s, s <= left, step)

        @pl.when(s == left)
        def _():
            dk_ref[...] = dk_sc[...]
            dv_ref[...] = dv_sc[...]

    kj = lambda t: nq - 1 - _tri_pos(t, nq)[0]
    qi = lambda t: nq - 1 - _tri_pos(t, nq)[0] + _tri_pos(t, nq)[1]
    row = pl.BlockSpec((None, 1, blk), lambda h, t: (h, 0, qi(t)))
    return pl.pallas_call(
        body, name="mla_dkv", grid=(H, nq * (nq + 1) // 2),
        in_specs=[pl.BlockSpec((blk, wq), lambda h, t: (qi(t), h)), pl.BlockSpec((blk, wq), lambda h, t: (kj(t), h)),
                  pl.BlockSpec((blk, HEAD), lambda h, t: (kj(t), h)), pl.BlockSpec((blk, HEAD), lambda h, t: (qi(t), h)),
                  row, row],
        out_specs=[pl.BlockSpec((blk, wq), lambda h, t: (kj(t), h)), pl.BlockSpec((blk, HEAD), lambda h, t: (kj(t), h))],
        out_shape=[jax.ShapeDtypeStruct((T, H * wq), F32), jax.ShapeDtypeStruct((T, H * HEAD), F32)],
        scratch_shapes=[pltpu.VMEM((blk, wq), F32), pltpu.VMEM((blk, HEAD), F32)],
        compiler_params=_cp(2))(q, k, v, do, lse, dl)


def _row_dot(name, a, b, H, tm=1024):
    T = a.shape[0]
    tm = _tile(T, tm)

    def body(a_ref, b_ref, o_ref):
        o_ref[...] = jnp.sum(a_ref[...].astype(F32) * b_ref[...].astype(F32), axis=-1, keepdims=True)

    blk = pl.BlockSpec((tm, HEAD), lambda h, i: (i, h))
    return pl.pallas_call(body, name=name, grid=(H, T // tm), in_specs=[blk, blk],
                          out_specs=pl.BlockSpec((None, tm, 1), lambda h, i: (h, i, 0)),
                          out_shape=jax.ShapeDtypeStruct((H, T, 1), F32), compiler_params=_cp(2))(a, b)


def _tri(n):
    row = lax.broadcasted_iota(jnp.int32, (n, n), 0)
    col = lax.broadcasted_iota(jnp.int32, (n, n), 1)
    return col <= row, col >= row


def _units(ref, U, n):
    return ref[...].reshape(U, n, ref.shape[-1])


def _shift_in(first, units):
    return jnp.concatenate([first[None], units[:-1]], axis=0)


def _shift_out(units, last):
    return jnp.concatenate([units[1:], last[None]], axis=0)


def _starts_inside(row0, U, n, L, limit=None):
    start = row0 + n * lax.broadcasted_iota(jnp.int32, (U, n, n), 0)
    ok = ((start & (L - 1)) if L & (L - 1) == 0 else (start % L)) != 0
    return ok if limit is None else ok & (start < limit)


def _bdot(a, b, mode):
    dims = (((2,), (2,)), ((0,), (0,))) if mode == "nt" else (((2,), (1,)), ((0,), (0,)))
    return lax.dot_general(a.astype(BF16), b.astype(BF16), dims, preferred_element_type=F32)


def _win_fwd(name, q, k, pd, H, n, L, QB):
    T = q.shape[0]
    U = QB // n

    def body(q_ref, k_ref, v_ref, kp_ref, vp_ref, o_ref, lse_ref):
        own_ok, before_ok = _tri(n)
        q3, k3, v3 = _units(q_ref, U, n), _units(k_ref, U, n), _units(v_ref, U, n)
        k_lo, v_lo = _shift_in(kp_ref[...], k3), _shift_in(vp_ref[...], v3)
        inside = _starts_inside(pl.program_id(1) * QB, U, n, L)
        s_hi = jnp.where(own_ok, _bdot(q3, k3, "nt"), NEG)
        s_lo = jnp.where(before_ok & inside, _bdot(q3, k_lo, "nt"), NEG)
        m = jnp.maximum(jnp.max(s_hi, axis=-1, keepdims=True), jnp.max(s_lo, axis=-1, keepdims=True))
        p_hi, p_lo = jnp.exp(s_hi - m), jnp.exp(s_lo - m)
        l = jnp.sum(p_hi, axis=-1, keepdims=True) + jnp.sum(p_lo, axis=-1, keepdims=True)
        acc = _bdot(p_hi, v3, "nn") + _bdot(p_lo, v_lo, "nn")
        o_ref[...] = (acc / l).reshape(QB, HEAD).astype(BF16)
        lse_ref[...] = (m + jnp.log(l)).reshape(QB, 1)

    cur = lambda c0: pl.BlockSpec((QB, HEAD), lambda h, i: (i, c0 + h))
    prev = lambda c0: pl.BlockSpec((n, HEAD), lambda h, i: (jnp.maximum(i * U - 1, 0), c0 + h))
    return pl.pallas_call(
        body, name=name, grid=(H, T // QB), in_specs=[cur(0), cur(0), cur(2 * H), prev(0), prev(2 * H)],
        out_specs=[pl.BlockSpec((QB, HEAD), lambda h, i: (i, h)), pl.BlockSpec((None, QB, 1), lambda h, i: (h, i, 0))],
        out_shape=[jax.ShapeDtypeStruct((T, H * HEAD), BF16), jax.ShapeDtypeStruct((H, T, 1), F32)],
        compiler_params=_cp(2))(q, k, pd, k, pd)


def _win_dq(name, q, k, pd, do, lse, dl, H, n, L, QB):
    T = q.shape[0]
    U = QB // n

    def body(q_ref, k_ref, v_ref, kp_ref, vp_ref, do_ref, lse_ref, dl_ref, dq_ref):
        own_ok, before_ok = _tri(n)
        q3, k3, v3, do3 = _units(q_ref, U, n), _units(k_ref, U, n), _units(v_ref, U, n), _units(do_ref, U, n)
        lse3, dl3 = _units(lse_ref, U, n), _units(dl_ref, U, n)
        k_lo, v_lo = _shift_in(kp_ref[...], k3), _shift_in(vp_ref[...], v3)
        inside = _starts_inside(pl.program_id(1) * QB, U, n, L)
        p_hi = jnp.exp(jnp.where(own_ok, _bdot(q3, k3, "nt"), NEG) - lse3)
        p_lo = jnp.exp(jnp.where(before_ok & inside, _bdot(q3, k_lo, "nt"), NEG) - lse3)
        ds_hi = p_hi * (_bdot(do3, v3, "nt") - dl3)
        ds_lo = p_lo * (_bdot(do3, v_lo, "nt") - dl3)
        dq_ref[...] = (_bdot(ds_hi, k3, "nn") + _bdot(ds_lo, k_lo, "nn")).reshape(QB, HEAD)

    cur = lambda c0: pl.BlockSpec((QB, HEAD), lambda h, i: (i, c0 + h))
    prev = lambda c0: pl.BlockSpec((n, HEAD), lambda h, i: (jnp.maximum(i * U - 1, 0), c0 + h))
    flat = pl.BlockSpec((QB, HEAD), lambda h, i: (i, h))
    col = pl.BlockSpec((None, QB, 1), lambda h, i: (h, i, 0))
    return pl.pallas_call(
        body, name=name, grid=(H, T // QB), in_specs=[cur(0), cur(0), cur(2 * H), prev(0), prev(2 * H), flat, col, col],
        out_specs=flat, out_shape=jax.ShapeDtypeStruct((T, H * HEAD), F32), compiler_params=_cp(2))(q, k, pd, k, pd, do, lse, dl)


def _win_dkv(name, q, k, pd, do, lse, dl, H, n, L, QB):
    T = q.shape[0]
    U = QB // n
    last = T // n - 1

    def body(q_ref, k_ref, v_ref, do_ref, lse_col, dl_col, qn_ref, don_ref, lsen_col, dln_col, dk_ref, dv_ref):
        own_ok, after_ok = _tri_t(n)
        q3, k3, v3, do3 = _units(q_ref, U, n), _units(k_ref, U, n), _units(v_ref, U, n), _units(do_ref, U, n)
        qn3, don3 = _shift_out(q3, qn_ref[...]), _shift_out(do3, don_ref[...])

        def rows(col_ref, next_col_ref):
            row = _as_lanes(col_ref[...])
            own = [row[:, u * n:(u + 1) * n] for u in range(U)]
            return jnp.stack(own), jnp.stack(own[1:] + [_as_lanes(next_col_ref[...])])

        lse_own, lse_aft = rows(lse_col, lsen_col)
        dl_own, dl_aft = rows(dl_col, dln_col)
        inside = _starts_inside(pl.program_id(1) * QB + n, U, n, L, limit=T)
        pt_own = jnp.exp(jnp.where(own_ok, _bdot(k3, q3, "nt"), NEG) - lse_own)
        pt_aft = jnp.exp(jnp.where(after_ok & inside, _bdot(k3, qn3, "nt"), NEG) - lse_aft)
        dst_own = pt_own * (_bdot(v3, do3, "nt") - dl_own)
        dst_aft = pt_aft * (_bdot(v3, don3, "nt") - dl_aft)
        dv_ref[...] = (_bdot(pt_own, do3, "nn") + _bdot(pt_aft, don3, "nn")).reshape(QB, HEAD)
        dk_ref[...] = (_bdot(dst_own, q3, "nn") + _bdot(dst_aft, qn3, "nn")).reshape(QB, HEAD)

    cur = lambda c0: pl.BlockSpec((QB, HEAD), lambda h, i: (i, c0 + h))
    flat = pl.BlockSpec((QB, HEAD), lambda h, i: (i, h))
    row = pl.BlockSpec((None, QB, 1), lambda h, i: (h, i, 0))
    nxt_unit = lambda h, i: jnp.minimum((i + 1) * U, last)
    return pl.pallas_call(
        body, name=name, grid=(H, T // QB),
        in_specs=[cur(0), cur(0), cur(2 * H), flat, row, row,
                  pl.BlockSpec((n, HEAD), lambda h, i: (nxt_unit(h, i), h)),
                  pl.BlockSpec((n, HEAD), lambda h, i: (nxt_unit(h, i), h)),
                  pl.BlockSpec((None, n, 1), lambda h, i: (h, nxt_unit(h, i), 0)),
                  pl.BlockSpec((None, n, 1), lambda h, i: (h, nxt_unit(h, i), 0))],
        out_specs=[flat, flat], out_shape=[jax.ShapeDtypeStruct((T, H * HEAD), F32)] * 2,
        compiler_params=_cp(2))(q, k, pd, do, lse, dl, q, do, lse, dl)


def _tri_t(n):
    key = lax.broadcasted_iota(jnp.int32, (n, n), 0)
    qry = lax.broadcasted_iota(jnp.int32, (n, n), 1)
    return key <= qry, key >= qry


def _merge_groups(os_, lses, H, tm=2048):
    G = len(os_)
    T = os_[0].shape[0]
    tm = _tile(T, tm)

    def body(*refs):
        o_refs, l_refs, o_out, lse_out = refs[:G], refs[G:2 * G], refs[2 * G], refs[2 * G + 1]
        ls = [r[...] for r in l_refs]
        m = ls[0]
        for x in ls[1:]:
            m = jnp.maximum(m, x)
        ws = [jnp.exp(x - m) for x in ls]
        tot = ws[0]
        for x in ws[1:]:
            tot = tot + x
        acc = ws[0] * o_refs[0][...]
        for x, r in zip(ws[1:], o_refs[1:]):
            acc = acc + x * r[...]
        o_out[...] = (acc / tot).astype(BF16)
        lse_out[...] = m + jnp.log(tot)

    flat = pl.BlockSpec((tm, HEAD), lambda h, i: (i, h))
    col = pl.BlockSpec((None, tm, 1), lambda h, i: (h, i, 0))
    return pl.pallas_call(body, name="dil_merge", grid=(H, T // tm), in_specs=[flat] * G + [col] * G, out_specs=[flat, col],
                          out_shape=[jax.ShapeDtypeStruct((T, H * HEAD), BF16), jax.ShapeDtypeStruct((H, T, 1), F32)],
                          compiler_params=_cp(2))(*os_, *lses)


def _dil_prep_fwd_pm(name, pd, tabs, gq, gk, H, tm=256):
    T = pd.shape[0]
    tm = _tile(T, tm)
    scale = HEAD ** -0.5

    def body(p_ref, c_ref, sa_ref, sb_ref, gq_ref, gk_ref, qo, ko):
        c, sa, sb = c_ref[...], sa_ref[...], sb_ref[...]
        for h in range(H):
            q = p_ref[:, h * 128:(h + 1) * 128].astype(F32)
            qo[:, h * 128:(h + 1) * 128] = (_rope(q * _rms_stats(q, HEAD) * gq_ref[...], c, sa, sb, 16) * scale).astype(BF16)
            k = p_ref[:, (H + h) * 128:(H + h + 1) * 128].astype(F32)
            ko[:, h * 128:(h + 1) * 128] = _rope(k * _rms_stats(k, HEAD) * gk_ref[...], c, sa, sb, 16).astype(BF16)

    lane = pl.BlockSpec((tm, 128), lambda i: (i, 0))
    gain = pl.BlockSpec((1, 128), lambda i: (0, 0))
    out = pl.BlockSpec((tm, H * 128), lambda i: (i, 0))
    return pl.pallas_call(body, name=name, grid=(T // tm,),
                          in_specs=[pl.BlockSpec((tm, 3 * H * 128), lambda i: (i, 0)), lane, lane, lane, gain, gain],
                          out_specs=[out, out], out_shape=[jax.ShapeDtypeStruct((T, H * 128), BF16)] * 2,
                          compiler_params=_cp(1))(pd, *tabs, gq, gk)


def _dil_prep_bwd_pm(name, dq, dk, dv, pd, tabs, gq, gk, H, tm=256):
    T = pd.shape[0]
    tm = _tile(T, tm)
    scale = HEAD ** -0.5

    def body(dq_ref, dk_ref, dv_ref, p_ref, c_ref, sa_ref, sb_ref, gq_ref, gk_ref, dp_ref, dgq, dgk):
        c, sa, sb = c_ref[...], sa_ref[...], sb_ref[...]
        sq = jnp.zeros((1, 128), F32)
        sk = jnp.zeros((1, 128), F32)
        for h in range(H):
            hs = slice(h * 128, (h + 1) * 128)
            q = p_ref[:, hs].astype(F32)
            dx, dgp = _rms_bwd_rows(_rope_t(dq_ref[:, hs] * scale, c, sa, sb, 16), q, gq_ref[...], HEAD)
            dp_ref[:, hs] = dx.astype(BF16)
            sq += jnp.sum(dgp, axis=0, keepdims=True)
            ks = slice((H + h) * 128, (H + h + 1) * 128)
            k = p_ref[:, ks].astype(F32)
            dx, dgp = _rms_bwd_rows(_rope_t(dk_ref[:, hs], c, sa, sb, 16), k, gk_ref[...], HEAD)
            dp_ref[:, ks] = dx.astype(BF16)
            sk += jnp.sum(dgp, axis=0, keepdims=True)
            dp_ref[:, (2 * H + h) * 128:(2 * H + h + 1) * 128] = dv_ref[:, hs].astype(BF16)

        @pl.when(pl.program_id(0) == 0)
        def _():
            dgq[...] = jnp.zeros_like(dgq)
            dgk[...] = jnp.zeros_like(dgk)

        dgq[...] += sq
        dgk[...] += sk

    lane = pl.BlockSpec((tm, 128), lambda i: (i, 0))
    flat = pl.BlockSpec((tm, H * 128), lambda i: (i, 0))
    vec = pl.BlockSpec((1, 128), lambda i: (0, 0))
    return pl.pallas_call(body, name=name, grid=(T // tm,),
                          in_specs=[flat, flat, flat, pl.BlockSpec((tm, 3 * H * 128), lambda i: (i, 0)),
                                    lane, lane, lane, vec, vec],
                          out_specs=[pl.BlockSpec((tm, 3 * H * 128), lambda i: (i, 0)), vec, vec],
                          out_shape=[jax.ShapeDtypeStruct((T, 3 * H * 128), BF16), jax.ShapeDtypeStruct((1, 128), F32),
                                     jax.ShapeDtypeStruct((1, 128), F32)],
                          compiler_params=_cp(1))(dq, dk, dv, pd, *tabs, gq, gk)


def _to_phase(a, d, axis=0):
    if d == 1:
        return a
    sh = a.shape
    T = sh[axis]
    b = a.reshape(*sh[:axis], T // d, d, *sh[axis + 1:])
    return jnp.swapaxes(b, axis, axis + 1).reshape(sh)


def _from_phase(a, d, axis=0):
    if d == 1:
        return a
    sh = a.shape
    T = sh[axis]
    b = a.reshape(*sh[:axis], d, T // d, *sh[axis + 1:])
    return jnp.swapaxes(b, axis, axis + 1).reshape(sh)


def _ffn_up(name, n, wg, wu, tm=1024, comm=None):
    T, D = n.shape
    nd, _, fc = wg.shape
    tm = _tile(T, tm)

    def epilogue(accs, in_refs, out_refs):
        a, b = accs
        out_refs[0][...] = a.astype(BF16)
        out_refs[1][...] = b.astype(BF16)
        out_refs[2][...] = (a * _sigmoid(a) * b).astype(BF16)

    w_spec = pl.BlockSpec((None, D, fc), lambda j, i: (j, 0, 0))
    o_spec = pl.BlockSpec((None, tm, fc), lambda j, i: (j, i, 0))
    sh = jax.ShapeDtypeStruct((nd, T, fc), BF16)
    return _mm(name, (nd, T // tm), [n, wg, wu], [pl.BlockSpec((tm, D), lambda j, i: (i, 0)), w_spec, w_spec],
               [(0, 1, "nn", 0), (0, 2, "nn", 1)], 2, None, epilogue, [sh, sh, sh], [o_spec] * 3, None, comm=comm)


def _ffn_down(name, s, wd, res, tm=512, tn=512, comm=None):
    nd, T, fc = s.shape
    D = wd.shape[2]
    tm, tn = _tile(T, tm), _tile(D, tn)

    def body(s_ref, w_ref, r_ref, o_ref):
        acc = _dot(s_ref[0], w_ref[0], "nn")
        for j in range(1, nd):
            acc += _dot(s_ref[j], w_ref[j], "nn")
        o_ref[...] = r_ref[...] + 0.5 * acc

    mn = pl.BlockSpec((tm, tn), lambda i, j: (i, j))
    return _call(body, name=name, grid=(T // tm, D // tn),
                 in_specs=[pl.BlockSpec((nd, tm, fc), lambda i, j: (0, i, 0)), pl.BlockSpec((nd, fc, tn), lambda i, j: (0, 0, j)), mn],
                 out_specs=mn, out_shape=jax.ShapeDtypeStruct((T, D), F32), operands=[s, wd, res], comm=comm)


def _ffn_bwd_act(name, dxb, wdt, a, b, tm=1024, comm=None):
    T, D = dxb.shape
    nd, _, fc = wdt.shape
    tm = _tile(T, tm)

    def epilogue(accs, in_refs, out_refs):
        ds = 0.5 * accs[0]
        av, bv = in_refs[2][...].astype(F32), in_refs[3][...].astype(F32)
        sg = _sigmoid(av)
        out_refs[0][...] = (ds * bv * sg * (1.0 + av * (1.0 - sg))).astype(BF16)
        out_refs[1][...] = (ds * av * sg).astype(BF16)

    act = pl.BlockSpec((None, tm, fc), lambda j, i: (j, i, 0))
    sh = jax.ShapeDtypeStruct((nd, T, fc), BF16)
    return _mm(name, (nd, T // tm), [dxb, wdt, a, b],
               [pl.BlockSpec((tm, D), lambda j, i: (i, 0)), pl.BlockSpec((None, D, fc), lambda j, i: (j, 0, 0)), act, act],
               [(0, 1, "nn", 0)], 1, None, epilogue, [sh, sh], [act, act], None, comm=comm)


def _ffn_dwd(name, s, dxb, tk=2048):
    nd, T, fc = s.shape
    D = dxb.shape[1]
    tk = _tile(T, tk)

    def epilogue(accs, in_refs, out_refs):
        out_refs[0][...] = (0.5 * accs[0]).astype(BF16)

    return _mm(name, (nd, T // tk), [s, dxb],
               [pl.BlockSpec((None, tk, fc), lambda j, k: (j, k, 0)), pl.BlockSpec((tk, D), lambda j, k: (k, 0))],
               [(0, 1, "tn", 0)], 1, (fc, D), epilogue, [jax.ShapeDtypeStruct((nd, fc, D), BF16)],
               [pl.BlockSpec((None, fc, D), lambda j, k: (j, 0, 0))], 1)[0]


def _ffn_dw(name, n, dact, tk=2048):
    T, D = n.shape
    nd, _, fc = dact.shape
    tk = _tile(T, tk)

    def epilogue(accs, in_refs, out_refs):
        out_refs[0][...] = accs[0].astype(BF16)

    return _mm(name, (nd, T // tk), [n, dact],
               [pl.BlockSpec((tk, D), lambda j, k: (k, 0)), pl.BlockSpec((None, tk, fc), lambda j, k: (j, k, 0))],
               [(0, 1, "tn", 0)], 1, (D, fc), epilogue, [jax.ShapeDtypeStruct((nd, D, fc), BF16)],
               [pl.BlockSpec((None, D, fc), lambda j, k: (j, 0, 0))], 1)[0]


def _ffn_dn(name, da, db, wg, wu, tm=512, tn=256, comm=None):
    nd, T, fc = da.shape
    D = wg.shape[1]
    tm, tn = _tile(T, tm), _tile(D, tn)

    def body(da_ref, db_ref, wg_ref, wu_ref, o_ref):
        acc = _dot(da_ref[0], wg_ref[0], "nt") + _dot(db_ref[0], wu_ref[0], "nt")
        for j in range(1, nd):
            acc += _dot(da_ref[j], wg_ref[j], "nt") + _dot(db_ref[j], wu_ref[j], "nt")
        o_ref[...] = acc

    act = pl.BlockSpec((nd, tm, fc), lambda i, j: (0, i, 0))
    w_spec = pl.BlockSpec((nd, tn, fc), lambda i, j: (0, j, 0))
    return _call(body, name=name, grid=(T // tm, D // tn), in_specs=[act, act, w_spec, w_spec],
                 out_specs=pl.BlockSpec((tm, tn), lambda i, j: (i, j)), out_shape=jax.ShapeDtypeStruct((T, D), F32),
                 operands=[da, db, wg, wu], comm=comm)


def _ffn_forward(tag, x, g, wg, wu, wd, comm_up=None, comm_down=None):
    n = _rmsnorm_fwd(tag + "_norm", x, g)
    a, b, s = _ffn_up(tag + "_up", n, wg, wu, comm=comm_up)
    return _ffn_down(tag + "_down", s, wd, x, comm=comm_down() if comm_down else None), (n, a, b, s)


def _ffn_backward(tag, dx, dxb, x, g, wg, wu, wd, saved, comm_act=None, comm_dn=None, comm_norm=None, weights_first=True):
    n, a, b, s = saved
    da, db = _ffn_bwd_act(tag + "_bwd_act", dxb, jnp.swapaxes(wd, 1, 2), a, b, comm=comm_act)

    def weight_grads():
        return _ffn_dw(tag + "_dwg", n, da), _ffn_dw(tag + "_dwu", n, db), _ffn_dwd(tag + "_dwd", s, dxb)

    dws = weight_grads() if weights_first else None
    dn = _ffn_dn(tag + "_dn", da, db, wg, wu, comm=comm_dn(dws) if comm_dn else None)
    d_wg, d_wu, d_wd = dws if weights_first else weight_grads()
    dx_in, dxb_in, dg = _rmsnorm_bwd(tag + "_norm_bwd", dn, x, g, dx, comm=comm_norm((d_wg, d_wu, d_wd)) if comm_norm else None)
    return dx_in, dxb_in, dg, d_wg, d_wu, d_wd


def _place():
    x, y, c = lax.axis_index("x"), lax.axis_index("y"), lax.axis_index("c")
    return x, y, c, [(1 - x, y), (x, 1 - y), (1 - x, 1 - y)]


def _allgather(name, shards):
    n = len(shards)

    def body(*refs):
        ins, outs = refs[:n], refs[n:2 * n]
        send_sems, recv_sems, local_sems = refs[2 * n:]
        x, y, c, chips = _place()
        me, sibling = (x, y, c), (x, y, 1 - c)

        def slot(a, p):
            return outs[a].at[4 * p[0] + 2 * p[1] + p[2]]

        def copy(a, kk, block, to, src=None):
            return pltpu.make_async_remote_copy(
                src_ref=slot(a, block) if src is None else src, dst_ref=slot(a, block),
                send_sem=send_sems.at[a * 7 + kk], recv_sem=recv_sems.at[a * 7 + kk],
                device_id=to, device_id_type=MESH)

        mine = [pltpu.make_async_copy(ins[a], slot(a, me), local_sems.at[a]) for a in range(n)]
        for cp in mine:
            cp.start()
        first = []
        for a in range(n):
            first.append(copy(a, 0, me, sibling, src=ins[a]))
            first += [copy(a, 1 + j, me, (*chip, c), src=ins[a]) for j, chip in enumerate(chips)]
        for cp in first:
            cp.start()
        passed = []
        for j, chip in enumerate(chips):
            for a in range(n):
                copy(a, 1 + j, (*chip, c), me).wait_recv()
                fwd = copy(a, 4 + j, (*chip, c), sibling)
                fwd.start()
                passed.append(fwd)
        for a in range(n):
            copy(a, 0, sibling, me).wait_recv()
        for j, chip in enumerate(chips):
            for a in range(n):
                copy(a, 4 + j, (*chip, 1 - c), me).wait_recv()
        for cp in first + passed:
            cp.wait_send()
        for cp in mine:
            cp.wait()

    return pl.pallas_call(
        body, name=name, in_specs=[HBM_SPEC] * n, out_specs=[HBM_SPEC] * n,
        out_shape=[jax.ShapeDtypeStruct((N_DEV, *s.shape), s.dtype) for s in shards],
        scratch_shapes=[pltpu.SemaphoreType.DMA((7 * n,)), pltpu.SemaphoreType.DMA((7 * n,)), pltpu.SemaphoreType.DMA((n,))],
    )(*shards)


def _slot(ref, p):
    return ref.at[4 * p[0] + 2 * p[1] + p[2]]


def _ag_first(shards):
    n = len(shards)

    def plan(ins, outs):
        x, y, c, chips = _place()
        me = (x, y, c)
        remote, local = [], []
        for a in range(n):
            local.append((ins[a], _slot(outs[a], me)))
            for peer in [(x, y, 1 - c)] + [(*chip, c) for chip in chips]:
                remote.append((ins[a], _slot(outs[a], me), _slot(outs[a], peer), peer))
        return remote, local

    return _Exchange(shards, [jax.ShapeDtypeStruct((N_DEV, *s.shape), s.dtype) for s in shards], plan, 4 * n, n)


def _ag_second(partial):
    n = len(partial)

    def plan(ins, outs):
        x, y, c, chips = _place()
        remote = []
        for a in range(n):
            for chip in chips:
                remote.append((_slot(ins[a], (*chip, c)), _slot(outs[a], (*chip, c)), _slot(outs[a], (*chip, 1 - c)), (x, y, 1 - c)))
        return remote, []

    return _Exchange(partial, [jax.ShapeDtypeStruct(p.shape, p.dtype) for p in partial], plan, 3 * n, 0,
                     aliases={a: a for a in range(n)})


def _rs_first(halves):
    n = len(halves)

    def plan(ins, outs):
        x, y, c, _ = _place()
        return [(ins[a].at[:, 1 - c], outs[a], outs[a], (x, y, 1 - c)) for a in range(n)], []

    return _Exchange(halves, [jax.ShapeDtypeStruct((h.shape[0], *h.shape[2:]), h.dtype) for h in halves], plan, n, 0)


def _rs_second(sums):
    n = len(sums)

    def plan(ins, outs):
        x, y, c, chips = _place()
        k_me = 2 * x + y
        remote = []
        for a in range(n):
            for chip in chips:
                k_peer = 2 * chip[0] + chip[1]
                remote.append((ins[a].at[k_peer], outs[a].at[k_me], outs[a].at[k_peer], (*chip, c)))
        return remote, [(ins[a].at[k_me], outs[a].at[k_me]) for a in range(n)]

    return _Exchange(sums, [jax.ShapeDtypeStruct(s.shape, s.dtype) for s in sums], plan, 3 * n, n)


def _pair_add(name, own, got, core):
    nch, _, K, N = own.shape
    tr = _row_tile(K, N)

    def body(c_ref, own_ref, got_ref, o_ref):
        o_ref[...] = (own_ref[...].astype(F32) + got_ref[...].astype(F32)).astype(o_ref.dtype)

    grid_spec = pltpu.PrefetchScalarGridSpec(
        num_scalar_prefetch=1, grid=(nch, K // tr),
        in_specs=[pl.BlockSpec((None, None, tr, N), lambda k, r, c_ref: (k, c_ref[0], r, 0)),
                  pl.BlockSpec((None, tr, N), lambda k, r, c_ref: (k, r, 0))],
        out_specs=pl.BlockSpec((None, tr, N), lambda k, r, c_ref: (k, r, 0)))
    return pl.pallas_call(body, name=name, grid_spec=grid_spec, out_shape=jax.ShapeDtypeStruct((nch, K, N), own.dtype),
                          compiler_params=_cp(2))(core, own, got)


def _row_tile(K, N):
    limit = max(16, 262144 // N)
    t = 1
    while t * 2 <= limit and K % (t * 2) == 0:
        t *= 2
    return t if t >= 16 else K


def _adamw(name, parts, w, m, v):
    P, K, N = parts.shape
    tr = _row_tile(K, N)

    def body(p_ref, w_ref, m_ref, v_ref, g_ref, d_ref, nm_ref, nv_ref):
        g = p_ref[0].astype(F32)
        for i in range(1, P):
            g = g + p_ref[i].astype(F32)
        m_new = ADAM_B1 * m_ref[...] + (1.0 - ADAM_B1) * g
        v_new = ADAM_B2 * v_ref[...] + (1.0 - ADAM_B2) * (g * g)
        m_hat = m_new / (1.0 - ADAM_B1 ** ADAM_STEP)
        v_hat = v_new / (1.0 - ADAM_B2 ** ADAM_STEP)
        g_ref[...] = g
        d_ref[...] = -ADAM_LR * (m_hat / (jnp.sqrt(v_hat) + ADAM_EPS) + ADAM_WD * w_ref[...])
        nm_ref[...] = m_new
        nv_ref[...] = v_new

    row = pl.BlockSpec((tr, N), lambda r: (r, 0))
    sh = jax.ShapeDtypeStruct((K, N), F32)
    return pl.pallas_call(body, name=name, grid=(K // tr,),
                          in_specs=[pl.BlockSpec((P, tr, N), lambda r: (0, r, 0)), row, row, row],
                          out_specs=[row] * 4, out_shape=[sh] * 4, compiler_params=_cp(1))(parts, w, m, v)


def _merge_fwd(o_mla, wbm, o_dil, wbd, g0, g1, tm=512, tn=512, comm=None):
    T, K1 = o_mla.shape
    K2, D = o_dil.shape[1], wbm.shape[1]
    tm, tn = _tile(T, tm), _tile(D, tn)

    def epilogue(accs, in_refs, out_refs):
        a, b = accs
        out_refs[0][...] = a.astype(BF16)
        out_refs[1][...] = b.astype(BF16)
        out_refs[2][...] = (in_refs[4][...].astype(F32) * a + in_refs[5][...].astype(F32) * b).astype(BF16)

    mn = pl.BlockSpec((tm, tn), lambda i, j: (i, j))
    sh = jax.ShapeDtypeStruct((T, D), BF16)
    return _mm("merge_fwd", (T // tm, D // tn), [o_mla, wbm, o_dil, wbd, g0, g1],
               [pl.BlockSpec((tm, K1), lambda i, j: (i, 0)), pl.BlockSpec((K1, tn), lambda i, j: (0, j)),
                pl.BlockSpec((tm, K2), lambda i, j: (i, 0)), pl.BlockSpec((K2, tn), lambda i, j: (0, j)), mn, mn],
               [(0, 1, "nn", 0), (2, 3, "nn", 1)], 2, None, epilogue, [sh, sh, sh], [mn, mn, mn], None, comm=comm)


def _ple_loss(n4, wpg, pe, wpp, x3, tgt, tm=512, tn=512):
    T, D = x3.shape
    Kp = pe.shape[1]
    tm, tn = _tile(T, tm), _tile(D, tn)

    def epilogue(accs, in_refs, out_refs):
        z, proj = accs
        pg = _sigmoid(z)
        err = in_refs[4][...] + pg * proj - in_refs[5][...]
        dy = err * (1.0 / D)
        out_refs[0][...] = dy
        out_refs[1][...] = (dy * proj * pg * (1.0 - pg)).astype(BF16)
        out_refs[2][...] = (dy * pg).astype(BF16)

        @pl.when(pl.program_id(1) == 0)
        def _():
            out_refs[3][...] = jnp.zeros_like(out_refs[3])

        out_refs[3][...] += jnp.sum(err * err, axis=-1, keepdims=True)

    mn = pl.BlockSpec((tm, tn), lambda i, j: (i, j))
    return _mm("ple_loss", (T // tm, D // tn), [n4, wpg, pe, wpp, x3, tgt],
               [pl.BlockSpec((tm, D), lambda i, j: (i, 0)), pl.BlockSpec((D, tn), lambda i, j: (0, j)),
                pl.BlockSpec((tm, Kp), lambda i, j: (i, 0)), pl.BlockSpec((Kp, tn), lambda i, j: (0, j)), mn, mn],
               [(0, 1, "nn", 0), (2, 3, "nn", 1)], 2, None, epilogue,
               [jax.ShapeDtypeStruct((T, D), F32), jax.ShapeDtypeStruct((T, D), BF16), jax.ShapeDtypeStruct((T, D), BF16),
                jax.ShapeDtypeStruct((T, 1), F32)],
               [mn, mn, mn, pl.BlockSpec((tm, 1), lambda i, j: (i, 0))], None)


def _epi_sigmoid(acc, ex, outs):
    outs[0][...] = _sigmoid(acc).astype(outs[0].dtype)


def _epi_add(acc, ex, outs):
    outs[0][...] = (acc + ex[0][...].astype(F32)).astype(outs[0].dtype)


def _epi_dmerge(acc, ex, outs):
    mp, dp, g0, g1 = [e[...].astype(F32) for e in ex]
    outs[0][...] = (acc * g0).astype(BF16)
    outs[1][...] = (acc * g1).astype(BF16)
    outs[2][...] = (acc * mp * g0 * (1.0 - g0)).astype(BF16)
    outs[3][...] = (acc * dp * g1 * (1.0 - g1)).astype(BF16)


_WEIGHTS = ("g_ffn1", "w1_gate", "w1_up", "w1_down", "g_mix", "w_in", "g_cq", "w_uq", "g_ckv", "w_ukv", "g_q_mla", "g_k_mla",
            "g_q_dil", "g_k_dil", "w_br_mla", "w_br_dil", "w_o", "g_ffn2", "w2_gate", "w2_up", "w2_down", "g_ple",
            "w_ple_gate", "w_ple_proj")
_MATRICES = ("w1_gate", "w1_up", "w1_down", "w_in", "w_uq", "w_ukv", "w_br_mla", "w_br_dil", "w_o", "w2_gate", "w2_up",
             "w2_down", "w_ple_gate", "w_ple_proj")
_GAINS = tuple(n for n in _WEIGHTS if n not in _MATRICES)
MLA_BLOCK = 2048
MLA_SUB = 256
DIL_ROWS = 2048
_FIRST = ("w1_gate", "w1_up", "w1_down")
_MID = ("w_in", "w_uq", "w_ukv", "w_br_mla", "w_br_dil", "w_o", "w_ple_gate", "w_ple_proj")
_LATE = ("w2_gate", "w2_up", "w2_down")
_RS_GROUPS = (("w_ple_proj", "w_ple_gate", "w2_gate", "w2_up", "w2_down"),
              ("w_o", "w_br_mla", "w_br_dil", "w_uq", "w_ukv", "w_in"),
              ("w1_gate", "w1_up", "w1_down"))


def _cols(g3):
    nd, K, n = g3.shape
    return g3.transpose(1, 0, 2).reshape(K, nd * n)


def _uncols(m):
    K, n = m.shape
    return m.reshape(K, N_DEV, n // N_DEV).transpose(1, 0, 2)


def _rows(g3):
    nd, k, N = g3.shape
    return g3.reshape(nd * k, N)


def _unrows(m):
    K, N = m.shape
    return m.reshape(N_DEV, K // N_DEV, N)


def _pack_gains(vals):
    flat = jnp.concatenate([vals[n].reshape(-1) for n in _GAINS])
    pad = (-flat.shape[0]) % 2048
    return jnp.pad(flat, (0, pad)).reshape(-1, 128)


def _unpack_gains(packed, like):
    flat = packed.reshape(-1)
    out, off = {}, 0
    for n in _GAINS:
        size = int(np.prod(like[n].shape))
        out[n] = flat[off:off + size].reshape(like[n].shape)
        off += size
    return out


def _train_step(x, p, positions, loss_target, W, M, V):
    T, D = x.shape[1], x.shape[2]
    xs, tgt, pe = x[0], loss_target[0], p[0, 0]
    pos_col = positions.reshape(T, 1).astype(F32)
    w = {n: (a[0] if n in _MATRICES else a.reshape(1, -1)) for n, a in W.items()}

    shard = {n: w[n].astype(BF16) for n in _MATRICES}
    gathered = dict(zip(_FIRST, _allgather("ag_first", [shard[n] for n in _FIRST])))
    ag_mid = _ag_first([shard[n] for n in _MID])
    ag_mid2 = []

    def pass_on_mid():
        ag_mid2.append(_ag_second(ag_mid.results))
        return ag_mid2[0]

    x1, ffn1 = _ffn_forward("ffn1", xs, w["g_ffn1"], gathered["w1_gate"], gathered["w1_up"], gathered["w1_down"],
                            comm_up=ag_mid, comm_down=pass_on_mid)
    gathered.update(zip(_MID, ag_mid2[0].results))
    nq_l, nkv_l = w["g_cq"].shape[-1], w["g_ckv"].shape[-1]
    H = w["w_uq"].shape[1] * N_DEV // MLA_QK
    G = len(DIL_GROUPS)
    off_kr = nq_l + nkv_l
    off_dil = off_kr + MLA_ROPE
    off_gate = off_dil + G * 3 * H * HEAD
    kr_block = off_kr // 128
    w_in = _cols(gathered["w_in"])
    wa = jnp.pad(w_in[:, :off_dil], ((0, 0), (0, 128 - MLA_ROPE)))
    wdil, wg0, wg1 = w_in[:, off_dil:off_gate], w_in[:, off_gate:off_gate + D], w_in[:, off_gate + D:]
    wuq = jnp.pad(_cols(gathered["w_uq"]).reshape(nq_l, H, MLA_QK), ((0, 0), (0, 0), (0, MLA_PAD - MLA_QK))).reshape(nq_l, H * MLA_PAD)
    wukv = _cols(gathered["w_ukv"])
    wbm, wbd, wpp = _cols(gathered["w_br_mla"]), _cols(gathered["w_br_dil"]), _cols(gathered["w_ple_proj"])
    wo, wpg = _rows(gathered["w_o"]), _rows(gathered["w_ple_gate"])
    gq_mla = jnp.pad(w["g_q_mla"], ((0, 0), (0, MLA_PAD - MLA_QK)))
    gk_mla = jnp.pad(w["g_k_mla"], ((0, 0), (0, MLA_PAD - MLA_QK)))
    gq_dil, gk_dil = w["g_q_dil"].reshape(G, 1, HEAD), w["g_k_dil"].reshape(G, 1, HEAD)

    half_m, half_d = MLA_ROPE // 2, DIL_ROT // 2
    inv_m = ROPE_THETA ** (-jnp.arange(half_m, dtype=F32) * 2.0 / MLA_ROPE)
    inv_d = ROPE_THETA ** (-jnp.arange(half_d, dtype=F32) * 2.0 / DIL_ROT)
    inv_m = jnp.tile(inv_m, 128 // half_m).reshape(1, 128)
    inv_d = jnp.tile(inv_d, 128 // half_d).reshape(1, 128)
    tabs = _rope_tables(pos_col, inv_m, inv_d)
    tabs_m, tabs_d = tabs[:3], tabs[3:]

    mla_blk, dil_qb = _tile(T, MLA_BLOCK), _tile(T, DIL_ROWS)
    gw = 3 * H * HEAD
    dils = [d for _, d in DIL_GROUPS]
    units = [win // d for win, d in DIL_GROUPS]
    wdil_g = [wdil[:, g * gw:(g + 1) * gw] for g in range(G)]
    tabs_g = [[_to_phase(t, d) for t in tabs_d] for d in dils]

    h = _rmsnorm_fwd("mix_norm", x1, w["g_mix"])
    pa = _mm2d("proj_a", h, wa, "nn", 512, 1024, 4096)
    pd = [_to_phase(_mm2d("proj_dil%d" % g, h, wdil_g[g], "nn", 1024, 1024, 4096, out_dtypes=(BF16,)), dils[g]) for g in range(G)]
    g0 = _mm2d("proj_gate0", h, wg0, "nn", 1024, 1024, 4096, out_dtypes=(BF16,), epilogue=_epi_sigmoid)
    g1 = _mm2d("proj_gate1", h, wg1, "nn", 1024, 1024, 4096, out_dtypes=(BF16,), epilogue=_epi_sigmoid)
    cq, ckv = _lora_fwd(pa, w["g_cq"], w["g_ckv"], nq_l, nkv_l)
    q_raw = _mm2d("q_up", cq, wuq, "nn", 512, 2048, 4096)
    kv = _mm2d("kv_up", ckv, wukv, "nn", 512, 2048, 4096)
    q_att, k_att, v_mla = _mla_prep_fwd(q_raw, kv, pa, tabs_m, gq_mla, gk_mla, H, kr_block)
    ag_late = _ag_first([shard[n] for n in _LATE])
    o_mla, lse_mla = _mla_fwd(q_att, k_att, v_mla, H, mla_blk, comm=ag_late)
    qd, kd, o_g, lse_g = [], [], [], []
    for g in range(G):
        qg, kg = _dil_prep_fwd_pm("dil_prep_fwd%d" % g, pd[g], tabs_g[g], gq_dil[g], gk_dil[g], H)
        og, lg = _win_fwd("dil_fwd%d" % g, qg, kg, pd[g], H, units[g], T // dils[g], dil_qb)
        qd.append(qg)
        kd.append(kg)
        o_g.append(_from_phase(og, dils[g]))
        lse_g.append(_from_phase(lg, dils[g], axis=1))
    o_dil, lse_dil = _merge_groups(o_g, lse_g, H)
    ag_late2 = _ag_second(ag_late.results)
    mla_p, dil_p, merged = _merge_fwd(o_mla, wbm, o_dil, wbd, g0, g1, comm=ag_late2)
    gathered.update(zip(_LATE, ag_late2.results))
    x2 = _mm2d("out_proj", merged, wo, "nn", 512, 1024, 4096, epilogue=_epi_add, extras=(x1,))
    x3, ffn2 = _ffn_forward("ffn2", x2, w["g_ffn2"], gathered["w2_gate"], gathered["w2_up"], gathered["w2_down"])
    n4 = _rmsnorm_fwd("ple_norm", x3, w["g_ple"])
    dy, dz, dproj, loss_rows = _ple_loss(n4, wpg, pe, wpp, x3, tgt)
    loss = lax.psum((0.5 / D) * jnp.sum(loss_rows), ("x", "y", "c"))

    dW, dG = {}, {}
    dW["w_ple_proj"] = _uncols(_mm2d("d_wpp", pe, dproj, "tn", 1024, 2048, 2048, out_dtypes=(BF16,)))
    dW["w_ple_gate"] = _unrows(_mm2d("d_wpg", n4, dz, "tn", 1024, 1024, 2048, out_dtypes=(BF16,)))
    dn4 = _mm2d("d_n4", dz, wpg, "nt", 512, 1024, 4096)
    dx3, dx3b, dG["g_ple"] = _rmsnorm_bwd("ple_norm_bwd", dn4, x3, w["g_ple"], dy)

    core = lax.axis_index("c").astype(jnp.int32).reshape(1)
    grads, deltas, new_m, new_v = {}, {}, {}, {}

    def sibling_exchange(names):
        return _rs_first([dW[n].reshape(4, 2, *dW[n].shape[1:]) for n in names])

    def chip_exchange(names, first):
        return _rs_second([_pair_add("rs_add_" + n, own, rec, core) for n, own, rec in zip(names, first.ins, first.results)])

    def update(names, second):
        for n, parts in zip(names, second.results):
            grads[n], deltas[n], new_m[n], new_v[n] = [a[None] for a in _adamw("adamw_" + n, parts, w[n], M[n][0], V[n][0])]

    rs_a = []

    def ffn2_grads_done(dws):
        dW["w2_gate"], dW["w2_up"], dW["w2_down"] = dws
        rs_a.append(sibling_exchange(_RS_GROUPS[0]))
        return rs_a[0]

    dx2, dx2b, dG["g_ffn2"], _, _, _ = _ffn_backward(
        "ffn2", dx3, dx3b, x2, w["g_ffn2"], gathered["w2_gate"], gathered["w2_up"], gathered["w2_down"], ffn2,
        comm_dn=ffn2_grads_done)
    rs_a2 = chip_exchange(_RS_GROUPS[0], rs_a[0])

    d_mla_p, d_dil_p, dpg0, dpg1 = _mm2d("d_merged", dx2b, wo, "nt", 512, 1024, 4096, out_dtypes=(BF16,) * 4,
                                         epilogue=_epi_dmerge, extras=(mla_p, dil_p, g0, g1))
    dW["w_o"] = _unrows(_mm2d("d_wo", merged, dx2b, "tn", 1024, 1024, 2048, out_dtypes=(BF16,)))
    dW["w_br_mla"] = _uncols(_mm2d("d_wbm", o_mla, d_mla_p, "tn", 1024, 1024, 2048, out_dtypes=(BF16,)))
    dW["w_br_dil"] = _uncols(_mm2d("d_wbd", o_dil, d_dil_p, "tn", 1024, 1024, 2048, out_dtypes=(BF16,)))
    do_mla = _mm2d("d_o_mla", d_mla_p, wbm, "nt", 512, 1024, 4096, out_dtypes=(BF16,))
    do_dil = _mm2d("d_o_dil", d_dil_p, wbd, "nt", 512, 1024, 4096, out_dtypes=(BF16,))

    dl_mla = _row_dot("mla_delta", do_mla, o_mla, H)
    dq_att = _mla_dq(q_att, k_att, v_mla, do_mla, lse_mla, dl_mla, H, mla_blk, comm=rs_a2)
    update(_RS_GROUPS[0], rs_a2)
    dk_att, dv_mla = _mla_dkv(q_att, k_att, v_mla, do_mla, lse_mla.reshape(H, 1, T), dl_mla.reshape(H, 1, T), H, mla_blk)
    dq_raw, dkv, dkr, dgq, dgk = _mla_prep_bwd(dq_att, dk_att, dv_mla, q_raw, kv, pa, tabs_m, gq_mla, gk_mla, H, kr_block)
    dG["g_q_mla"], dG["g_k_mla"] = dgq[:, :MLA_QK], dgk[:, :MLA_QK]
    d_wuq = _mm2d("d_wuq", cq, dq_raw, "tn", 512, 2048, 2048, out_dtypes=(BF16,))
    dW["w_uq"] = _uncols(d_wuq.reshape(nq_l, H, MLA_PAD)[:, :, :MLA_QK].reshape(nq_l, H * MLA_QK))
    dW["w_ukv"] = _uncols(_mm2d("d_wukv", ckv, dkv, "tn", 512, 2048, 2048, out_dtypes=(BF16,)))
    dcq = _mm2d("d_cq", dq_raw, wuq, "nt", 512, 1024, 4096)
    dckv = _mm2d("d_ckv", dkv, wukv, "nt", 512, 1024, 4096)
    dpa, dG["g_cq"], dG["g_ckv"] = _lora_bwd(dcq, dckv, dkr, pa, w["g_cq"], w["g_ckv"])

    dl_dil = _row_dot("dil_delta", do_dil, o_dil, H)
    dpd, dgqd, dgkd = [], [], []
    for g in range(G):
        d, n, L = dils[g], units[g], T // dils[g]
        do_g, lse_pg, dl_pg = _to_phase(do_dil, d), _to_phase(lse_dil, d, axis=1), _to_phase(dl_dil, d, axis=1)
        dq_g = _win_dq("dil_dq%d" % g, qd[g], kd[g], pd[g], do_g, lse_pg, dl_pg, H, n, L, dil_qb)
        dk_g, dv_g = _win_dkv("dil_dkv%d" % g, qd[g], kd[g], pd[g], do_g, lse_pg, dl_pg, H, n, L, dil_qb)
        dp_g, dgq_g, dgk_g = _dil_prep_bwd_pm("dil_prep_bwd%d" % g, dq_g, dk_g, dv_g, pd[g], tabs_g[g], gq_dil[g], gk_dil[g], H)
        dpd.append(_from_phase(dp_g, d))
        dgqd.append(dgq_g)
        dgkd.append(dgk_g)
    dG["g_q_dil"], dG["g_k_dil"] = jnp.concatenate(dgqd).reshape(1, G, HEAD), jnp.concatenate(dgkd).reshape(1, G, HEAD)

    d_wa = _mm2d("d_wa", h, dpa, "tn", 1024, 1024, 2048, out_dtypes=(BF16,))
    d_wdil = [_mm2d("d_wdil%d" % g, h, dpd[g], "tn", 1024, 1024, 2048, out_dtypes=(BF16,)) for g in range(G)]
    d_wg0 = _mm2d("d_wg0", h, dpg0, "tn", 1024, 1024, 2048, out_dtypes=(BF16,))
    d_wg1 = _mm2d("d_wg1", h, dpg1, "tn", 1024, 1024, 2048, out_dtypes=(BF16,))
    dW["w_in"] = _uncols(jnp.concatenate([d_wa[:, :off_dil], *d_wdil, d_wg0, d_wg1], axis=1))
    dh = _mm2d("d_h_a", dpa, wa, "nt", 512, 1024, 4096)
    for g in range(G):
        dh = _mm2d("d_h_dil%d" % g, dpd[g], wdil_g[g], "nt", 512, 1024, 4096, epilogue=_epi_add, extras=(dh,))
    dh = _mm2d("d_h_g0", dpg0, wg0, "nt", 512, 1024, 4096, epilogue=_epi_add, extras=(dh,))
    dh = _mm2d("d_h_g1", dpg1, wg1, "nt", 512, 1024, 4096, epilogue=_epi_add, extras=(dh,))
    dx1, dx1b, dG["g_mix"] = _rmsnorm_bwd("mix_norm_bwd", dh, x1, w["g_mix"], dx2)
    rs_b = sibling_exchange(_RS_GROUPS[1])
    rs_b2 = []

    def attention_grads_summed(_):
        rs_b2.append(chip_exchange(_RS_GROUPS[1], rs_b))
        return rs_b2[0]

    rs_c2 = []

    def ffn1_grads_done(dws):
        dW["w1_gate"], dW["w1_up"], dW["w1_down"] = dws
        rs_c = sibling_exchange(_RS_GROUPS[2])
        _run_exchange("rs_first_w1", rs_c)
        rs_c2.append(chip_exchange(_RS_GROUPS[2], rs_c))
        return rs_c2[0]

    dx0, _, dG["g_ffn1"], _, _, _ = _ffn_backward(
        "ffn1", dx1, dx1b, xs, w["g_ffn1"], gathered["w1_gate"], gathered["w1_up"], gathered["w1_down"], ffn1,
        comm_act=rs_b, comm_dn=attention_grads_summed, comm_norm=ffn1_grads_done, weights_first=False)
    update(_RS_GROUPS[1], rs_b2[0])
    update(_RS_GROUPS[2], rs_c2[0])

    parts = _allgather("ag_gain_grads", [_pack_gains(dG)])[0]
    packed = _adamw("adamw_gains", parts, _pack_gains(W), _pack_gains(M), _pack_gains(V))
    for out, pk in zip((grads, deltas, new_m, new_v), packed):
        out.update(_unpack_gains(pk, W))

    return (loss, dx0[None], *[grads[n] for n in _WEIGHTS], *[deltas[n] for n in _WEIGHTS],
            *[new_m[n] for n in _WEIGHTS], *[new_v[n] for n in _WEIGHTS])


def kernel(x, p, positions, g_ffn1, w1_gate, w1_up, w1_down, g_mix, w_in, g_cq, w_uq, g_ckv, w_ukv, g_q_mla, g_k_mla, g_q_dil, g_k_dil, w_br_mla, w_br_dil, w_o, g_ffn2, w2_gate, w2_up, w2_down, g_ple, w_ple_gate, w_ple_proj, loss_target, m_g_ffn1, m_w1_gate, m_w1_up, m_w1_down, m_g_mix, m_w_in, m_g_cq, m_w_uq, m_g_ckv, m_w_ukv, m_g_q_mla, m_g_k_mla, m_g_q_dil, m_g_k_dil, m_w_br_mla, m_w_br_dil, m_w_o, m_g_ffn2, m_w2_gate, m_w2_up, m_w2_down, m_g_ple, m_w_ple_gate, m_w_ple_proj, v_g_ffn1, v_w1_gate, v_w1_up, v_w1_down, v_g_mix, v_w_in, v_g_cq, v_w_uq, v_g_ckv, v_w_ukv, v_g_q_mla, v_g_k_mla, v_g_q_dil, v_g_k_dil, v_w_br_mla, v_w_br_dil, v_w_o, v_g_ffn2, v_w2_gate, v_w2_up, v_w2_down, v_g_ple, v_w_ple_gate, v_w_ple_proj):
    W = dict(zip(_WEIGHTS, (g_ffn1, w1_gate, w1_up, w1_down, g_mix, w_in, g_cq, w_uq, g_ckv, w_ukv, g_q_mla, g_k_mla, g_q_dil,
                            g_k_dil, w_br_mla, w_br_dil, w_o, g_ffn2, w2_gate, w2_up, w2_down, g_ple, w_ple_gate, w_ple_proj)))
    M = dict(zip(_WEIGHTS, (m_g_ffn1, m_w1_gate, m_w1_up, m_w1_down, m_g_mix, m_w_in, m_g_cq, m_w_uq, m_g_ckv, m_w_ukv, m_g_q_mla,
                            m_g_k_mla, m_g_q_dil, m_g_k_dil, m_w_br_mla, m_w_br_dil, m_w_o, m_g_ffn2, m_w2_gate, m_w2_up,
                            m_w2_down, m_g_ple, m_w_ple_gate, m_w_ple_proj)))
    V = dict(zip(_WEIGHTS, (v_g_ffn1, v_w1_gate, v_w1_up, v_w1_down, v_g_mix, v_w_in, v_g_cq, v_w_uq, v_g_ckv, v_w_ukv, v_g_q_mla,
                            v_g_k_mla, v_g_q_dil, v_g_k_dil, v_w_br_mla, v_w_br_dil, v_w_o, v_g_ffn2, v_w2_gate, v_w2_up,
                            v_w2_down, v_g_ple, v_w_ple_gate, v_w_ple_proj)))
    return _train_step(x, p, positions, loss_target, W, M, V)
```

```python
import numpy as np
import jax
import jax.numpy as jnp
from jax import lax
from jax.experimental import pallas as pl
from jax.experimental.pallas import tpu as pltpu

F32 = jnp.float32
BF16 = jnp.bfloat16

EPS = 1e-6
ROPE_THETA = 500000.0
MLA_NOPE = 128
MLA_ROPE = 64
MLA_QK = MLA_NOPE + MLA_ROPE
MLA_PAD = 256
HEAD = 128
DIL_ROT = 32
DIL_GROUPS = ((128, 1), (512, 4), (2048, 16))
NEG = -1e30
NO_WINDOW = 1 << 30
N_DEV = 8
ADAM_LR, ADAM_B1, ADAM_B2, ADAM_EPS, ADAM_WD, ADAM_STEP = 0.001, 0.9, 0.999, 1e-08, 0.01, 10
VMEM_LIMIT_V7X = 56 * 1024 * 1024
MESH = pl.DeviceIdType.MESH
HBM_SPEC = pl.BlockSpec(memory_space=pltpu.HBM)


def _cp(n_axes):
    return pltpu.CompilerParams(dimension_semantics=("arbitrary",) * n_axes,
                                vmem_limit_bytes=VMEM_LIMIT_V7X)


def _tile(n, t):
    return t if (n >= t and n % t == 0) else n


def _sigmoid(x):
    return 1.0 / (1.0 + jnp.exp(-x))


_DIMS = {"nn": (((1,), (0,)), ((), ())), "nt": (((1,), (1,)), ((), ())), "tn": (((0,), (0,)), ((), ()))}


def _dot(a, b, mode):
    return lax.dot_general(a.astype(BF16), b.astype(BF16), _DIMS[mode], preferred_element_type=F32)


class _Exchange:
    def __init__(self, ins, out_shapes, plan, n_remote, n_local, aliases=None):
        self.ins, self.out_shapes, self.plan = list(ins), list(out_shapes), plan
        self.n_remote, self.n_local, self.aliases = n_remote, n_local, dict(aliases or {})
        self.results = None

    def sems(self):
        return [pltpu.SemaphoreType.DMA((self.n_remote,)), pltpu.SemaphoreType.DMA((self.n_remote,)),
                pltpu.SemaphoreType.DMA((max(self.n_local, 1),))]

    def start(self, in_refs, out_refs, sems):
        remote, local = self.plan(in_refs, out_refs)
        for i, (src, dst_there, _, peer) in enumerate(remote):
            pltpu.make_async_remote_copy(src_ref=src, dst_ref=dst_there, send_sem=sems[0].at[i], recv_sem=sems[1].at[i],
                                         device_id=peer, device_id_type=MESH).start()
        for i, (src, dst) in enumerate(local):
            pltpu.make_async_copy(src, dst, sems[2].at[i]).start()

    def finish(self, in_refs, out_refs, sems):
        remote, local = self.plan(in_refs, out_refs)
        for i, (src, _, dst_here, peer) in enumerate(remote):
            pltpu.make_async_remote_copy(src_ref=src, dst_ref=dst_here, send_sem=sems[0].at[i], recv_sem=sems[1].at[i],
                                         device_id=peer, device_id_type=MESH).wait_recv()
        for i, (src, dst_there, _, peer) in enumerate(remote):
            pltpu.make_async_remote_copy(src_ref=src, dst_ref=dst_there, send_sem=sems[0].at[i], recv_sem=sems[1].at[i],
                                         device_id=peer, device_id_type=MESH).wait_send()
        for i, (src, dst) in enumerate(local):
            pltpu.make_async_copy(src, dst, sems[2].at[i]).wait()


def _run_exchange(name, ex):
    ci = len(ex.ins)

    def body(*refs):
        ins, outs, sems = refs[:ci], refs[ci:ci + len(ex.out_shapes)], refs[ci + len(ex.out_shapes):]
        ex.start(ins, outs, sems)
        ex.finish(ins, outs, sems)

    ex.results = pl.pallas_call(body, name=name, in_specs=[HBM_SPEC] * ci, out_specs=[HBM_SPEC] * len(ex.out_shapes),
                                out_shape=ex.out_shapes, scratch_shapes=ex.sems(), input_output_aliases=ex.aliases)(*ex.ins)
    return ex.results


def _call(body, *, name, grid, in_specs, out_specs, out_shape, operands, scratch=(), comm=None):
    multi = isinstance(out_shape, (list, tuple))
    outs = list(out_shape) if multi else [out_shape]
    ospecs = list(out_specs) if multi else [out_specs]
    if comm is None:
        res = pl.pallas_call(body, name=name, grid=grid, in_specs=list(in_specs), out_specs=ospecs, out_shape=outs,
                             scratch_shapes=list(scratch), compiler_params=_cp(len(grid)))(*operands)
        return res if multi else res[0]
    n_in, n_out, n_scr = len(in_specs), len(outs), len(scratch)
    ci, co = len(comm.ins), len(comm.out_shapes)

    def hosted(*refs):
        bounds = np.cumsum([0, n_in, ci, n_out, co, n_scr])
        ins, cins, os_, cos, scr = (refs[bounds[i]:bounds[i + 1]] for i in range(5))
        sems = refs[bounds[5]:]
        ids = [pl.program_id(a) for a in range(len(grid))]
        first, last = ids[0] == 0, ids[0] == grid[0] - 1
        for a in range(1, len(grid)):
            first, last = first & (ids[a] == 0), last & (ids[a] == grid[a] - 1)

        @pl.when(first)
        def _():
            comm.start(cins, cos, sems)

        body(*ins, *os_, *scr)

        @pl.when(last)
        def _():
            comm.finish(cins, cos, sems)

    res = pl.pallas_call(hosted, name=name, grid=grid, in_specs=[*in_specs, *[HBM_SPEC] * ci],
                         out_specs=[*ospecs, *[HBM_SPEC] * co], out_shape=[*outs, *comm.out_shapes],
                         scratch_shapes=[*scratch, *comm.sems()],
                         input_output_aliases={n_in + i: n_out + o for i, o in comm.aliases.items()},
                         compiler_params=_cp(len(grid)))(*operands, *comm.ins)
    comm.results = res[n_out:]
    return res[:n_out] if multi else res[0]


def _mm(name, grid, ins, in_specs, pairs, n_acc, acc_shape, epilogue, out_shapes, out_specs, k_axis, comm=None):
    n_in, n_out = len(ins), len(out_shapes)
    nk = grid[k_axis] if k_axis is not None else 1

    def body(*refs):
        in_refs, out_refs, acc_refs = refs[:n_in], refs[n_in:n_in + n_out], refs[n_in + n_out:]
        parts = [None] * n_acc
        for ai, bi, mode, ci in pairs:
            d = _dot(in_refs[ai][...], in_refs[bi][...], mode)
            parts[ci] = d if parts[ci] is None else parts[ci] + d
        if nk == 1:
            epilogue(parts, in_refs, out_refs)
            return
        k = pl.program_id(k_axis)

        @pl.when(k == 0)
        def _():
            for c in range(n_acc):
                acc_refs[c][...] = parts[c]

        @pl.when(k > 0)
        def _():
            for c in range(n_acc):
                acc_refs[c][...] += parts[c]

        @pl.when(k == nk - 1)
        def _():
            epilogue([r[...] for r in acc_refs], in_refs, out_refs)

    scratch = [pltpu.VMEM(acc_shape, F32) for _ in range(n_acc)] if nk > 1 else []
    return _call(body, name=name, grid=grid, in_specs=in_specs, out_specs=list(out_specs), out_shape=list(out_shapes),
                 operands=ins, scratch=scratch, comm=comm)


def _mm2d(name, a, b, mode, tm, tn, tk, out_dtypes=(F32,), epilogue=None, extras=()):
    if mode == "nn":
        (M, K), N = a.shape, b.shape[1]
    elif mode == "nt":
        (M, K), N = a.shape, b.shape[0]
    else:
        (K, M), N = a.shape, b.shape[1]
    tm, tn, tk = _tile(M, tm), _tile(N, tn), _tile(K, tk)
    a_spec = pl.BlockSpec((tk, tm), lambda i, j, k: (k, i)) if mode == "tn" else pl.BlockSpec((tm, tk), lambda i, j, k: (i, k))
    b_spec = pl.BlockSpec((tn, tk), lambda i, j, k: (j, k)) if mode == "nt" else pl.BlockSpec((tk, tn), lambda i, j, k: (k, j))
    mn_spec = pl.BlockSpec((tm, tn), lambda i, j, k: (i, j))
    n_ex = len(extras)

    def default_epilogue(acc, ex_refs, out_refs):
        out_refs[0][...] = acc.astype(out_refs[0].dtype)

    epi = epilogue or default_epilogue

    def wrapped(accs, in_refs, out_refs):
        epi(accs[0], in_refs[2:2 + n_ex], out_refs)

    outs = _mm(name, (M // tm, N // tn, K // tk), [a, b, *extras], [a_spec, b_spec] + [mn_spec] * n_ex,
               [(0, 1, mode, 0)], 1, (tm, tn), wrapped,
               [jax.ShapeDtypeStruct((M, N), dt) for dt in out_dtypes], [mn_spec] * len(out_dtypes), 2)
    return outs[0] if len(out_dtypes) == 1 else outs


def _rms_stats(x, n):
    return lax.rsqrt(jnp.sum(x * x, axis=-1, keepdims=True) * (1.0 / n) + EPS)


def _rmsnorm_fwd(name, x, g, tm=512):
    T, D = x.shape
    tm = _tile(T, tm)

    def body(x_ref, g_ref, o_ref):
        xv = x_ref[...]
        o_ref[...] = (xv * _rms_stats(xv, D) * g_ref[...]).astype(BF16)

    return pl.pallas_call(body, name=name, grid=(T // tm,),
                          in_specs=[pl.BlockSpec((tm, D), lambda i: (i, 0)), pl.BlockSpec((1, D), lambda i: (0, 0))],
                          out_specs=pl.BlockSpec((tm, D), lambda i: (i, 0)),
                          out_shape=jax.ShapeDtypeStruct((T, D), BF16), compiler_params=_cp(1))(x, g)


def _rms_bwd_rows(dy, x, g, n):
    r = _rms_stats(x, n)
    xh = x * r
    gd = dy * g
    mean = jnp.sum(gd * xh, axis=-1, keepdims=True) * (1.0 / n)
    return r * (gd - xh * mean), dy * xh


def _rmsnorm_bwd(name, dn, x, g, res, tm=256, comm=None):
    T, D = x.shape
    tm = _tile(T, tm)

    def body(dn_ref, x_ref, g_ref, res_ref, dx_ref, dxb_ref, dg_ref):
        dx, dgp = _rms_bwd_rows(dn_ref[...].astype(F32), x_ref[...], g_ref[...], D)
        dx = dx + res_ref[...]
        dx_ref[...] = dx
        dxb_ref[...] = dx.astype(BF16)

        @pl.when(pl.program_id(0) == 0)
        def _():
            dg_ref[...] = jnp.zeros_like(dg_ref)

        dg_ref[...] += jnp.sum(dgp, axis=0, keepdims=True)

    row = pl.BlockSpec((tm, D), lambda i: (i, 0))
    vec = pl.BlockSpec((1, D), lambda i: (0, 0))
    return _call(body, name=name, grid=(T // tm,), in_specs=[row, row, vec, row], out_specs=[row, row, vec],
                 out_shape=[jax.ShapeDtypeStruct((T, D), F32), jax.ShapeDtypeStruct((T, D), BF16), jax.ShapeDtypeStruct((1, D), F32)],
                 operands=[dn, x, g, res], comm=comm)


def _rope_tables(pos_col, inv_mla, inv_dil, tm=512):
    T = pos_col.shape[0]
    tm = _tile(T, tm)

    def body(p_ref, im_ref, id_ref, cm, sam, sbm, cd, sad, sbd):
        lane = lax.broadcasted_iota(jnp.int32, (tm, 128), 1)
        p = p_ref[...]
        am = p * im_ref[...]
        c, s = jnp.cos(am), jnp.sin(am)
        cm[...] = jnp.where(lane < 64, c, 0.0)
        sam[...] = jnp.where(lane < 32, -s, 0.0)
        sbm[...] = jnp.where((lane >= 32) & (lane < 64), s, 0.0)
        ad = p * id_ref[...]
        c, s = jnp.cos(ad), jnp.sin(ad)
        cd[...] = jnp.where(lane < 32, c, 1.0)
        sad[...] = jnp.where(lane < 16, -s, 0.0)
        sbd[...] = jnp.where((lane >= 16) & (lane < 32), s, 0.0)

    row = pl.BlockSpec((tm, 128), lambda i: (i, 0))
    vec = pl.BlockSpec((1, 128), lambda i: (0, 0))
    return pl.pallas_call(body, name="rope_tables", grid=(T // tm,),
                          in_specs=[pl.BlockSpec((tm, 1), lambda i: (i, 0)), vec, vec], out_specs=[row] * 6,
                          out_shape=[jax.ShapeDtypeStruct((T, 128), F32)] * 6, compiler_params=_cp(1))(pos_col, inv_mla, inv_dil)


def _rope(v, c, sa, sb, sh):
    return v * c + pltpu.roll(v, 128 - sh, 1) * sa + pltpu.roll(v, sh, 1) * sb


def _rope_t(d, c, sa, sb, sh):
    return d * c + pltpu.roll(d * sa, sh, 1) + pltpu.roll(d * sb, 128 - sh, 1)


def _lora_fwd(pa, g_cq, g_ckv, nq, nkv, tm=512):
    T, W = pa.shape
    tm = _tile(T, tm)

    def body(pa_ref, gq_ref, gk_ref, cq_ref, ckv_ref):
        a = pa_ref[:, :nq]
        cq_ref[...] = (a * _rms_stats(a, nq) * gq_ref[...]).astype(BF16)
        b = pa_ref[:, nq:nq + nkv]
        ckv_ref[...] = (b * _rms_stats(b, nkv) * gk_ref[...]).astype(BF16)

    return pl.pallas_call(body, name="lora_fwd", grid=(T // tm,),
                          in_specs=[pl.BlockSpec((tm, W), lambda i: (i, 0)), pl.BlockSpec((1, nq), lambda i: (0, 0)),
                                    pl.BlockSpec((1, nkv), lambda i: (0, 0))],
                          out_specs=[pl.BlockSpec((tm, nq), lambda i: (i, 0)), pl.BlockSpec((tm, nkv), lambda i: (i, 0))],
                          out_shape=[jax.ShapeDtypeStruct((T, nq), BF16), jax.ShapeDtypeStruct((T, nkv), BF16)],
                          compiler_params=_cp(1))(pa, g_cq, g_ckv)


def _lora_bwd(dcq, dckv, dkr, pa, g_cq, g_ckv, tm=512):
    T, W = pa.shape
    nq, nkv = dcq.shape[1], dckv.shape[1]
    tm = _tile(T, tm)

    def body(dcq_ref, dckv_ref, dkr_ref, pa_ref, gq_ref, gk_ref, dpa_ref, dgq_ref, dgk_ref):
        dx, dgp = _rms_bwd_rows(dcq_ref[...], pa_ref[:, :nq], gq_ref[...], nq)
        dpa_ref[:, :nq] = dx.astype(BF16)
        dx2, dgp2 = _rms_bwd_rows(dckv_ref[...], pa_ref[:, nq:nq + nkv], gk_ref[...], nkv)
        dpa_ref[:, nq:nq + nkv] = dx2.astype(BF16)
        dpa_ref[:, nq + nkv:] = dkr_ref[...].astype(BF16)

        @pl.when(pl.program_id(0) == 0)
        def _():
            dgq_ref[...] = jnp.zeros_like(dgq_ref)
            dgk_ref[...] = jnp.zeros_like(dgk_ref)

        dgq_ref[...] += jnp.sum(dgp, axis=0, keepdims=True)
        dgk_ref[...] += jnp.sum(dgp2, axis=0, keepdims=True)

    def row(n):
        return pl.BlockSpec((tm, n), lambda i: (i, 0))

    def vec(n):
        return pl.BlockSpec((1, n), lambda i: (0, 0))

    return pl.pallas_call(body, name="lora_bwd", grid=(T // tm,),
                          in_specs=[row(nq), row(nkv), row(128), row(W), vec(nq), vec(nkv)],
                          out_specs=[row(W), vec(nq), vec(nkv)],
                          out_shape=[jax.ShapeDtypeStruct((T, W), BF16), jax.ShapeDtypeStruct((1, nq), F32),
                                     jax.ShapeDtypeStruct((1, nkv), F32)],
                          compiler_params=_cp(1))(dcq, dckv, dkr, pa, g_cq, g_ckv)


def _sumsq(v):
    return jnp.sum(v * v, axis=-1, keepdims=True)


def _mla_prep_fwd(q_raw, kv, pa, tabs, gq, gk, H, kr_block, tm=256):
    T = q_raw.shape[0]
    tm = _tile(T, tm)
    scale = MLA_QK ** -0.5
    P = MLA_PAD

    def body(q_ref, kv_ref, kr_ref, c_ref, sa_ref, sb_ref, gq_ref, gk_ref, qo, ko, vo):
        c, sa, sb = c_ref[...], sa_ref[...], sb_ref[...]
        kr = kr_ref[...]
        kr2 = _sumsq(kr)
        for h in range(H):
            lo, hi = q_ref[:, h * P:h * P + 128], q_ref[:, h * P + 128:(h + 1) * P]
            r = lax.rsqrt((_sumsq(lo) + _sumsq(hi)) * (1.0 / MLA_QK) + EPS)
            qo[:, h * P:h * P + 128] = (lo * r * gq_ref[:, :128] * scale).astype(BF16)
            qo[:, h * P + 128:(h + 1) * P] = (_rope(hi * r * gq_ref[:, 128:], c, sa, sb, 32) * scale).astype(BF16)
            kn = kv_ref[:, h * P:h * P + 128]
            r = lax.rsqrt((_sumsq(kn) + kr2) * (1.0 / MLA_QK) + EPS)
            ko[:, h * P:h * P + 128] = (kn * r * gk_ref[:, :128]).astype(BF16)
            ko[:, h * P + 128:(h + 1) * P] = _rope(kr * r * gk_ref[:, 128:], c, sa, sb, 32).astype(BF16)
            vo[:, h * 128:(h + 1) * 128] = kv_ref[:, h * P + 128:(h + 1) * P].astype(BF16)

    wide = pl.BlockSpec((tm, H * P), lambda i: (i, 0))
    lane = pl.BlockSpec((tm, 128), lambda i: (i, 0))
    vec = pl.BlockSpec((1, P), lambda i: (0, 0))
    return pl.pallas_call(body, name="mla_prep_fwd", grid=(T // tm,),
                          in_specs=[wide, wide, pl.BlockSpec((tm, 128), lambda i: (i, kr_block)), lane, lane, lane, vec, vec],
                          out_specs=[wide, wide, pl.BlockSpec((tm, H * 128), lambda i: (i, 0))],
                          out_shape=[jax.ShapeDtypeStruct((T, H * P), BF16), jax.ShapeDtypeStruct((T, H * P), BF16),
                                     jax.ShapeDtypeStruct((T, H * 128), BF16)],
                          compiler_params=_cp(1))(q_raw, kv, pa, *tabs, gq, gk)


def _mla_prep_bwd(dq, dk, dv, q_raw, kv, pa, tabs, gq, gk, H, kr_block, tm=256):
    T = q_raw.shape[0]
    tm = _tile(T, tm)
    scale = MLA_QK ** -0.5
    P = MLA_PAD

    def body(dq_ref, dk_ref, dv_ref, q_ref, kv_ref, kr_ref, c_ref, sa_ref, sb_ref, gq_ref, gk_ref,
             dqr, dkv, dkr, dgq, dgk):
        c, sa, sb = c_ref[...], sa_ref[...], sb_ref[...]
        kr = kr_ref[...]
        kr2 = _sumsq(kr)
        gql, gqh, gkl, gkh = gq_ref[:, :128], gq_ref[:, 128:], gk_ref[:, :128], gk_ref[:, 128:]
        dkr_acc = jnp.zeros((tm, 128), F32)
        sums = [jnp.zeros((1, 128), F32) for _ in range(4)]
        for h in range(H):
            lo_s, hi_s = slice(h * P, h * P + 128), slice(h * P + 128, (h + 1) * P)
            lo, hi = q_ref[:, lo_s], q_ref[:, hi_s]
            r = lax.rsqrt((_sumsq(lo) + _sumsq(hi)) * (1.0 / MLA_QK) + EPS)
            ql, qh = lo * r, hi * r
            dyl = dq_ref[:, lo_s] * scale
            dyh = _rope_t(dq_ref[:, hi_s] * scale, c, sa, sb, 32)
            gl, gh = dyl * gql, dyh * gqh
            mean = (jnp.sum(gl * ql, axis=-1, keepdims=True) + jnp.sum(gh * qh, axis=-1, keepdims=True)) * (1.0 / MLA_QK)
            dqr[:, lo_s] = (r * (gl - ql * mean)).astype(BF16)
            dqr[:, hi_s] = (r * (gh - qh * mean)).astype(BF16)
            sums[0] += jnp.sum(dyl * ql, axis=0, keepdims=True)
            sums[1] += jnp.sum(dyh * qh, axis=0, keepdims=True)
            kn = kv_ref[:, lo_s]
            r = lax.rsqrt((_sumsq(kn) + kr2) * (1.0 / MLA_QK) + EPS)
            kl, kh = kn * r, kr * r
            dkl = dk_ref[:, lo_s]
            dkh = _rope_t(dk_ref[:, hi_s], c, sa, sb, 32)
            gl, gh = dkl * gkl, dkh * gkh
            mean = (jnp.sum(gl * kl, axis=-1, keepdims=True) + jnp.sum(gh * kh, axis=-1, keepdims=True)) * (1.0 / MLA_QK)
            dkv[:, lo_s] = (r * (gl - kl * mean)).astype(BF16)
            dkr_acc += r * (gh - kh * mean)
            dkv[:, hi_s] = dv_ref[:, h * 128:(h + 1) * 128].astype(BF16)
            sums[2] += jnp.sum(dkl * kl, axis=0, keepdims=True)
            sums[3] += jnp.sum(dkh * kh, axis=0, keepdims=True)
        dkr[...] = dkr_acc

        @pl.when(pl.program_id(0) == 0)
        def _():
            dgq[...] = jnp.zeros_like(dgq)
            dgk[...] = jnp.zeros_like(dgk)

        dgq[:, :128] += sums[0]
        dgq[:, 128:] += sums[1]
        dgk[:, :128] += sums[2]
        dgk[:, 128:] += sums[3]

    wide = pl.BlockSpec((tm, H * P), lambda i: (i, 0))
    lane = pl.BlockSpec((tm, 128), lambda i: (i, 0))
    vec = pl.BlockSpec((1, P), lambda i: (0, 0))
    return pl.pallas_call(body, name="mla_prep_bwd", grid=(T // tm,),
                          in_specs=[wide, wide, pl.BlockSpec((tm, H * 128), lambda i: (i, 0)), wide, wide,
                                    pl.BlockSpec((tm, 128), lambda i: (i, kr_block)), lane, lane, lane, vec, vec],
                          out_specs=[wide, wide, lane, vec, vec],
                          out_shape=[jax.ShapeDtypeStruct((T, H * P), BF16), jax.ShapeDtypeStruct((T, H * P), BF16),
                                     jax.ShapeDtypeStruct((T, 128), F32), jax.ShapeDtypeStruct((1, P), F32),
                                     jax.ShapeDtypeStruct((1, P), F32)],
                          compiler_params=_cp(1))(dq, dk, dv, q_raw, kv, pa, *tabs, gq, gk)


def _on_blocks(s, live, step):
    @pl.when(s == 0)
    def _():
        step(True)

    @pl.when((s > 0) & live)
    def _():
        step(False)


def _sub_mask(sub, width, row0, col0, q_axis):
    r = lax.broadcasted_iota(jnp.int32, (sub, width), 0) + row0
    c = lax.broadcasted_iota(jnp.int32, (sub, width), 1) + col0
    return (r >= c) if q_axis == 0 else (c >= r)


def _tri_pos(t, nq):
    row = 0
    for k in range(1, nq):
        row = row + (t >= k * (k + 1) // 2).astype(jnp.int32)
    return row, t - (row * (row + 1)) // 2


def _mla_fwd(q, k, v, H, blk, comm=None):
    T, wq = q.shape[0], q.shape[1] // H
    nq = T // blk
    sub = _tile(blk, MLA_SUB)

    def body(q_ref, k_ref, v_ref, o_ref, lse_ref, m_sc, l_sc, acc_sc):
        i, s = _tri_pos(pl.program_id(1), nq)

        @pl.when(s == 0)
        def _():
            m_sc[...] = jnp.full_like(m_sc, NEG)
            l_sc[...] = jnp.zeros_like(l_sc)
            acc_sc[...] = jnp.zeros_like(acc_sc)

        def step(masked):
            for r in range(blk // sub):
                rs = slice(r * sub, (r + 1) * sub)
                nk = (r + 1) * sub if masked else blk
                sc = _dot(q_ref[rs, :], k_ref[:nk, :], "nt")
                if masked:
                    sc = jnp.where(_sub_mask(sub, nk, r * sub, 0, 0), sc, NEG)
                m_prev = m_sc[rs, :]
                m_new = jnp.maximum(m_prev, jnp.max(sc, axis=-1, keepdims=True))
                p = jnp.exp(sc - m_new)
                alpha = jnp.exp(m_prev - m_new)
                l_sc[rs, :] = alpha * l_sc[rs, :] + jnp.sum(p, axis=-1, keepdims=True)
                acc_sc[rs, :] = alpha * acc_sc[rs, :] + _dot(p, v_ref[:nk, :], "nn")
                m_sc[rs, :] = m_new

        _on_blocks(s, i - s >= 0, step)

        @pl.when(s == i)
        def _():
            o_ref[...] = (acc_sc[...] / l_sc[...]).astype(BF16)
            lse_ref[...] = m_sc[...] + jnp.log(l_sc[...])

    qi = lambda t: _tri_pos(t, nq)[0]
    kj = lambda t: _tri_pos(t, nq)[0] - _tri_pos(t, nq)[1]
    kv_spec = lambda w: pl.BlockSpec((blk, w), lambda h, t: (kj(t), h))
    return _call(
        body, name="mla_fwd", grid=(H, nq * (nq + 1) // 2),
        in_specs=[pl.BlockSpec((blk, wq), lambda h, t: (qi(t), h)), kv_spec(wq), kv_spec(HEAD)],
        out_specs=[pl.BlockSpec((blk, HEAD), lambda h, t: (qi(t), h)), pl.BlockSpec((None, blk, 1), lambda h, t: (h, qi(t), 0))],
        out_shape=[jax.ShapeDtypeStruct((T, H * HEAD), BF16), jax.ShapeDtypeStruct((H, T, 1), F32)],
        scratch=[pltpu.VMEM((blk, 1), F32), pltpu.VMEM((blk, 1), F32), pltpu.VMEM((blk, HEAD), F32)],
        operands=[q, k, v], comm=comm)


def _mla_dq(q, k, v, do, lse, dl, H, blk, comm=None):
    T, wq = q.shape[0], q.shape[1] // H
    nq = T // blk
    sub = _tile(blk, MLA_SUB)

    def body(q_ref, k_ref, v_ref, do_ref, lse_ref, dl_ref, dq_ref, acc_sc):
        i, s = _tri_pos(pl.program_id(1), nq)

        @pl.when(s == 0)
        def _():
            acc_sc[...] = jnp.zeros_like(acc_sc)

        def step(masked):
            for r in range(blk // sub):
                rs = slice(r * sub, (r + 1) * sub)
                nk = (r + 1) * sub if masked else blk
                sc = _dot(q_ref[rs, :], k_ref[:nk, :], "nt")
                if masked:
                    sc = jnp.where(_sub_mask(sub, nk, r * sub, 0, 0), sc, NEG)
                p = jnp.exp(sc - lse_ref[rs, :])
                dp = _dot(do_ref[rs, :], v_ref[:nk, :], "nt")
                acc_sc[rs, :] += _dot(p * (dp - dl_ref[rs, :]), k_ref[:nk, :], "nn")

        _on_blocks(s, i - s >= 0, step)

        @pl.when(s == i)
        def _():
            dq_ref[...] = acc_sc[...]

    qi = lambda t: _tri_pos(t, nq)[0]
    kj = lambda t: _tri_pos(t, nq)[0] - _tri_pos(t, nq)[1]
    kv_spec = lambda w: pl.BlockSpec((blk, w), lambda h, t: (kj(t), h))
    col = pl.BlockSpec((None, blk, 1), lambda h, t: (h, qi(t), 0))
    return _call(
        body, name="mla_dq", grid=(H, nq * (nq + 1) // 2),
        in_specs=[pl.BlockSpec((blk, wq), lambda h, t: (qi(t), h)), kv_spec(wq), kv_spec(HEAD),
                  pl.BlockSpec((blk, HEAD), lambda h, t: (qi(t), h)), col, col],
        out_specs=pl.BlockSpec((blk, wq), lambda h, t: (qi(t), h)), out_shape=jax.ShapeDtypeStruct((T, H * wq), F32),
        scratch=[pltpu.VMEM((blk, wq), F32)], operands=[q, k, v, do, lse, dl], comm=comm)


def _as_lanes(col):
    return jnp.transpose(jnp.broadcast_to(col, (col.shape[0], 128)))[0:1, :]


def _mla_dkv(q, k, v, do, lse, dl, H, blk):
    T, wq = q.shape[0], q.shape[1] // H
    nq = T // blk
    sub = _tile(blk, MLA_SUB)

    def body(q_ref, k_ref, v_ref, do_ref, lse_ref, dl_ref, dk_ref, dv_ref, dk_sc, dv_sc):
        left, s = _tri_pos(pl.program_id(1), nq)

        @pl.when(s == 0)
        def _():
            dk_sc[...] = jnp.zeros_like(dk_sc)
            dv_sc[...] = jnp.zeros_like(dv_sc)

        def step(masked):
            for r in range(blk // sub):
                rs = slice(r * sub, (r + 1) * sub)
                q0 = r * sub if masked else 0
                st = _dot(k_ref[rs, :], q_ref[q0:, :], "nt")
                if masked:
                    st = jnp.where(_sub_mask(sub, blk - q0, r * sub, q0, 1), st, NEG)
                pt = jnp.exp(st - lse_ref[:, q0:])
                dpt = _dot(v_ref[rs, :], do_ref[q0:, :], "nt")
                dv_sc[rs, :] += _dot(pt, do_ref[q0:, :], "nn")
                dk_sc[rs, :] += _dot(pt * (dpt - dl_ref[:, q0:]), q_ref[q0:, :], "nn")

        _on_blocks(s, s <= left, step)

        @pl.when(s == left)
        def _():
            dk_ref[...] = dk_sc[...]
            dv_ref[...] = dv_sc[...]

    kj = lambda t: nq - 1 - _tri_pos(t, nq)[0]
    qi = lambda t: nq - 1 - _tri_pos(t, nq)[0] + _tri_pos(t, nq)[1]
    row = pl.BlockSpec((None, 1, blk), lambda h, t: (h, 0, qi(t)))
    return pl.pallas_call(
        body, name="mla_dkv", grid=(H, nq * (nq + 1) // 2),
        in_specs=[pl.BlockSpec((blk, wq), lambda h, t: (qi(t), h)), pl.BlockSpec((blk, wq), lambda h, t: (kj(t), h)),
                  pl.BlockSpec((blk, HEAD), lambda h, t: (kj(t), h)), pl.BlockSpec((blk, HEAD), lambda h, t: (qi(t), h)),
                  row, row],
        out_specs=[pl.BlockSpec((blk, wq), lambda h, t: (kj(t), h)), pl.BlockSpec((blk, HEAD), lambda h, t: (kj(t), h))],
        out_shape=[jax.ShapeDtypeStruct((T, H * wq), F32), jax.ShapeDtypeStruct((T, H * HEAD), F32)],
        scratch_shapes=[pltpu.VMEM((blk, wq), F32), pltpu.VMEM((blk, HEAD), F32)],
        compiler_params=_cp(2))(q, k, v, do, lse, dl)


def _row_dot(name, a, b, H, tm=1024):
    T = a.shape[0]
    tm = _tile(T, tm)

    def body(a_ref, b_ref, o_ref):
        o_ref[...] = jnp.sum(a_ref[...].astype(F32) * b_ref[...].astype(F32), axis=-1, keepdims=True)

    blk = pl.BlockSpec((tm, HEAD), lambda h, i: (i, h))
    return pl.pallas_call(body, name=name, grid=(H, T // tm), in_specs=[blk, blk],
                          out_specs=pl.BlockSpec((None, tm, 1), lambda h, i: (h, i, 0)),
                          out_shape=jax.ShapeDtypeStruct((H, T, 1), F32), compiler_params=_cp(2))(a, b)


def _tri(n):
    row = lax.broadcasted_iota(jnp.int32, (n, n), 0)
    col = lax.broadcasted_iota(jnp.int32, (n, n), 1)
    return col <= row, col >= row


def _units(ref, U, n):
    return ref[...].reshape(U, n, ref.shape[-1])


def _shift_in(first, units):
    return jnp.concatenate([first[None], units[:-1]], axis=0)


def _shift_out(units, last):
    return jnp.concatenate([units[1:], last[None]], axis=0)


def _starts_inside(row0, U, n, L, limit=None):
    start = row0 + n * lax.broadcasted_iota(jnp.int32, (U, n, n), 0)
    ok = ((start & (L - 1)) if L & (L - 1) == 0 else (start % L)) != 0
    return ok if limit is None else ok & (start < limit)


def _bdot(a, b, mode):
    dims = (((2,), (2,)), ((0,), (0,))) if mode == "nt" else (((2,), (1,)), ((0,), (0,)))
    return lax.dot_general(a.astype(BF16), b.astype(BF16), dims, preferred_element_type=F32)


def _win_fwd(name, q, k, pd, H, n, L, QB):
    T = q.shape[0]
    U = QB // n

    def body(q_ref, k_ref, v_ref, kp_ref, vp_ref, o_ref, lse_ref):
        own_ok, before_ok = _tri(n)
        q3, k3, v3 = _units(q_ref, U, n), _units(k_ref, U, n), _units(v_ref, U, n)
        k_lo, v_lo = _shift_in(kp_ref[...], k3), _shift_in(vp_ref[...], v3)
        inside = _starts_inside(pl.program_id(1) * QB, U, n, L)
        s_hi = jnp.where(own_ok, _bdot(q3, k3, "nt"), NEG)
        s_lo = jnp.where(before_ok & inside, _bdot(q3, k_lo, "nt"), NEG)
        m = jnp.maximum(jnp.max(s_hi, axis=-1, keepdims=True), jnp.max(s_lo, axis=-1, keepdims=True))
        p_hi, p_lo = jnp.exp(s_hi - m), jnp.exp(s_lo - m)
        l = jnp.sum(p_hi, axis=-1, keepdims=True) + jnp.sum(p_lo, axis=-1, keepdims=True)
        acc = _bdot(p_hi, v3, "nn") + _bdot(p_lo, v_lo, "nn")
        o_ref[...] = (acc / l).reshape(QB, HEAD).astype(BF16)
        lse_ref[...] = (m + jnp.log(l)).reshape(QB, 1)

    cur = lambda c0: pl.BlockSpec((QB, HEAD), lambda h, i: (i, c0 + h))
    prev = lambda c0: pl.BlockSpec((n, HEAD), lambda h, i: (jnp.maximum(i * U - 1, 0), c0 + h))
    return pl.pallas_call(
        body, name=name, grid=(H, T // QB), in_specs=[cur(0), cur(0), cur(2 * H), prev(0), prev(2 * H)],
        out_specs=[pl.BlockSpec((QB, HEAD), lambda h, i: (i, h)), pl.BlockSpec((None, QB, 1), lambda h, i: (h, i, 0))],
        out_shape=[jax.ShapeDtypeStruct((T, H * HEAD), BF16), jax.ShapeDtypeStruct((H, T, 1), F32)],
        compiler_params=_cp(2))(q, k, pd, k, pd)


def _win_dq(name, q, k, pd, do, lse, dl, H, n, L, QB):
    T = q.shape[0]
    U = QB // n

    def body(q_ref, k_ref, v_ref, kp_ref, vp_ref, do_ref, lse_ref, dl_ref, dq_ref):
        own_ok, before_ok = _tri(n)
        q3, k3, v3, do3 = _units(q_ref, U, n), _units(k_ref, U, n), _units(v_ref, U, n), _units(do_ref, U, n)
        lse3, dl3 = _units(lse_ref, U, n), _units(dl_ref, U, n)
        k_lo, v_lo = _shift_in(kp_ref[...], k3), _shift_in(vp_ref[...], v3)
        inside = _starts_inside(pl.program_id(1) * QB, U, n, L)
        p_hi = jnp.exp(jnp.where(own_ok, _bdot(q3, k3, "nt"), NEG) - lse3)
        p_lo = jnp.exp(jnp.where(before_ok & inside, _bdot(q3, k_lo, "nt"), NEG) - lse3)
        ds_hi = p_hi * (_bdot(do3, v3, "nt") - dl3)
        ds_lo = p_lo * (_bdot(do3, v_lo, "nt") - dl3)
        dq_ref[...] = (_bdot(ds_hi, k3, "nn") + _bdot(ds_lo, k_lo, "nn")).reshape(QB, HEAD)

    cur = lambda c0: pl.BlockSpec((QB, HEAD), lambda h, i: (i, c0 + h))
    prev = lambda c0: pl.BlockSpec((n, HEAD), lambda h, i: (jnp.maximum(i * U - 1, 0), c0 + h))
    flat = pl.BlockSpec((QB, HEAD), lambda h, i: (i, h))
    col = pl.BlockSpec((None, QB, 1), lambda h, i: (h, i, 0))
    return pl.pallas_call(
        body, name=name, grid=(H, T // QB), in_specs=[cur(0), cur(0), cur(2 * H), prev(0), prev(2 * H), flat, col, col],
        out_specs=flat, out_shape=jax.ShapeDtypeStruct((T, H * HEAD), F32), compiler_params=_cp(2))(q, k, pd, k, pd, do, lse, dl)


def _win_dkv(name, q, k, pd, do, lse, dl, H, n, L, QB):
    T = q.shape[0]
    U = QB // n
    last = T // n - 1

    def body(q_ref, k_ref, v_ref, do_ref, lse_col, dl_col, qn_ref, don_ref, lsen_col, dln_col, dk_ref, dv_ref):
        own_ok, after_ok = _tri_t(n)
        q3, k3, v3, do3 = _units(q_ref, U, n), _units(k_ref, U, n), _units(v_ref, U, n), _units(do_ref, U, n)
        qn3, don3 = _shift_out(q3, qn_ref[...]), _shift_out(do3, don_ref[...])

        def rows(col_ref, next_col_ref):
            row = _as_lanes(col_ref[...])
            own = [row[:, u * n:(u + 1) * n] for u in range(U)]
            return jnp.stack(own), jnp.stack(own[1:] + [_as_lanes(next_col_ref[...])])

        lse_own, lse_aft = rows(lse_col, lsen_col)
        dl_own, dl_aft = rows(dl_col, dln_col)
        inside = _starts_inside(pl.program_id(1) * QB + n, U, n, L, limit=T)
        pt_own = jnp.exp(jnp.where(own_ok, _bdot(k3, q3, "nt"), NEG) - lse_own)
        pt_aft = jnp.exp(jnp.where(after_ok & inside, _bdot(k3, qn3, "nt"), NEG) - lse_aft)
        dst_own = pt_own * (_bdot(v3, do3, "nt") - dl_own)
        dst_aft = pt_aft * (_bdot(v3, don3, "nt") - dl_aft)
        dv_ref[...] = (_bdot(pt_own, do3, "nn") + _bdot(pt_aft, don3, "nn")).reshape(QB, HEAD)
        dk_ref[...] = (_bdot(dst_own, q3, "nn") + _bdot(dst_aft, qn3, "nn")).reshape(QB, HEAD)

    cur = lambda c0: pl.BlockSpec((QB, HEAD), lambda h, i: (i, c0 + h))
    flat = pl.BlockSpec((QB, HEAD), lambda h, i: (i, h))
    row = pl.BlockSpec((None, QB, 1), lambda h, i: (h, i, 0))
    nxt_unit = lambda h, i: jnp.minimum((i + 1) * U, last)
    return pl.pallas_call(
        body, name=name, grid=(H, T // QB),
        in_specs=[cur(0), cur(0), cur(2 * H), flat, row, row,
                  pl.BlockSpec((n, HEAD), lambda h, i: (nxt_unit(h, i), h)),
                  pl.BlockSpec((n, HEAD), lambda h, i: (nxt_unit(h, i), h)),
                  pl.BlockSpec((None, n, 1), lambda h, i: (h, nxt_unit(h, i), 0)),
                  pl.BlockSpec((None, n, 1), lambda h, i: (h, nxt_unit(h, i), 0))],
        out_specs=[flat, flat], out_shape=[jax.ShapeDtypeStruct((T, H * HEAD), F32)] * 2,
        compiler_params=_cp(2))(q, k, pd, do, lse, dl, q, do, lse, dl)


def _tri_t(n):
    key = lax.broadcasted_iota(jnp.int32, (n, n), 0)
    qry = lax.broadcasted_iota(jnp.int32, (n, n), 1)
    return key <= qry, key >= qry


def _merge_groups(os_, lses, H, tm=2048):
    G = len(os_)
    T = os_[0].shape[0]
    tm = _tile(T, tm)

    def body(*refs):
        o_refs, l_refs, o_out, lse_out = refs[:G], refs[G:2 * G], refs[2 * G], refs[2 * G + 1]
        ls = [r[...] for r in l_refs]
        m = ls[0]
        for x in ls[1:]:
            m = jnp.maximum(m, x)
        ws = [jnp.exp(x - m) for x in ls]
        tot = ws[0]
        for x in ws[1:]:
            tot = tot + x
        acc = ws[0] * o_refs[0][...]
        for x, r in zip(ws[1:], o_refs[1:]):
            acc = acc + x * r[...]
        o_out[...] = (acc / tot).astype(BF16)
        lse_out[...] = m + jnp.log(tot)

    flat = pl.BlockSpec((tm, HEAD), lambda h, i: (i, h))
    col = pl.BlockSpec((None, tm, 1), lambda h, i: (h, i, 0))
    return pl.pallas_call(body, name="dil_merge", grid=(H, T // tm), in_specs=[flat] * G + [col] * G, out_specs=[flat, col],
                          out_shape=[jax.ShapeDtypeStruct((T, H * HEAD), BF16), jax.ShapeDtypeStruct((H, T, 1), F32)],
                          compiler_params=_cp(2))(*os_, *lses)


def _dil_prep_fwd_pm(name, pd, tabs, gq, gk, H, tm=256):
    T = pd.shape[0]
    tm = _tile(T, tm)
    scale = HEAD ** -0.5

    def body(p_ref, c_ref, sa_ref, sb_ref, gq_ref, gk_ref, qo, ko):
        c, sa, sb = c_ref[...], sa_ref[...], sb_ref[...]
        for h in range(H):
            q = p_ref[:, h * 128:(h + 1) * 128].astype(F32)
            qo[:, h * 128:(h + 1) * 128] = (_rope(q * _rms_stats(q, HEAD) * gq_ref[...], c, sa, sb, 16) * scale).astype(BF16)
            k = p_ref[:, (H + h) * 128:(H + h + 1) * 128].astype(F32)
            ko[:, h * 128:(h + 1) * 128] = _rope(k * _rms_stats(k, HEAD) * gk_ref[...], c, sa, sb, 16).astype(BF16)

    lane = pl.BlockSpec((tm, 128), lambda i: (i, 0))
    gain = pl.BlockSpec((1, 128), lambda i: (0, 0))
    out = pl.BlockSpec((tm, H * 128), lambda i: (i, 0))
    return pl.pallas_call(body, name=name, grid=(T // tm,),
                          in_specs=[pl.BlockSpec((tm, 3 * H * 128), lambda i: (i, 0)), lane, lane, lane, gain, gain],
                          out_specs=[out, out], out_shape=[jax.ShapeDtypeStruct((T, H * 128), BF16)] * 2,
                          compiler_params=_cp(1))(pd, *tabs, gq, gk)


def _dil_prep_bwd_pm(name, dq, dk, dv, pd, tabs, gq, gk, H, tm=256):
    T = pd.shape[0]
    tm = _tile(T, tm)
    scale = HEAD ** -0.5

    def body(dq_ref, dk_ref, dv_ref, p_ref, c_ref, sa_ref, sb_ref, gq_ref, gk_ref, dp_ref, dgq, dgk):
        c, sa, sb = c_ref[...], sa_ref[...], sb_ref[...]
        sq = jnp.zeros((1, 128), F32)
        sk = jnp.zeros((1, 128), F32)
        for h in range(H):
            hs = slice(h * 128, (h + 1) * 128)
            q = p_ref[:, hs].astype(F32)
            dx, dgp = _rms_bwd_rows(_rope_t(dq_ref[:, hs] * scale, c, sa, sb, 16), q, gq_ref[...], HEAD)
            dp_ref[:, hs] = dx.astype(BF16)
            sq += jnp.sum(dgp, axis=0, keepdims=True)
            ks = slice((H + h) * 128, (H + h + 1) * 128)
            k = p_ref[:, ks].astype(F32)
            dx, dgp = _rms_bwd_rows(_rope_t(dk_ref[:, hs], c, sa, sb, 16), k, gk_ref[...], HEAD)
            dp_ref[:, ks] = dx.astype(BF16)
            sk += jnp.sum(dgp, axis=0, keepdims=True)
            dp_ref[:, (2 * H + h) * 128:(2 * H + h + 1) * 128] = dv_ref[:, hs].astype(BF16)

        @pl.when(pl.program_id(0) == 0)
        def _():
            dgq[...] = jnp.zeros_like(dgq)
            dgk[...] = jnp.zeros_like(dgk)

        dgq[...] += sq
        dgk[...] += sk

    lane = pl.BlockSpec((tm, 128), lambda i: (i, 0))
    flat = pl.BlockSpec((tm, H * 128), lambda i: (i, 0))
    vec = pl.BlockSpec((1, 128), lambda i: (0, 0))
    return pl.pallas_call(body, name=name, grid=(T // tm,),
                          in_specs=[flat, flat, flat, pl.BlockSpec((tm, 3 * H * 128), lambda i: (i, 0)),
                                    lane, lane, lane, vec, vec],
                          out_specs=[pl.BlockSpec((tm, 3 * H * 128), lambda i: (i, 0)), vec, vec],
                          out_shape=[jax.ShapeDtypeStruct((T, 3 * H * 128), BF16), jax.ShapeDtypeStruct((1, 128), F32),
                                     jax.ShapeDtypeStruct((1, 128), F32)],
                          compiler_params=_cp(1))(dq, dk, dv, pd, *tabs, gq, gk)


def _to_phase(a, d, axis=0):
    if d == 1:
        return a
    sh = a.shape
    T = sh[axis]
    b = a.reshape(*sh[:axis], T // d, d, *sh[axis + 1:])
    return jnp.swapaxes(b, axis, axis + 1).reshape(sh)


def _from_phase(a, d, axis=0):
    if d == 1:
        return a
    sh = a.shape
    T = sh[axis]
    b = a.reshape(*sh[:axis], d, T // d, *sh[axis + 1:])
    return jnp.swapaxes(b, axis, axis + 1).reshape(sh)


def _ffn_up(name, n, wg, wu, tm=1024, comm=None):
    T, D = n.shape
    nd, _, fc = wg.shape
    tm = _tile(T, tm)

    def epilogue(accs, in_refs, out_refs):
        a, b = accs
        out_refs[0][...] = a.astype(BF16)
        out_refs[1][...] = b.astype(BF16)
        out_refs[2][...] = (a * _sigmoid(a) * b).astype(BF16)

    w_spec = pl.BlockSpec((None, D, fc), lambda j, i: (j, 0, 0))
    o_spec = pl.BlockSpec((None, tm, fc), lambda j, i: (j, i, 0))
    sh = jax.ShapeDtypeStruct((nd, T, fc), BF16)
    return _mm(name, (nd, T // tm), [n, wg, wu], [pl.BlockSpec((tm, D), lambda j, i: (i, 0)), w_spec, w_spec],
               [(0, 1, "nn", 0), (0, 2, "nn", 1)], 2, None, epilogue, [sh, sh, sh], [o_spec] * 3, None, comm=comm)


def _ffn_down(name, s, wd, res, tm=512, tn=512, comm=None):
    nd, T, fc = s.shape
    D = wd.shape[2]
    tm, tn = _tile(T, tm), _tile(D, tn)

    def body(s_ref, w_ref, r_ref, o_ref):
        acc = _dot(s_ref[0], w_ref[0], "nn")
        for j in range(1, nd):
            acc += _dot(s_ref[j], w_ref[j], "nn")
        o_ref[...] = r_ref[...] + 0.5 * acc

    mn = pl.BlockSpec((tm, tn), lambda i, j: (i, j))
    return _call(body, name=name, grid=(T // tm, D // tn),
                 in_specs=[pl.BlockSpec((nd, tm, fc), lambda i, j: (0, i, 0)), pl.BlockSpec((nd, fc, tn), lambda i, j: (0, 0, j)), mn],
                 out_specs=mn, out_shape=jax.ShapeDtypeStruct((T, D), F32), operands=[s, wd, res], comm=comm)


def _ffn_bwd_act(name, dxb, wdt, a, b, tm=1024, comm=None):
    T, D = dxb.shape
    nd, _, fc = wdt.shape
    tm = _tile(T, tm)

    def epilogue(accs, in_refs, out_refs):
        ds = 0.5 * accs[0]
        av, bv = in_refs[2][...].astype(F32), in_refs[3][...].astype(F32)
        sg = _sigmoid(av)
        out_refs[0][...] = (ds * bv * sg * (1.0 + av * (1.0 - sg))).astype(BF16)
        out_refs[1][...] = (ds * av * sg).astype(BF16)

    act = pl.BlockSpec((None, tm, fc), lambda j, i: (j, i, 0))
    sh = jax.ShapeDtypeStruct((nd, T, fc), BF16)
    return _mm(name, (nd, T // tm), [dxb, wdt, a, b],
               [pl.BlockSpec((tm, D), lambda j, i: (i, 0)), pl.BlockSpec((None, D, fc), lambda j, i: (j, 0, 0)), act, act],
               [(0, 1, "nn", 0)], 1, None, epilogue, [sh, sh], [act, act], None, comm=comm)


def _ffn_dwd(name, s, dxb, tk=2048):
    nd, T, fc = s.shape
    D = dxb.shape[1]
    tk = _tile(T, tk)

    def epilogue(accs, in_refs, out_refs):
        out_refs[0][...] = (0.5 * accs[0]).astype(BF16)

    return _mm(name, (nd, T // tk), [s, dxb],
               [pl.BlockSpec((None, tk, fc), lambda j, k: (j, k, 0)), pl.BlockSpec((tk, D), lambda j, k: (k, 0))],
               [(0, 1, "tn", 0)], 1, (fc, D), epilogue, [jax.ShapeDtypeStruct((nd, fc, D), BF16)],
               [pl.BlockSpec((None, fc, D), lambda j, k: (j, 0, 0))], 1)[0]


def _ffn_dw(name, n, dact, tk=2048):
    T, D = n.shape
    nd, _, fc = dact.shape
    tk = _tile(T, tk)

    def epilogue(accs, in_refs, out_refs):
        out_refs[0][...] = accs[0].astype(BF16)

    return _mm(name, (nd, T // tk), [n, dact],
               [pl.BlockSpec((tk, D), lambda j, k: (k, 0)), pl.BlockSpec((None, tk, fc), lambda j, k: (j, k, 0))],
               [(0, 1, "tn", 0)], 1, (D, fc), epilogue, [jax.ShapeDtypeStruct((nd, D, fc), BF16)],
               [pl.BlockSpec((None, D, fc), lambda j, k: (j, 0, 0))], 1)[0]


def _ffn_dn(name, da, db, wg, wu, tm=512, tn=256, comm=None):
    nd, T, fc = da.shape
    D = wg.shape[1]
    tm, tn = _tile(T, tm), _tile(D, tn)

    def body(da_ref, db_ref, wg_ref, wu_ref, o_ref):
        acc = _dot(da_ref[0], wg_ref[0], "nt") + _dot(db_ref[0], wu_ref[0], "nt")
        for j in range(1, nd):
            acc += _dot(da_ref[j], wg_ref[j], "nt") + _dot(db_ref[j], wu_ref[j], "nt")
        o_ref[...] = acc

    act = pl.BlockSpec((nd, tm, fc), lambda i, j: (0, i, 0))
    w_spec = pl.BlockSpec((nd, tn, fc), lambda i, j: (0, j, 0))
    return _call(body, name=name, grid=(T // tm, D // tn), in_specs=[act, act, w_spec, w_spec],
                 out_specs=pl.BlockSpec((tm, tn), lambda i, j: (i, j)), out_shape=jax.ShapeDtypeStruct((T, D), F32),
                 operands=[da, db, wg, wu], comm=comm)


def _ffn_forward(tag, x, g, wg, wu, wd, comm_up=None, comm_down=None):
    n = _rmsnorm_fwd(tag + "_norm", x, g)
    a, b, s = _ffn_up(tag + "_up", n, wg, wu, comm=comm_up)
    return _ffn_down(tag + "_down", s, wd, x, comm=comm_down() if comm_down else None), (n, a, b, s)


def _ffn_backward(tag, dx, dxb, x, g, wg, wu, wd, saved, comm_act=None, comm_dn=None, comm_norm=None, weights_first=True):
    n, a, b, s = saved
    da, db = _ffn_bwd_act(tag + "_bwd_act", dxb, jnp.swapaxes(wd, 1, 2), a, b, comm=comm_act)

    def weight_grads():
        return _ffn_dw(tag + "_dwg", n, da), _ffn_dw(tag + "_dwu", n, db), _ffn_dwd(tag + "_dwd", s, dxb)

    dws = weight_grads() if weights_first else None
    dn = _ffn_dn(tag + "_dn", da, db, wg, wu, comm=comm_dn(dws) if comm_dn else None)
    d_wg, d_wu, d_wd = dws if weights_first else weight_grads()
    dx_in, dxb_in, dg = _rmsnorm_bwd(tag + "_norm_bwd", dn, x, g, dx, comm=comm_norm((d_wg, d_wu, d_wd)) if comm_norm else None)
    return dx_in, dxb_in, dg, d_wg, d_wu, d_wd


def _place():
    x, y, c = lax.axis_index("x"), lax.axis_index("y"), lax.axis_index("c")
    return x, y, c, [(1 - x, y), (x, 1 - y), (1 - x, 1 - y)]


def _allgather(name, shards):
    n = len(shards)

    def body(*refs):
        ins, outs = refs[:n], refs[n:2 * n]
        send_sems, recv_sems, local_sems = refs[2 * n:]
        x, y, c, chips = _place()
        me, sibling = (x, y, c), (x, y, 1 - c)

        def slot(a, p):
            return outs[a].at[4 * p[0] + 2 * p[1] + p[2]]

        def copy(a, kk, block, to, src=None):
            return pltpu.make_async_remote_copy(
                src_ref=slot(a, block) if src is None else src, dst_ref=slot(a, block),
                send_sem=send_sems.at[a * 7 + kk], recv_sem=recv_sems.at[a * 7 + kk],
                device_id=to, device_id_type=MESH)

        mine = [pltpu.make_async_copy(ins[a], slot(a, me), local_sems.at[a]) for a in range(n)]
        for cp in mine:
            cp.start()
        first = []
        for a in range(n):
            first.append(copy(a, 0, me, sibling, src=ins[a]))
            first += [copy(a, 1 + j, me, (*chip, c), src=ins[a]) for j, chip in enumerate(chips)]
        for cp in first:
            cp.start()
        passed = []
        for j, chip in enumerate(chips):
            for a in range(n):
                copy(a, 1 + j, (*chip, c), me).wait_recv()
                fwd = copy(a, 4 + j, (*chip, c), sibling)
                fwd.start()
                passed.append(fwd)
        for a in range(n):
            copy(a, 0, sibling, me).wait_recv()
        for j, chip in enumerate(chips):
            for a in range(n):
                copy(a, 4 + j, (*chip, 1 - c), me).wait_recv()
        for cp in first + passed:
            cp.wait_send()
        for cp in mine:
            cp.wait()

    return pl.pallas_call(
        body, name=name, in_specs=[HBM_SPEC] * n, out_specs=[HBM_SPEC] * n,
        out_shape=[jax.ShapeDtypeStruct((N_DEV, *s.shape), s.dtype) for s in shards],
        scratch_shapes=[pltpu.SemaphoreType.DMA((7 * n,)), pltpu.SemaphoreType.DMA((7 * n,)), pltpu.SemaphoreType.DMA((n,))],
    )(*shards)


def _slot(ref, p):
    return ref.at[4 * p[0] + 2 * p[1] + p[2]]


def _ag_first(shards):
    n = len(shards)

    def plan(ins, outs):
        x, y, c, chips = _place()
        me = (x, y, c)
        remote, local = [], []
        for a in range(n):
            local.append((ins[a], _slot(outs[a], me)))
            for peer in [(x, y, 1 - c)] + [(*chip, c) for chip in chips]:
                remote.append((ins[a], _slot(outs[a], me), _slot(outs[a], peer), peer))
        return remote, local

    return _Exchange(shards, [jax.ShapeDtypeStruct((N_DEV, *s.shape), s.dtype) for s in shards], plan, 4 * n, n)


def _ag_second(partial):
    n = len(partial)

    def plan(ins, outs):
        x, y, c, chips = _place()
        remote = []
        for a in range(n):
            for chip in chips:
                remote.append((_slot(ins[a], (*chip, c)), _slot(outs[a], (*chip, c)), _slot(outs[a], (*chip, 1 - c)), (x, y, 1 - c)))
        return remote, []

    return _Exchange(partial, [jax.ShapeDtypeStruct(p.shape, p.dtype) for p in partial], plan, 3 * n, 0,
                     aliases={a: a for a in range(n)})


def _rs_first(halves):
    n = len(halves)

    def plan(ins, outs):
        x, y, c, _ = _place()
        return [(ins[a].at[:, 1 - c], outs[a], outs[a], (x, y, 1 - c)) for a in range(n)], []

    return _Exchange(halves, [jax.ShapeDtypeStruct((h.shape[0], *h.shape[2:]), h.dtype) for h in halves], plan, n, 0)


def _rs_second(sums):
    n = len(sums)

    def plan(ins, outs):
        x, y, c, chips = _place()
        k_me = 2 * x + y
        remote = []
        for a in range(n):
            for chip in chips:
                k_peer = 2 * chip[0] + chip[1]
                remote.append((ins[a].at[k_peer], outs[a].at[k_me], outs[a].at[k_peer], (*chip, c)))
        return remote, [(ins[a].at[k_me], outs[a].at[k_me]) for a in range(n)]

    return _Exchange(sums, [jax.ShapeDtypeStruct(s.shape, s.dtype) for s in sums], plan, 3 * n, n)


def _pair_add(name, own, got, core):
    nch, _, K, N = own.shape
    tr = _row_tile(K, N)

    def body(c_ref, own_ref, got_ref, o_ref):
        o_ref[...] = (own_ref[...].astype(F32) + got_ref[...].astype(F32)).astype(o_ref.dtype)

    grid_spec = pltpu.PrefetchScalarGridSpec(
        num_scalar_prefetch=1, grid=(nch, K // tr),
        in_specs=[pl.BlockSpec((None, None, tr, N), lambda k, r, c_ref: (k, c_ref[0], r, 0)),
                  pl.BlockSpec((None, tr, N), lambda k, r, c_ref: (k, r, 0))],
        out_specs=pl.BlockSpec((None, tr, N), lambda k, r, c_ref: (k, r, 0)))
    return pl.pallas_call(body, name=name, grid_spec=grid_spec, out_shape=jax.ShapeDtypeStruct((nch, K, N), own.dtype),
                          compiler_params=_cp(2))(core, own, got)


def _row_tile(K, N):
    limit = max(16, 262144 // N)
    t = 1
    while t * 2 <= limit and K % (t * 2) == 0:
        t *= 2
    return t if t >= 16 else K


def _adamw(name, parts, w, m, v):
    P, K, N = parts.shape
    tr = _row_tile(K, N)

    def body(p_ref, w_ref, m_ref, v_ref, g_ref, d_ref, nm_ref, nv_ref):
        g = p_ref[0].astype(F32)
        for i in range(1, P):
            g = g + p_ref[i].astype(F32)
        m_new = ADAM_B1 * m_ref[...] + (1.0 - ADAM_B1) * g
        v_new = ADAM_B2 * v_ref[...] + (1.0 - ADAM_B2) * (g * g)
        m_hat = m_new / (1.0 - ADAM_B1 ** ADAM_STEP)
        v_hat = v_new / (1.0 - ADAM_B2 ** ADAM_STEP)
        g_ref[...] = g
        d_ref[...] = -ADAM_LR * (m_hat / (jnp.sqrt(v_hat) + ADAM_EPS) + ADAM_WD * w_ref[...])
        nm_ref[...] = m_new
        nv_ref[...] = v_new

    row = pl.BlockSpec((tr, N), lambda r: (r, 0))
    sh = jax.ShapeDtypeStruct((K, N), F32)
    return pl.pallas_call(body, name=name, grid=(K // tr,),
                          in_specs=[pl.BlockSpec((P, tr, N), lambda r: (0, r, 0)), row, row, row],
                          out_specs=[row] * 4, out_shape=[sh] * 4, compiler_params=_cp(1))(parts, w, m, v)


def _merge_fwd(o_mla, wbm, o_dil, wbd, g0, g1, tm=512, tn=512, comm=None):
    T, K1 = o_mla.shape
    K2, D = o_dil.shape[1], wbm.shape[1]
    tm, tn = _tile(T, tm), _tile(D, tn)

    def epilogue(accs, in_refs, out_refs):
        a, b = accs
        out_refs[0][...] = a.astype(BF16)
        out_refs[1][...] = b.astype(BF16)
        out_refs[2][...] = (in_refs[4][...].astype(F32) * a + in_refs[5][...].astype(F32) * b).astype(BF16)

    mn = pl.BlockSpec((tm, tn), lambda i, j: (i, j))
    sh = jax.ShapeDtypeStruct((T, D), BF16)
    return _mm("merge_fwd", (T // tm, D // tn), [o_mla, wbm, o_dil, wbd, g0, g1],
               [pl.BlockSpec((tm, K1), lambda i, j: (i, 0)), pl.BlockSpec((K1, tn), lambda i, j: (0, j)),
                pl.BlockSpec((tm, K2), lambda i, j: (i, 0)), pl.BlockSpec((K2, tn), lambda i, j: (0, j)), mn, mn],
               [(0, 1, "nn", 0), (2, 3, "nn", 1)], 2, None, epilogue, [sh, sh, sh], [mn, mn, mn], None, comm=comm)


def _ple_loss(n4, wpg, pe, wpp, x3, tgt, tm=512, tn=512):
    T, D = x3.shape
    Kp = pe.shape[1]
    tm, tn = _tile(T, tm), _tile(D, tn)

    def epilogue(accs, in_refs, out_refs):
        z, proj = accs
        pg = _sigmoid(z)
        err = in_refs[4][...] + pg * proj - in_refs[5][...]
        dy = err * (1.0 / D)
        out_refs[0][...] = dy
        out_refs[1][...] = (dy * proj * pg * (1.0 - pg)).astype(BF16)
        out_refs[2][...] = (dy * pg).astype(BF16)

        @pl.when(pl.program_id(1) == 0)
        def _():
            out_refs[3][...] = jnp.zeros_like(out_refs[3])

        out_refs[3][...] += jnp.sum(err * err, axis=-1, keepdims=True)

    mn = pl.BlockSpec((tm, tn), lambda i, j: (i, j))
    return _mm("ple_loss", (T // tm, D // tn), [n4, wpg, pe, wpp, x3, tgt],
               [pl.BlockSpec((tm, D), lambda i, j: (i, 0)), pl.BlockSpec((D, tn), lambda i, j: (0, j)),
                pl.BlockSpec((tm, Kp), lambda i, j: (i, 0)), pl.BlockSpec((Kp, tn), lambda i, j: (0, j)), mn, mn],
               [(0, 1, "nn", 0), (2, 3, "nn", 1)], 2, None, epilogue,
               [jax.ShapeDtypeStruct((T, D), F32), jax.ShapeDtypeStruct((T, D), BF16), jax.ShapeDtypeStruct((T, D), BF16),
                jax.ShapeDtypeStruct((T, 1), F32)],
               [mn, mn, mn, pl.BlockSpec((tm, 1), lambda i, j: (i, 0))], None)


def _sum_nt(name, pairs, prev, tm, tn):
    M, N = pairs[0][0].shape[0], pairs[0][1].shape[0]
    tm, tn = _tile(M, tm), _tile(N, tn)
    ins, specs, terms = [], [], []
    for a, b in pairs:
        K = a.shape[1]
        terms.append((len(ins), len(ins) + 1, "nt", 0))
        ins += [a, b]
        specs += [pl.BlockSpec((tm, K), lambda i, j: (i, 0)), pl.BlockSpec((tn, K), lambda i, j: (j, 0))]
    mn = pl.BlockSpec((tm, tn), lambda i, j: (i, j))
    n_in = len(ins)

    def epilogue(accs, in_refs, out_refs):
        out_refs[0][...] = accs[0] + in_refs[n_in][...]

    return _mm(name, (M // tm, N // tn), ins + [prev], specs + [mn], terms, 1, None, epilogue,
               [jax.ShapeDtypeStruct((M, N), F32)], [mn], None)[0]


def _epi_sigmoid(acc, ex, outs):
    outs[0][...] = _sigmoid(acc).astype(outs[0].dtype)


def _epi_add(acc, ex, outs):
    outs[0][...] = (acc + ex[0][...].astype(F32)).astype(outs[0].dtype)


def _epi_dmerge(acc, ex, outs):
    mp, dp, g0, g1 = [e[...].astype(F32) for e in ex]
    outs[0][...] = (acc * g0).astype(BF16)
    outs[1][...] = (acc * g1).astype(BF16)
    outs[2][...] = (acc * mp * g0 * (1.0 - g0)).astype(BF16)
    outs[3][...] = (acc * dp * g1 * (1.0 - g1)).astype(BF16)


_WEIGHTS = ("g_ffn1", "w1_gate", "w1_up", "w1_down", "g_mix", "w_in", "g_cq", "w_uq", "g_ckv", "w_ukv", "g_q_mla", "g_k_mla",
            "g_q_dil", "g_k_dil", "w_br_mla", "w_br_dil", "w_o", "g_ffn2", "w2_gate", "w2_up", "w2_down", "g_ple",
            "w_ple_gate", "w_ple_proj")
_MATRICES = ("w1_gate", "w1_up", "w1_down", "w_in", "w_uq", "w_ukv", "w_br_mla", "w_br_dil", "w_o", "w2_gate", "w2_up",
             "w2_down", "w_ple_gate", "w_ple_proj")
_GAINS = tuple(n for n in _WEIGHTS if n not in _MATRICES)
MLA_BLOCK = 2048
MLA_SUB = 256
DIL_ROWS = 2048
_FIRST = ("w1_gate", "w1_up", "w1_down")
_MID = ("w_in", "w_uq", "w_ukv", "w_br_mla", "w_br_dil", "w_o", "w_ple_gate", "w_ple_proj")
_LATE = ("w2_gate", "w2_up", "w2_down")
_RS_GROUPS = (("w_ple_proj", "w_ple_gate", "w2_gate", "w2_up", "w2_down"),
              ("w_o", "w_br_mla", "w_br_dil", "w_uq", "w_ukv", "w_in"),
              ("w1_gate", "w1_up", "w1_down"))


def _cols(g3):
    nd, K, n = g3.shape
    return g3.transpose(1, 0, 2).reshape(K, nd * n)


def _uncols(m):
    K, n = m.shape
    return m.reshape(K, N_DEV, n // N_DEV).transpose(1, 0, 2)


def _rows(g3):
    nd, k, N = g3.shape
    return g3.reshape(nd * k, N)


def _unrows(m):
    K, N = m.shape
    return m.reshape(N_DEV, K // N_DEV, N)


def _pack_gains(vals):
    flat = jnp.concatenate([vals[n].reshape(-1) for n in _GAINS])
    pad = (-flat.shape[0]) % 2048
    return jnp.pad(flat, (0, pad)).reshape(-1, 128)


def _unpack_gains(packed, like):
    flat = packed.reshape(-1)
    out, off = {}, 0
    for n in _GAINS:
        size = int(np.prod(like[n].shape))
        out[n] = flat[off:off + size].reshape(like[n].shape)
        off += size
    return out


def _train_step(x, p, positions, loss_target, W, M, V):
    T, D = x.shape[1], x.shape[2]
    xs, tgt, pe = x[0], loss_target[0], p[0, 0]
    pos_col = positions.reshape(T, 1).astype(F32)
    w = {n: (a[0] if n in _MATRICES else a.reshape(1, -1)) for n, a in W.items()}

    shard = {n: w[n].astype(BF16) for n in _MATRICES}
    gathered = dict(zip(_FIRST, _allgather("ag_first", [shard[n] for n in _FIRST])))
    ag_mid = _ag_first([shard[n] for n in _MID])
    ag_mid2 = []

    def pass_on_mid():
        ag_mid2.append(_ag_second(ag_mid.results))
        return ag_mid2[0]

    x1, ffn1 = _ffn_forward("ffn1", xs, w["g_ffn1"], gathered["w1_gate"], gathered["w1_up"], gathered["w1_down"],
                            comm_up=ag_mid, comm_down=pass_on_mid)
    gathered.update(zip(_MID, ag_mid2[0].results))
    nq_l, nkv_l = w["g_cq"].shape[-1], w["g_ckv"].shape[-1]
    H = w["w_uq"].shape[1] * N_DEV // MLA_QK
    G = len(DIL_GROUPS)
    off_kr = nq_l + nkv_l
    off_dil = off_kr + MLA_ROPE
    off_gate = off_dil + G * 3 * H * HEAD
    kr_block = off_kr // 128
    w_in = _cols(gathered["w_in"])
    wa = jnp.pad(w_in[:, :off_dil], ((0, 0), (0, 128 - MLA_ROPE)))
    wdil, wg0, wg1 = w_in[:, off_dil:off_gate], w_in[:, off_gate:off_gate + D], w_in[:, off_gate + D:]
    wuq = jnp.pad(_cols(gathered["w_uq"]).reshape(nq_l, H, MLA_QK), ((0, 0), (0, 0), (0, MLA_PAD - MLA_QK))).reshape(nq_l, H * MLA_PAD)
    wukv = _cols(gathered["w_ukv"])
    wbm, wbd, wpp = _cols(gathered["w_br_mla"]), _cols(gathered["w_br_dil"]), _cols(gathered["w_ple_proj"])
    wo, wpg = _rows(gathered["w_o"]), _rows(gathered["w_ple_gate"])
    gq_mla = jnp.pad(w["g_q_mla"], ((0, 0), (0, MLA_PAD - MLA_QK)))
    gk_mla = jnp.pad(w["g_k_mla"], ((0, 0), (0, MLA_PAD - MLA_QK)))
    gq_dil, gk_dil = w["g_q_dil"].reshape(G, 1, HEAD), w["g_k_dil"].reshape(G, 1, HEAD)

    half_m, half_d = MLA_ROPE // 2, DIL_ROT // 2
    inv_m = ROPE_THETA ** (-jnp.arange(half_m, dtype=F32) * 2.0 / MLA_ROPE)
    inv_d = ROPE_THETA ** (-jnp.arange(half_d, dtype=F32) * 2.0 / DIL_ROT)
    inv_m = jnp.tile(inv_m, 128 // half_m).reshape(1, 128)
    inv_d = jnp.tile(inv_d, 128 // half_d).reshape(1, 128)
    tabs = _rope_tables(pos_col, inv_m, inv_d)
    tabs_m, tabs_d = tabs[:3], tabs[3:]

    mla_blk, dil_qb = _tile(T, MLA_BLOCK), _tile(T, DIL_ROWS)
    gw = 3 * H * HEAD
    dils = [d for _, d in DIL_GROUPS]
    units = [win // d for win, d in DIL_GROUPS]
    wdil_g = [wdil[:, g * gw:(g + 1) * gw] for g in range(G)]
    tabs_g = [[_to_phase(t, d) for t in tabs_d] for d in dils]

    h = _rmsnorm_fwd("mix_norm", x1, w["g_mix"])
    pa = _mm2d("proj_a", h, wa, "nn", 512, 1024, 4096)
    pd = [_to_phase(_mm2d("proj_dil%d" % g, h, wdil_g[g], "nn", 1024, 1024, 4096, out_dtypes=(BF16,)), dils[g]) for g in range(G)]
    g0 = _mm2d("proj_gate0", h, wg0, "nn", 1024, 1024, 4096, out_dtypes=(BF16,), epilogue=_epi_sigmoid)
    g1 = _mm2d("proj_gate1", h, wg1, "nn", 1024, 1024, 4096, out_dtypes=(BF16,), epilogue=_epi_sigmoid)
    cq, ckv = _lora_fwd(pa, w["g_cq"], w["g_ckv"], nq_l, nkv_l)
    q_raw = _mm2d("q_up", cq, wuq, "nn", 512, 2048, 4096)
    kv = _mm2d("kv_up", ckv, wukv, "nn", 512, 2048, 4096)
    q_att, k_att, v_mla = _mla_prep_fwd(q_raw, kv, pa, tabs_m, gq_mla, gk_mla, H, kr_block)
    ag_late = _ag_first([shard[n] for n in _LATE])
    o_mla, lse_mla = _mla_fwd(q_att, k_att, v_mla, H, mla_blk, comm=ag_late)
    qd, kd, o_g, lse_g = [], [], [], []
    for g in range(G):
        qg, kg = _dil_prep_fwd_pm("dil_prep_fwd%d" % g, pd[g], tabs_g[g], gq_dil[g], gk_dil[g], H)
        og, lg = _win_fwd("dil_fwd%d" % g, qg, kg, pd[g], H, units[g], T // dils[g], dil_qb)
        qd.append(qg)
        kd.append(kg)
        o_g.append(_from_phase(og, dils[g]))
        lse_g.append(_from_phase(lg, dils[g], axis=1))
    o_dil, lse_dil = _merge_groups(o_g, lse_g, H)
    ag_late2 = _ag_second(ag_late.results)
    mla_p, dil_p, merged = _merge_fwd(o_mla, wbm, o_dil, wbd, g0, g1, comm=ag_late2)
    gathered.update(zip(_LATE, ag_late2.results))
    x2 = _mm2d("out_proj", merged, wo, "nn", 512, 1024, 4096, epilogue=_epi_add, extras=(x1,))
    x3, ffn2 = _ffn_forward("ffn2", x2, w["g_ffn2"], gathered["w2_gate"], gathered["w2_up"], gathered["w2_down"])
    n4 = _rmsnorm_fwd("ple_norm", x3, w["g_ple"])
    dy, dz, dproj, loss_rows = _ple_loss(n4, wpg, pe, wpp, x3, tgt)
    loss = lax.psum((0.5 / D) * jnp.sum(loss_rows), ("x", "y", "c"))

    dW, dG = {}, {}
    dW["w_ple_proj"] = _uncols(_mm2d("d_wpp", pe, dproj, "tn", 1024, 2048, 2048, out_dtypes=(BF16,)))
    dW["w_ple_gate"] = _unrows(_mm2d("d_wpg", n4, dz, "tn", 1024, 1024, 2048, out_dtypes=(BF16,)))
    dn4 = _mm2d("d_n4", dz, wpg, "nt", 512, 1024, 4096)
    dx3, dx3b, dG["g_ple"] = _rmsnorm_bwd("ple_norm_bwd", dn4, x3, w["g_ple"], dy)

    core = lax.axis_index("c").astype(jnp.int32).reshape(1)
    grads, deltas, new_m, new_v = {}, {}, {}, {}

    def sibling_exchange(names):
        return _rs_first([dW[n].reshape(4, 2, *dW[n].shape[1:]) for n in names])

    def chip_exchange(names, first):
        return _rs_second([_pair_add("rs_add_" + n, own, rec, core) for n, own, rec in zip(names, first.ins, first.results)])

    def update(names, second):
        for n, parts in zip(names, second.results):
            grads[n], deltas[n], new_m[n], new_v[n] = [a[None] for a in _adamw("adamw_" + n, parts, w[n], M[n][0], V[n][0])]

    rs_a = []

    def ffn2_grads_done(dws):
        dW["w2_gate"], dW["w2_up"], dW["w2_down"] = dws
        rs_a.append(sibling_exchange(_RS_GROUPS[0]))
        return rs_a[0]

    dx2, dx2b, dG["g_ffn2"], _, _, _ = _ffn_backward(
        "ffn2", dx3, dx3b, x2, w["g_ffn2"], gathered["w2_gate"], gathered["w2_up"], gathered["w2_down"], ffn2,
        comm_dn=ffn2_grads_done)
    rs_a2 = chip_exchange(_RS_GROUPS[0], rs_a[0])

    d_mla_p, d_dil_p, dpg0, dpg1 = _mm2d("d_merged", dx2b, wo, "nt", 512, 1024, 4096, out_dtypes=(BF16,) * 4,
                                         epilogue=_epi_dmerge, extras=(mla_p, dil_p, g0, g1))
    dW["w_o"] = _unrows(_mm2d("d_wo", merged, dx2b, "tn", 1024, 1024, 2048, out_dtypes=(BF16,)))
    dW["w_br_mla"] = _uncols(_mm2d("d_wbm", o_mla, d_mla_p, "tn", 1024, 1024, 2048, out_dtypes=(BF16,)))
    dW["w_br_dil"] = _uncols(_mm2d("d_wbd", o_dil, d_dil_p, "tn", 1024, 1024, 2048, out_dtypes=(BF16,)))
    do_mla = _mm2d("d_o_mla", d_mla_p, wbm, "nt", 512, 1024, 4096, out_dtypes=(BF16,))
    do_dil = _mm2d("d_o_dil", d_dil_p, wbd, "nt", 512, 1024, 4096, out_dtypes=(BF16,))

    dl_mla = _row_dot("mla_delta", do_mla, o_mla, H)
    dq_att = _mla_dq(q_att, k_att, v_mla, do_mla, lse_mla, dl_mla, H, mla_blk, comm=rs_a2)
    update(_RS_GROUPS[0], rs_a2)
    dk_att, dv_mla = _mla_dkv(q_att, k_att, v_mla, do_mla, lse_mla.reshape(H, 1, T), dl_mla.reshape(H, 1, T), H, mla_blk)
    dq_raw, dkv, dkr, dgq, dgk = _mla_prep_bwd(dq_att, dk_att, dv_mla, q_raw, kv, pa, tabs_m, gq_mla, gk_mla, H, kr_block)
    dG["g_q_mla"], dG["g_k_mla"] = dgq[:, :MLA_QK], dgk[:, :MLA_QK]
    d_wuq = _mm2d("d_wuq", cq, dq_raw, "tn", 512, 2048, 2048, out_dtypes=(BF16,))
    dW["w_uq"] = _uncols(d_wuq.reshape(nq_l, H, MLA_PAD)[:, :, :MLA_QK].reshape(nq_l, H * MLA_QK))
    dW["w_ukv"] = _uncols(_mm2d("d_wukv", ckv, dkv, "tn", 512, 2048, 2048, out_dtypes=(BF16,)))
    dcq = _mm2d("d_cq", dq_raw, wuq, "nt", 512, 1024, 4096)
    dckv = _mm2d("d_ckv", dkv, wukv, "nt", 512, 1024, 4096)
    dpa, dG["g_cq"], dG["g_ckv"] = _lora_bwd(dcq, dckv, dkr, pa, w["g_cq"], w["g_ckv"])

    dl_dil = _row_dot("dil_delta", do_dil, o_dil, H)
    dpd, dgqd, dgkd = [], [], []
    for g in range(G):
        d, n, L = dils[g], units[g], T // dils[g]
        do_g, lse_pg, dl_pg = _to_phase(do_dil, d), _to_phase(lse_dil, d, axis=1), _to_phase(dl_dil, d, axis=1)
        dq_g = _win_dq("dil_dq%d" % g, qd[g], kd[g], pd[g], do_g, lse_pg, dl_pg, H, n, L, dil_qb)
        dk_g, dv_g = _win_dkv("dil_dkv%d" % g, qd[g], kd[g], pd[g], do_g, lse_pg, dl_pg, H, n, L, dil_qb)
        dp_g, dgq_g, dgk_g = _dil_prep_bwd_pm("dil_prep_bwd%d" % g, dq_g, dk_g, dv_g, pd[g], tabs_g[g], gq_dil[g], gk_dil[g], H)
        dpd.append(_from_phase(dp_g, d))
        dgqd.append(dgq_g)
        dgkd.append(dgk_g)
    dG["g_q_dil"], dG["g_k_dil"] = jnp.concatenate(dgqd).reshape(1, G, HEAD), jnp.concatenate(dgkd).reshape(1, G, HEAD)

    d_wa = _mm2d("d_wa", h, dpa, "tn", 1024, 1024, 2048, out_dtypes=(BF16,))
    d_wdil = [_mm2d("d_wdil%d" % g, h, dpd[g], "tn", 1024, 1024, 2048, out_dtypes=(BF16,)) for g in range(G)]
    d_wg0 = _mm2d("d_wg0", h, dpg0, "tn", 1024, 1024, 2048, out_dtypes=(BF16,))
    d_wg1 = _mm2d("d_wg1", h, dpg1, "tn", 1024, 1024, 2048, out_dtypes=(BF16,))
    dW["w_in"] = _uncols(jnp.concatenate([d_wa[:, :off_dil], *d_wdil, d_wg0, d_wg1], axis=1))
    dh = _mm2d("d_h_a", dpa, wa, "nt", 512, 1024, 4096)
    dh = _sum_nt("d_h_dil", list(zip(dpd, wdil_g)), dh, 512, 512)
    dh = _sum_nt("d_h_gates", [(dpg0, wg0), (dpg1, wg1)], dh, 512, 1024)
    dx1, dx1b, dG["g_mix"] = _rmsnorm_bwd("mix_norm_bwd", dh, x1, w["g_mix"], dx2)
    rs_b = sibling_exchange(_RS_GROUPS[1])
    rs_b2 = []

    def attention_grads_summed(_):
        rs_b2.append(chip_exchange(_RS_GROUPS[1], rs_b))
        return rs_b2[0]

    rs_c2 = []

    def ffn1_grads_done(dws):
        dW["w1_gate"], dW["w1_up"], dW["w1_down"] = dws
        rs_c = sibling_exchange(_RS_GROUPS[2])
        _run_exchange("rs_first_w1", rs_c)
        rs_c2.append(chip_exchange(_RS_GROUPS[2], rs_c))
        return rs_c2[0]

    dx0, _, dG["g_ffn1"], _, _, _ = _ffn_backward(
        "ffn1", dx1, dx1b, xs, w["g_ffn1"], gathered["w1_gate"], gathered["w1_up"], gathered["w1_down"], ffn1,
        comm_act=rs_b, comm_dn=attention_grads_summed, comm_norm=ffn1_grads_done, weights_first=False)
    update(_RS_GROUPS[1], rs_b2[0])
    update(_RS_GROUPS[2], rs_c2[0])

    parts = _allgather("ag_gain_grads", [_pack_gains(dG)])[0]
    packed = _adamw("adamw_gains", parts, _pack_gains(W), _pack_gains(M), _pack_gains(V))
    for out, pk in zip((grads, deltas, new_m, new_v), packed):
        out.update(_unpack_gains(pk, W))

    return (loss, dx0[None], *[grads[n] for n in _WEIGHTS], *[deltas[n] for n in _WEIGHTS],
            *[new_m[n] for n in _WEIGHTS], *[new_v[n] for n in _WEIGHTS])


def kernel(x, p, positions, g_ffn1, w1_gate, w1_up, w1_down, g_mix, w_in, g_cq, w_uq, g_ckv, w_ukv, g_q_mla, g_k_mla, g_q_dil, g_k_dil, w_br_mla, w_br_dil, w_o, g_ffn2, w2_gate, w2_up, w2_down, g_ple, w_ple_gate, w_ple_proj, loss_target, m_g_ffn1, m_w1_gate, m_w1_up, m_w1_down, m_g_mix, m_w_in, m_g_cq, m_w_uq, m_g_ckv, m_w_ukv, m_g_q_mla, m_g_k_mla, m_g_q_dil, m_g_k_dil, m_w_br_mla, m_w_br_dil, m_w_o, m_g_ffn2, m_w2_gate, m_w2_up, m_w2_down, m_g_ple, m_w_ple_gate, m_w_ple_proj, v_g_ffn1, v_w1_gate, v_w1_up, v_w1_down, v_g_mix, v_w_in, v_g_cq, v_w_uq, v_g_ckv, v_w_ukv, v_g_q_mla, v_g_k_mla, v_g_q_dil, v_g_k_dil, v_w_br_mla, v_w_br_dil, v_w_o, v_g_ffn2, v_w2_gate, v_w2_up, v_w2_down, v_g_ple, v_w_ple_gate, v_w_ple_proj):
    W = dict(zip(_WEIGHTS, (g_ffn1, w1_gate, w1_up, w1_down, g_mix, w_in, g_cq, w_uq, g_ckv, w_ukv, g_q_mla, g_k_mla, g_q_dil,
                            g_k_dil, w_br_mla, w_br_dil, w_o, g_ffn2, w2_gate, w2_up, w2_down, g_ple, w_ple_gate, w_ple_proj)))
    M = dict(zip(_WEIGHTS, (m_g_ffn1, m_w1_gate, m_w1_up, m_w1_down, m_g_mix, m_w_in, m_g_cq, m_w_uq, m_g_ckv, m_w_ukv, m_g_q_mla,
                            m_g_k_mla, m_g_q_dil, m_g_k_dil, m_w_br_mla, m_w_br_dil, m_w_o, m_g_ffn2, m_w2_gate, m_w2_up,
                            m_w2_down, m_g_ple, m_w_ple_gate, m_w_ple_proj)))
    V = dict(zip(_WEIGHTS, (v_g_ffn1, v_w1_gate, v_w1_up, v_w1_down, v_g_mix, v_w_in, v_g_cq, v_w_uq, v_g_ckv, v_w_ukv, v_g_q_mla,
                            v_g_k_mla, v_g_q_dil, v_g_k_dil, v_w_br_mla, v_w_br_dil, v_w_o, v_g_ffn2, v_w2_gate, v_w2_up,
                            v_w2_down, v_g_ple, v_w_ple_gate, v_w_ple_proj)))
    return _train_step(x, p, positions, loss_target, W, M, V)
```
